```python
import jax, jax.numpy as jnp
from jax import lax
import numpy as np

D_MODEL = 2048
BATCH = 8
SEQ = 8192
DEPTH = 4

CHUNK = 64
RET_HEADS = 8
RET_DK = 128
RET_DV = 256
RET_QK = RET_HEADS * RET_DK
RET_V = RET_HEADS * RET_DV
SGU_GROUPS = 8
SGU_LEN = 128
SGU_WIDTH = D_MODEL
SGU_CH = SGU_WIDTH // SGU_GROUPS
D_FF = -(-(8 * D_MODEL) // (3 * 256)) * 256
ROPE_BASE = 10000.0
EPS = 1e-6
SPLITS = [RET_QK, 2 * RET_QK, 2 * RET_QK + RET_V, 2 * RET_QK + 2 * RET_V,
          2 * RET_QK + 2 * RET_V + SGU_WIDTH, 2 * RET_QK + 2 * RET_V + 2 * SGU_WIDTH,
          2 * RET_QK + 2 * RET_V + 2 * SGU_WIDTH + D_MODEL]
IN_COLS = 2 * RET_QK + 2 * RET_V + 2 * SGU_WIDTH + 2 * D_MODEL

kernel_name = "hybrid_retention_sgu_griffin_merge"


def rms_norm(x, w):
    xf = x.astype(jnp.float32)
    y = xf * lax.rsqrt(jnp.mean(xf * xf, axis=-1, keepdims=True) + EPS)
    return (y * w.astype(jnp.float32)).astype(x.dtype)


def layer_norm(x, w, b):
    xf = x.astype(jnp.float32)
    mu = jnp.mean(xf, axis=-1, keepdims=True)
    var = jnp.mean(jnp.square(xf - mu), axis=-1, keepdims=True)
    y = (xf - mu) * lax.rsqrt(var + EPS) * w.astype(jnp.float32) + b.astype(jnp.float32)
    return y.astype(x.dtype)


def head_group_norm(o, w):
    B, S, H, dv = o.shape
    of = o.astype(jnp.float32)
    mu = jnp.mean(of, axis=-1, keepdims=True)
    var = jnp.mean(jnp.square(of - mu), axis=-1, keepdims=True)
    y = ((of - mu) * lax.rsqrt(var + EPS)).reshape(B, S, H * dv) * w.astype(jnp.float32)
    return y.astype(o.dtype)


def rotary(x, pos):
    half = x.shape[-1] // 2
    inv = ROPE_BASE ** (-jnp.arange(half, dtype=jnp.float32) / half)
    ang = pos.astype(jnp.float32)[:, None] * inv[None, :]
    cos = jnp.cos(ang)[None, :, None, :]
    sin = jnp.sin(ang)[None, :, None, :]
    xf = x.astype(jnp.float32)
    x1, x2 = xf[..., :half], xf[..., half:]
    return jnp.concatenate([x1 * cos - x2 * sin, x2 * cos + x1 * sin], axis=-1).astype(x.dtype)


def retention(q, k, v):
    B, S, H, dk = q.shape
    dv = v.shape[-1]
    nc = S // CHUNK
    dt = q.dtype
    log_g = jnp.log1p(-(2.0 ** (-5.0 - jnp.arange(H, dtype=jnp.float32))))
    idx = jnp.arange(CHUNK, dtype=jnp.float32)
    intra_decay = jnp.exp(log_g[:, None, None] * jnp.abs(idx[:, None] - idx[None, :]))
    q_decay = jnp.exp(log_g[None, :] * (idx[:, None] + 1.0))
    k_decay = jnp.exp(log_g[None, :] * (CHUNK - 1.0 - idx[:, None]))
    chunk_decay = jnp.exp(log_g * CHUNK)

    q = q.reshape(B, nc, CHUNK, H, dk) * (dk ** -0.5)
    k = k.reshape(B, nc, CHUNK, H, dk)
    v = v.reshape(B, nc, CHUNK, H, dv)

    scores = jnp.einsum('bnihd,bnjhd->bnhij', q, k) * intra_decay.astype(dt)
    o_intra = jnp.einsum('bnhij,bnjhe->bnihe', scores, v)

    qs = q * q_decay.astype(dt)[:, :, None]
    ks = k * k_decay.astype(dt)[:, :, None]
    cdec = chunk_decay.astype(dt)[None, :, None, None]

    def step(state, xs):
        qc, kc, vc = xs
        o = jnp.einsum('bihd,bhde->bihe', qc, state)
        state = state * cdec + jnp.einsum('bjhd,bjhe->bhde', kc, vc)
        return state, o

    s0 = jnp.zeros((B, H, dk, dv), dtype=v.dtype)
    _, o_inter = lax.scan(step, s0, (jnp.swapaxes(qs, 0, 1), jnp.swapaxes(ks, 0, 1),
                                     jnp.swapaxes(v, 0, 1)))
    o = o_intra + jnp.swapaxes(o_inter, 0, 1)
    return o.reshape(B, S, H, dv)


def spatial_gating(u, v, ln_w, ln_b, w_s, b_s):
    B, S, W = v.shape
    ng = S // SGU_LEN
    vn = layer_norm(v, ln_w, ln_b).reshape(B, ng, SGU_LEN, SGU_GROUPS, SGU_CH)
    pos = jnp.arange(SGU_LEN)
    mask = (pos[None, :] // CHUNK) <= (pos[:, None] // CHUNK)
    w = jnp.where(mask[None], w_s, jnp.zeros_like(w_s))
    mixed = jnp.einsum('gij,bnjgc->bnigc', w, vn) + b_s.T[None, None, :, :, None]
    return u * mixed.reshape(B, S, W)


def hybrid_layer(x, pos, norm_mix_w, w_in, ret_gn_w, ret_proj, sgu_ln_w, sgu_ln_b,
                 sgu_w_s, sgu_b_s, sgu_proj, w_out, norm_ffn_w, w_ffn_in, w_ffn_out):
    B, S, _ = x.shape
    h = rms_norm(x, norm_mix_w)
    z = h @ w_in
    q, k, v, g, su, sv, gate_a, gate_b = jnp.split(z, SPLITS, axis=-1)

    q = rotary(q.reshape(B, S, RET_HEADS, RET_DK), pos)
    k = rotary(k.reshape(B, S, RET_HEADS, RET_DK), pos)
    v = v.reshape(B, S, RET_HEADS, RET_DV)
    ret = head_group_norm(retention(q, k, v), ret_gn_w)
    branch_a = (jax.nn.silu(g) * ret) @ ret_proj

    zu = jax.nn.gelu(su, approximate=False)
    zv = jax.nn.gelu(sv, approximate=False)
    branch_b = spatial_gating(zu, zv, sgu_ln_w, sgu_ln_b, sgu_w_s, sgu_b_s) @ sgu_proj

    merged = jax.nn.sigmoid(gate_a) * branch_a + jax.nn.sigmoid(gate_b) * branch_b
    x = x + merged @ w_out

    h = rms_norm(x, norm_ffn_w)
    a, c = jnp.split(h @ w_ffn_in, 2, axis=-1)
    x = x + (jax.nn.silu(a) * c) @ w_ffn_out
    return x


def _fwd_setup_inputs(seed: int = 0) -> dict:
    key = jax.random.key(seed)
    ks = jax.random.split(key, 16)
    f32 = jnp.float32
    nrm = lambda k, shape, scale: jax.random.normal(k, shape, f32) * scale
    return {
        "x": jax.random.normal(ks[0], (BATCH, SEQ, D_MODEL), f32),
        "norm_mix_w": 1.0 + nrm(ks[1], (DEPTH, D_MODEL), 0.02),
        "w_in": nrm(ks[2], (DEPTH, D_MODEL, IN_COLS), D_MODEL ** -0.5),
        "ret_gn_w": 1.0 + nrm(ks[3], (DEPTH, RET_V), 0.02),
        "ret_proj": nrm(ks[4], (DEPTH, RET_V, D_MODEL), RET_V ** -0.5),
        "sgu_ln_w": 1.0 + nrm(ks[5], (DEPTH, SGU_WIDTH), 0.02),
        "sgu_ln_b": nrm(ks[6], (DEPTH, SGU_WIDTH), 0.02),
        "sgu_w_s": nrm(ks[7], (DEPTH, SGU_GROUPS, SGU_LEN, SGU_LEN), SGU_LEN ** -0.5),
        "sgu_b_s": 1.0 + nrm(ks[8], (DEPTH, SGU_GROUPS, SGU_LEN), 0.02),
        "sgu_proj": nrm(ks[9], (DEPTH, SGU_WIDTH, D_MODEL), SGU_WIDTH ** -0.5),
        "w_out": nrm(ks[10], (DEPTH, D_MODEL, D_MODEL), D_MODEL ** -0.5),
        "norm_ffn_w": 1.0 + nrm(ks[11], (DEPTH, D_MODEL), 0.02),
        "w_ffn_in": nrm(ks[12], (DEPTH, D_MODEL, 2 * D_FF), D_MODEL ** -0.5),
        "w_ffn_out": nrm(ks[13], (DEPTH, D_FF, D_MODEL), D_FF ** -0.5),
        "final_norm_w": 1.0 + nrm(ks[14], (D_MODEL,), 0.02),
    }


def _fwd_reference(x, norm_mix_w, w_in, ret_gn_w, ret_proj, sgu_ln_w, sgu_ln_b, sgu_w_s, sgu_b_s,
              sgu_proj, w_out, norm_ffn_w, w_ffn_in, w_ffn_out, final_norm_w):
    pos = jnp.arange(x.shape[1], dtype=jnp.int32)
    for l in range(DEPTH):
        x = hybrid_layer(x, pos, norm_mix_w[l], w_in[l], ret_gn_w[l], ret_proj[l], sgu_ln_w[l],
                         sgu_ln_b[l], sgu_w_s[l], sgu_b_s[l], sgu_proj[l], w_out[l],
                         norm_ffn_w[l], w_ffn_in[l], w_ffn_out[l])
    return rms_norm(x, final_norm_w)


import jax as _jax
import jax.numpy as _jnp

TWIN_FORMAT = 'train_step'
FWD_PARAMS = ['x', 'norm_mix_w', 'w_in', 'ret_gn_w', 'ret_proj', 'sgu_ln_w', 'sgu_ln_b', 'sgu_w_s', 'sgu_b_s', 'sgu_proj', 'w_out', 'norm_ffn_w', 'w_ffn_in', 'w_ffn_out', 'final_norm_w']
TWIN_WEIGHTS = ['norm_mix_w', 'w_in', 'ret_gn_w', 'ret_proj', 'sgu_ln_w', 'sgu_ln_b', 'sgu_w_s', 'sgu_b_s', 'sgu_proj', 'w_out', 'norm_ffn_w', 'w_ffn_in', 'w_ffn_out', 'final_norm_w']
TWIN_DIFF_INPUT = 'x'
TWIN_INPUTS = ['x', 'norm_mix_w', 'w_in', 'ret_gn_w', 'ret_proj', 'sgu_ln_w', 'sgu_ln_b', 'sgu_w_s', 'sgu_b_s', 'sgu_proj', 'w_out', 'norm_ffn_w', 'w_ffn_in', 'w_ffn_out', 'final_norm_w', 'loss_target', 'm_norm_mix_w', 'm_w_in', 'm_ret_gn_w', 'm_ret_proj', 'm_sgu_ln_w', 'm_sgu_ln_b', 'm_sgu_w_s', 'm_sgu_b_s', 'm_sgu_proj', 'm_w_out', 'm_norm_ffn_w', 'm_w_ffn_in', 'm_w_ffn_out', 'm_final_norm_w', 'v_norm_mix_w', 'v_w_in', 'v_ret_gn_w', 'v_ret_proj', 'v_sgu_ln_w', 'v_sgu_ln_b', 'v_sgu_w_s', 'v_sgu_b_s', 'v_sgu_proj', 'v_w_out', 'v_norm_ffn_w', 'v_w_ffn_in', 'v_w_ffn_out', 'v_final_norm_w']
TWIN_OUTPUTS = ['loss', 'grad_x', 'grad_norm_mix_w', 'grad_w_in', 'grad_ret_gn_w', 'grad_ret_proj', 'grad_sgu_ln_w', 'grad_sgu_ln_b', 'grad_sgu_w_s', 'grad_sgu_b_s', 'grad_sgu_proj', 'grad_w_out', 'grad_norm_ffn_w', 'grad_w_ffn_in', 'grad_w_ffn_out', 'grad_final_norm_w', 'delta_norm_mix_w', 'delta_w_in', 'delta_ret_gn_w', 'delta_ret_proj', 'delta_sgu_ln_w', 'delta_sgu_ln_b', 'delta_sgu_w_s', 'delta_sgu_b_s', 'delta_sgu_proj', 'delta_w_out', 'delta_norm_ffn_w', 'delta_w_ffn_in', 'delta_w_ffn_out', 'delta_final_norm_w', 'new_m_norm_mix_w', 'new_m_w_in', 'new_m_ret_gn_w', 'new_m_ret_proj', 'new_m_sgu_ln_w', 'new_m_sgu_ln_b', 'new_m_sgu_w_s', 'new_m_sgu_b_s', 'new_m_sgu_proj', 'new_m_w_out', 'new_m_norm_ffn_w', 'new_m_w_ffn_in', 'new_m_w_ffn_out', 'new_m_final_norm_w', 'new_v_norm_mix_w', 'new_v_w_in', 'new_v_ret_gn_w', 'new_v_ret_proj', 'new_v_sgu_ln_w', 'new_v_sgu_ln_b', 'new_v_sgu_w_s', 'new_v_sgu_b_s', 'new_v_sgu_proj', 'new_v_w_out', 'new_v_norm_ffn_w', 'new_v_w_ffn_in', 'new_v_w_ffn_out', 'new_v_final_norm_w']
TWIN_LEAF_KINDS = {'loss': 'loss', 'grad_x': 'grad_x', 'grad_norm_mix_w': 'grad_w', 'grad_w_in': 'grad_w', 'grad_ret_gn_w': 'grad_w', 'grad_ret_proj': 'grad_w', 'grad_sgu_ln_w': 'grad_w', 'grad_sgu_ln_b': 'grad_w', 'grad_sgu_w_s': 'grad_w', 'grad_sgu_b_s': 'grad_w', 'grad_sgu_proj': 'grad_w', 'grad_w_out': 'grad_w', 'grad_norm_ffn_w': 'grad_w', 'grad_w_ffn_in': 'grad_w', 'grad_w_ffn_out': 'grad_w', 'grad_final_norm_w': 'grad_w', 'delta_norm_mix_w': 'delta_w', 'delta_w_in': 'delta_w', 'delta_ret_gn_w': 'delta_w', 'delta_ret_proj': 'delta_w', 'delta_sgu_ln_w': 'delta_w', 'delta_sgu_ln_b': 'delta_w', 'delta_sgu_w_s': 'delta_w', 'delta_sgu_b_s': 'delta_w', 'delta_sgu_proj': 'delta_w', 'delta_w_out': 'delta_w', 'delta_norm_ffn_w': 'delta_w', 'delta_w_ffn_in': 'delta_w', 'delta_w_ffn_out': 'delta_w', 'delta_final_norm_w': 'delta_w', 'new_m_norm_mix_w': 'new_m', 'new_m_w_in': 'new_m', 'new_m_ret_gn_w': 'new_m', 'new_m_ret_proj': 'new_m', 'new_m_sgu_ln_w': 'new_m', 'new_m_sgu_ln_b': 'new_m', 'new_m_sgu_w_s': 'new_m', 'new_m_sgu_b_s': 'new_m', 'new_m_sgu_proj': 'new_m', 'new_m_w_out': 'new_m', 'new_m_norm_ffn_w': 'new_m', 'new_m_w_ffn_in': 'new_m', 'new_m_w_ffn_out': 'new_m', 'new_m_final_norm_w': 'new_m', 'new_v_norm_mix_w': 'new_v', 'new_v_w_in': 'new_v', 'new_v_ret_gn_w': 'new_v', 'new_v_ret_proj': 'new_v', 'new_v_sgu_ln_w': 'new_v', 'new_v_sgu_ln_b': 'new_v', 'new_v_sgu_w_s': 'new_v', 'new_v_sgu_b_s': 'new_v', 'new_v_sgu_proj': 'new_v', 'new_v_w_out': 'new_v', 'new_v_norm_ffn_w': 'new_v', 'new_v_w_ffn_in': 'new_v', 'new_v_w_ffn_out': 'new_v', 'new_v_final_norm_w': 'new_v'}


def _forward(args):
    return _fwd_reference(*[args[k] for k in FWD_PARAMS])


def _output_shape():
    def fwd():
        inp = _fwd_setup_inputs(0)
        return _fwd_reference(*[inp[k] for k in FWD_PARAMS])
    out = _jax.eval_shape(fwd)
    return out.shape, out.dtype

N_MICROBATCH = 1
ADAM_LR = 0.001
ADAM_B1 = 0.9
ADAM_B2 = 0.999
ADAM_EPS = 1e-08
ADAM_WD = 0.01
ADAM_STEP = 10
PER_EXAMPLE_BATCH_AXIS = {'x': 0, 'loss_target': 0}
SHARED_INPUTS = []
_WEIGHT_DTYPES = {'norm_mix_w': _jnp.float32, 'w_in': _jnp.float32, 'ret_gn_w': _jnp.float32, 'ret_proj': _jnp.float32, 'sgu_ln_w': _jnp.float32, 'sgu_ln_b': _jnp.float32, 'sgu_w_s': _jnp.float32, 'sgu_b_s': _jnp.float32, 'sgu_proj': _jnp.float32, 'w_out': _jnp.float32, 'norm_ffn_w': _jnp.float32, 'w_ffn_in': _jnp.float32, 'w_ffn_out': _jnp.float32, 'final_norm_w': _jnp.float32}
MOMENT_SCALE = {'norm_mix_w': 1.176547e-01, 'w_in': 4.365959e-02, 'ret_gn_w': 4.132314e-02, 'ret_proj': 4.102738e-02, 'sgu_ln_w': 4.093802e-02, 'sgu_ln_b': 3.981471e-02, 'sgu_w_s': 5.469066e-02, 'sgu_b_s': 6.321653e-02, 'sgu_proj': 6.000988e-02, 'w_out': 7.264063e-02, 'norm_ffn_w': 9.215232e-02, 'w_ffn_in': 3.874340e-02, 'w_ffn_out': 6.324351e-02, 'final_norm_w': 3.194329e+01}


def _to_microbatches(a, axis):
    t = _jnp.moveaxis(a, axis, 0)
    t = t.reshape((N_MICROBATCH, t.shape[0] // N_MICROBATCH) + t.shape[1:])
    return _jnp.moveaxis(t, 1, axis + 1)


def setup_inputs(seed: int = 0) -> dict:
    inp = _fwd_setup_inputs(seed)
    key = _jax.random.fold_in(_jax.random.key(seed), 7919)
    shape, _ = _output_shape()
    out = dict(inp)
    out["loss_target"] = _jax.random.normal(_jax.random.fold_in(key, 0), shape, _jnp.float32)
    for i, name in enumerate(TWIN_WEIGHTS):
        w = inp[name].astype(_jnp.float32)
        if MOMENT_SCALE is None:
            s = _jnp.sqrt(_jnp.mean(_jnp.square(w)) + 1e-30)
        else:
            s = MOMENT_SCALE[name]
        km, kv = _jax.random.split(_jax.random.fold_in(key, i + 1))
        out[name] = w
        out["m_" + name] = s * _jax.random.normal(km, w.shape, _jnp.float32)
        out["v_" + name] = (s * s) * _jax.random.uniform(kv, w.shape, _jnp.float32, 0.5, 1.5)
    if N_MICROBATCH > 1:
        for name, axis in PER_EXAMPLE_BATCH_AXIS.items():
            out[name] = _to_microbatches(out[name], axis)
    return {'x': out['x'], 'norm_mix_w': out['norm_mix_w'], 'w_in': out['w_in'], 'ret_gn_w': out['ret_gn_w'], 'ret_proj': out['ret_proj'], 'sgu_ln_w': out['sgu_ln_w'], 'sgu_ln_b': out['sgu_ln_b'], 'sgu_w_s': out['sgu_w_s'], 'sgu_b_s': out['sgu_b_s'], 'sgu_proj': out['sgu_proj'], 'w_out': out['w_out'], 'norm_ffn_w': out['norm_ffn_w'], 'w_ffn_in': out['w_ffn_in'], 'w_ffn_out': out['w_ffn_out'], 'final_norm_w': out['final_norm_w'], 'loss_target': out['loss_target'], 'm_norm_mix_w': out['m_norm_mix_w'], 'm_w_in': out['m_w_in'], 'm_ret_gn_w': out['m_ret_gn_w'], 'm_ret_proj': out['m_ret_proj'], 'm_sgu_ln_w': out['m_sgu_ln_w'], 'm_sgu_ln_b': out['m_sgu_ln_b'], 'm_sgu_w_s': out['m_sgu_w_s'], 'm_sgu_b_s': out['m_sgu_b_s'], 'm_sgu_proj': out['m_sgu_proj'], 'm_w_out': out['m_w_out'], 'm_norm_ffn_w': out['m_norm_ffn_w'], 'm_w_ffn_in': out['m_w_ffn_in'], 'm_w_ffn_out': out['m_w_ffn_out'], 'm_final_norm_w': out['m_final_norm_w'], 'v_norm_mix_w': out['v_norm_mix_w'], 'v_w_in': out['v_w_in'], 'v_ret_gn_w': out['v_ret_gn_w'], 'v_ret_proj': out['v_ret_proj'], 'v_sgu_ln_w': out['v_sgu_ln_w'], 'v_sgu_ln_b': out['v_sgu_ln_b'], 'v_sgu_w_s': out['v_sgu_w_s'], 'v_sgu_b_s': out['v_sgu_b_s'], 'v_sgu_proj': out['v_sgu_proj'], 'v_w_out': out['v_w_out'], 'v_norm_ffn_w': out['v_norm_ffn_w'], 'v_w_ffn_in': out['v_w_ffn_in'], 'v_w_ffn_out': out['v_w_ffn_out'], 'v_final_norm_w': out['v_final_norm_w']}


def _loss(weights, diff, rest, loss_target):
    with _jax.named_scope("forward"):
        args = {**rest, TWIN_DIFF_INPUT: diff, **{k: w.astype(_WEIGHT_DTYPES[k]) for k, w in weights.items()}}
        y = _forward(args)
    with _jax.named_scope("loss_head"):
        err = _jnp.square(y.astype(_jnp.float32) - loss_target)
        return 0.5 * _jnp.sum(_jnp.mean(err, axis=-1)) if err.ndim else 0.5 * err


def _adamw(w, g, m, v):
    m = ADAM_B1 * m + (1.0 - ADAM_B1) * g
    v = ADAM_B2 * v + (1.0 - ADAM_B2) * _jnp.square(g)
    m_hat = m / (1.0 - ADAM_B1 ** ADAM_STEP)
    v_hat = v / (1.0 - ADAM_B2 ** ADAM_STEP)
    delta = -ADAM_LR * (m_hat / (_jnp.sqrt(v_hat) + ADAM_EPS) + ADAM_WD * w)
    return delta, m, v


def reference(x, norm_mix_w, w_in, ret_gn_w, ret_proj, sgu_ln_w, sgu_ln_b, sgu_w_s, sgu_b_s, sgu_proj, w_out, norm_ffn_w, w_ffn_in, w_ffn_out, final_norm_w, loss_target, m_norm_mix_w, m_w_in, m_ret_gn_w, m_ret_proj, m_sgu_ln_w, m_sgu_ln_b, m_sgu_w_s, m_sgu_b_s, m_sgu_proj, m_w_out, m_norm_ffn_w, m_w_ffn_in, m_w_ffn_out, m_final_norm_w, v_norm_mix_w, v_w_in, v_ret_gn_w, v_ret_proj, v_sgu_ln_w, v_sgu_ln_b, v_sgu_w_s, v_sgu_b_s, v_sgu_proj, v_w_out, v_norm_ffn_w, v_w_ffn_in, v_w_ffn_out, v_final_norm_w):
    given = dict(x=x, norm_mix_w=norm_mix_w, w_in=w_in, ret_gn_w=ret_gn_w, ret_proj=ret_proj, sgu_ln_w=sgu_ln_w, sgu_ln_b=sgu_ln_b, sgu_w_s=sgu_w_s, sgu_b_s=sgu_b_s, sgu_proj=sgu_proj, w_out=w_out, norm_ffn_w=norm_ffn_w, w_ffn_in=w_ffn_in, w_ffn_out=w_ffn_out, final_norm_w=final_norm_w, loss_target=loss_target, m_norm_mix_w=m_norm_mix_w, m_w_in=m_w_in, m_ret_gn_w=m_ret_gn_w, m_ret_proj=m_ret_proj, m_sgu_ln_w=m_sgu_ln_w, m_sgu_ln_b=m_sgu_ln_b, m_sgu_w_s=m_sgu_w_s, m_sgu_b_s=m_sgu_b_s, m_sgu_proj=m_sgu_proj, m_w_out=m_w_out, m_norm_ffn_w=m_norm_ffn_w, m_w_ffn_in=m_w_ffn_in, m_w_ffn_out=m_w_ffn_out, m_final_norm_w=m_final_norm_w, v_norm_mix_w=v_norm_mix_w, v_w_in=v_w_in, v_ret_gn_w=v_ret_gn_w, v_ret_proj=v_ret_proj, v_sgu_ln_w=v_sgu_ln_w, v_sgu_ln_b=v_sgu_ln_b, v_sgu_w_s=v_sgu_w_s, v_sgu_b_s=v_sgu_b_s, v_sgu_proj=v_sgu_proj, v_w_out=v_w_out, v_norm_ffn_w=v_norm_ffn_w, v_w_ffn_in=v_w_ffn_in, v_w_ffn_out=v_w_ffn_out, v_final_norm_w=v_final_norm_w)
    weights = {n: given[n] for n in TWIN_WEIGHTS}
    shared = {n: given[n] for n in SHARED_INPUTS}
    per_example = {n: given[n] for n in ['x']}
    grad_fn = _jax.value_and_grad(_loss, argnums=(0, 1))

    def one_microbatch(ex, loss_target):
        ex = dict(ex)
        diff = ex.pop(TWIN_DIFF_INPUT)
        return grad_fn(weights, diff, {**shared, **ex}, loss_target)

    if N_MICROBATCH == 1:
        loss, (grad_w, grad_x) = one_microbatch(per_example, given["loss_target"])
    else:
        def body(carry, xs):
            loss_sum, grad_sum = carry
            l_k, (gw_k, gx_k) = one_microbatch(xs[0], xs[1])
            with _jax.named_scope("update"):
                return (loss_sum + l_k, _jax.tree.map(_jnp.add, grad_sum, gw_k)), gx_k

        init = (_jnp.zeros((), _jnp.float32), _jax.tree.map(_jnp.zeros_like, weights))
        (loss, grad_w), grad_x = _jax.lax.scan(body, init, (per_example, given["loss_target"]))
    with _jax.named_scope("update"):
        delta_w, new_m, new_v = {}, {}, {}
        for n in TWIN_WEIGHTS:
            delta_w[n], new_m[n], new_v[n] = _adamw(weights[n], grad_w[n], given["m_" + n], given["v_" + n])
    return (loss, grad_x, *[grad_w[n] for n in TWIN_WEIGHTS], *[delta_w[n] for n in TWIN_WEIGHTS],
            *[new_m[n] for n in TWIN_WEIGHTS], *[new_v[n] for n in TWIN_WEIGHTS])
```

```python
import math

import jax
import jax.numpy as jnp
from jax import lax
from jax.experimental import pallas as pl
from jax.experimental.pallas import tpu as pltpu

F32 = jnp.float32
BF16 = jnp.bfloat16

CHUNK = 64
RET_DK = 128
RET_DV = 256
SGU_LEN = 128
ROPE_BASE = 10000.0
EPS = 1e-6
ADAM_LR = 0.001
ADAM_B1 = 0.9
ADAM_B2 = 0.999
ADAM_EPS = 1e-08
ADAM_WD = 0.01
ADAM_STEP = 10

LANES = 128
VMEM_LIMIT = 56 * 1024 * 1024
RET_BLOCK = 256
SGU_BLOCK = 256
ROW_BLOCK = 256
MM_TILE = 1024
MM_KTILE = 2048
MESH = pl.DeviceIdType.MESH
ANY = pl.BlockSpec(memory_space=pl.ANY)
INV_SQRT2 = 1.0 / math.sqrt(2.0)
INV_SQRT_2PI = 1.0 / math.sqrt(2.0 * math.pi)

DN = {"nn": (((1,), (0,)), ((), ())), "nt": (((1,), (1,)), ((), ())), "tn": (((0,), (0,)), ((), ()))}


def _dot(a, b, mode="nn"):
    return lax.dot_general(a, b, DN[mode], preferred_element_type=F32)


def _tile(n, target):
    t = min(n, target) // LANES * LANES
    while t >= LANES:
        if n % t == 0:
            return t
        t -= LANES
    return n


def _rtile(n, target):
    t = min(n, target) // 16 * 16
    while t >= 16:
        if n % t == 0:
            return t
        t -= 16
    return n


def _params(sem):
    return pltpu.CompilerParams(dimension_semantics=sem, vmem_limit_bytes=VMEM_LIMIT)


def _sigmoid(x):
    return 1.0 / (1.0 + jnp.exp(-x))


def _gelu(x):
    return 0.5 * x * (1.0 + lax.erf(x * INV_SQRT2))


def _gelu_grad(x):
    return 0.5 * (1.0 + lax.erf(x * INV_SQRT2)) + x * jnp.exp(-0.5 * x * x) * INV_SQRT_2PI


def _matmul(a, b, mode, out_dtype, name, res=None):
    if mode == "nn":
        (m, k), n = a.shape, b.shape[1]
    elif mode == "nt":
        (m, k), n = a.shape, b.shape[0]
    else:
        (k, m), n = a.shape, b.shape[1]
    tm, tn, tk = _tile(m, MM_TILE), _tile(n, MM_TILE), _tile(k, MM_KTILE)
    nk = k // tk
    if mode == "tn":
        a_spec = pl.BlockSpec((tk, tm), lambda i, j, kk: (kk, i))
    else:
        a_spec = pl.BlockSpec((tm, tk), lambda i, j, kk: (i, kk))
    if mode == "nt":
        b_spec = pl.BlockSpec((tn, tk), lambda i, j, kk: (j, kk))
    else:
        b_spec = pl.BlockSpec((tk, tn), lambda i, j, kk: (kk, j))
    o_spec = pl.BlockSpec((tm, tn), lambda i, j, kk: (i, j))
    in_specs = [a_spec, b_spec] + ([o_spec] if res is not None else [])
    acc_in_out = out_dtype == F32
    scratch = [] if (nk == 1 or acc_in_out) else [pltpu.VMEM((tm, tn), F32)]

    def body(*refs):
        a_ref, b_ref = refs[0], refs[1]
        r_ref = refs[2] if res is not None else None
        o_ref = refs[3] if res is not None else refs[2]
        p = _dot(a_ref[...], b_ref[...], mode)
        if nk == 1:
            if r_ref is not None:
                p = p + r_ref[...]
            o_ref[...] = p.astype(o_ref.dtype)
            return
        acc = o_ref if acc_in_out else refs[-1]
        kk = pl.program_id(2)

        @pl.when(kk == 0)
        def _():
            acc[...] = p if r_ref is None or not acc_in_out else p + r_ref[...]

        @pl.when(kk > 0)
        def _():
            acc[...] += p

        if not acc_in_out:
            @pl.when(kk == nk - 1)
            def _():
                o = acc[...]
                if r_ref is not None:
                    o = o + r_ref[...]
                o_ref[...] = o.astype(o_ref.dtype)

    args = (a, b) + ((res,) if res is not None else ())
    return pl.pallas_call(
        body, name=name, grid=(m // tm, n // tn, nk), in_specs=in_specs, out_specs=o_spec,
        out_shape=jax.ShapeDtypeStruct((m, n), out_dtype), scratch_shapes=scratch,
        compiler_params=_params(("parallel", "parallel", "arbitrary")))(*args)


def _rms_fwd(x, w, name):
    s, d = x.shape
    tr = _tile(s, ROW_BLOCK)

    def body(x_ref, w_ref, o_ref):
        xv = x_ref[...]
        r = lax.rsqrt(jnp.mean(xv * xv, axis=-1, keepdims=True) + EPS)
        o_ref[...] = (xv * r * w_ref[...]).astype(BF16)

    row = pl.BlockSpec((tr, d), lambda i: (i, 0))
    vec = pl.BlockSpec((1, d), lambda i: (0, 0))
    return pl.pallas_call(body, name=name, grid=(s // tr,), in_specs=[row, vec], out_specs=row,
                          out_shape=jax.ShapeDtypeStruct((s, d), BF16),
                          compiler_params=_params(("parallel",)))(x, w.reshape(1, d))


def _rms_bwd(x, w, dh, dres, name):
    s, d = x.shape
    tr = _tile(s, ROW_BLOCK)

    def body(x_ref, w_ref, dh_ref, dr_ref, dx_ref, dxb_ref, dw_ref):
        xv = x_ref[...]
        r = lax.rsqrt(jnp.mean(xv * xv, axis=-1, keepdims=True) + EPS)
        xh = xv * r
        dy = dh_ref[...].astype(F32)
        dxh = dy * w_ref[...]
        dx = dr_ref[...] + r * (dxh - xh * jnp.mean(dxh * xh, axis=-1, keepdims=True))
        dx_ref[...] = dx
        dxb_ref[...] = dx.astype(BF16)

        @pl.when(pl.program_id(0) == 0)
        def _():
            dw_ref[...] = jnp.zeros_like(dw_ref)

        dw_ref[...] += jnp.sum(dy * xh, axis=0, keepdims=True)

    row = pl.BlockSpec((tr, d), lambda i: (i, 0))
    vec = pl.BlockSpec((1, d), lambda i: (0, 0))
    return pl.pallas_call(
        body, name=name, grid=(s // tr,), in_specs=[row, vec, row, row], out_specs=[row, row, vec],
        out_shape=[jax.ShapeDtypeStruct((s, d), F32), jax.ShapeDtypeStruct((s, d), BF16),
                   jax.ShapeDtypeStruct((1, d), F32)],
        compiler_params=_params(("arbitrary",)))(x, w.reshape(1, d), dh, dres)


def _loss_head(x, w, tgt):
    s, d = x.shape
    tr = _tile(s, ROW_BLOCK)

    def body(x_ref, w_ref, t_ref, dx_ref, dxb_ref, dw_ref, l_ref):
        xv = x_ref[...]
        r = lax.rsqrt(jnp.mean(xv * xv, axis=-1, keepdims=True) + EPS)
        xh = xv * r
        e = xh * w_ref[...] - t_ref[...]
        dy = e * (1.0 / d)
        dxh = dy * w_ref[...]
        dx = r * (dxh - xh * jnp.mean(dxh * xh, axis=-1, keepdims=True))
        dx_ref[...] = dx
        dxb_ref[...] = dx.astype(BF16)

        @pl.when(pl.program_id(0) == 0)
        def _():
            dw_ref[...] = jnp.zeros_like(dw_ref)
            l_ref[...] = jnp.zeros_like(l_ref)

        dw_ref[...] += jnp.sum(dy * xh, axis=0, keepdims=True)
        l_ref[...] += jnp.sum(jnp.sum(e * e, axis=1, keepdims=True), axis=0, keepdims=True)

    row = pl.BlockSpec((tr, d), lambda i: (i, 0))
    vec = pl.BlockSpec((1, d), lambda i: (0, 0))
    one = pl.BlockSpec((1, 1), lambda i: (0, 0))
    return pl.pallas_call(
        body, name="loss_head", grid=(s // tr,), in_specs=[row, vec, row], out_specs=[row, row, vec, one],
        out_shape=[jax.ShapeDtypeStruct((s, d), F32), jax.ShapeDtypeStruct((s, d), BF16),
                   jax.ShapeDtypeStruct((1, d), F32), jax.ShapeDtypeStruct((1, 1), F32)],
        compiler_params=_params(("arbitrary",)))(x, w.reshape(1, d), tgt)


def _ret_tables(s, h, t):
    half = RET_DK // 2
    inv = ROPE_BASE ** (-jnp.arange(half, dtype=F32) / half)
    ang = jnp.arange(s, dtype=F32)[:, None] * inv[None, :]
    cos, sin = jnp.cos(ang), jnp.sin(ang)
    cosf = jnp.concatenate([cos, cos], axis=1)
    sinf = jnp.concatenate([-sin, sin], axis=1)
    log_g = jnp.log1p(-(2.0 ** (-5.0 - jnp.arange(h, dtype=F32))))
    idx = jnp.arange(t, dtype=F32)
    chunk = jnp.arange(t) // CHUNK
    allowed = chunk[None, :] <= chunk[:, None]
    dm = jnp.where(allowed[None], jnp.exp(log_g[:, None, None] * jnp.abs(idx[:, None] - idx[None, :])), 0.0)
    qd = jnp.exp(log_g[:, None] * (idx[None, :] + 1.0))
    kd = jnp.exp(log_g[:, None] * (t - 1.0 - idx[None, :]))
    qd = jnp.broadcast_to(qd[:, :, None], (h, t, RET_DK))
    kd = jnp.broadcast_to(kd[:, :, None], (h, t, RET_DK))
    cd = jnp.broadcast_to(jnp.exp(log_g * t)[:, None, None], (h, 1, RET_DV))
    return cosf, sinf, dm, qd, kd, cd


def _rot(x, cos, sin):
    return x * cos + pltpu.roll(x, RET_DK // 2, 1) * sin


def _rot_t(x, cos, sin):
    return x * cos - pltpu.roll(x, RET_DK // 2, 1) * sin


def _ret_in_specs(h, t, rev_nb=None):
    blk = (lambda b: b) if rev_nb is None else (lambda b: rev_nb - 1 - b)
    return [
        pl.BlockSpec((t, RET_DK), lambda hh, b: (blk(b), hh)),
        pl.BlockSpec((t, RET_DK), lambda hh, b: (blk(b), h + hh)),
        pl.BlockSpec((t, RET_DV), lambda hh, b: (blk(b), h + hh)),
        pl.BlockSpec((t, RET_DV), lambda hh, b: (blk(b), 2 * h + hh)),
        pl.BlockSpec((t, RET_DK), lambda hh, b: (blk(b), 0)),
        pl.BlockSpec((t, RET_DK), lambda hh, b: (blk(b), 0)),
        pl.BlockSpec((1, t, t), lambda hh, b: (hh, 0, 0)),
        pl.BlockSpec((1, t, RET_DK), lambda hh, b: (hh, 0, 0)),
        pl.BlockSpec((1, t, RET_DK), lambda hh, b: (hh, 0, 0)),
        pl.BlockSpec((1, 1, RET_DV), lambda hh, b: (hh, 0, 0)),
        pl.BlockSpec((1, RET_DV), lambda hh, b: (0, hh)),
    ]


def _ret_fwd(z, gn_w, tables, h, name):
    s = z.shape[0]
    t = _tile(s, RET_BLOCK)
    nb = s // t
    scale = RET_DK ** -0.5

    def body(q_ref, k_ref, v_ref, g_ref, cos_ref, sin_ref, dm_ref, qd_ref, kd_ref, cd_ref, gn_ref,
             o_ref, st_ref, st_scr):
        @pl.when(pl.program_id(1) == 0)
        def _():
            st_scr[...] = jnp.zeros_like(st_scr)

        cos, sin = cos_ref[...], sin_ref[...]
        qf = _rot(q_ref[...].astype(F32), cos, sin) * scale
        kf = _rot(k_ref[...].astype(F32), cos, sin)
        vb = v_ref[...]
        p = _dot(qf.astype(BF16), kf.astype(BF16), "nt") * dm_ref[0]
        st = st_scr[...]
        stb = st.astype(BF16)
        st_ref[0, 0] = stb
        o = _dot(p.astype(BF16), vb) + _dot((qf * qd_ref[0]).astype(BF16), stb)
        st_scr[...] = st * cd_ref[0] + _dot((kf * kd_ref[0]).astype(BF16), vb, "tn")
        dlt = o - jnp.mean(o, axis=-1, keepdims=True)
        oh = dlt * lax.rsqrt(jnp.mean(dlt * dlt, axis=-1, keepdims=True) + EPS)
        g = g_ref[...].astype(F32)
        o_ref[...] = (g * _sigmoid(g) * oh * gn_ref[...]).astype(BF16)

    return pl.pallas_call(
        body, name=name, grid=(h, nb), in_specs=_ret_in_specs(h, t),
        out_specs=[pl.BlockSpec((t, RET_DV), lambda hh, b: (b, hh)),
                   pl.BlockSpec((1, 1, RET_DK, RET_DV), lambda hh, b: (b, hh, 0, 0))],
        out_shape=[jax.ShapeDtypeStruct((s, h * RET_DV), BF16),
                   jax.ShapeDtypeStruct((nb, h, RET_DK, RET_DV), BF16)],
        scratch_shapes=[pltpu.VMEM((RET_DK, RET_DV), F32)],
        compiler_params=_params(("parallel", "arbitrary")))(z, z, z, z, *tables, gn_w.reshape(1, -1))


def _ret_bwd(z, dga, states, gn_w, tables, h, name):
    s = z.shape[0]
    t = _tile(s, RET_BLOCK)
    nb = s // t
    scale = RET_DK ** -0.5

    def body(q_ref, k_ref, v_ref, g_ref, cos_ref, sin_ref, dm_ref, qd_ref, kd_ref, cd_ref, gn_ref,
             dga_ref, st_ref, dq_ref, dk_ref, dv_ref, dg_ref, dgn_ref, dst_scr):
        @pl.when(pl.program_id(1) == 0)
        def _():
            dst_scr[...] = jnp.zeros_like(dst_scr)
            dgn_ref[...] = jnp.zeros_like(dgn_ref)

        cos, sin = cos_ref[...], sin_ref[...]
        dm = dm_ref[0]
        qf = _rot(q_ref[...].astype(F32), cos, sin) * scale
        kf = _rot(k_ref[...].astype(F32), cos, sin)
        qb, kb, vb = qf.astype(BF16), kf.astype(BF16), v_ref[...]
        qdb = (qf * qd_ref[0]).astype(BF16)
        kdb = (kf * kd_ref[0]).astype(BF16)
        stb = st_ref[0, 0]
        pb = (_dot(qb, kb, "nt") * dm).astype(BF16)
        o = _dot(pb, vb) + _dot(qdb, stb)
        dlt = o - jnp.mean(o, axis=-1, keepdims=True)
        rstd = lax.rsqrt(jnp.mean(dlt * dlt, axis=-1, keepdims=True) + EPS)
        oh = dlt * rstd
        gn = gn_ref[...]
        g = g_ref[...].astype(F32)
        sg = _sigmoid(g)
        dga_v = dga_ref[...].astype(F32)
        dret = dga_v * g * sg
        dg_ref[...] = (dga_v * oh * gn * sg * (1.0 + g * (1.0 - sg))).astype(BF16)
        dgn_ref[...] += jnp.sum(dret * oh, axis=0, keepdims=True)
        doh = dret * gn
        do = rstd * (doh - jnp.mean(doh, axis=-1, keepdims=True)
                     - oh * jnp.mean(doh * oh, axis=-1, keepdims=True))
        dob = do.astype(BF16)
        dst = dst_scr[...]
        dstb = dst.astype(BF16)
        dv_ref[...] = (_dot(pb, dob, "tn") + _dot(kdb, dstb)).astype(BF16)
        dpb = (_dot(dob, vb, "nt") * dm).astype(BF16)
        dqf = _dot(dpb, kb) + _dot(dob, stb, "nt") * qd_ref[0]
        dkf = _dot(dpb, qb, "tn") + _dot(vb, dstb, "nt") * kd_ref[0]
        dst_scr[...] = dst * cd_ref[0] + _dot(qdb, dob, "tn")
        dq_ref[...] = _rot_t(dqf * scale, cos, sin).astype(BF16)
        dk_ref[...] = _rot_t(dkf, cos, sin).astype(BF16)

    rb = lambda hh, b: (nb - 1 - b, hh)
    in_specs = _ret_in_specs(h, t, rev_nb=nb) + [
        pl.BlockSpec((t, RET_DV), rb),
        pl.BlockSpec((1, 1, RET_DK, RET_DV), lambda hh, b: (nb - 1 - b, hh, 0, 0))]
    return pl.pallas_call(
        body, name=name, grid=(h, nb), in_specs=in_specs,
        out_specs=[pl.BlockSpec((t, RET_DK), rb), pl.BlockSpec((t, RET_DK), rb),
                   pl.BlockSpec((t, RET_DV), rb), pl.BlockSpec((t, RET_DV), rb),
                   pl.BlockSpec((1, RET_DV), lambda hh, b: (0, hh))],
        out_shape=[jax.ShapeDtypeStruct((s, h * RET_DK), BF16), jax.ShapeDtypeStruct((s, h * RET_DK), BF16),
                   jax.ShapeDtypeStruct((s, h * RET_DV), BF16), jax.ShapeDtypeStruct((s, h * RET_DV), BF16),
                   jax.ShapeDtypeStruct((1, h * RET_DV), F32)],
        scratch_shapes=[pltpu.VMEM((RET_DK, RET_DV), F32)],
        compiler_params=_params(("parallel", "arbitrary")))(z, z, z, z, *tables, gn_w.reshape(1, -1), dga, states)


def _sgu_fwd(z, ln_w, ln_b, ws_m, bs, col0, w, name):
    s = z.shape[0]
    t = _tile(s, SGU_BLOCK)
    groups = ws_m.shape[0]
    ch = w // groups
    cb = col0 // w

    def body(su_ref, sv_ref, lw_ref, lb_ref, ws_ref, bs_ref, o_ref):
        zv = _gelu(sv_ref[...].astype(F32))
        dlt = zv - jnp.mean(zv, axis=-1, keepdims=True)
        vn = dlt * lax.rsqrt(jnp.mean(dlt * dlt, axis=-1, keepdims=True) + EPS) * lw_ref[...] + lb_ref[...]
        vnb = vn.astype(BF16)
        for r in range(t // SGU_LEN):
            rows = slice(r * SGU_LEN, (r + 1) * SGU_LEN)
            for gi in range(groups):
                cols = slice(gi * ch, (gi + 1) * ch)
                mixed = _dot(ws_ref[gi], vnb[rows, cols]) + bs_ref[gi]
                o_ref[rows, cols] = (_gelu(su_ref[rows, cols].astype(F32)) * mixed).astype(BF16)

    row = lambda off: pl.BlockSpec((t, w), lambda i: (i, cb + off))
    vec = pl.BlockSpec((1, w), lambda i: (0, 0))
    return pl.pallas_call(
        body, name=name, grid=(s // t,),
        in_specs=[row(0), row(1), vec, vec,
                  pl.BlockSpec((groups, SGU_LEN, SGU_LEN), lambda i: (0, 0, 0)),
                  pl.BlockSpec((groups, SGU_LEN, 1), lambda i: (0, 0, 0))],
        out_specs=pl.BlockSpec((t, w), lambda i: (i, 0)),
        out_shape=jax.ShapeDtypeStruct((s, w), BF16),
        compiler_params=_params(("parallel",)))(z, z, ln_w.reshape(1, w), ln_b.reshape(1, w), ws_m, bs)


def _sgu_bwd(z, dsg, ln_w, ln_b, ws_m, ws_mt, bs, col0, w, name):
    s = z.shape[0]
    t = _tile(s, SGU_BLOCK)
    groups = ws_m.shape[0]
    ch = w // groups
    cb = col0 // w

    def body(su_ref, sv_ref, dsg_ref, lw_ref, lb_ref, ws_ref, wst_ref, bs_ref,
             dsu_ref, dsv_ref, dlw_ref, dlb_ref, dws_ref, dbs_ref, dvn_scr):
        @pl.when(pl.program_id(0) == 0)
        def _():
            dlw_ref[...] = jnp.zeros_like(dlw_ref)
            dlb_ref[...] = jnp.zeros_like(dlb_ref)
            dws_ref[...] = jnp.zeros_like(dws_ref)
            dbs_ref[...] = jnp.zeros_like(dbs_ref)

        sv = sv_ref[...].astype(F32)
        zv = _gelu(sv)
        dlt = zv - jnp.mean(zv, axis=-1, keepdims=True)
        rstd = lax.rsqrt(jnp.mean(dlt * dlt, axis=-1, keepdims=True) + EPS)
        vh = dlt * rstd
        vnb = (vh * lw_ref[...] + lb_ref[...]).astype(BF16)
        for r in range(t // SGU_LEN):
            rows = slice(r * SGU_LEN, (r + 1) * SGU_LEN)
            for gi in range(groups):
                cols = slice(gi * ch, (gi + 1) * ch)
                vn_p = vnb[rows, cols]
                mixed = _dot(ws_ref[gi], vn_p) + bs_ref[gi]
                su = su_ref[rows, cols].astype(F32)
                dsg_p = dsg_ref[rows, cols].astype(F32)
                dsu_ref[rows, cols] = (dsg_p * mixed * _gelu_grad(su)).astype(BF16)
                dmix = dsg_p * _gelu(su)
                dmixb = dmix.astype(BF16)
                dvn_scr[rows, cols] = _dot(wst_ref[gi], dmixb)
                dws_ref[gi] += _dot(dmixb, vn_p, "nt")
                dbs_ref[gi] += jnp.sum(dmix, axis=1, keepdims=True)
        dvn = dvn_scr[...]
        dlw_ref[...] += jnp.sum(dvn * vh, axis=0, keepdims=True)
        dlb_ref[...] += jnp.sum(dvn, axis=0, keepdims=True)
        dvh = dvn * lw_ref[...]
        dzv = rstd * (dvh - jnp.mean(dvh, axis=-1, keepdims=True)
                      - vh * jnp.mean(dvh * vh, axis=-1, keepdims=True))
        dsv_ref[...] = (dzv * _gelu_grad(sv)).astype(BF16)

    row = lambda off: pl.BlockSpec((t, w), lambda i: (i, cb + off))
    out_row = pl.BlockSpec((t, w), lambda i: (i, 0))
    vec = pl.BlockSpec((1, w), lambda i: (0, 0))
    mat = pl.BlockSpec((groups, SGU_LEN, SGU_LEN), lambda i: (0, 0, 0))
    col = pl.BlockSpec((groups, SGU_LEN, 1), lambda i: (0, 0, 0))
    return pl.pallas_call(
        body, name=name, grid=(s // t,),
        in_specs=[row(0), row(1), out_row, vec, vec, mat, mat, col],
        out_specs=[out_row, out_row, vec, vec, mat, col],
        out_shape=[jax.ShapeDtypeStruct((s, w), BF16), jax.ShapeDtypeStruct((s, w), BF16),
                   jax.ShapeDtypeStruct((1, w), F32), jax.ShapeDtypeStruct((1, w), F32),
                   jax.ShapeDtypeStruct((groups, SGU_LEN, SGU_LEN), F32),
                   jax.ShapeDtypeStruct((groups, SGU_LEN, 1), F32)],
        scratch_shapes=[pltpu.VMEM((t, w), F32)],
        compiler_params=_params(("arbitrary",)))(z, z, dsg, ln_w.reshape(1, w), ln_b.reshape(1, w), ws_m, ws_mt, bs)


def _merge_fwd(a, b, z, col0, name):
    s, d = a.shape
    tr = _tile(s, ROW_BLOCK)
    cb = col0 // d

    def body(a_ref, b_ref, ga_ref, gb_ref, o_ref):
        o_ref[...] = (_sigmoid(ga_ref[...].astype(F32)) * a_ref[...].astype(F32)
                      + _sigmoid(gb_ref[...].astype(F32)) * b_ref[...].astype(F32)).astype(BF16)

    row = pl.BlockSpec((tr, d), lambda i: (i, 0))
    gate = lambda off: pl.BlockSpec((tr, d), lambda i: (i, cb + off))
    return pl.pallas_call(body, name=name, grid=(s // tr,), in_specs=[row, row, gate(0), gate(1)],
                          out_specs=row, out_shape=jax.ShapeDtypeStruct((s, d), BF16),
                          compiler_params=_params(("parallel",)))(a, b, z, z)


def _merge_bwd(dmg, a, b, z, col0, name):
    s, d = a.shape
    tr = _tile(s, ROW_BLOCK)
    cb = col0 // d

    def body(dm_ref, a_ref, b_ref, ga_ref, gb_ref, da_ref, db_ref, dgt_ref):
        dm = dm_ref[...].astype(F32)
        sa = _sigmoid(ga_ref[...].astype(F32))
        sb = _sigmoid(gb_ref[...].astype(F32))
        da_ref[...] = (dm * sa).astype(BF16)
        db_ref[...] = (dm * sb).astype(BF16)
        dgt_ref[:, :d] = (dm * a_ref[...].astype(F32) * sa * (1.0 - sa)).astype(BF16)
        dgt_ref[:, d:] = (dm * b_ref[...].astype(F32) * sb * (1.0 - sb)).astype(BF16)

    row = pl.BlockSpec((tr, d), lambda i: (i, 0))
    wide = pl.BlockSpec((tr, 2 * d), lambda i: (i, 0))
    gate = lambda off: pl.BlockSpec((tr, d), lambda i: (i, cb + off))
    return pl.pallas_call(
        body, name=name, grid=(s // tr,), in_specs=[row, row, row, gate(0), gate(1)],
        out_specs=[row, row, wide],
        out_shape=[jax.ShapeDtypeStruct((s, d), BF16), jax.ShapeDtypeStruct((s, d), BF16),
                   jax.ShapeDtypeStruct((s, 2 * d), BF16)],
        compiler_params=_params(("parallel",)))(dmg, a, b, z, z)


def _swiglu_fwd(ac, name):
    s, f2 = ac.shape
    f = f2 // 2
    tr = _tile(s, ROW_BLOCK)

    def body(a_ref, c_ref, o_ref):
        a = a_ref[...].astype(F32)
        o_ref[...] = (a * _sigmoid(a) * c_ref[...].astype(F32)).astype(BF16)

    half = lambda off: pl.BlockSpec((tr, f), lambda i: (i, off))
    return pl.pallas_call(body, name=name, grid=(s // tr,), in_specs=[half(0), half(1)], out_specs=half(0),
                          out_shape=jax.ShapeDtypeStruct((s, f), BF16),
                          compiler_params=_params(("parallel",)))(ac, ac)


def _swiglu_bwd(ac, df, name):
    s, f2 = ac.shape
    f = f2 // 2
    tr = _tile(s, ROW_BLOCK)

    def body(a_ref, c_ref, df_ref, o_ref):
        a = a_ref[...].astype(F32)
        sg = _sigmoid(a)
        dfv = df_ref[...].astype(F32)
        o_ref[:, :f] = (dfv * c_ref[...].astype(F32) * sg * (1.0 + a * (1.0 - sg))).astype(BF16)
        o_ref[:, f:] = (dfv * a * sg).astype(BF16)

    half = lambda off: pl.BlockSpec((tr, f), lambda i: (i, off))
    return pl.pallas_call(body, name=name, grid=(s // tr,), in_specs=[half(0), half(1), half(0)],
                          out_specs=pl.BlockSpec((tr, f2), lambda i: (i, 0)),
                          out_shape=jax.ShapeDtypeStruct((s, f2), BF16),
                          compiler_params=_params(("parallel",)))(ac, ac, df)


def _adamw(w, g, m, v, name):
    r, c = w.shape
    tr = _rtile(r, LANES)
    c1 = 1.0 - ADAM_B1 ** ADAM_STEP
    c2 = 1.0 - ADAM_B2 ** ADAM_STEP

    def body(w_ref, g_ref, m_ref, v_ref, d_ref, mo_ref, vo_ref):
        gv = g_ref[...]
        mn = ADAM_B1 * m_ref[...] + (1.0 - ADAM_B1) * gv
        vn = ADAM_B2 * v_ref[...] + (1.0 - ADAM_B2) * (gv * gv)
        mo_ref[...] = mn
        vo_ref[...] = vn
        d_ref[...] = -ADAM_LR * ((mn / c1) / (jnp.sqrt(vn / c2) + ADAM_EPS) + ADAM_WD * w_ref[...])

    blk = pl.BlockSpec((tr, c), lambda i: (i, 0))
    shp = jax.ShapeDtypeStruct((r, c), F32)
    return pl.pallas_call(body, name=name, grid=(r // tr,), in_specs=[blk] * 4, out_specs=[blk] * 3,
                          out_shape=[shp] * 3, compiler_params=_params(("parallel",)))(w, g, m, v)


def _place():
    x, y, c = lax.axis_index("x"), lax.axis_index("y"), lax.axis_index("c")
    chips = [(1 - x, y), (x, 1 - y), (1 - x, 1 - y)]
    return x, y, c, chips


def _block(ref, kind, chip, half, shard_shape):
    rs, cs = shard_shape
    if kind == "col":
        rows = pl.ds(0, rs) if half is None else pl.ds(half * (rs // 2), rs // 2)
        return ref.at[rows, pl.ds(chip * cs, cs)]
    rows = pl.ds(chip * rs, rs) if half is None else pl.ds(chip * rs + half * (rs // 2), rs // 2)
    return ref.at[rows, :]


def _half_rows(ref, half):
    rs = ref.shape[0]
    return ref.at[pl.ds(half * (rs // 2), rs // 2), :]


def _all_gather_weights(shards, kinds, name):
    n = len(shards)
    out_shapes = [jax.ShapeDtypeStruct((s.shape[0], 4 * s.shape[1]) if k == "col" else (4 * s.shape[0], s.shape[1]),
                                       s.dtype) for s, k in zip(shards, kinds)]

    def body(*refs):
        ins, outs = refs[:n], refs[n:2 * n]
        send_sems, recv_sems, fsend_sems, frecv_sems, local_sems = refs[2 * n:]
        x, y, c, chips = _place()
        me = 2 * x + y
        waits = []
        for i in range(n):
            shp = ins[i].shape
            own = pltpu.make_async_copy(ins[i], _block(outs[i], kinds[i], me, None, shp), local_sems.at[i])
            own.start()
            waits.append(own.wait)
            for j, (px, py) in enumerate(chips):
                cp = pltpu.make_async_remote_copy(
                    src_ref=_half_rows(ins[i], c), dst_ref=_block(outs[i], kinds[i], me, c, shp),
                    send_sem=send_sems.at[3 * i + j], recv_sem=recv_sems.at[3 * i + j],
                    device_id=(px, py, c), device_id_type=MESH)
                cp.start()
                waits.append(cp.wait_send)
        for i in range(n):
            shp = ins[i].shape
            for j, (px, py) in enumerate(chips):
                landed = _block(outs[i], kinds[i], 2 * px + py, c, shp)
                pltpu.make_async_remote_copy(
                    src_ref=landed, dst_ref=landed, send_sem=send_sems.at[3 * i + j],
                    recv_sem=recv_sems.at[3 * i + j], device_id=(px, py, c), device_id_type=MESH).wait_recv()
                fwd = pltpu.make_async_remote_copy(
                    src_ref=landed, dst_ref=landed, send_sem=fsend_sems.at[3 * i + j],
                    recv_sem=frecv_sems.at[3 * i + j], device_id=(x, y, 1 - c), device_id_type=MESH)
                fwd.start()
                waits.append(fwd.wait_send)
        for i in range(n):
            shp = ins[i].shape
            for j, (px, py) in enumerate(chips):
                passed = _block(outs[i], kinds[i], 2 * px + py, 1 - c, shp)
                pltpu.make_async_remote_copy(
                    src_ref=passed, dst_ref=passed, send_sem=fsend_sems.at[3 * i + j],
                    recv_sem=frecv_sems.at[3 * i + j], device_id=(x, y, 1 - c), device_id_type=MESH).wait_recv()
        for w in waits:
            w()

    return pl.pallas_call(
        body, name=name, in_specs=[ANY] * n, out_specs=[ANY] * n, out_shape=out_shapes,
        scratch_shapes=[pltpu.SemaphoreType.DMA((3 * n,))] * 4 + [pltpu.SemaphoreType.DMA((n,))],
        compiler_params=pltpu.CompilerParams(has_side_effects=True))(*shards)


def _exchange_core_halves(grads, kinds, shard_shapes, name):
    n = len(grads)
    out_shapes = [jax.ShapeDtypeStruct((4, rs // 2, cs), F32) for rs, cs in shard_shapes]

    def body(*refs):
        ins, outs = refs[:n], refs[n:2 * n]
        send_sems, recv_sems = refs[2 * n:]
        x, y, c, _ = _place()
        copies = []
        for i in range(n):
            for q in range(4):
                cp = pltpu.make_async_remote_copy(
                    src_ref=_block(ins[i], kinds[i], q, 1 - c, shard_shapes[i]), dst_ref=outs[i].at[q],
                    send_sem=send_sems.at[4 * i + q], recv_sem=recv_sems.at[4 * i + q],
                    device_id=(x, y, 1 - c), device_id_type=MESH)
                cp.start()
                copies.append(cp)
        for cp in copies:
            cp.wait()

    return pl.pallas_call(
        body, name=name, in_specs=[ANY] * n, out_specs=[ANY] * n, out_shape=out_shapes,
        scratch_shapes=[pltpu.SemaphoreType.DMA((4 * n,))] * 2,
        compiler_params=pltpu.CompilerParams(has_side_effects=True))(*grads)


def _chip_sum(grad, sib, kind, shard_shape, place, name):
    rs, cs = shard_shape
    hr = rs // 2
    tr = _rtile(hr, 256)
    nt = hr // tr

    def g_map(j, t, p):
        if kind == "col":
            return (p[0] * nt + t, p[1 + j])
        return ((p[1 + j] * 2 + p[0]) * nt + t, 0)

    def body(p_ref, g_ref, s_ref, o_ref):
        o_ref[0] = (g_ref[...] + s_ref[0]).astype(o_ref.dtype)

    def call(nj, first, dtype, nm):
        return pl.pallas_call(
            body, name=nm,
            grid_spec=pltpu.PrefetchScalarGridSpec(
                num_scalar_prefetch=1, grid=(nj, nt),
                in_specs=[pl.BlockSpec((tr, cs), lambda j, t, p: g_map(j + first, t, p)),
                          pl.BlockSpec((1, tr, cs), lambda j, t, p: (p[1 + j + first], t, 0))],
                out_specs=pl.BlockSpec((1, tr, cs), lambda j, t, p: (j, t, 0))),
            out_shape=jax.ShapeDtypeStruct((nj, hr, cs), dtype),
            compiler_params=_params(("arbitrary", "arbitrary")))(place, grad, sib)

    return call(3, 0, BF16, name + "_peers"), call(1, 3, F32, name + "_own")[0]


def _scatter_partials(parts, name):
    n = len(parts)

    def body(*refs):
        ins, outs = refs[:n], refs[n:2 * n]
        send_sems, recv_sems = refs[2 * n:]
        _, _, c, chips = _place()
        copies = []
        for i in range(n):
            for j, (px, py) in enumerate(chips):
                cp = pltpu.make_async_remote_copy(
                    src_ref=ins[i].at[j], dst_ref=outs[i].at[j], send_sem=send_sems.at[3 * i + j],
                    recv_sem=recv_sems.at[3 * i + j], device_id=(px, py, c), device_id_type=MESH)
                cp.start()
                copies.append(cp)
        for cp in copies:
            cp.wait()

    return pl.pallas_call(
        body, name=name, in_specs=[ANY] * n, out_specs=[ANY] * n,
        out_shape=[jax.ShapeDtypeStruct(p.shape, p.dtype) for p in parts],
        scratch_shapes=[pltpu.SemaphoreType.DMA((3 * n,))] * 2,
        compiler_params=pltpu.CompilerParams(has_side_effects=True))(*parts)


def _final_sum(own, recv, name):
    hr, cs = own.shape
    tr = _rtile(hr, 256)

    def body(o_ref, r_ref, out_ref):
        acc = o_ref[...]
        for j in range(3):
            acc = acc + r_ref[j].astype(F32)
        out_ref[...] = acc

    return pl.pallas_call(
        body, name=name, grid=(hr // tr,),
        in_specs=[pl.BlockSpec((tr, cs), lambda t: (t, 0)), pl.BlockSpec((3, tr, cs), lambda t: (0, t, 0))],
        out_specs=pl.BlockSpec((tr, cs), lambda t: (t, 0)), out_shape=jax.ShapeDtypeStruct((hr, cs), F32),
        compiler_params=_params(("parallel",)))(own, recv)


def _join_core_halves(halves, name):
    n = len(halves)

    def body(*refs):
        ins, outs = refs[:n], refs[n:2 * n]
        send_sems, recv_sems, local_sems = refs[2 * n:]
        x, y, c, _ = _place()
        copies = []
        for i in range(n):
            own = pltpu.make_async_copy(ins[i], _half_rows(outs[i], c), local_sems.at[i])
            own.start()
            cp = pltpu.make_async_remote_copy(
                src_ref=ins[i], dst_ref=_half_rows(outs[i], c), send_sem=send_sems.at[i],
                recv_sem=recv_sems.at[i], device_id=(x, y, 1 - c), device_id_type=MESH)
            cp.start()
            copies.append((own, cp))
        for i, (own, cp) in enumerate(copies):
            cp.wait_send()
            mine = _half_rows(outs[i], 1 - c)
            pltpu.make_async_remote_copy(
                src_ref=mine, dst_ref=mine, send_sem=send_sems.at[i], recv_sem=recv_sems.at[i],
                device_id=(x, y, 1 - c), device_id_type=MESH).wait_recv()
            own.wait()

    return pl.pallas_call(
        body, name=name, in_specs=[ANY] * n, out_specs=[ANY] * n,
        out_shape=[jax.ShapeDtypeStruct((2 * h.shape[0], h.shape[1]), F32) for h in halves],
        scratch_shapes=[pltpu.SemaphoreType.DMA((n,))] * 3,
        compiler_params=pltpu.CompilerParams(has_side_effects=True))(*halves)


def _all_reduce_small(v, name):
    rows = v.shape[0]

    def body(v_ref, o_ref, buf, send_sems, recv_sems):
        x, y, c, _ = _place()
        coord = lambda p: ((1 - x) if p & 4 else x, (1 - y) if p & 2 else y, (1 - c) if p & 1 else c)
        me = 4 * x + 2 * y + c
        buf[me] = v_ref[...]
        copies = []
        for p in range(1, 8):
            cp = pltpu.make_async_remote_copy(
                src_ref=v_ref, dst_ref=buf.at[me], send_sem=send_sems.at[p - 1], recv_sem=recv_sems.at[p - 1],
                device_id=coord(p), device_id_type=MESH)
            cp.start()
            copies.append(cp)
        for p in range(1, 8):
            px, py, pc = coord(p)
            pltpu.make_async_remote_copy(
                src_ref=v_ref, dst_ref=buf.at[4 * px + 2 * py + pc], send_sem=send_sems.at[p - 1],
                recv_sem=recv_sems.at[p - 1], device_id=coord(p), device_id_type=MESH).wait_recv()
        for cp in copies:
            cp.wait_send()
        acc = buf[0]
        for dev in range(1, 8):
            acc = acc + buf[dev]
        o_ref[...] = acc

    vm = pl.BlockSpec(memory_space=pltpu.VMEM)
    return pl.pallas_call(
        body, name=name, in_specs=[vm], out_specs=vm, out_shape=jax.ShapeDtypeStruct(v.shape, F32),
        scratch_shapes=[pltpu.VMEM((8, rows, LANES), F32), pltpu.SemaphoreType.DMA((7,)),
                        pltpu.SemaphoreType.DMA((7,))],
        compiler_params=pltpu.CompilerParams(vmem_limit_bytes=VMEM_LIMIT))(v)


def _reduce_scatter_grads(grads, kinds, shard_shapes, place, tag):
    sib = _exchange_core_halves(grads, kinds, shard_shapes, "rs_core_exchange_" + tag)
    peers, owns = [], []
    for i, g in enumerate(grads):
        p, o = _chip_sum(g, sib[i], kinds[i], shard_shapes[i], place, "rs_chip_sum%d_%s" % (i, tag))
        peers.append(p)
        owns.append(o)
    recv = _scatter_partials(peers, "rs_scatter_" + tag)
    halves = [_final_sum(o, r, "rs_final_sum%d_%s" % (i, tag)) for i, (o, r) in enumerate(zip(owns, recv))]
    return _join_core_halves(halves, "rs_core_join_" + tag)


BIG = ["w_in", "ret_proj", "sgu_proj", "w_out", "w_ffn_in", "w_ffn_out"]
BIG_KIND = {"w_in": "col", "ret_proj": "row", "sgu_proj": "row", "w_out": "row", "w_ffn_in": "col",
            "w_ffn_out": "row"}
SMALL = ["norm_mix_w", "ret_gn_w", "sgu_ln_w", "sgu_ln_b", "sgu_w_s", "sgu_b_s", "norm_ffn_w"]
ORDER = ["norm_mix_w", "w_in", "ret_gn_w", "ret_proj", "sgu_ln_w", "sgu_ln_b", "sgu_w_s", "sgu_b_s",
         "sgu_proj", "w_out", "norm_ffn_w", "w_ffn_in", "w_ffn_out", "final_norm_w"]


def _layer_fwd(x, wt, sm, tables, dims, tag):
    h, d, w = dims
    c_su, c_gate = 6 * h * RET_DK, 6 * h * RET_DK + 2 * w
    h1 = _rms_fwd(x, sm["norm_mix_w"], "rms_mix_fwd_" + tag)
    z = _matmul(h1, wt["w_in"], "nn", BF16, "mm_in_" + tag)
    ga, states = _ret_fwd(z, sm["ret_gn_w"], tables, h, "ret_fwd_" + tag)
    sg = _sgu_fwd(z, sm["sgu_ln_w"], sm["sgu_ln_b"], sm["ws_m"], sm["bs"], c_su, w, "sgu_fwd_" + tag)
    a = _matmul(ga, wt["ret_proj"], "nn", BF16, "mm_ret_proj_" + tag)
    b = _matmul(sg, wt["sgu_proj"], "nn", BF16, "mm_sgu_proj_" + tag)
    mg = _merge_fwd(a, b, z, c_gate, "merge_fwd_" + tag)
    x1 = _matmul(mg, wt["w_out"], "nn", F32, "mm_out_" + tag, res=x)
    h2 = _rms_fwd(x1, sm["norm_ffn_w"], "rms_ffn_fwd_" + tag)
    ac = _matmul(h2, wt["w_ffn_in"], "nn", BF16, "mm_ffn_in_" + tag)
    f = _swiglu_fwd(ac, "swiglu_fwd_" + tag)
    x2 = _matmul(f, wt["w_ffn_out"], "nn", F32, "mm_ffn_out_" + tag, res=x1)
    saved = dict(x=x, h1=h1, z=z, states=states, ga=ga, sg=sg, a=a, b=b, mg=mg, x1=x1, h2=h2, ac=ac, f=f)
    return x2, saved


def _layer_bwd(dx2, dx2b, sv, wt, sm, tables, dims, tag):
    h, d, w = dims
    c_su, c_gate = 6 * h * RET_DK, 6 * h * RET_DK + 2 * w
    gw, gs = {}, {}
    df = _matmul(dx2b, wt["w_ffn_out"], "nt", BF16, "mm_dffn_out_x_" + tag)
    gw["w_ffn_out"] = _matmul(sv["f"], dx2b, "tn", F32, "mm_dffn_out_w_" + tag)
    dac = _swiglu_bwd(sv["ac"], df, "swiglu_bwd_" + tag)
    dh2 = _matmul(dac, wt["w_ffn_in"], "nt", BF16, "mm_dffn_in_x_" + tag)
    gw["w_ffn_in"] = _matmul(sv["h2"], dac, "tn", F32, "mm_dffn_in_w_" + tag)
    dx1, dx1b, gs["norm_ffn_w"] = _rms_bwd(sv["x1"], sm["norm_ffn_w"], dh2, dx2, "rms_ffn_bwd_" + tag)
    dmg = _matmul(dx1b, wt["w_out"], "nt", BF16, "mm_dout_x_" + tag)
    gw["w_out"] = _matmul(sv["mg"], dx1b, "tn", F32, "mm_dout_w_" + tag)
    da, db, dgate = _merge_bwd(dmg, sv["a"], sv["b"], sv["z"], c_gate, "merge_bwd_" + tag)
    dga = _matmul(da, wt["ret_proj"], "nt", BF16, "mm_dret_proj_x_" + tag)
    gw["ret_proj"] = _matmul(sv["ga"], da, "tn", F32, "mm_dret_proj_w_" + tag)
    dsg = _matmul(db, wt["sgu_proj"], "nt", BF16, "mm_dsgu_proj_x_" + tag)
    gw["sgu_proj"] = _matmul(sv["sg"], db, "tn", F32, "mm_dsgu_proj_w_" + tag)
    dsu, dsv, gs["sgu_ln_w"], gs["sgu_ln_b"], gs["sgu_w_s"], gs["sgu_b_s"] = _sgu_bwd(
        sv["z"], dsg, sm["sgu_ln_w"], sm["sgu_ln_b"], sm["ws_m"], sm["ws_mt"], sm["bs"], c_su, w,
        "sgu_bwd_" + tag)
    dq, dk, dv, dg, gs["ret_gn_w"] = _ret_bwd(sv["z"], dga, sv["states"], sm["ret_gn_w"], tables, h,
                                             "ret_bwd_" + tag)
    dz = jnp.concatenate([dq, dk, dv, dg, dsu, dsv, dgate], axis=1)
    dh1 = _matmul(dz, wt["w_in"], "nt", BF16, "mm_din_x_" + tag)
    gw["w_in"] = _matmul(sv["h1"], dz, "tn", F32, "mm_din_w_" + tag)
    dx, dxb, gs["norm_mix_w"] = _rms_bwd(sv["x"], sm["norm_mix_w"], dh1, dx1, "rms_mix_bwd_" + tag)
    return dx, dxb, gw, gs


def _sgu_mask():
    pos = jnp.arange(SGU_LEN)
    return (pos[None, :] // CHUNK) <= (pos[:, None] // CHUNK)


def kernel(x, norm_mix_w, w_in, ret_gn_w, ret_proj, sgu_ln_w, sgu_ln_b, sgu_w_s, sgu_b_s, sgu_proj, w_out, norm_ffn_w, w_ffn_in, w_ffn_out, final_norm_w, loss_target, m_norm_mix_w, m_w_in, m_ret_gn_w, m_ret_proj, m_sgu_ln_w, m_sgu_ln_b, m_sgu_w_s, m_sgu_b_s, m_sgu_proj, m_w_out, m_norm_ffn_w, m_w_ffn_in, m_w_ffn_out, m_final_norm_w, v_norm_mix_w, v_w_in, v_ret_gn_w, v_ret_proj, v_sgu_ln_w, v_sgu_ln_b, v_sgu_w_s, v_sgu_b_s, v_sgu_proj, v_w_out, v_norm_ffn_w, v_w_ffn_in, v_w_ffn_out, v_final_norm_w):
    weights = dict(norm_mix_w=norm_mix_w, w_in=w_in, ret_gn_w=ret_gn_w, ret_proj=ret_proj, sgu_ln_w=sgu_ln_w,
                   sgu_ln_b=sgu_ln_b, sgu_w_s=sgu_w_s, sgu_b_s=sgu_b_s, sgu_proj=sgu_proj, w_out=w_out,
                   norm_ffn_w=norm_ffn_w, w_ffn_in=w_ffn_in, w_ffn_out=w_ffn_out, final_norm_w=final_norm_w)
    m_in = dict(norm_mix_w=m_norm_mix_w, w_in=m_w_in, ret_gn_w=m_ret_gn_w, ret_proj=m_ret_proj,
                sgu_ln_w=m_sgu_ln_w, sgu_ln_b=m_sgu_ln_b, sgu_w_s=m_sgu_w_s, sgu_b_s=m_sgu_b_s,
                sgu_proj=m_sgu_proj, w_out=m_w_out, norm_ffn_w=m_norm_ffn_w, w_ffn_in=m_w_ffn_in,
                w_ffn_out=m_w_ffn_out, final_norm_w=m_final_norm_w)
    v_in = dict(norm_mix_w=v_norm_mix_w, w_in=v_w_in, ret_gn_w=v_ret_gn_w, ret_proj=v_ret_proj,
                sgu_ln_w=v_sgu_ln_w, sgu_ln_b=v_sgu_ln_b, sgu_w_s=v_sgu_w_s, sgu_b_s=v_sgu_b_s,
                sgu_proj=v_sgu_proj, w_out=v_w_out, norm_ffn_w=v_norm_ffn_w, w_ffn_in=v_w_ffn_in,
                w_ffn_out=v_w_ffn_out, final_norm_w=v_final_norm_w)

    depth = w_in.shape[0]
    _, s, d = x.shape
    w = d
    in_cols = 4 * w_in.shape[2]
    h = (in_cols - 4 * d) // (2 * RET_DK + 2 * RET_DV)
    groups = sgu_w_s.shape[1]
    assert in_cols == h * (2 * RET_DK + 2 * RET_DV) + 4 * d and (6 * h * RET_DK) % d == 0
    assert s % SGU_LEN == 0 and w % groups == 0 and (w // groups) % LANES == 0
    dims = (h, d, w)
    tables = _ret_tables(s, h, _tile(s, RET_BLOCK))
    mask = _sgu_mask()
    cx, cy, cc = lax.axis_index("x"), lax.axis_index("y"), lax.axis_index("c")
    place = jnp.stack([cc, 2 * (1 - cx) + cy, 2 * cx + (1 - cy), 2 * (1 - cx) + (1 - cy),
                       2 * cx + cy]).astype(jnp.int32)

    shard_shapes = {n: weights[n].shape[1:] for n in BIG}
    kinds = [BIG_KIND[n] for n in BIG]
    full = []
    for l in range(depth):
        gathered = _all_gather_weights([weights[n][l].astype(BF16) for n in BIG], kinds, "ag_weights_l%d" % l)
        full.append(dict(zip(BIG, gathered)))

    small = []
    for l in range(depth):
        sm = {n: weights[n][l] for n in SMALL}
        ws_m = jnp.where(mask[None], sgu_w_s[l], 0.0)
        sm["ws_m"] = ws_m.astype(BF16)
        sm["ws_mt"] = jnp.swapaxes(ws_m, 1, 2).astype(BF16)
        sm["bs"] = sgu_b_s[l][:, :, None]
        small.append(sm)

    xs = x[0]
    saved = []
    for l in range(depth):
        xs, sv = _layer_fwd(xs, full[l], small[l], tables, dims, "l%d" % l)
        saved.append(sv)
    dx, dxb, g_final, sq = _loss_head(xs, final_norm_w, loss_target[0])
    loss = lax.psum(sq[0, 0], ("x", "y", "c")) * (0.5 / d)

    grads_big = [None] * depth
    grads_small = [None] * depth
    for l in reversed(range(depth)):
        dx, dxb, gw, gs = _layer_bwd(dx, dxb, saved[l], full[l], small[l], tables, dims, "l%d" % l)
        grads_small[l] = gs
        grads_big[l] = _reduce_scatter_grads([gw[n] for n in BIG], kinds, [shard_shapes[n] for n in BIG],
                                             place, "l%d" % l)
    grad_x = dx[None]

    pieces = []
    for l in range(depth):
        gs = dict(grads_small[l])
        gs["sgu_w_s"] = jnp.where(mask[None], gs["sgu_w_s"], 0.0)
        pieces += [gs[n].reshape(-1) for n in SMALL]
    pieces.append(g_final.reshape(-1))
    flat = jnp.concatenate(pieces)
    total = flat.shape[0]
    rows = -(-total // (8 * LANES)) * 8
    flat = jnp.pad(flat, (0, rows * LANES - total)).reshape(rows, LANES)
    summed = _all_reduce_small(flat, "ar_small").reshape(-1)
    grad = {}
    off = 0
    per_layer = {n: [] for n in SMALL}
    for l in range(depth):
        for n in SMALL:
            shp = weights[n].shape[1:]
            size = math.prod(shp)
            per_layer[n].append(summed[off:off + size].reshape(shp))
            off += size
    for n in SMALL:
        grad[n] = jnp.stack(per_layer[n])
    grad["final_norm_w"] = summed[off:off + d]
    for i, n in enumerate(BIG):
        grad[n] = jnp.stack([grads_big[l][i] for l in range(depth)])

    delta, new_m, new_v = {}, {}, {}
    for n in BIG:
        shp = weights[n].shape
        two_d = lambda a: a.reshape(shp[0] * shp[1], shp[2])
        dl, mn, vn = _adamw(two_d(weights[n]), two_d(grad[n]), two_d(m_in[n]), two_d(v_in[n]), "adamw_" + n)
        delta[n], new_m[n], new_v[n] = dl.reshape(shp), mn.reshape(shp), vn.reshape(shp)
    small_names = SMALL + ["final_norm_w"]

    def pack(tree):
        fl = jnp.concatenate([tree[n].reshape(-1) for n in small_names])
        return jnp.pad(fl, (0, rows * LANES - fl.shape[0])).reshape(rows, LANES)

    dl, mn, vn = _adamw(pack(weights), pack(grad), pack(m_in), pack(v_in), "adamw_small")
    off = 0
    for n in small_names:
        shp = weights[n].shape
        size = math.prod(shp)
        for src, dst in ((dl, delta), (mn, new_m), (vn, new_v)):
            dst[n] = src.reshape(-1)[off:off + size].reshape(shp)
        off += size

    return (loss, grad_x, *[grad[n] for n in ORDER], *[delta[n] for n in ORDER],
            *[new_m[n] for n in ORDER], *[new_v[n] for n in ORDER])
```

```python
import math

import jax
import jax.numpy as jnp
from jax import lax
from jax.experimental import pallas as pl
from jax.experimental.pallas import tpu as pltpu

F32 = jnp.float32
BF16 = jnp.bfloat16

CHUNK = 64
RET_DK = 128
RET_DV = 256
SGU_LEN = 128
ROPE_BASE = 10000.0
EPS = 1e-6
ADAM_LR = 0.001
ADAM_B1 = 0.9
ADAM_B2 = 0.999
ADAM_EPS = 1e-08
ADAM_WD = 0.01
ADAM_STEP = 10

LANES = 128
VMEM_LIMIT = 56 * 1024 * 1024
RET_BLOCK = 256
SGU_BLOCK = 256
ROW_BLOCK = 256
MM_TILE = 1024
MM_KTILE = 2048
MESH = pl.DeviceIdType.MESH
ANY = pl.BlockSpec(memory_space=pl.ANY)
INV_SQRT2 = 1.0 / math.sqrt(2.0)
INV_SQRT_2PI = 1.0 / math.sqrt(2.0 * math.pi)

DN = {"nn": (((1,), (0,)), ((), ())), "nt": (((1,), (1,)), ((), ())), "tn": (((0,), (0,)), ((), ()))}


def _dot(a, b, mode="nn"):
    return lax.dot_general(a, b, DN[mode], preferred_element_type=F32)


def _tile(n, target):
    t = min(n, target) // LANES * LANES
    while t >= LANES:
        if n % t == 0:
            return t
        t -= LANES
    return n


def _rtile(n, target):
    t = min(n, target) // 16 * 16
    while t >= 16:
        if n % t == 0:
            return t
        t -= 16
    return n


def _params(sem):
    return pltpu.CompilerParams(dimension_semantics=sem, vmem_limit_bytes=VMEM_LIMIT)


def _sigmoid(x):
    return 1.0 / (1.0 + jnp.exp(-x))


def _gelu(x):
    return 0.5 * x * (1.0 + lax.erf(x * INV_SQRT2))


def _gelu_grad(x):
    return 0.5 * (1.0 + lax.erf(x * INV_SQRT2)) + x * jnp.exp(-0.5 * x * x) * INV_SQRT_2PI


def _matmul(a, b, mode, out_dtype, name, res=None):
    if mode == "nn":
        (m, k), n = a.shape, b.shape[1]
    elif mode == "nt":
        (m, k), n = a.shape, b.shape[0]
    else:
        (k, m), n = a.shape, b.shape[1]
    tm, tn, tk = _tile(m, MM_TILE), _tile(n, MM_TILE), _tile(k, MM_KTILE)
    nk = k // tk
    if mode == "tn":
        a_spec = pl.BlockSpec((tk, tm), lambda i, j, kk: (kk, i))
    else:
        a_spec = pl.BlockSpec((tm, tk), lambda i, j, kk: (i, kk))
    if mode == "nt":
        b_spec = pl.BlockSpec((tn, tk), lambda i, j, kk: (j, kk))
    else:
        b_spec = pl.BlockSpec((tk, tn), lambda i, j, kk: (kk, j))
    o_spec = pl.BlockSpec((tm, tn), lambda i, j, kk: (i, j))
    in_specs = [a_spec, b_spec] + ([o_spec] if res is not None else [])
    acc_in_out = out_dtype == F32
    scratch = [] if (nk == 1 or acc_in_out) else [pltpu.VMEM((tm, tn), F32)]

    def body(*refs):
        a_ref, b_ref = refs[0], refs[1]
        r_ref = refs[2] if res is not None else None
        o_ref = refs[3] if res is not None else refs[2]
        p = _dot(a_ref[...], b_ref[...], mode)
        if nk == 1:
            if r_ref is not None:
                p = p + r_ref[...]
            o_ref[...] = p.astype(o_ref.dtype)
            return
        acc = o_ref if acc_in_out else refs[-1]
        kk = pl.program_id(2)

        @pl.when(kk == 0)
        def _():
            acc[...] = p if r_ref is None or not acc_in_out else p + r_ref[...]

        @pl.when(kk > 0)
        def _():
            acc[...] += p

        if not acc_in_out:
            @pl.when(kk == nk - 1)
            def _():
                o = acc[...]
                if r_ref is not None:
                    o = o + r_ref[...]
                o_ref[...] = o.astype(o_ref.dtype)

    args = (a, b) + ((res,) if res is not None else ())
    return pl.pallas_call(
        body, name=name, grid=(m // tm, n // tn, nk), in_specs=in_specs, out_specs=o_spec,
        out_shape=jax.ShapeDtypeStruct((m, n), out_dtype), scratch_shapes=scratch,
        compiler_params=_params(("parallel", "parallel", "arbitrary")))(*args)


def _rms_fwd(x, w, name):
    s, d = x.shape
    tr = _tile(s, ROW_BLOCK)

    def body(x_ref, w_ref, o_ref):
        xv = x_ref[...]
        r = lax.rsqrt(jnp.mean(xv * xv, axis=-1, keepdims=True) + EPS)
        o_ref[...] = (xv * r * w_ref[...]).astype(BF16)

    row = pl.BlockSpec((tr, d), lambda i: (i, 0))
    vec = pl.BlockSpec((1, d), lambda i: (0, 0))
    return pl.pallas_call(body, name=name, grid=(s // tr,), in_specs=[row, vec], out_specs=row,
                          out_shape=jax.ShapeDtypeStruct((s, d), BF16),
                          compiler_params=_params(("parallel",)))(x, w.reshape(1, d))


def _rms_bwd(x, w, dh, dres, name):
    s, d = x.shape
    tr = _tile(s, ROW_BLOCK)

    def body(x_ref, w_ref, dh_ref, dr_ref, dx_ref, dxb_ref, dw_ref):
        xv = x_ref[...]
        r = lax.rsqrt(jnp.mean(xv * xv, axis=-1, keepdims=True) + EPS)
        xh = xv * r
        dy = dh_ref[...].astype(F32)
        dxh = dy * w_ref[...]
        dx = dr_ref[...] + r * (dxh - xh * jnp.mean(dxh * xh, axis=-1, keepdims=True))
        dx_ref[...] = dx
        dxb_ref[...] = dx.astype(BF16)

        @pl.when(pl.program_id(0) == 0)
        def _():
            dw_ref[...] = jnp.zeros_like(dw_ref)

        dw_ref[...] += jnp.sum(dy * xh, axis=0, keepdims=True)

    row = pl.BlockSpec((tr, d), lambda i: (i, 0))
    vec = pl.BlockSpec((1, d), lambda i: (0, 0))
    return pl.pallas_call(
        body, name=name, grid=(s // tr,), in_specs=[row, vec, row, row], out_specs=[row, row, vec],
        out_shape=[jax.ShapeDtypeStruct((s, d), F32), jax.ShapeDtypeStruct((s, d), BF16),
                   jax.ShapeDtypeStruct((1, d), F32)],
        compiler_params=_params(("arbitrary",)))(x, w.reshape(1, d), dh, dres)


def _loss_head(x, w, tgt):
    s, d = x.shape
    tr = _tile(s, ROW_BLOCK)

    def body(x_ref, w_ref, t_ref, dx_ref, dxb_ref, dw_ref, l_ref):
        xv = x_ref[...]
        r = lax.rsqrt(jnp.mean(xv * xv, axis=-1, keepdims=True) + EPS)
        xh = xv * r
        e = xh * w_ref[...] - t_ref[...]
        dy = e * (1.0 / d)
        dxh = dy * w_ref[...]
        dx = r * (dxh - xh * jnp.mean(dxh * xh, axis=-1, keepdims=True))
        dx_ref[...] = dx
        dxb_ref[...] = dx.astype(BF16)

        @pl.when(pl.program_id(0) == 0)
        def _():
            dw_ref[...] = jnp.zeros_like(dw_ref)
            l_ref[...] = jnp.zeros_like(l_ref)

        dw_ref[...] += jnp.sum(dy * xh, axis=0, keepdims=True)
        l_ref[...] += jnp.sum(jnp.sum(e * e, axis=1, keepdims=True), axis=0, keepdims=True)

    row = pl.BlockSpec((tr, d), lambda i: (i, 0))
    vec = pl.BlockSpec((1, d), lambda i: (0, 0))
    one = pl.BlockSpec((1, 1), lambda i: (0, 0))
    return pl.pallas_call(
        body, name="loss_head", grid=(s // tr,), in_specs=[row, vec, row], out_specs=[row, row, vec, one],
        out_shape=[jax.ShapeDtypeStruct((s, d), F32), jax.ShapeDtypeStruct((s, d), BF16),
                   jax.ShapeDtypeStruct((1, d), F32), jax.ShapeDtypeStruct((1, 1), F32)],
        compiler_params=_params(("arbitrary",)))(x, w.reshape(1, d), tgt)


def _ret_tables(s, h, t):
    half = RET_DK // 2
    inv = ROPE_BASE ** (-jnp.arange(half, dtype=F32) / half)
    ang = jnp.arange(s, dtype=F32)[:, None] * inv[None, :]
    cos, sin = jnp.cos(ang), jnp.sin(ang)
    cosf = jnp.concatenate([cos, cos], axis=1)
    sinf = jnp.concatenate([-sin, sin], axis=1)
    log_g = jnp.log1p(-(2.0 ** (-5.0 - jnp.arange(h, dtype=F32))))
    idx = jnp.arange(t, dtype=F32)
    chunk = jnp.arange(t) // CHUNK
    allowed = chunk[None, :] <= chunk[:, None]
    dm = jnp.where(allowed[None], jnp.exp(log_g[:, None, None] * jnp.abs(idx[:, None] - idx[None, :])), 0.0)
    qd = jnp.exp(log_g[:, None] * (idx[None, :] + 1.0))
    kd = jnp.exp(log_g[:, None] * (t - 1.0 - idx[None, :]))
    qd = jnp.broadcast_to(qd[:, :, None], (h, t, RET_DK))
    kd = jnp.broadcast_to(kd[:, :, None], (h, t, RET_DK))
    cd = jnp.broadcast_to(jnp.exp(log_g * t)[:, None, None], (h, 1, RET_DV))
    return cosf, sinf, dm, qd, kd, cd


def _rot(x, cos, sin):
    return x * cos + pltpu.roll(x, RET_DK // 2, 1) * sin


def _rot_t(x, cos, sin):
    return x * cos - pltpu.roll(x, RET_DK // 2, 1) * sin


def _ret_in_specs(h, t, rev_nb=None):
    blk = (lambda b: b) if rev_nb is None else (lambda b: rev_nb - 1 - b)
    return [
        pl.BlockSpec((t, RET_DK), lambda hh, b: (blk(b), hh)),
        pl.BlockSpec((t, RET_DK), lambda hh, b: (blk(b), h + hh)),
        pl.BlockSpec((t, RET_DV), lambda hh, b: (blk(b), h + hh)),
        pl.BlockSpec((t, RET_DV), lambda hh, b: (blk(b), 2 * h + hh)),
        pl.BlockSpec((t, RET_DK), lambda hh, b: (blk(b), 0)),
        pl.BlockSpec((t, RET_DK), lambda hh, b: (blk(b), 0)),
        pl.BlockSpec((1, t, t), lambda hh, b: (hh, 0, 0)),
        pl.BlockSpec((1, t, RET_DK), lambda hh, b: (hh, 0, 0)),
        pl.BlockSpec((1, t, RET_DK), lambda hh, b: (hh, 0, 0)),
        pl.BlockSpec((1, 1, RET_DV), lambda hh, b: (hh, 0, 0)),
        pl.BlockSpec((1, RET_DV), lambda hh, b: (0, hh)),
    ]


def _ret_fwd(z, gn_w, tables, h, name):
    s = z.shape[0]
    t = _tile(s, RET_BLOCK)
    nb = s // t
    scale = RET_DK ** -0.5

    def body(q_ref, k_ref, v_ref, g_ref, cos_ref, sin_ref, dm_ref, qd_ref, kd_ref, cd_ref, gn_ref,
             o_ref, st_ref, st_scr):
        @pl.when(pl.program_id(1) == 0)
        def _():
            st_scr[...] = jnp.zeros_like(st_scr)

        cos, sin = cos_ref[...], sin_ref[...]
        qf = _rot(q_ref[...].astype(F32), cos, sin) * scale
        kf = _rot(k_ref[...].astype(F32), cos, sin)
        vb = v_ref[...]
        p = _dot(qf.astype(BF16), kf.astype(BF16), "nt") * dm_ref[0]
        st = st_scr[...]
        stb = st.astype(BF16)
        st_ref[0, 0] = stb
        o = _dot(p.astype(BF16), vb) + _dot((qf * qd_ref[0]).astype(BF16), stb)
        st_scr[...] = st * cd_ref[0] + _dot((kf * kd_ref[0]).astype(BF16), vb, "tn")
        dlt = o - jnp.mean(o, axis=-1, keepdims=True)
        oh = dlt * lax.rsqrt(jnp.mean(dlt * dlt, axis=-1, keepdims=True) + EPS)
        g = g_ref[...].astype(F32)
        o_ref[...] = (g * _sigmoid(g) * oh * gn_ref[...]).astype(BF16)

    return pl.pallas_call(
        body, name=name, grid=(h, nb), in_specs=_ret_in_specs(h, t),
        out_specs=[pl.BlockSpec((t, RET_DV), lambda hh, b: (b, hh)),
                   pl.BlockSpec((1, 1, RET_DK, RET_DV), lambda hh, b: (b, hh, 0, 0))],
        out_shape=[jax.ShapeDtypeStruct((s, h * RET_DV), BF16),
                   jax.ShapeDtypeStruct((nb, h, RET_DK, RET_DV), BF16)],
        scratch_shapes=[pltpu.VMEM((RET_DK, RET_DV), F32)],
        compiler_params=_params(("parallel", "arbitrary")))(z, z, z, z, *tables, gn_w.reshape(1, -1))


def _ret_bwd(z, dga, states, gn_w, tables, h, name):
    s = z.shape[0]
    t = _tile(s, RET_BLOCK)
    nb = s // t
    scale = RET_DK ** -0.5

    def body(q_ref, k_ref, v_ref, g_ref, cos_ref, sin_ref, dm_ref, qd_ref, kd_ref, cd_ref, gn_ref,
             dga_ref, st_ref, dq_ref, dk_ref, dv_ref, dg_ref, dgn_ref, dst_scr):
        @pl.when(pl.program_id(1) == 0)
        def _():
            dst_scr[...] = jnp.zeros_like(dst_scr)
            dgn_ref[...] = jnp.zeros_like(dgn_ref)

        cos, sin = cos_ref[...], sin_ref[...]
        dm = dm_ref[0]
        qf = _rot(q_ref[...].astype(F32), cos, sin) * scale
        kf = _rot(k_ref[...].astype(F32), cos, sin)
        qb, kb, vb = qf.astype(BF16), kf.astype(BF16), v_ref[...]
        qdb = (qf * qd_ref[0]).astype(BF16)
        kdb = (kf * kd_ref[0]).astype(BF16)
        stb = st_ref[0, 0]
        pb = (_dot(qb, kb, "nt") * dm).astype(BF16)
        o = _dot(pb, vb) + _dot(qdb, stb)
        dlt = o - jnp.mean(o, axis=-1, keepdims=True)
        rstd = lax.rsqrt(jnp.mean(dlt * dlt, axis=-1, keepdims=True) + EPS)
        oh = dlt * rstd
        gn = gn_ref[...]
        g = g_ref[...].astype(F32)
        sg = _sigmoid(g)
        dga_v = dga_ref[...].astype(F32)
        dret = dga_v * g * sg
        dg_ref[...] = (dga_v * oh * gn * sg * (1.0 + g * (1.0 - sg))).astype(BF16)
        dgn_ref[...] += jnp.sum(dret * oh, axis=0, keepdims=True)
        doh = dret * gn
        do = rstd * (doh - jnp.mean(doh, axis=-1, keepdims=True)
                     - oh * jnp.mean(doh * oh, axis=-1, keepdims=True))
        dob = do.astype(BF16)
        dst = dst_scr[...]
        dstb = dst.astype(BF16)
        dv_ref[...] = (_dot(pb, dob, "tn") + _dot(kdb, dstb)).astype(BF16)
        dpb = (_dot(dob, vb, "nt") * dm).astype(BF16)
        dqf = _dot(dpb, kb) + _dot(dob, stb, "nt") * qd_ref[0]
        dkf = _dot(dpb, qb, "tn") + _dot(vb, dstb, "nt") * kd_ref[0]
        dst_scr[...] = dst * cd_ref[0] + _dot(qdb, dob, "tn")
        dq_ref[...] = _rot_t(dqf * scale, cos, sin).astype(BF16)
        dk_ref[...] = _rot_t(dkf, cos, sin).astype(BF16)

    rb = lambda hh, b: (nb - 1 - b, hh)
    in_specs = _ret_in_specs(h, t, rev_nb=nb) + [
        pl.BlockSpec((t, RET_DV), rb),
        pl.BlockSpec((1, 1, RET_DK, RET_DV), lambda hh, b: (nb - 1 - b, hh, 0, 0))]
    return pl.pallas_call(
        body, name=name, grid=(h, nb), in_specs=in_specs,
        out_specs=[pl.BlockSpec((t, RET_DK), rb), pl.BlockSpec((t, RET_DK), rb),
                   pl.BlockSpec((t, RET_DV), rb), pl.BlockSpec((t, RET_DV), rb),
                   pl.BlockSpec((1, RET_DV), lambda hh, b: (0, hh))],
        out_shape=[jax.ShapeDtypeStruct((s, h * RET_DK), BF16), jax.ShapeDtypeStruct((s, h * RET_DK), BF16),
                   jax.ShapeDtypeStruct((s, h * RET_DV), BF16), jax.ShapeDtypeStruct((s, h * RET_DV), BF16),
                   jax.ShapeDtypeStruct((1, h * RET_DV), F32)],
        scratch_shapes=[pltpu.VMEM((RET_DK, RET_DV), F32)],
        compiler_params=_params(("parallel", "arbitrary")))(z, z, z, z, *tables, gn_w.reshape(1, -1), dga, states)


def _sgu_fwd(z, ln_w, ln_b, ws_m, bs, col0, w, name):
    s = z.shape[0]
    t = _tile(s, SGU_BLOCK)
    groups = ws_m.shape[0]
    ch = w // groups
    cb = col0 // w

    def body(su_ref, sv_ref, lw_ref, lb_ref, ws_ref, bs_ref, o_ref):
        zv = _gelu(sv_ref[...].astype(F32))
        dlt = zv - jnp.mean(zv, axis=-1, keepdims=True)
        vn = dlt * lax.rsqrt(jnp.mean(dlt * dlt, axis=-1, keepdims=True) + EPS) * lw_ref[...] + lb_ref[...]
        vnb = vn.astype(BF16)
        for r in range(t // SGU_LEN):
            rows = slice(r * SGU_LEN, (r + 1) * SGU_LEN)
            for gi in range(groups):
                cols = slice(gi * ch, (gi + 1) * ch)
                mixed = _dot(ws_ref[gi], vnb[rows, cols]) + bs_ref[gi]
                o_ref[rows, cols] = (_gelu(su_ref[rows, cols].astype(F32)) * mixed).astype(BF16)

    row = lambda off: pl.BlockSpec((t, w), lambda i: (i, cb + off))
    vec = pl.BlockSpec((1, w), lambda i: (0, 0))
    return pl.pallas_call(
        body, name=name, grid=(s // t,),
        in_specs=[row(0), row(1), vec, vec,
                  pl.BlockSpec((groups, SGU_LEN, SGU_LEN), lambda i: (0, 0, 0)),
                  pl.BlockSpec((groups, SGU_LEN, 1), lambda i: (0, 0, 0))],
        out_specs=pl.BlockSpec((t, w), lambda i: (i, 0)),
        out_shape=jax.ShapeDtypeStruct((s, w), BF16),
        compiler_params=_params(("parallel",)))(z, z, ln_w.reshape(1, w), ln_b.reshape(1, w), ws_m, bs)


def _sgu_bwd(z, dsg, ln_w, ln_b, ws_m, ws_mt, bs, col0, w, name):
    s = z.shape[0]
    t = _tile(s, SGU_BLOCK)
    groups = ws_m.shape[0]
    ch = w // groups
    cb = col0 // w

    def body(su_ref, sv_ref, dsg_ref, lw_ref, lb_ref, ws_ref, wst_ref, bs_ref,
             dsu_ref, dsv_ref, dlw_ref, dlb_ref, dws_ref, dbs_ref, dvn_scr):
        @pl.when(pl.program_id(0) == 0)
        def _():
            dlw_ref[...] = jnp.zeros_like(dlw_ref)
            dlb_ref[...] = jnp.zeros_like(dlb_ref)
            dws_ref[...] = jnp.zeros_like(dws_ref)
            dbs_ref[...] = jnp.zeros_like(dbs_ref)

        sv = sv_ref[...].astype(F32)
        zv = _gelu(sv)
        dlt = zv - jnp.mean(zv, axis=-1, keepdims=True)
        rstd = lax.rsqrt(jnp.mean(dlt * dlt, axis=-1, keepdims=True) + EPS)
        vh = dlt * rstd
        vnb = (vh * lw_ref[...] + lb_ref[...]).astype(BF16)
        for r in range(t // SGU_LEN):
            rows = slice(r * SGU_LEN, (r + 1) * SGU_LEN)
            for gi in range(groups):
                cols = slice(gi * ch, (gi + 1) * ch)
                vn_p = vnb[rows, cols]
                mixed = _dot(ws_ref[gi], vn_p) + bs_ref[gi]
                su = su_ref[rows, cols].astype(F32)
                dsg_p = dsg_ref[rows, cols].astype(F32)
                dsu_ref[rows, cols] = (dsg_p * mixed * _gelu_grad(su)).astype(BF16)
                dmix = dsg_p * _gelu(su)
                dmixb = dmix.astype(BF16)
                dvn_scr[rows, cols] = _dot(wst_ref[gi], dmixb)
                dws_ref[gi] += _dot(dmixb, vn_p, "nt")
                dbs_ref[gi] += jnp.sum(dmix, axis=1, keepdims=True)
        dvn = dvn_scr[...]
        dlw_ref[...] += jnp.sum(dvn * vh, axis=0, keepdims=True)
        dlb_ref[...] += jnp.sum(dvn, axis=0, keepdims=True)
        dvh = dvn * lw_ref[...]
        dzv = rstd * (dvh - jnp.mean(dvh, axis=-1, keepdims=True)
                      - vh * jnp.mean(dvh * vh, axis=-1, keepdims=True))
        dsv_ref[...] = (dzv * _gelu_grad(sv)).astype(BF16)

    row = lambda off: pl.BlockSpec((t, w), lambda i: (i, cb + off))
    out_row = pl.BlockSpec((t, w), lambda i: (i, 0))
    vec = pl.BlockSpec((1, w), lambda i: (0, 0))
    mat = pl.BlockSpec((groups, SGU_LEN, SGU_LEN), lambda i: (0, 0, 0))
    col = pl.BlockSpec((groups, SGU_LEN, 1), lambda i: (0, 0, 0))
    return pl.pallas_call(
        body, name=name, grid=(s // t,),
        in_specs=[row(0), row(1), out_row, vec, vec, mat, mat, col],
        out_specs=[out_row, out_row, vec, vec, mat, col],
        out_shape=[jax.ShapeDtypeStruct((s, w), BF16), jax.ShapeDtypeStruct((s, w), BF16),
                   jax.ShapeDtypeStruct((1, w), F32), jax.ShapeDtypeStruct((1, w), F32),
                   jax.ShapeDtypeStruct((groups, SGU_LEN, SGU_LEN), F32),
                   jax.ShapeDtypeStruct((groups, SGU_LEN, 1), F32)],
        scratch_shapes=[pltpu.VMEM((t, w), F32)],
        compiler_params=_params(("arbitrary",)))(z, z, dsg, ln_w.reshape(1, w), ln_b.reshape(1, w), ws_m, ws_mt, bs)


def _merge_fwd(a, b, z, col0, name):
    s, d = a.shape
    tr = _tile(s, ROW_BLOCK)
    cb = col0 // d

    def body(a_ref, b_ref, ga_ref, gb_ref, o_ref):
        o_ref[...] = (_sigmoid(ga_ref[...].astype(F32)) * a_ref[...].astype(F32)
                      + _sigmoid(gb_ref[...].astype(F32)) * b_ref[...].astype(F32)).astype(BF16)

    row = pl.BlockSpec((tr, d), lambda i: (i, 0))
    gate = lambda off: pl.BlockSpec((tr, d), lambda i: (i, cb + off))
    return pl.pallas_call(body, name=name, grid=(s // tr,), in_specs=[row, row, gate(0), gate(1)],
                          out_specs=row, out_shape=jax.ShapeDtypeStruct((s, d), BF16),
                          compiler_params=_params(("parallel",)))(a, b, z, z)


def _merge_bwd(dmg, a, b, z, col0, name):
    s, d = a.shape
    tr = _tile(s, ROW_BLOCK)
    cb = col0 // d

    def body(dm_ref, a_ref, b_ref, ga_ref, gb_ref, da_ref, db_ref, dgt_ref):
        dm = dm_ref[...].astype(F32)
        sa = _sigmoid(ga_ref[...].astype(F32))
        sb = _sigmoid(gb_ref[...].astype(F32))
        da_ref[...] = (dm * sa).astype(BF16)
        db_ref[...] = (dm * sb).astype(BF16)
        dgt_ref[:, :d] = (dm * a_ref[...].astype(F32) * sa * (1.0 - sa)).astype(BF16)
        dgt_ref[:, d:] = (dm * b_ref[...].astype(F32) * sb * (1.0 - sb)).astype(BF16)

    row = pl.BlockSpec((tr, d), lambda i: (i, 0))
    wide = pl.BlockSpec((tr, 2 * d), lambda i: (i, 0))
    gate = lambda off: pl.BlockSpec((tr, d), lambda i: (i, cb + off))
    return pl.pallas_call(
        body, name=name, grid=(s // tr,), in_specs=[row, row, row, gate(0), gate(1)],
        out_specs=[row, row, wide],
        out_shape=[jax.ShapeDtypeStruct((s, d), BF16), jax.ShapeDtypeStruct((s, d), BF16),
                   jax.ShapeDtypeStruct((s, 2 * d), BF16)],
        compiler_params=_params(("parallel",)))(dmg, a, b, z, z)


def _swiglu_fwd(ac, name):
    s, f2 = ac.shape
    f = f2 // 2
    tr = _tile(s, ROW_BLOCK)

    def body(a_ref, c_ref, o_ref):
        a = a_ref[...].astype(F32)
        o_ref[...] = (a * _sigmoid(a) * c_ref[...].astype(F32)).astype(BF16)

    half = lambda off: pl.BlockSpec((tr, f), lambda i: (i, off))
    return pl.pallas_call(body, name=name, grid=(s // tr,), in_specs=[half(0), half(1)], out_specs=half(0),
                          out_shape=jax.ShapeDtypeStruct((s, f), BF16),
                          compiler_params=_params(("parallel",)))(ac, ac)


def _swiglu_bwd(ac, df, name):
    s, f2 = ac.shape
    f = f2 // 2
    tr = _tile(s, ROW_BLOCK)

    def body(a_ref, c_ref, df_ref, o_ref):
        a = a_ref[...].astype(F32)
        sg = _sigmoid(a)
        dfv = df_ref[...].astype(F32)
        o_ref[:, :f] = (dfv * c_ref[...].astype(F32) * sg * (1.0 + a * (1.0 - sg))).astype(BF16)
        o_ref[:, f:] = (dfv * a * sg).astype(BF16)

    half = lambda off: pl.BlockSpec((tr, f), lambda i: (i, off))
    return pl.pallas_call(body, name=name, grid=(s // tr,), in_specs=[half(0), half(1), half(0)],
                          out_specs=pl.BlockSpec((tr, f2), lambda i: (i, 0)),
                          out_shape=jax.ShapeDtypeStruct((s, f2), BF16),
                          compiler_params=_params(("parallel",)))(ac, ac, df)


def _adamw(w, g, m, v, name):
    r, c = w.shape
    tr = _rtile(r, LANES)
    c1 = 1.0 - ADAM_B1 ** ADAM_STEP
    c2 = 1.0 - ADAM_B2 ** ADAM_STEP

    def body(w_ref, g_ref, m_ref, v_ref, d_ref, mo_ref, vo_ref):
        gv = g_ref[...]
        mn = ADAM_B1 * m_ref[...] + (1.0 - ADAM_B1) * gv
        vn = ADAM_B2 * v_ref[...] + (1.0 - ADAM_B2) * (gv * gv)
        mo_ref[...] = mn
        vo_ref[...] = vn
        d_ref[...] = -ADAM_LR * ((mn / c1) / (jnp.sqrt(vn / c2) + ADAM_EPS) + ADAM_WD * w_ref[...])

    blk = pl.BlockSpec((tr, c), lambda i: (i, 0))
    shp = jax.ShapeDtypeStruct((r, c), F32)
    return pl.pallas_call(body, name=name, grid=(r // tr,), in_specs=[blk] * 4, out_specs=[blk] * 3,
                          out_shape=[shp] * 3, compiler_params=_params(("parallel",)))(w, g, m, v)


def _place():
    x, y, c = lax.axis_index("x"), lax.axis_index("y"), lax.axis_index("c")
    chips = [(1 - x, y), (x, 1 - y), (1 - x, 1 - y)]
    return x, y, c, chips


def _block(ref, kind, chip, half, shard_shape):
    rs, cs = shard_shape
    if kind == "col":
        rows = pl.ds(0, rs) if half is None else pl.ds(half * (rs // 2), rs // 2)
        return ref.at[rows, pl.ds(chip * cs, cs)]
    rows = pl.ds(chip * rs, rs) if half is None else pl.ds(chip * rs + half * (rs // 2), rs // 2)
    return ref.at[rows, :]


def _half_rows(ref, half):
    rs = ref.shape[0]
    return ref.at[pl.ds(half * (rs // 2), rs // 2), :]


def _all_gather_weights(shards, kinds, name):
    n = len(shards)
    out_shapes = [jax.ShapeDtypeStruct((s.shape[0], 4 * s.shape[1]) if k == "col" else (4 * s.shape[0], s.shape[1]),
                                       s.dtype) for s, k in zip(shards, kinds)]

    def body(*refs):
        ins, outs = refs[:n], refs[n:2 * n]
        send_sems, recv_sems, fsend_sems, frecv_sems, own_send_sems, own_recv_sems = refs[2 * n:]
        x, y, c, chips = _place()
        me = 2 * x + y
        waits = []
        for i in range(n):
            shp = ins[i].shape
            mine = _block(outs[i], kinds[i], me, None, shp)
            own = pltpu.make_async_remote_copy(
                src_ref=ins[i], dst_ref=mine, send_sem=own_send_sems.at[i], recv_sem=own_recv_sems.at[i],
                device_id=(x, y, 1 - c), device_id_type=MESH)
            own.start()
            waits.append(own.wait)
            for j, (px, py) in enumerate(chips):
                cp = pltpu.make_async_remote_copy(
                    src_ref=_half_rows(ins[i], c), dst_ref=_block(outs[i], kinds[i], me, c, shp),
                    send_sem=send_sems.at[3 * i + j], recv_sem=recv_sems.at[3 * i + j],
                    device_id=(px, py, c), device_id_type=MESH)
                cp.start()
                waits.append(cp.wait_send)
        for i in range(n):
            shp = ins[i].shape
            for j, (px, py) in enumerate(chips):
                landed = _block(outs[i], kinds[i], 2 * px + py, c, shp)
                pltpu.make_async_remote_copy(
                    src_ref=landed, dst_ref=landed, send_sem=send_sems.at[3 * i + j],
                    recv_sem=recv_sems.at[3 * i + j], device_id=(px, py, c), device_id_type=MESH).wait_recv()
                fwd = pltpu.make_async_remote_copy(
                    src_ref=landed, dst_ref=landed, send_sem=fsend_sems.at[3 * i + j],
                    recv_sem=frecv_sems.at[3 * i + j], device_id=(x, y, 1 - c), device_id_type=MESH)
                fwd.start()
                waits.append(fwd.wait_send)
        for i in range(n):
            shp = ins[i].shape
            for j, (px, py) in enumerate(chips):
                passed = _block(outs[i], kinds[i], 2 * px + py, 1 - c, shp)
                pltpu.make_async_remote_copy(
                    src_ref=passed, dst_ref=passed, send_sem=fsend_sems.at[3 * i + j],
                    recv_sem=frecv_sems.at[3 * i + j], device_id=(x, y, 1 - c), device_id_type=MESH).wait_recv()
        for w in waits:
            w()

    return pl.pallas_call(
        body, name=name, in_specs=[ANY] * n, out_specs=[ANY] * n, out_shape=out_shapes,
        scratch_shapes=[pltpu.SemaphoreType.DMA((3 * n,))] * 4 + [pltpu.SemaphoreType.DMA((n,))] * 2,
        compiler_params=pltpu.CompilerParams(has_side_effects=True))(*shards)


def _exchange_core_halves(grads, kinds, shard_shapes, name):
    n = len(grads)
    out_shapes = [jax.ShapeDtypeStruct((4, rs // 2, cs), F32) for rs, cs in shard_shapes]

    def body(*refs):
        ins, outs = refs[:n], refs[n:2 * n]
        send_sems, recv_sems = refs[2 * n:]
        x, y, c, _ = _place()
        copies = []
        for i in range(n):
            for q in range(4):
                cp = pltpu.make_async_remote_copy(
                    src_ref=_block(ins[i], kinds[i], q, 1 - c, shard_shapes[i]), dst_ref=outs[i].at[q],
                    send_sem=send_sems.at[4 * i + q], recv_sem=recv_sems.at[4 * i + q],
                    device_id=(x, y, 1 - c), device_id_type=MESH)
                cp.start()
                copies.append(cp)
        for cp in copies:
            cp.wait()

    return pl.pallas_call(
        body, name=name, in_specs=[ANY] * n, out_specs=[ANY] * n, out_shape=out_shapes,
        scratch_shapes=[pltpu.SemaphoreType.DMA((4 * n,))] * 2,
        compiler_params=pltpu.CompilerParams(has_side_effects=True))(*grads)


def _grad_block_map(kind, nt):
    if kind == "col":
        return lambda j, t, p: (p[0] * nt + t, p[1 + j])
    return lambda j, t, p: ((p[1 + j] * 2 + p[0]) * nt + t, 0)


def _chip_sum(grad, sib, kind, shard_shape, place, name):
    rs, cs = shard_shape
    hr = rs // 2
    tr = _rtile(hr, 256)
    nt = hr // tr
    g_map = _grad_block_map(kind, nt)

    def body(p_ref, g_ref, s_ref, o_ref):
        o_ref[0] = (g_ref[...] + s_ref[0]).astype(BF16)

    return pl.pallas_call(
        body, name=name,
        grid_spec=pltpu.PrefetchScalarGridSpec(
            num_scalar_prefetch=1, grid=(3, nt),
            in_specs=[pl.BlockSpec((tr, cs), g_map),
                      pl.BlockSpec((1, tr, cs), lambda j, t, p: (p[1 + j], t, 0))],
            out_specs=pl.BlockSpec((1, tr, cs), lambda j, t, p: (j, t, 0))),
        out_shape=jax.ShapeDtypeStruct((3, hr, cs), BF16),
        compiler_params=_params(("arbitrary", "arbitrary")))(place, grad, sib)


def _scatter_partials(parts, name):
    n = len(parts)

    def body(*refs):
        ins, outs = refs[:n], refs[n:2 * n]
        send_sems, recv_sems = refs[2 * n:]
        _, _, c, chips = _place()
        copies = []
        for i in range(n):
            for j, (px, py) in enumerate(chips):
                cp = pltpu.make_async_remote_copy(
                    src_ref=ins[i].at[j], dst_ref=outs[i].at[j], send_sem=send_sems.at[3 * i + j],
                    recv_sem=recv_sems.at[3 * i + j], device_id=(px, py, c), device_id_type=MESH)
                cp.start()
                copies.append(cp)
        for cp in copies:
            cp.wait()

    return pl.pallas_call(
        body, name=name, in_specs=[ANY] * n, out_specs=[ANY] * n,
        out_shape=[jax.ShapeDtypeStruct(p.shape, p.dtype) for p in parts],
        scratch_shapes=[pltpu.SemaphoreType.DMA((3 * n,))] * 2,
        compiler_params=pltpu.CompilerParams(has_side_effects=True))(*parts)


def _final_sum(grad, sib, recv, kind, shard_shape, place, name):
    rs, cs = shard_shape
    hr = rs // 2
    tr = _rtile(hr, 256)
    nt = hr // tr
    g_map = _grad_block_map(kind, nt)

    def body(p_ref, g_ref, s_ref, r_ref, out_ref):
        acc = g_ref[...] + s_ref[0]
        for j in range(3):
            acc = acc + r_ref[j].astype(F32)
        out_ref[...] = acc

    return pl.pallas_call(
        body, name=name,
        grid_spec=pltpu.PrefetchScalarGridSpec(
            num_scalar_prefetch=1, grid=(nt,),
            in_specs=[pl.BlockSpec((tr, cs), lambda t, p: g_map(3, t, p)),
                      pl.BlockSpec((1, tr, cs), lambda t, p: (p[4], t, 0)),
                      pl.BlockSpec((3, tr, cs), lambda t, p: (0, t, 0))],
            out_specs=pl.BlockSpec((tr, cs), lambda t, p: (p[0] * nt + t, 0))),
        out_shape=jax.ShapeDtypeStruct((rs, cs), F32),
        compiler_params=_params(("arbitrary",)))(place, grad, sib, recv)


def _join_core_halves(shards, name):
    n = len(shards)

    def body(*refs):
        outs = refs[n:2 * n]
        send_sems, recv_sems = refs[2 * n:]
        x, y, c, _ = _place()
        copies = []
        for i in range(n):
            mine = _half_rows(outs[i], c)
            cp = pltpu.make_async_remote_copy(
                src_ref=mine, dst_ref=mine, send_sem=send_sems.at[i], recv_sem=recv_sems.at[i],
                device_id=(x, y, 1 - c), device_id_type=MESH)
            cp.start()
            copies.append(cp)
        for i, cp in enumerate(copies):
            cp.wait_send()
            theirs = _half_rows(outs[i], 1 - c)
            pltpu.make_async_remote_copy(
                src_ref=theirs, dst_ref=theirs, send_sem=send_sems.at[i], recv_sem=recv_sems.at[i],
                device_id=(x, y, 1 - c), device_id_type=MESH).wait_recv()

    return pl.pallas_call(
        body, name=name, in_specs=[ANY] * n, out_specs=[ANY] * n,
        out_shape=[jax.ShapeDtypeStruct(s.shape, F32) for s in shards],
        input_output_aliases={i: i for i in range(n)},
        scratch_shapes=[pltpu.SemaphoreType.DMA((n,))] * 2,
        compiler_params=pltpu.CompilerParams(has_side_effects=True))(*shards)


def _all_reduce_small(v, name):
    rows = v.shape[0]

    def body(v_ref, o_ref, buf, send_sems, recv_sems):
        x, y, c, _ = _place()
        coord = lambda p: ((1 - x) if p & 4 else x, (1 - y) if p & 2 else y, (1 - c) if p & 1 else c)
        me = 4 * x + 2 * y + c
        buf[me] = v_ref[...]
        copies = []
        for p in range(1, 8):
            cp = pltpu.make_async_remote_copy(
                src_ref=v_ref, dst_ref=buf.at[me], send_sem=send_sems.at[p - 1], recv_sem=recv_sems.at[p - 1],
                device_id=coord(p), device_id_type=MESH)
            cp.start()
            copies.append(cp)
        for p in range(1, 8):
            px, py, pc = coord(p)
            pltpu.make_async_remote_copy(
                src_ref=v_ref, dst_ref=buf.at[4 * px + 2 * py + pc], send_sem=send_sems.at[p - 1],
                recv_sem=recv_sems.at[p - 1], device_id=coord(p), device_id_type=MESH).wait_recv()
        for cp in copies:
            cp.wait_send()
        acc = buf[0]
        for dev in range(1, 8):
            acc = acc + buf[dev]
        o_ref[...] = acc

    vm = pl.BlockSpec(memory_space=pltpu.VMEM)
    return pl.pallas_call(
        body, name=name, in_specs=[vm], out_specs=vm, out_shape=jax.ShapeDtypeStruct(v.shape, F32),
        scratch_shapes=[pltpu.VMEM((8, rows, LANES), F32), pltpu.SemaphoreType.DMA((7,)),
                        pltpu.SemaphoreType.DMA((7,))],
        compiler_params=pltpu.CompilerParams(vmem_limit_bytes=VMEM_LIMIT))(v)


def _reduce_scatter_grads(grads, kinds, shard_shapes, place, tag):
    sib = _exchange_core_halves(grads, kinds, shard_shapes, "rs_core_exchange_" + tag)
    peers = [_chip_sum(g, sib[i], kinds[i], shard_shapes[i], place, "rs_chip_sum%d_%s" % (i, tag))
             for i, g in enumerate(grads)]
    recv = _scatter_partials(peers, "rs_scatter_" + tag)
    halves = [_final_sum(g, sib[i], recv[i], kinds[i], shard_shapes[i], place, "rs_final_sum%d_%s" % (i, tag))
              for i, g in enumerate(grads)]
    return _join_core_halves(halves, "rs_core_join_" + tag)


BIG = ["w_in", "ret_proj", "sgu_proj", "w_out", "w_ffn_in", "w_ffn_out"]
BIG_KIND = {"w_in": "col", "ret_proj": "row", "sgu_proj": "row", "w_out": "row", "w_ffn_in": "col",
            "w_ffn_out": "row"}
SMALL = ["norm_mix_w", "ret_gn_w", "sgu_ln_w", "sgu_ln_b", "sgu_w_s", "sgu_b_s", "norm_ffn_w"]
ORDER = ["norm_mix_w", "w_in", "ret_gn_w", "ret_proj", "sgu_ln_w", "sgu_ln_b", "sgu_w_s", "sgu_b_s",
         "sgu_proj", "w_out", "norm_ffn_w", "w_ffn_in", "w_ffn_out", "final_norm_w"]


def _layer_fwd(x, wt, sm, tables, dims, tag):
    h, d, w = dims
    c_su, c_gate = 6 * h * RET_DK, 6 * h * RET_DK + 2 * w
    h1 = _rms_fwd(x, sm["norm_mix_w"], "rms_mix_fwd_" + tag)
    z = _matmul(h1, wt["w_in"], "nn", BF16, "mm_in_" + tag)
    ga, states = _ret_fwd(z, sm["ret_gn_w"], tables, h, "ret_fwd_" + tag)
    sg = _sgu_fwd(z, sm["sgu_ln_w"], sm["sgu_ln_b"], sm["ws_m"], sm["bs"], c_su, w, "sgu_fwd_" + tag)
    a = _matmul(ga, wt["ret_proj"], "nn", BF16, "mm_ret_proj_" + tag)
    b = _matmul(sg, wt["sgu_proj"], "nn", BF16, "mm_sgu_proj_" + tag)
    mg = _merge_fwd(a, b, z, c_gate, "merge_fwd_" + tag)
    x1 = _matmul(mg, wt["w_out"], "nn", F32, "mm_out_" + tag, res=x)
    h2 = _rms_fwd(x1, sm["norm_ffn_w"], "rms_ffn_fwd_" + tag)
    ac = _matmul(h2, wt["w_ffn_in"], "nn", BF16, "mm_ffn_in_" + tag)
    f = _swiglu_fwd(ac, "swiglu_fwd_" + tag)
    x2 = _matmul(f, wt["w_ffn_out"], "nn", F32, "mm_ffn_out_" + tag, res=x1)
    saved = dict(x=x, h1=h1, z=z, states=states, ga=ga, sg=sg, a=a, b=b, mg=mg, x1=x1, h2=h2, ac=ac, f=f)
    return x2, saved


def _layer_bwd(dx2, dx2b, sv, wt, sm, tables, dims, tag):
    h, d, w = dims
    c_su, c_gate = 6 * h * RET_DK, 6 * h * RET_DK + 2 * w
    gw, gs = {}, {}
    df = _matmul(dx2b, wt["w_ffn_out"], "nt", BF16, "mm_dffn_out_x_" + tag)
    gw["w_ffn_out"] = _matmul(sv["f"], dx2b, "tn", F32, "mm_dffn_out_w_" + tag)
    dac = _swiglu_bwd(sv["ac"], df, "swiglu_bwd_" + tag)
    dh2 = _matmul(dac, wt["w_ffn_in"], "nt", BF16, "mm_dffn_in_x_" + tag)
    gw["w_ffn_in"] = _matmul(sv["h2"], dac, "tn", F32, "mm_dffn_in_w_" + tag)
    dx1, dx1b, gs["norm_ffn_w"] = _rms_bwd(sv["x1"], sm["norm_ffn_w"], dh2, dx2, "rms_ffn_bwd_" + tag)
    dmg = _matmul(dx1b, wt["w_out"], "nt", BF16, "mm_dout_x_" + tag)
    gw["w_out"] = _matmul(sv["mg"], dx1b, "tn", F32, "mm_dout_w_" + tag)
    da, db, dgate = _merge_bwd(dmg, sv["a"], sv["b"], sv["z"], c_gate, "merge_bwd_" + tag)
    dga = _matmul(da, wt["ret_proj"], "nt", BF16, "mm_dret_proj_x_" + tag)
    gw["ret_proj"] = _matmul(sv["ga"], da, "tn", F32, "mm_dret_proj_w_" + tag)
    dsg = _matmul(db, wt["sgu_proj"], "nt", BF16, "mm_dsgu_proj_x_" + tag)
    gw["sgu_proj"] = _matmul(sv["sg"], db, "tn", F32, "mm_dsgu_proj_w_" + tag)
    dsu, dsv, gs["sgu_ln_w"], gs["sgu_ln_b"], gs["sgu_w_s"], gs["sgu_b_s"] = _sgu_bwd(
        sv["z"], dsg, sm["sgu_ln_w"], sm["sgu_ln_b"], sm["ws_m"], sm["ws_mt"], sm["bs"], c_su, w,
        "sgu_bwd_" + tag)
    dq, dk, dv, dg, gs["ret_gn_w"] = _ret_bwd(sv["z"], dga, sv["states"], sm["ret_gn_w"], tables, h,
                                             "ret_bwd_" + tag)
    dz = jnp.concatenate([dq, dk, dv, dg, dsu, dsv, dgate], axis=1)
    dh1 = _matmul(dz, wt["w_in"], "nt", BF16, "mm_din_x_" + tag)
    gw["w_in"] = _matmul(sv["h1"], dz, "tn", F32, "mm_din_w_" + tag)
    dx, dxb, gs["norm_mix_w"] = _rms_bwd(sv["x"], sm["norm_mix_w"], dh1, dx1, "rms_mix_bwd_" + tag)
    return dx, dxb, gw, gs


def _sgu_mask():
    pos = jnp.arange(SGU_LEN)
    return (pos[None, :] // CHUNK) <= (pos[:, None] // CHUNK)


def kernel(x, norm_mix_w, w_in, ret_gn_w, ret_proj, sgu_ln_w, sgu_ln_b, sgu_w_s, sgu_b_s, sgu_proj, w_out, norm_ffn_w, w_ffn_in, w_ffn_out, final_norm_w, loss_target, m_norm_mix_w, m_w_in, m_ret_gn_w, m_ret_proj, m_sgu_ln_w, m_sgu_ln_b, m_sgu_w_s, m_sgu_b_s, m_sgu_proj, m_w_out, m_norm_ffn_w, m_w_ffn_in, m_w_ffn_out, m_final_norm_w, v_norm_mix_w, v_w_in, v_ret_gn_w, v_ret_proj, v_sgu_ln_w, v_sgu_ln_b, v_sgu_w_s, v_sgu_b_s, v_sgu_proj, v_w_out, v_norm_ffn_w, v_w_ffn_in, v_w_ffn_out, v_final_norm_w):
    weights = dict(norm_mix_w=norm_mix_w, w_in=w_in, ret_gn_w=ret_gn_w, ret_proj=ret_proj, sgu_ln_w=sgu_ln_w,
                   sgu_ln_b=sgu_ln_b, sgu_w_s=sgu_w_s, sgu_b_s=sgu_b_s, sgu_proj=sgu_proj, w_out=w_out,
                   norm_ffn_w=norm_ffn_w, w_ffn_in=w_ffn_in, w_ffn_out=w_ffn_out, final_norm_w=final_norm_w)
    m_in = dict(norm_mix_w=m_norm_mix_w, w_in=m_w_in, ret_gn_w=m_ret_gn_w, ret_proj=m_ret_proj,
                sgu_ln_w=m_sgu_ln_w, sgu_ln_b=m_sgu_ln_b, sgu_w_s=m_sgu_w_s, sgu_b_s=m_sgu_b_s,
                sgu_proj=m_sgu_proj, w_out=m_w_out, norm_ffn_w=m_norm_ffn_w, w_ffn_in=m_w_ffn_in,
                w_ffn_out=m_w_ffn_out, final_norm_w=m_final_norm_w)
    v_in = dict(norm_mix_w=v_norm_mix_w, w_in=v_w_in, ret_gn_w=v_ret_gn_w, ret_proj=v_ret_proj,
                sgu_ln_w=v_sgu_ln_w, sgu_ln_b=v_sgu_ln_b, sgu_w_s=v_sgu_w_s, sgu_b_s=v_sgu_b_s,
                sgu_proj=v_sgu_proj, w_out=v_w_out, norm_ffn_w=v_norm_ffn_w, w_ffn_in=v_w_ffn_in,
                w_ffn_out=v_w_ffn_out, final_norm_w=v_final_norm_w)

    depth = w_in.shape[0]
    _, s, d = x.shape
    w = d
    in_cols = 4 * w_in.shape[2]
    h = (in_cols - 4 * d) // (2 * RET_DK + 2 * RET_DV)
    groups = sgu_w_s.shape[1]
    assert in_cols == h * (2 * RET_DK + 2 * RET_DV) + 4 * d and (6 * h * RET_DK) % d == 0
    assert s % SGU_LEN == 0 and w % groups == 0 and (w // groups) % LANES == 0
    dims = (h, d, w)
    tables = _ret_tables(s, h, _tile(s, RET_BLOCK))
    mask = _sgu_mask()
    cx, cy, cc = lax.axis_index("x"), lax.axis_index("y"), lax.axis_index("c")
    place = jnp.stack([cc, 2 * (1 - cx) + cy, 2 * cx + (1 - cy), 2 * (1 - cx) + (1 - cy),
                       2 * cx + cy]).astype(jnp.int32)

    shard_shapes = {n: weights[n].shape[1:] for n in BIG}
    kinds = [BIG_KIND[n] for n in BIG]
    full = []
    for l in range(depth):
        gathered = _all_gather_weights([weights[n][l].astype(BF16) for n in BIG], kinds, "ag_weights_l%d" % l)
        full.append(dict(zip(BIG, gathered)))

    small = []
    for l in range(depth):
        sm = {n: weights[n][l] for n in SMALL}
        ws_m = jnp.where(mask[None], sgu_w_s[l], 0.0)
        sm["ws_m"] = ws_m.astype(BF16)
        sm["ws_mt"] = jnp.swapaxes(ws_m, 1, 2).astype(BF16)
        sm["bs"] = sgu_b_s[l][:, :, None]
        small.append(sm)

    xs = x[0]
    saved = []
    for l in range(depth):
        xs, sv = _layer_fwd(xs, full[l], small[l], tables, dims, "l%d" % l)
        saved.append(sv)
    dx, dxb, g_final, sq = _loss_head(xs, final_norm_w, loss_target[0])
    loss = lax.psum(sq[0, 0], ("x", "y", "c")) * (0.5 / d)

    grads_big = [None] * depth
    grads_small = [None] * depth
    for l in reversed(range(depth)):
        dx, dxb, gw, gs = _layer_bwd(dx, dxb, saved[l], full[l], small[l], tables, dims, "l%d" % l)
        grads_small[l] = gs
        grads_big[l] = _reduce_scatter_grads([gw[n] for n in BIG], kinds, [shard_shapes[n] for n in BIG],
                                             place, "l%d" % l)
    grad_x = dx[None]

    pieces = []
    for l in range(depth):
        gs = dict(grads_small[l])
        gs["sgu_w_s"] = jnp.where(mask[None], gs["sgu_w_s"], 0.0)
        pieces += [gs[n].reshape(-1) for n in SMALL]
    pieces.append(g_final.reshape(-1))
    flat = jnp.concatenate(pieces)
    total = flat.shape[0]
    rows = -(-total // (8 * LANES)) * 8
    flat = jnp.pad(flat, (0, rows * LANES - total)).reshape(rows, LANES)
    summed = _all_reduce_small(flat, "ar_small").reshape(-1)
    grad = {}
    off = 0
    per_layer = {n: [] for n in SMALL}
    for l in range(depth):
        for n in SMALL:
            shp = weights[n].shape[1:]
            size = math.prod(shp)
            per_layer[n].append(summed[off:off + size].reshape(shp))
            off += size
    for n in SMALL:
        grad[n] = jnp.stack(per_layer[n])
    grad["final_norm_w"] = summed[off:off + d]
    for i, n in enumerate(BIG):
        grad[n] = jnp.stack([grads_big[l][i] for l in range(depth)])

    delta, new_m, new_v = {}, {}, {}
    for n in BIG:
        shp = weights[n].shape
        two_d = lambda a: a.reshape(shp[0] * shp[1], shp[2])
        dl, mn, vn = _adamw(two_d(weights[n]), two_d(grad[n]), two_d(m_in[n]), two_d(v_in[n]), "adamw_" + n)
        delta[n], new_m[n], new_v[n] = dl.reshape(shp), mn.reshape(shp), vn.reshape(shp)
    small_names = SMALL + ["final_norm_w"]

    def pack(tree):
        fl = jnp.concatenate([tree[n].reshape(-1) for n in small_names])
        return jnp.pad(fl, (0, rows * LANES - fl.shape[0])).reshape(rows, LANES)

    dl, mn, vn = _adamw(pack(weights), pack(grad), pack(m_in), pack(v_in), "adamw_small")
    off = 0
    for n in small_names:
        shp = weights[n].shape
        size = math.prod(shp)
        for src, dst in ((dl, delta), (mn, new_m), (vn, new_v)):
            dst[n] = src.reshape(-1)[off:off + size].reshape(shp)
        off += size

    return (loss, grad_x, *[grad[n] for n in ORDER], *[delta[n] for n in ORDER],
            *[new_m[n] for n in ORDER], *[new_v[n] for n in ORDER])
```

```python
import math

import jax
import jax.numpy as jnp
from jax import lax
from jax.experimental import pallas as pl
from jax.experimental.pallas import tpu as pltpu

F32 = jnp.float32
BF16 = jnp.bfloat16

CHUNK = 64
RET_DK = 128
RET_DV = 256
SGU_LEN = 128
ROPE_BASE = 10000.0
EPS = 1e-6
ADAM_LR = 0.001
ADAM_B1 = 0.9
ADAM_B2 = 0.999
ADAM_EPS = 1e-08
ADAM_WD = 0.01
ADAM_STEP = 10

LANES = 128
VMEM_LIMIT = 56 * 1024 * 1024
RET_BLOCK = 256
SGU_BLOCK = 256
ROW_BLOCK = 256
MM_TILE = 1024
MM_KTILE = 2048
MESH = pl.DeviceIdType.MESH
ANY = pl.BlockSpec(memory_space=pl.ANY)
INV_SQRT2 = 1.0 / math.sqrt(2.0)
INV_SQRT_2PI = 1.0 / math.sqrt(2.0 * math.pi)

DN = {"nn": (((1,), (0,)), ((), ())), "nt": (((1,), (1,)), ((), ())), "tn": (((0,), (0,)), ((), ()))}


def _dot(a, b, mode="nn"):
    return lax.dot_general(a, b, DN[mode], preferred_element_type=F32)


def _tile(n, target):
    t = min(n, target) // LANES * LANES
    while t >= LANES:
        if n % t == 0:
            return t
        t -= LANES
    return n


def _rtile(n, target):
    t = min(n, target) // 16 * 16
    while t >= 16:
        if n % t == 0:
            return t
        t -= 16
    return n


def _params(sem):
    return pltpu.CompilerParams(dimension_semantics=sem, vmem_limit_bytes=VMEM_LIMIT)


def _sigmoid(x):
    return 1.0 / (1.0 + jnp.exp(-x))


def _gelu(x):
    return 0.5 * x * (1.0 + lax.erf(x * INV_SQRT2))


def _gelu_grad(x):
    return 0.5 * (1.0 + lax.erf(x * INV_SQRT2)) + x * jnp.exp(-0.5 * x * x) * INV_SQRT_2PI


def _matmul(a, b, mode, out_dtype, name, res=None, task=None):
    if mode == "nn":
        (m, k), n = a.shape, b.shape[1]
    elif mode == "nt":
        (m, k), n = a.shape, b.shape[0]
    else:
        (k, m), n = a.shape, b.shape[1]
    tm, tn, tk = _tile(m, MM_TILE), _tile(n, MM_TILE), _tile(k, MM_KTILE)
    ni, nj, nk = m // tm, n // tn, k // tk
    if mode == "tn":
        a_spec = pl.BlockSpec((tk, tm), lambda i, j, kk: (kk, i))
    else:
        a_spec = pl.BlockSpec((tm, tk), lambda i, j, kk: (i, kk))
    if mode == "nt":
        b_spec = pl.BlockSpec((tn, tk), lambda i, j, kk: (j, kk))
    else:
        b_spec = pl.BlockSpec((tk, tn), lambda i, j, kk: (kk, j))
    o_spec = pl.BlockSpec((tm, tn), lambda i, j, kk: (i, j))
    n_mm_in = 2 + (res is not None)
    t_ins = task.ins if task is not None else []
    t_outs = task.out_shapes if task is not None else []
    t_sems = task.sems if task is not None else []
    in_specs = [a_spec, b_spec] + ([o_spec] if res is not None else []) + [ANY] * len(t_ins)
    acc_in_out = out_dtype == F32
    scratch = [] if (nk == 1 or acc_in_out) else [pltpu.VMEM((tm, tn), F32)]

    def body(*refs):
        a_ref, b_ref = refs[0], refs[1]
        r_ref = refs[2] if res is not None else None
        tin = refs[n_mm_in:n_mm_in + len(t_ins)]
        o_ref = refs[n_mm_in + len(t_ins)]
        tout = refs[n_mm_in + len(t_ins) + 1:n_mm_in + len(t_ins) + 1 + len(t_outs)]
        rest = refs[n_mm_in + len(t_ins) + 1 + len(t_outs):]
        acc_scr, sems = (rest[0], rest[1:]) if scratch else (None, rest)
        i, j, kk = pl.program_id(0), pl.program_id(1), pl.program_id(2)
        if task is not None:
            @pl.when((i == 0) & (j == 0) & (kk == 0))
            def _():
                task.start(tin, tout, sems)

        p = _dot(a_ref[...], b_ref[...], mode)
        if nk == 1:
            if r_ref is not None:
                p = p + r_ref[...]
            o_ref[...] = p.astype(o_ref.dtype)
        else:
            acc = o_ref if acc_in_out else acc_scr

            @pl.when(kk == 0)
            def _():
                acc[...] = p if r_ref is None or not acc_in_out else p + r_ref[...]

            @pl.when(kk > 0)
            def _():
                acc[...] += p

            if not acc_in_out:
                @pl.when(kk == nk - 1)
                def _():
                    o = acc[...]
                    if r_ref is not None:
                        o = o + r_ref[...]
                    o_ref[...] = o.astype(o_ref.dtype)

        if task is not None:
            @pl.when((i == ni - 1) & (j == nj - 1) & (kk == nk - 1))
            def _():
                task.finish(tin, tout, sems)

    args = (a, b) + ((res,) if res is not None else ()) + tuple(t_ins)
    out_shape = jax.ShapeDtypeStruct((m, n), out_dtype)
    if task is None:
        return pl.pallas_call(
            body, name=name, grid=(ni, nj, nk), in_specs=in_specs, out_specs=o_spec, out_shape=out_shape,
            scratch_shapes=scratch, compiler_params=_params(("parallel", "parallel", "arbitrary")))(*args)
    aliases = {n_mm_in + src: 1 + dst for src, dst in task.aliases.items()}
    return pl.pallas_call(
        body, name=name, grid=(ni, nj, nk), in_specs=in_specs, out_specs=[o_spec] + [ANY] * len(t_outs),
        out_shape=[out_shape] + list(t_outs), input_output_aliases=aliases, scratch_shapes=scratch + list(t_sems),
        compiler_params=pltpu.CompilerParams(dimension_semantics=("arbitrary", "arbitrary", "arbitrary"),
                                             vmem_limit_bytes=VMEM_LIMIT, has_side_effects=True))(*args)


def _rms_fwd(x, w, name):
    s, d = x.shape
    tr = _tile(s, ROW_BLOCK)

    def body(x_ref, w_ref, o_ref):
        xv = x_ref[...]
        r = lax.rsqrt(jnp.mean(xv * xv, axis=-1, keepdims=True) + EPS)
        o_ref[...] = (xv * r * w_ref[...]).astype(BF16)

    row = pl.BlockSpec((tr, d), lambda i: (i, 0))
    vec = pl.BlockSpec((1, d), lambda i: (0, 0))
    return pl.pallas_call(body, name=name, grid=(s // tr,), in_specs=[row, vec], out_specs=row,
                          out_shape=jax.ShapeDtypeStruct((s, d), BF16),
                          compiler_params=_params(("parallel",)))(x, w.reshape(1, d))


def _rms_bwd(x, w, dh, dres, name):
    s, d = x.shape
    tr = _tile(s, ROW_BLOCK)

    def body(x_ref, w_ref, dh_ref, dr_ref, dx_ref, dxb_ref, dw_ref):
        xv = x_ref[...]
        r = lax.rsqrt(jnp.mean(xv * xv, axis=-1, keepdims=True) + EPS)
        xh = xv * r
        dy = dh_ref[...].astype(F32)
        dxh = dy * w_ref[...]
        dx = dr_ref[...] + r * (dxh - xh * jnp.mean(dxh * xh, axis=-1, keepdims=True))
        dx_ref[...] = dx
        dxb_ref[...] = dx.astype(BF16)

        @pl.when(pl.program_id(0) == 0)
        def _():
            dw_ref[...] = jnp.zeros_like(dw_ref)

        dw_ref[...] += jnp.sum(dy * xh, axis=0, keepdims=True)

    row = pl.BlockSpec((tr, d), lambda i: (i, 0))
    vec = pl.BlockSpec((1, d), lambda i: (0, 0))
    return pl.pallas_call(
        body, name=name, grid=(s // tr,), in_specs=[row, vec, row, row], out_specs=[row, row, vec],
        out_shape=[jax.ShapeDtypeStruct((s, d), F32), jax.ShapeDtypeStruct((s, d), BF16),
                   jax.ShapeDtypeStruct((1, d), F32)],
        compiler_params=_params(("arbitrary",)))(x, w.reshape(1, d), dh, dres)


def _loss_head(x, w, tgt):
    s, d = x.shape
    tr = _tile(s, ROW_BLOCK)

    def body(x_ref, w_ref, t_ref, dx_ref, dxb_ref, dw_ref, l_ref):
        xv = x_ref[...]
        r = lax.rsqrt(jnp.mean(xv * xv, axis=-1, keepdims=True) + EPS)
        xh = xv * r
        e = xh * w_ref[...] - t_ref[...]
        dy = e * (1.0 / d)
        dxh = dy * w_ref[...]
        dx = r * (dxh - xh * jnp.mean(dxh * xh, axis=-1, keepdims=True))
        dx_ref[...] = dx
        dxb_ref[...] = dx.astype(BF16)

        @pl.when(pl.program_id(0) == 0)
        def _():
            dw_ref[...] = jnp.zeros_like(dw_ref)
            l_ref[...] = jnp.zeros_like(l_ref)

        dw_ref[...] += jnp.sum(dy * xh, axis=0, keepdims=True)
        l_ref[...] += jnp.sum(jnp.sum(e * e, axis=1, keepdims=True), axis=0, keepdims=True)

    row = pl.BlockSpec((tr, d), lambda i: (i, 0))
    vec = pl.BlockSpec((1, d), lambda i: (0, 0))
    one = pl.BlockSpec((1, 1), lambda i: (0, 0))
    return pl.pallas_call(
        body, name="loss_head", grid=(s // tr,), in_specs=[row, vec, row], out_specs=[row, row, vec, one],
        out_shape=[jax.ShapeDtypeStruct((s, d), F32), jax.ShapeDtypeStruct((s, d), BF16),
                   jax.ShapeDtypeStruct((1, d), F32), jax.ShapeDtypeStruct((1, 1), F32)],
        compiler_params=_params(("arbitrary",)))(x, w.reshape(1, d), tgt)


def _ret_tables(s, h, t):
    half = RET_DK // 2
    inv = ROPE_BASE ** (-jnp.arange(half, dtype=F32) / half)
    ang = jnp.arange(s, dtype=F32)[:, None] * inv[None, :]
    cos, sin = jnp.cos(ang), jnp.sin(ang)
    cosf = jnp.concatenate([cos, cos], axis=1)
    sinf = jnp.concatenate([-sin, sin], axis=1)
    log_g = jnp.log1p(-(2.0 ** (-5.0 - jnp.arange(h, dtype=F32))))
    idx = jnp.arange(t, dtype=F32)
    chunk = jnp.arange(t) // CHUNK
    allowed = chunk[None, :] <= chunk[:, None]
    dm = jnp.where(allowed[None], jnp.exp(log_g[:, None, None] * jnp.abs(idx[:, None] - idx[None, :])), 0.0)
    qd = jnp.exp(log_g[:, None] * (idx[None, :] + 1.0))
    kd = jnp.exp(log_g[:, None] * (t - 1.0 - idx[None, :]))
    qd = jnp.broadcast_to(qd[:, :, None], (h, t, RET_DK))
    kd = jnp.broadcast_to(kd[:, :, None], (h, t, RET_DK))
    cd = jnp.broadcast_to(jnp.exp(log_g * t)[:, None, None], (h, 1, RET_DV))
    return cosf, sinf, dm, qd, kd, cd


def _rot(x, cos, sin):
    return x * cos + pltpu.roll(x, RET_DK // 2, 1) * sin


def _rot_t(x, cos, sin):
    return x * cos - pltpu.roll(x, RET_DK // 2, 1) * sin


def _ret_in_specs(h, t, rev_nb=None):
    blk = (lambda b: b) if rev_nb is None else (lambda b: rev_nb - 1 - b)
    return [
        pl.BlockSpec((t, RET_DK), lambda hh, b: (blk(b), hh)),
        pl.BlockSpec((t, RET_DK), lambda hh, b: (blk(b), h + hh)),
        pl.BlockSpec((t, RET_DV), lambda hh, b: (blk(b), h + hh)),
        pl.BlockSpec((t, RET_DV), lambda hh, b: (blk(b), 2 * h + hh)),
        pl.BlockSpec((t, RET_DK), lambda hh, b: (blk(b), 0)),
        pl.BlockSpec((t, RET_DK), lambda hh, b: (blk(b), 0)),
        pl.BlockSpec((1, t, t), lambda hh, b: (hh, 0, 0)),
        pl.BlockSpec((1, t, RET_DK), lambda hh, b: (hh, 0, 0)),
        pl.BlockSpec((1, t, RET_DK), lambda hh, b: (hh, 0, 0)),
        pl.BlockSpec((1, 1, RET_DV), lambda hh, b: (hh, 0, 0)),
        pl.BlockSpec((1, RET_DV), lambda hh, b: (0, hh)),
    ]


def _ret_fwd(z, gn_w, tables, h, name):
    s = z.shape[0]
    t = _tile(s, RET_BLOCK)
    nb = s // t
    scale = RET_DK ** -0.5

    def body(q_ref, k_ref, v_ref, g_ref, cos_ref, sin_ref, dm_ref, qd_ref, kd_ref, cd_ref, gn_ref,
             o_ref, st_ref, st_scr):
        @pl.when(pl.program_id(1) == 0)
        def _():
            st_scr[...] = jnp.zeros_like(st_scr)

        cos, sin = cos_ref[...], sin_ref[...]
        qf = _rot(q_ref[...].astype(F32), cos, sin) * scale
        kf = _rot(k_ref[...].astype(F32), cos, sin)
        vb = v_ref[...]
        p = _dot(qf.astype(BF16), kf.astype(BF16), "nt") * dm_ref[0]
        st = st_scr[...]
        stb = st.astype(BF16)
        st_ref[0, 0] = stb
        o = _dot(p.astype(BF16), vb) + _dot((qf * qd_ref[0]).astype(BF16), stb)
        st_scr[...] = st * cd_ref[0] + _dot((kf * kd_ref[0]).astype(BF16), vb, "tn")
        dlt = o - jnp.mean(o, axis=-1, keepdims=True)
        oh = dlt * lax.rsqrt(jnp.mean(dlt * dlt, axis=-1, keepdims=True) + EPS)
        g = g_ref[...].astype(F32)
        o_ref[...] = (g * _sigmoid(g) * oh * gn_ref[...]).astype(BF16)

    return pl.pallas_call(
        body, name=name, grid=(h, nb), in_specs=_ret_in_specs(h, t),
        out_specs=[pl.BlockSpec((t, RET_DV), lambda hh, b: (b, hh)),
                   pl.BlockSpec((1, 1, RET_DK, RET_DV), lambda hh, b: (b, hh, 0, 0))],
        out_shape=[jax.ShapeDtypeStruct((s, h * RET_DV), BF16),
                   jax.ShapeDtypeStruct((nb, h, RET_DK, RET_DV), BF16)],
        scratch_shapes=[pltpu.VMEM((RET_DK, RET_DV), F32)],
        compiler_params=_params(("parallel", "arbitrary")))(z, z, z, z, *tables, gn_w.reshape(1, -1))


def _ret_bwd(z, dga, states, gn_w, tables, h, name):
    s = z.shape[0]
    t = _tile(s, RET_BLOCK)
    nb = s // t
    scale = RET_DK ** -0.5

    def body(q_ref, k_ref, v_ref, g_ref, cos_ref, sin_ref, dm_ref, qd_ref, kd_ref, cd_ref, gn_ref,
             dga_ref, st_ref, dq_ref, dk_ref, dv_ref, dg_ref, dgn_ref, dst_scr):
        @pl.when(pl.program_id(1) == 0)
        def _():
            dst_scr[...] = jnp.zeros_like(dst_scr)
            dgn_ref[...] = jnp.zeros_like(dgn_ref)

        cos, sin = cos_ref[...], sin_ref[...]
        dm = dm_ref[0]
        qf = _rot(q_ref[...].astype(F32), cos, sin) * scale
        kf = _rot(k_ref[...].astype(F32), cos, sin)
        qb, kb, vb = qf.astype(BF16), kf.astype(BF16), v_ref[...]
        qdb = (qf * qd_ref[0]).astype(BF16)
        kdb = (kf * kd_ref[0]).astype(BF16)
        stb = st_ref[0, 0]
        pb = (_dot(qb, kb, "nt") * dm).astype(BF16)
        o = _dot(pb, vb) + _dot(qdb, stb)
        dlt = o - jnp.mean(o, axis=-1, keepdims=True)
        rstd = lax.rsqrt(jnp.mean(dlt * dlt, axis=-1, keepdims=True) + EPS)
        oh = dlt * rstd
        gn = gn_ref[...]
        g = g_ref[...].astype(F32)
        sg = _sigmoid(g)
        dga_v = dga_ref[...].astype(F32)
        dret = dga_v * g * sg
        dg_ref[...] = (dga_v * oh * gn * sg * (1.0 + g * (1.0 - sg))).astype(BF16)
        dgn_ref[...] += jnp.sum(dret * oh, axis=0, keepdims=True)
        doh = dret * gn
        do = rstd * (doh - jnp.mean(doh, axis=-1, keepdims=True)
                     - oh * jnp.mean(doh * oh, axis=-1, keepdims=True))
        dob = do.astype(BF16)
        dst = dst_scr[...]
        dstb = dst.astype(BF16)
        dv_ref[...] = (_dot(pb, dob, "tn") + _dot(kdb, dstb)).astype(BF16)
        dpb = (_dot(dob, vb, "nt") * dm).astype(BF16)
        dqf = _dot(dpb, kb) + _dot(dob, stb, "nt") * qd_ref[0]
        dkf = _dot(dpb, qb, "tn") + _dot(vb, dstb, "nt") * kd_ref[0]
        dst_scr[...] = dst * cd_ref[0] + _dot(qdb, dob, "tn")
        dq_ref[...] = _rot_t(dqf * scale, cos, sin).astype(BF16)
        dk_ref[...] = _rot_t(dkf, cos, sin).astype(BF16)

    rb = lambda hh, b: (nb - 1 - b, hh)
    in_specs = _ret_in_specs(h, t, rev_nb=nb) + [
        pl.BlockSpec((t, RET_DV), rb),
        pl.BlockSpec((1, 1, RET_DK, RET_DV), lambda hh, b: (nb - 1 - b, hh, 0, 0))]
    return pl.pallas_call(
        body, name=name, grid=(h, nb), in_specs=in_specs,
        out_specs=[pl.BlockSpec((t, RET_DK), rb), pl.BlockSpec((t, RET_DK), rb),
                   pl.BlockSpec((t, RET_DV), rb), pl.BlockSpec((t, RET_DV), rb),
                   pl.BlockSpec((1, RET_DV), lambda hh, b: (0, hh))],
        out_shape=[jax.ShapeDtypeStruct((s, h * RET_DK), BF16), jax.ShapeDtypeStruct((s, h * RET_DK), BF16),
                   jax.ShapeDtypeStruct((s, h * RET_DV), BF16), jax.ShapeDtypeStruct((s, h * RET_DV), BF16),
                   jax.ShapeDtypeStruct((1, h * RET_DV), F32)],
        scratch_shapes=[pltpu.VMEM((RET_DK, RET_DV), F32)],
        compiler_params=_params(("parallel", "arbitrary")))(z, z, z, z, *tables, gn_w.reshape(1, -1), dga, states)


def _sgu_fwd(z, ln_w, ln_b, ws_m, bs, col0, w, name):
    s = z.shape[0]
    t = _tile(s, SGU_BLOCK)
    groups = ws_m.shape[0]
    ch = w // groups
    cb = col0 // w

    def body(su_ref, sv_ref, lw_ref, lb_ref, ws_ref, bs_ref, o_ref):
        zv = _gelu(sv_ref[...].astype(F32))
        dlt = zv - jnp.mean(zv, axis=-1, keepdims=True)
        vn = dlt * lax.rsqrt(jnp.mean(dlt * dlt, axis=-1, keepdims=True) + EPS) * lw_ref[...] + lb_ref[...]
        vnb = vn.astype(BF16)
        for r in range(t // SGU_LEN):
            rows = slice(r * SGU_LEN, (r + 1) * SGU_LEN)
            for gi in range(groups):
                cols = slice(gi * ch, (gi + 1) * ch)
                mixed = _dot(ws_ref[gi], vnb[rows, cols]) + bs_ref[gi]
                o_ref[rows, cols] = (_gelu(su_ref[rows, cols].astype(F32)) * mixed).astype(BF16)

    row = lambda off: pl.BlockSpec((t, w), lambda i: (i, cb + off))
    vec = pl.BlockSpec((1, w), lambda i: (0, 0))
    return pl.pallas_call(
        body, name=name, grid=(s // t,),
        in_specs=[row(0), row(1), vec, vec,
                  pl.BlockSpec((groups, SGU_LEN, SGU_LEN), lambda i: (0, 0, 0)),
                  pl.BlockSpec((groups, SGU_LEN, 1), lambda i: (0, 0, 0))],
        out_specs=pl.BlockSpec((t, w), lambda i: (i, 0)),
        out_shape=jax.ShapeDtypeStruct((s, w), BF16),
        compiler_params=_params(("parallel",)))(z, z, ln_w.reshape(1, w), ln_b.reshape(1, w), ws_m, bs)


def _sgu_bwd(z, dsg, ln_w, ln_b, ws_m, ws_mt, bs, col0, w, name):
    s = z.shape[0]
    t = _tile(s, SGU_BLOCK)
    groups = ws_m.shape[0]
    ch = w // groups
    cb = col0 // w

    def body(su_ref, sv_ref, dsg_ref, lw_ref, lb_ref, ws_ref, wst_ref, bs_ref,
             dsu_ref, dsv_ref, dlw_ref, dlb_ref, dws_ref, dbs_ref, dvn_scr):
        @pl.when(pl.program_id(0) == 0)
        def _():
            dlw_ref[...] = jnp.zeros_like(dlw_ref)
            dlb_ref[...] = jnp.zeros_like(dlb_ref)
            dws_ref[...] = jnp.zeros_like(dws_ref)
            dbs_ref[...] = jnp.zeros_like(dbs_ref)

        sv = sv_ref[...].astype(F32)
        zv = _gelu(sv)
        dlt = zv - jnp.mean(zv, axis=-1, keepdims=True)
        rstd = lax.rsqrt(jnp.mean(dlt * dlt, axis=-1, keepdims=True) + EPS)
        vh = dlt * rstd
        vnb = (vh * lw_ref[...] + lb_ref[...]).astype(BF16)
        for r in range(t // SGU_LEN):
            rows = slice(r * SGU_LEN, (r + 1) * SGU_LEN)
            for gi in range(groups):
                cols = slice(gi * ch, (gi + 1) * ch)
                vn_p = vnb[rows, cols]
                mixed = _dot(ws_ref[gi], vn_p) + bs_ref[gi]
                su = su_ref[rows, cols].astype(F32)
                dsg_p = dsg_ref[rows, cols].astype(F32)
                dsu_ref[rows, cols] = (dsg_p * mixed * _gelu_grad(su)).astype(BF16)
                dmix = dsg_p * _gelu(su)
                dmixb = dmix.astype(BF16)
                dvn_scr[rows, cols] = _dot(wst_ref[gi], dmixb)
                dws_ref[gi] += _dot(dmixb, vn_p, "nt")
                dbs_ref[gi] += jnp.sum(dmix, axis=1, keepdims=True)
        dvn = dvn_scr[...]
        dlw_ref[...] += jnp.sum(dvn * vh, axis=0, keepdims=True)
        dlb_ref[...] += jnp.sum(dvn, axis=0, keepdims=True)
        dvh = dvn * lw_ref[...]
        dzv = rstd * (dvh - jnp.mean(dvh, axis=-1, keepdims=True)
                      - vh * jnp.mean(dvh * vh, axis=-1, keepdims=True))
        dsv_ref[...] = (dzv * _gelu_grad(sv)).astype(BF16)

    row = lambda off: pl.BlockSpec((t, w), lambda i: (i, cb + off))
    out_row = pl.BlockSpec((t, w), lambda i: (i, 0))
    vec = pl.BlockSpec((1, w), lambda i: (0, 0))
    mat = pl.BlockSpec((groups, SGU_LEN, SGU_LEN), lambda i: (0, 0, 0))
    col = pl.BlockSpec((groups, SGU_LEN, 1), lambda i: (0, 0, 0))
    return pl.pallas_call(
        body, name=name, grid=(s // t,),
        in_specs=[row(0), row(1), out_row, vec, vec, mat, mat, col],
        out_specs=[out_row, out_row, vec, vec, mat, col],
        out_shape=[jax.ShapeDtypeStruct((s, w), BF16), jax.ShapeDtypeStruct((s, w), BF16),
                   jax.ShapeDtypeStruct((1, w), F32), jax.ShapeDtypeStruct((1, w), F32),
                   jax.ShapeDtypeStruct((groups, SGU_LEN, SGU_LEN), F32),
                   jax.ShapeDtypeStruct((groups, SGU_LEN, 1), F32)],
        scratch_shapes=[pltpu.VMEM((t, w), F32)],
        compiler_params=_params(("arbitrary",)))(z, z, dsg, ln_w.reshape(1, w), ln_b.reshape(1, w), ws_m, ws_mt, bs)


def _merge_fwd(a, b, z, col0, name):
    s, d = a.shape
    tr = _tile(s, ROW_BLOCK)
    cb = col0 // d

    def body(a_ref, b_ref, ga_ref, gb_ref, o_ref):
        o_ref[...] = (_sigmoid(ga_ref[...].astype(F32)) * a_ref[...].astype(F32)
                      + _sigmoid(gb_ref[...].astype(F32)) * b_ref[...].astype(F32)).astype(BF16)

    row = pl.BlockSpec((tr, d), lambda i: (i, 0))
    gate = lambda off: pl.BlockSpec((tr, d), lambda i: (i, cb + off))
    return pl.pallas_call(body, name=name, grid=(s // tr,), in_specs=[row, row, gate(0), gate(1)],
                          out_specs=row, out_shape=jax.ShapeDtypeStruct((s, d), BF16),
                          compiler_params=_params(("parallel",)))(a, b, z, z)


def _merge_bwd(dmg, a, b, z, col0, name):
    s, d = a.shape
    tr = _tile(s, ROW_BLOCK)
    cb = col0 // d

    def body(dm_ref, a_ref, b_ref, ga_ref, gb_ref, da_ref, db_ref, dgt_ref):
        dm = dm_ref[...].astype(F32)
        sa = _sigmoid(ga_ref[...].astype(F32))
        sb = _sigmoid(gb_ref[...].astype(F32))
        da_ref[...] = (dm * sa).astype(BF16)
        db_ref[...] = (dm * sb).astype(BF16)
        dgt_ref[:, :d] = (dm * a_ref[...].astype(F32) * sa * (1.0 - sa)).astype(BF16)
        dgt_ref[:, d:] = (dm * b_ref[...].astype(F32) * sb * (1.0 - sb)).astype(BF16)

    row = pl.BlockSpec((tr, d), lambda i: (i, 0))
    wide = pl.BlockSpec((tr, 2 * d), lambda i: (i, 0))
    gate = lambda off: pl.BlockSpec((tr, d), lambda i: (i, cb + off))
    return pl.pallas_call(
        body, name=name, grid=(s // tr,), in_specs=[row, row, row, gate(0), gate(1)],
        out_specs=[row, row, wide],
        out_shape=[jax.ShapeDtypeStruct((s, d), BF16), jax.ShapeDtypeStruct((s, d), BF16),
                   jax.ShapeDtypeStruct((s, 2 * d), BF16)],
        compiler_params=_params(("parallel",)))(dmg, a, b, z, z)


def _swiglu_fwd(ac, name):
    s, f2 = ac.shape
    f = f2 // 2
    tr = _tile(s, ROW_BLOCK)

    def body(a_ref, c_ref, o_ref):
        a = a_ref[...].astype(F32)
        o_ref[...] = (a * _sigmoid(a) * c_ref[...].astype(F32)).astype(BF16)

    half = lambda off: pl.BlockSpec((tr, f), lambda i: (i, off))
    return pl.pallas_call(body, name=name, grid=(s // tr,), in_specs=[half(0), half(1)], out_specs=half(0),
                          out_shape=jax.ShapeDtypeStruct((s, f), BF16),
                          compiler_params=_params(("parallel",)))(ac, ac)


def _swiglu_bwd(ac, df, name):
    s, f2 = ac.shape
    f = f2 // 2
    tr = _tile(s, ROW_BLOCK)

    def body(a_ref, c_ref, df_ref, o_ref):
        a = a_ref[...].astype(F32)
        sg = _sigmoid(a)
        dfv = df_ref[...].astype(F32)
        o_ref[:, :f] = (dfv * c_ref[...].astype(F32) * sg * (1.0 + a * (1.0 - sg))).astype(BF16)
        o_ref[:, f:] = (dfv * a * sg).astype(BF16)

    half = lambda off: pl.BlockSpec((tr, f), lambda i: (i, off))
    return pl.pallas_call(body, name=name, grid=(s // tr,), in_specs=[half(0), half(1), half(0)],
                          out_specs=pl.BlockSpec((tr, f2), lambda i: (i, 0)),
                          out_shape=jax.ShapeDtypeStruct((s, f2), BF16),
                          compiler_params=_params(("parallel",)))(ac, ac, df)


def _adamw(w, g, m, v, name):
    r, c = w.shape
    tr = _rtile(r, LANES)
    c1 = 1.0 - ADAM_B1 ** ADAM_STEP
    c2 = 1.0 - ADAM_B2 ** ADAM_STEP

    def body(w_ref, g_ref, m_ref, v_ref, d_ref, mo_ref, vo_ref):
        gv = g_ref[...]
        mn = ADAM_B1 * m_ref[...] + (1.0 - ADAM_B1) * gv
        vn = ADAM_B2 * v_ref[...] + (1.0 - ADAM_B2) * (gv * gv)
        mo_ref[...] = mn
        vo_ref[...] = vn
        d_ref[...] = -ADAM_LR * ((mn / c1) / (jnp.sqrt(vn / c2) + ADAM_EPS) + ADAM_WD * w_ref[...])

    blk = pl.BlockSpec((tr, c), lambda i: (i, 0))
    shp = jax.ShapeDtypeStruct((r, c), F32)
    return pl.pallas_call(body, name=name, grid=(r // tr,), in_specs=[blk] * 4, out_specs=[blk] * 3,
                          out_shape=[shp] * 3, compiler_params=_params(("parallel",)))(w, g, m, v)


def _place():
    x, y, c = lax.axis_index("x"), lax.axis_index("y"), lax.axis_index("c")
    chips = [(1 - x, y), (x, 1 - y), (1 - x, 1 - y)]
    return x, y, c, chips


def _block(ref, kind, chip, half, shard_shape):
    rs, cs = shard_shape
    if kind == "col":
        rows = pl.ds(0, rs) if half is None else pl.ds(half * (rs // 2), rs // 2)
        return ref.at[rows, pl.ds(chip * cs, cs)]
    rows = pl.ds(chip * rs, rs) if half is None else pl.ds(chip * rs + half * (rs // 2), rs // 2)
    return ref.at[rows, :]


def _half_rows(ref, half):
    rs = ref.shape[0]
    return ref.at[pl.ds(half * (rs // 2), rs // 2), :]


class _Task:
    def __init__(self, ins, out_shapes, sems, start, finish, aliases=None):
        self.ins, self.out_shapes, self.sems = list(ins), list(out_shapes), list(sems)
        self.start, self.finish, self.aliases = start, finish, dict(aliases or {})


def _run_task(task, name):
    n_in, n_out = len(task.ins), len(task.out_shapes)

    def body(*refs):
        ins, outs, sems = refs[:n_in], refs[n_in:n_in + n_out], refs[n_in + n_out:]
        task.start(ins, outs, sems)
        task.finish(ins, outs, sems)

    return pl.pallas_call(
        body, name=name, in_specs=[ANY] * n_in, out_specs=[ANY] * n_out, out_shape=task.out_shapes,
        input_output_aliases=task.aliases, scratch_shapes=task.sems,
        compiler_params=pltpu.CompilerParams(has_side_effects=True))(*task.ins)


def _remote(src, dst, send_sem, recv_sem, device):
    return pltpu.make_async_remote_copy(src_ref=src, dst_ref=dst, send_sem=send_sem, recv_sem=recv_sem,
                                        device_id=device, device_id_type=MESH)


def _ag_task(shards, kinds):
    n = len(shards)
    out_shapes = [jax.ShapeDtypeStruct((s.shape[0], 4 * s.shape[1]) if k == "col" else (4 * s.shape[0], s.shape[1]),
                                       s.dtype) for s, k in zip(shards, kinds)]

    def copies(ins, outs, sems):
        send_sems, recv_sems, fsend_sems, frecv_sems, own_send_sems, own_recv_sems = sems
        x, y, c, chips = _place()
        me = 2 * x + y
        own, ici, landed, fwd, passed = [], [], [], [], []
        for i in range(n):
            shp = ins[i].shape
            own.append(_remote(ins[i], _block(outs[i], kinds[i], me, None, shp), own_send_sems.at[i],
                               own_recv_sems.at[i], (x, y, 1 - c)))
            for j, (px, py) in enumerate(chips):
                k = 3 * i + j
                ici.append(_remote(_half_rows(ins[i], c), _block(outs[i], kinds[i], me, c, shp),
                                   send_sems.at[k], recv_sems.at[k], (px, py, c)))
                got = _block(outs[i], kinds[i], 2 * px + py, c, shp)
                landed.append(_remote(got, got, send_sems.at[k], recv_sems.at[k], (px, py, c)))
                fwd.append(_remote(got, got, fsend_sems.at[k], frecv_sems.at[k], (x, y, 1 - c)))
                theirs = _block(outs[i], kinds[i], 2 * px + py, 1 - c, shp)
                passed.append(_remote(theirs, theirs, fsend_sems.at[k], frecv_sems.at[k], (x, y, 1 - c)))
        return own, ici, landed, fwd, passed

    def start(ins, outs, sems):
        own, ici, _, _, _ = copies(ins, outs, sems)
        for cp in own + ici:
            cp.start()

    def finish(ins, outs, sems):
        own, ici, landed, fwd, passed = copies(ins, outs, sems)
        for got, cp in zip(landed, fwd):
            got.wait_recv()
            cp.start()
        for cp in passed:
            cp.wait_recv()
        for cp in own:
            cp.wait()
        for cp in ici + fwd:
            cp.wait_send()

    sems = [pltpu.SemaphoreType.DMA((3 * n,))] * 4 + [pltpu.SemaphoreType.DMA((n,))] * 2
    return _Task(shards, out_shapes, sems, start, finish)


def _exchange_task(grads, kinds, shard_shapes):
    n = len(grads)
    out_shapes = [jax.ShapeDtypeStruct((4, rs // 2, cs), F32) for rs, cs in shard_shapes]

    def copies(ins, outs, sems):
        send_sems, recv_sems = sems
        x, y, c, _ = _place()
        return [_remote(_block(ins[i], kinds[i], q, 1 - c, shard_shapes[i]), outs[i].at[q],
                        send_sems.at[4 * i + q], recv_sems.at[4 * i + q], (x, y, 1 - c))
                for i in range(n) for q in range(4)]

    def start(ins, outs, sems):
        for cp in copies(ins, outs, sems):
            cp.start()

    def finish(ins, outs, sems):
        for cp in copies(ins, outs, sems):
            cp.wait()

    return _Task(grads, out_shapes, [pltpu.SemaphoreType.DMA((4 * n,))] * 2, start, finish)


def _grad_block_map(kind, nt):
    if kind == "col":
        return lambda j, t, p: (p[0] * nt + t, p[1 + j])
    return lambda j, t, p: ((p[1 + j] * 2 + p[0]) * nt + t, 0)


def _chip_sum(grad, sib, kind, shard_shape, place, name):
    rs, cs = shard_shape
    hr = rs // 2
    tr = _rtile(hr, 256)
    nt = hr // tr
    g_map = _grad_block_map(kind, nt)

    def body(p_ref, g_ref, s_ref, o_ref):
        o_ref[0] = (g_ref[...] + s_ref[0]).astype(BF16)

    return pl.pallas_call(
        body, name=name,
        grid_spec=pltpu.PrefetchScalarGridSpec(
            num_scalar_prefetch=1, grid=(3, nt),
            in_specs=[pl.BlockSpec((tr, cs), g_map),
                      pl.BlockSpec((1, tr, cs), lambda j, t, p: (p[1 + j], t, 0))],
            out_specs=pl.BlockSpec((1, tr, cs), lambda j, t, p: (j, t, 0))),
        out_shape=jax.ShapeDtypeStruct((3, hr, cs), BF16),
        compiler_params=_params(("arbitrary", "arbitrary")))(place, grad, sib)


def _scatter_task(parts):
    n = len(parts)

    def copies(ins, outs, sems):
        send_sems, recv_sems = sems
        _, _, c, chips = _place()
        return [_remote(ins[i].at[j], outs[i].at[j], send_sems.at[3 * i + j], recv_sems.at[3 * i + j], (px, py, c))
                for i in range(n) for j, (px, py) in enumerate(chips)]

    def start(ins, outs, sems):
        for cp in copies(ins, outs, sems):
            cp.start()

    def finish(ins, outs, sems):
        for cp in copies(ins, outs, sems):
            cp.wait()

    return _Task(parts, [jax.ShapeDtypeStruct(p.shape, p.dtype) for p in parts],
                 [pltpu.SemaphoreType.DMA((3 * n,))] * 2, start, finish)


def _final_sum(grad, sib, recv, kind, shard_shape, place, name):
    rs, cs = shard_shape
    hr = rs // 2
    tr = _rtile(hr, 256)
    nt = hr // tr
    g_map = _grad_block_map(kind, nt)

    def body(p_ref, g_ref, s_ref, r_ref, out_ref):
        acc = g_ref[...] + s_ref[0]
        for j in range(3):
            acc = acc + r_ref[j].astype(F32)
        out_ref[...] = acc

    return pl.pallas_call(
        body, name=name,
        grid_spec=pltpu.PrefetchScalarGridSpec(
            num_scalar_prefetch=1, grid=(nt,),
            in_specs=[pl.BlockSpec((tr, cs), lambda t, p: g_map(3, t, p)),
                      pl.BlockSpec((1, tr, cs), lambda t, p: (p[4], t, 0)),
                      pl.BlockSpec((3, tr, cs), lambda t, p: (0, t, 0))],
            out_specs=pl.BlockSpec((tr, cs), lambda t, p: (p[0] * nt + t, 0))),
        out_shape=jax.ShapeDtypeStruct((rs, cs), F32),
        compiler_params=_params(("arbitrary",)))(place, grad, sib, recv)


def _join_task(shards):
    n = len(shards)

    def copies(outs, sems):
        send_sems, recv_sems = sems
        x, y, c, _ = _place()
        mine = [_half_rows(outs[i], c) for i in range(n)]
        theirs = [_half_rows(outs[i], 1 - c) for i in range(n)]
        send = [_remote(mine[i], mine[i], send_sems.at[i], recv_sems.at[i], (x, y, 1 - c)) for i in range(n)]
        recv = [_remote(theirs[i], theirs[i], send_sems.at[i], recv_sems.at[i], (x, y, 1 - c)) for i in range(n)]
        return send, recv

    def start(ins, outs, sems):
        for cp in copies(outs, sems)[0]:
            cp.start()

    def finish(ins, outs, sems):
        send, recv = copies(outs, sems)
        for cp in send:
            cp.wait_send()
        for cp in recv:
            cp.wait_recv()

    return _Task(shards, [jax.ShapeDtypeStruct(s.shape, F32) for s in shards],
                 [pltpu.SemaphoreType.DMA((n,))] * 2, start, finish, aliases={i: i for i in range(n)})


def _all_reduce_small(v, name):
    rows = v.shape[0]

    def body(v_ref, o_ref, buf, send_sems, recv_sems):
        x, y, c, _ = _place()
        coord = lambda p: ((1 - x) if p & 4 else x, (1 - y) if p & 2 else y, (1 - c) if p & 1 else c)
        me = 4 * x + 2 * y + c
        buf[me] = v_ref[...]
        copies = []
        for p in range(1, 8):
            cp = _remote(v_ref, buf.at[me], send_sems.at[p - 1], recv_sems.at[p - 1], coord(p))
            cp.start()
            copies.append(cp)
        for p in range(1, 8):
            px, py, pc = coord(p)
            _remote(v_ref, buf.at[4 * px + 2 * py + pc], send_sems.at[p - 1], recv_sems.at[p - 1],
                    coord(p)).wait_recv()
        for cp in copies:
            cp.wait_send()
        acc = buf[0]
        for dev in range(1, 8):
            acc = acc + buf[dev]
        o_ref[...] = acc

    vm = pl.BlockSpec(memory_space=pltpu.VMEM)
    return pl.pallas_call(
        body, name=name, in_specs=[vm], out_specs=vm, out_shape=jax.ShapeDtypeStruct(v.shape, F32),
        scratch_shapes=[pltpu.VMEM((8, rows, LANES), F32), pltpu.SemaphoreType.DMA((7,)),
                        pltpu.SemaphoreType.DMA((7,))],
        compiler_params=pltpu.CompilerParams(vmem_limit_bytes=VMEM_LIMIT))(v)


def _reduce_scatter_grads(grads, kinds, shard_shapes, place, tag):
    sib = _run_task(_exchange_task(grads, kinds, shard_shapes), "rs_core_exchange_" + tag)
    peers = [_chip_sum(g, sib[i], kinds[i], shard_shapes[i], place, "rs_chip_sum%d_%s" % (i, tag))
             for i, g in enumerate(grads)]
    recv = _run_task(_scatter_task(peers), "rs_scatter_" + tag)
    halves = [_final_sum(g, sib[i], recv[i], kinds[i], shard_shapes[i], place, "rs_final_sum%d_%s" % (i, tag))
              for i, g in enumerate(grads)]
    return _run_task(_join_task(halves), "rs_core_join_" + tag)


BIG = ["w_in", "ret_proj", "sgu_proj", "w_out", "w_ffn_in", "w_ffn_out"]
BIG_KIND = {"w_in": "col", "ret_proj": "row", "sgu_proj": "row", "w_out": "row", "w_ffn_in": "col",
            "w_ffn_out": "row"}
KINDS = [BIG_KIND[n] for n in BIG]
SMALL = ["norm_mix_w", "ret_gn_w", "sgu_ln_w", "sgu_ln_b", "sgu_w_s", "sgu_b_s", "norm_ffn_w"]
ORDER = ["norm_mix_w", "w_in", "ret_gn_w", "ret_proj", "sgu_ln_w", "sgu_ln_b", "sgu_w_s", "sgu_b_s",
         "sgu_proj", "w_out", "norm_ffn_w", "w_ffn_in", "w_ffn_out", "final_norm_w"]
AG_BEHIND = {"mm_in": [0], "mm_ffn_in": [4, 5], "mm_ffn_out": [1, 2, 3]}
EXCHANGE_BEHIND = {"mm_dffn_out_x": [0], "mm_dffn_out_w": [1, 2, 3, 4, 5]}
SCATTER_BEHIND = {"mm_dffn_in_x": [0, 1, 2, 3], "mm_dffn_in_w": [4, 5]}


def _layer_fwd(x, wt, sm, tables, dims, tag, next_shards):
    h, d, w = dims
    c_su, c_gate = 6 * h * RET_DK, 6 * h * RET_DK + 2 * w
    gathered = [None] * len(BIG)

    def mm(a, b, out_dtype, key, res=None):
        idx = AG_BEHIND.get(key) if next_shards is not None else None
        if idx is None:
            return _matmul(a, b, "nn", out_dtype, key + "_" + tag, res=res)
        task = _ag_task([next_shards[i] for i in idx], [KINDS[i] for i in idx])
        out, *full = _matmul(a, b, "nn", out_dtype, key + "_" + tag, res=res, task=task)
        for i, f in zip(idx, full):
            gathered[i] = f
        return out

    h1 = _rms_fwd(x, sm["norm_mix_w"], "rms_mix_fwd_" + tag)
    z = mm(h1, wt["w_in"], BF16, "mm_in")
    ga, states = _ret_fwd(z, sm["ret_gn_w"], tables, h, "ret_fwd_" + tag)
    sg = _sgu_fwd(z, sm["sgu_ln_w"], sm["sgu_ln_b"], sm["ws_m"], sm["bs"], c_su, w, "sgu_fwd_" + tag)
    a = mm(ga, wt["ret_proj"], BF16, "mm_ret_proj")
    b = mm(sg, wt["sgu_proj"], BF16, "mm_sgu_proj")
    mg = _merge_fwd(a, b, z, c_gate, "merge_fwd_" + tag)
    x1 = mm(mg, wt["w_out"], F32, "mm_out", res=x)
    h2 = _rms_fwd(x1, sm["norm_ffn_w"], "rms_ffn_fwd_" + tag)
    ac = mm(h2, wt["w_ffn_in"], BF16, "mm_ffn_in")
    f = _swiglu_fwd(ac, "swiglu_fwd_" + tag)
    x2 = mm(f, wt["w_ffn_out"], F32, "mm_ffn_out", res=x1)
    saved = dict(x=x, h1=h1, z=z, states=states, ga=ga, sg=sg, a=a, b=b, mg=mg, x1=x1, h2=h2, ac=ac, f=f)
    return x2, saved, gathered


def _layer_bwd(dx2, dx2b, sv, wt, sm, tables, dims, tag, carried, shard_shapes, place):
    h, d, w = dims
    c_su, c_gate = 6 * h * RET_DK, 6 * h * RET_DK + 2 * w
    gw, gs = {}, {}
    n_big = len(BIG)
    cg, ctag = carried if carried is not None else (None, None)
    sib, recv = [None] * n_big, [None] * n_big

    def mm(a, b, mode, out_dtype, key, task_of=None, into=None):
        idx = task_of[1].get(key) if (carried is not None and task_of is not None) else None
        if idx is None:
            return _matmul(a, b, mode, out_dtype, key + "_" + tag)
        out, *got = _matmul(a, b, mode, out_dtype, key + "_" + tag, task=task_of[0](idx))
        for i, g in zip(idx, got):
            into[i] = g
        return out

    exchange = (lambda idx: _exchange_task([cg[i] for i in idx], [KINDS[i] for i in idx],
                                           [shard_shapes[i] for i in idx]), EXCHANGE_BEHIND)
    df = mm(dx2b, wt["w_ffn_out"], "nt", BF16, "mm_dffn_out_x", exchange, sib)
    gw["w_ffn_out"] = mm(sv["f"], dx2b, "tn", F32, "mm_dffn_out_w", exchange, sib)
    peers = None
    if carried is not None:
        peers = [_chip_sum(cg[i], sib[i], KINDS[i], shard_shapes[i], place, "rs_chip_sum%d_%s" % (i, ctag))
                 for i in range(n_big)]
    scatter = (lambda idx: _scatter_task([peers[i] for i in idx]), SCATTER_BEHIND)
    dac = _swiglu_bwd(sv["ac"], df, "swiglu_bwd_" + tag)
    dh2 = mm(dac, wt["w_ffn_in"], "nt", BF16, "mm_dffn_in_x", scatter, recv)
    gw["w_ffn_in"] = mm(sv["h2"], dac, "tn", F32, "mm_dffn_in_w", scatter, recv)
    halves = None
    if carried is not None:
        halves = [_final_sum(cg[i], sib[i], recv[i], KINDS[i], shard_shapes[i], place,
                             "rs_final_sum%d_%s" % (i, ctag)) for i in range(n_big)]
    dx1, dx1b, gs["norm_ffn_w"] = _rms_bwd(sv["x1"], sm["norm_ffn_w"], dh2, dx2, "rms_ffn_bwd_" + tag)
    dmg = mm(dx1b, wt["w_out"], "nt", BF16, "mm_dout_x")
    gw["w_out"] = mm(sv["mg"], dx1b, "tn", F32, "mm_dout_w")
    da, db, dgate = _merge_bwd(dmg, sv["a"], sv["b"], sv["z"], c_gate, "merge_bwd_" + tag)
    dga = mm(da, wt["ret_proj"], "nt", BF16, "mm_dret_proj_x")
    gw["ret_proj"] = mm(sv["ga"], da, "tn", F32, "mm_dret_proj_w")
    dsg = mm(db, wt["sgu_proj"], "nt", BF16, "mm_dsgu_proj_x")
    gw["sgu_proj"] = mm(sv["sg"], db, "tn", F32, "mm_dsgu_proj_w")
    dsu, dsv, gs["sgu_ln_w"], gs["sgu_ln_b"], gs["sgu_w_s"], gs["sgu_b_s"] = _sgu_bwd(
        sv["z"], dsg, sm["sgu_ln_w"], sm["sgu_ln_b"], sm["ws_m"], sm["ws_mt"], sm["bs"], c_su, w,
        "sgu_bwd_" + tag)
    dq, dk, dv, dg, gs["ret_gn_w"] = _ret_bwd(sv["z"], dga, sv["states"], sm["ret_gn_w"], tables, h,
                                             "ret_bwd_" + tag)
    dz = jnp.concatenate([dq, dk, dv, dg, dsu, dsv, dgate], axis=1)
    reduced = [None] * n_big
    join = (lambda idx: _join_task([halves[i] for i in idx]), {"mm_din_x": list(range(n_big))})
    dh1 = mm(dz, wt["w_in"], "nt", BF16, "mm_din_x", join, reduced)
    gw["w_in"] = mm(sv["h1"], dz, "tn", F32, "mm_din_w")
    dx, dxb, gs["norm_mix_w"] = _rms_bwd(sv["x"], sm["norm_mix_w"], dh1, dx1, "rms_mix_bwd_" + tag)
    return dx, dxb, gw, gs, reduced


def _sgu_mask():
    pos = jnp.arange(SGU_LEN)
    return (pos[None, :] // CHUNK) <= (pos[:, None] // CHUNK)


def kernel(x, norm_mix_w, w_in, ret_gn_w, ret_proj, sgu_ln_w, sgu_ln_b, sgu_w_s, sgu_b_s, sgu_proj, w_out, norm_ffn_w, w_ffn_in, w_ffn_out, final_norm_w, loss_target, m_norm_mix_w, m_w_in, m_ret_gn_w, m_ret_proj, m_sgu_ln_w, m_sgu_ln_b, m_sgu_w_s, m_sgu_b_s, m_sgu_proj, m_w_out, m_norm_ffn_w, m_w_ffn_in, m_w_ffn_out, m_final_norm_w, v_norm_mix_w, v_w_in, v_ret_gn_w, v_ret_proj, v_sgu_ln_w, v_sgu_ln_b, v_sgu_w_s, v_sgu_b_s, v_sgu_proj, v_w_out, v_norm_ffn_w, v_w_ffn_in, v_w_ffn_out, v_final_norm_w):
    weights = dict(norm_mix_w=norm_mix_w, w_in=w_in, ret_gn_w=ret_gn_w, ret_proj=ret_proj, sgu_ln_w=sgu_ln_w,
                   sgu_ln_b=sgu_ln_b, sgu_w_s=sgu_w_s, sgu_b_s=sgu_b_s, sgu_proj=sgu_proj, w_out=w_out,
                   norm_ffn_w=norm_ffn_w, w_ffn_in=w_ffn_in, w_ffn_out=w_ffn_out, final_norm_w=final_norm_w)
    m_in = dict(norm_mix_w=m_norm_mix_w, w_in=m_w_in, ret_gn_w=m_ret_gn_w, ret_proj=m_ret_proj,
                sgu_ln_w=m_sgu_ln_w, sgu_ln_b=m_sgu_ln_b, sgu_w_s=m_sgu_w_s, sgu_b_s=m_sgu_b_s,
                sgu_proj=m_sgu_proj, w_out=m_w_out, norm_ffn_w=m_norm_ffn_w, w_ffn_in=m_w_ffn_in,
                w_ffn_out=m_w_ffn_out, final_norm_w=m_final_norm_w)
    v_in = dict(norm_mix_w=v_norm_mix_w, w_in=v_w_in, ret_gn_w=v_ret_gn_w, ret_proj=v_ret_proj,
                sgu_ln_w=v_sgu_ln_w, sgu_ln_b=v_sgu_ln_b, sgu_w_s=v_sgu_w_s, sgu_b_s=v_sgu_b_s,
                sgu_proj=v_sgu_proj, w_out=v_w_out, norm_ffn_w=v_norm_ffn_w, w_ffn_in=v_w_ffn_in,
                w_ffn_out=v_w_ffn_out, final_norm_w=v_final_norm_w)

    depth = w_in.shape[0]
    _, s, d = x.shape
    w = d
    in_cols = 4 * w_in.shape[2]
    h = (in_cols - 4 * d) // (2 * RET_DK + 2 * RET_DV)
    groups = sgu_w_s.shape[1]
    assert in_cols == h * (2 * RET_DK + 2 * RET_DV) + 4 * d and (6 * h * RET_DK) % d == 0
    assert s % SGU_LEN == 0 and w % groups == 0 and (w // groups) % LANES == 0
    dims = (h, d, w)
    tables = _ret_tables(s, h, _tile(s, RET_BLOCK))
    mask = _sgu_mask()
    cx, cy, cc = lax.axis_index("x"), lax.axis_index("y"), lax.axis_index("c")
    place = jnp.stack([cc, 2 * (1 - cx) + cy, 2 * cx + (1 - cy), 2 * (1 - cx) + (1 - cy),
                       2 * cx + cy]).astype(jnp.int32)

    shard_shapes = [weights[n].shape[1:] for n in BIG]
    shards = [[weights[n][l].astype(BF16) for n in BIG] for l in range(depth)]

    small = []
    for l in range(depth):
        sm = {n: weights[n][l] for n in SMALL}
        ws_m = jnp.where(mask[None], sgu_w_s[l], 0.0)
        sm["ws_m"] = ws_m.astype(BF16)
        sm["ws_mt"] = jnp.swapaxes(ws_m, 1, 2).astype(BF16)
        sm["bs"] = sgu_b_s[l][:, :, None]
        small.append(sm)

    xs = x[0]
    saved, full = [], []
    gathered = _run_task(_ag_task(shards[0], KINDS), "ag_weights_l0")
    for l in range(depth):
        full.append(dict(zip(BIG, gathered)))
        xs, sv, gathered = _layer_fwd(xs, full[l], small[l], tables, dims, "l%d" % l,
                                      shards[l + 1] if l + 1 < depth else None)
        saved.append(sv)
    dx, dxb, g_final, sq = _loss_head(xs, final_norm_w, loss_target[0])
    loss = lax.psum(sq[0, 0], ("x", "y", "c")) * (0.5 / d)

    grads_big = [None] * depth
    grads_small = [None] * depth
    carried = None
    for l in reversed(range(depth)):
        dx, dxb, gw, gs, reduced = _layer_bwd(dx, dxb, saved[l], full[l], small[l], tables, dims, "l%d" % l,
                                              carried, shard_shapes, place)
        grads_small[l] = gs
        if carried is not None:
            grads_big[l + 1] = reduced
        carried = ([gw[n] for n in BIG], "l%d" % l)
    grads_big[0] = _reduce_scatter_grads(carried[0], KINDS, shard_shapes, place, carried[1])
    grad_x = dx[None]

    pieces = []
    for l in range(depth):
        gs = dict(grads_small[l])
        gs["sgu_w_s"] = jnp.where(mask[None], gs["sgu_w_s"], 0.0)
        pieces += [gs[n].reshape(-1) for n in SMALL]
    pieces.append(g_final.reshape(-1))
    flat = jnp.concatenate(pieces)
    total = flat.shape[0]
    rows = -(-total // (8 * LANES)) * 8
    flat = jnp.pad(flat, (0, rows * LANES - total)).reshape(rows, LANES)
    summed = _all_reduce_small(flat, "ar_small").reshape(-1)
    grad = {}
    off = 0
    per_layer = {n: [] for n in SMALL}
    for l in range(depth):
        for n in SMALL:
            shp = weights[n].shape[1:]
            size = math.prod(shp)
            per_layer[n].append(summed[off:off + size].reshape(shp))
            off += size
    for n in SMALL:
        grad[n] = jnp.stack(per_layer[n])
    grad["final_norm_w"] = summed[off:off + d]
    for i, n in enumerate(BIG):
        grad[n] = jnp.stack([grads_big[l][i] for l in range(depth)])

    delta, new_m, new_v = {}, {}, {}
    for n in BIG:
        shp = weights[n].shape
        two_d = lambda a: a.reshape(shp[0] * shp[1], shp[2])
        dl, mn, vn = _adamw(two_d(weights[n]), two_d(grad[n]), two_d(m_in[n]), two_d(v_in[n]), "adamw_" + n)
        delta[n], new_m[n], new_v[n] = dl.reshape(shp), mn.reshape(shp), vn.reshape(shp)
    small_names = SMALL + ["final_norm_w"]

    def pack(tree):
        fl = jnp.concatenate([tree[n].reshape(-1) for n in small_names])
        return jnp.pad(fl, (0, rows * LANES - fl.shape[0])).reshape(rows, LANES)

    dl, mn, vn = _adamw(pack(weights), pack(grad), pack(m_in), pack(v_in), "adamw_small")
    off = 0
    for n in small_names:
        shp = weights[n].shape
        size = math.prod(shp)
        for src, dst in ((dl, delta), (mn, new_m), (vn, new_v)):
            dst[n] = src.reshape(-1)[off:off + size].reshape(shp)
        off += size

    return (loss, grad_x, *[grad[n] for n in ORDER], *[delta[n] for n in ORDER],
            *[new_m[n] for n in ORDER], *[new_v[n] for n in ORDER])
```

```python
import math

import jax
import jax.numpy as jnp
from jax import lax
from jax.experimental import pallas as pl
from jax.experimental.pallas import tpu as pltpu

F32 = jnp.float32
BF16 = jnp.bfloat16

CHUNK = 64
RET_DK = 128
RET_DV = 256
SGU_LEN = 128
ROPE_BASE = 10000.0
EPS = 1e-6
ADAM_LR = 0.001
ADAM_B1 = 0.9
ADAM_B2 = 0.999
ADAM_EPS = 1e-08
ADAM_WD = 0.01
ADAM_STEP = 10

LANES = 128
VMEM_LIMIT = 56 * 1024 * 1024
RET_BLOCK = 256
SGU_BLOCK = 256
ROW_BLOCK = 256
MM_TILE = 1024
MM_KTILE = 2816
RET_HEADS_PER_STEP = 2
MESH = pl.DeviceIdType.MESH
ANY = pl.BlockSpec(memory_space=pl.ANY)
INV_SQRT2 = 1.0 / math.sqrt(2.0)
INV_SQRT_2PI = 1.0 / math.sqrt(2.0 * math.pi)

DN = {"nn": (((1,), (0,)), ((), ())), "nt": (((1,), (1,)), ((), ())), "tn": (((0,), (0,)), ((), ()))}


def _dot(a, b, mode="nn"):
    return lax.dot_general(a, b, DN[mode], preferred_element_type=F32)


def _tile(n, target):
    t = min(n, target) // LANES * LANES
    while t >= LANES:
        if n % t == 0:
            return t
        t -= LANES
    return n


def _rtile(n, target):
    t = min(n, target) // 16 * 16
    while t >= 16:
        if n % t == 0:
            return t
        t -= 16
    return n


def _params(sem):
    return pltpu.CompilerParams(dimension_semantics=sem, vmem_limit_bytes=VMEM_LIMIT)


def _sigmoid(x):
    return 1.0 / (1.0 + jnp.exp(-x))


def _gelu(x):
    return 0.5 * x * (1.0 + lax.erf(x * INV_SQRT2))


def _gelu_grad(x):
    return 0.5 * (1.0 + lax.erf(x * INV_SQRT2)) + x * jnp.exp(-0.5 * x * x) * INV_SQRT_2PI


def _matmul(a, b, mode, out_dtype, name, res=None, task=None):
    if mode == "nn":
        (m, k), n = a.shape, b.shape[1]
    elif mode == "nt":
        (m, k), n = a.shape, b.shape[0]
    else:
        (k, m), n = a.shape, b.shape[1]
    tm, tn, tk = _tile(m, MM_TILE), _tile(n, MM_TILE), _tile(k, MM_KTILE)
    ni, nj, nk = m // tm, n // tn, k // tk
    if mode == "tn":
        a_spec = pl.BlockSpec((tk, tm), lambda i, j, kk: (kk, i))
    else:
        a_spec = pl.BlockSpec((tm, tk), lambda i, j, kk: (i, kk))
    if mode == "nt":
        b_spec = pl.BlockSpec((tn, tk), lambda i, j, kk: (j, kk))
    else:
        b_spec = pl.BlockSpec((tk, tn), lambda i, j, kk: (kk, j))
    o_spec = pl.BlockSpec((tm, tn), lambda i, j, kk: (i, j))
    n_mm_in = 2 + (res is not None)
    t_ins = task.ins if task is not None else []
    t_outs = task.out_shapes if task is not None else []
    t_sems = task.sems if task is not None else []
    in_specs = [a_spec, b_spec] + ([o_spec] if res is not None else []) + [ANY] * len(t_ins)
    acc_in_out = out_dtype == F32
    scratch = [] if (nk == 1 or acc_in_out) else [pltpu.VMEM((tm, tn), F32)]

    def body(*refs):
        a_ref, b_ref = refs[0], refs[1]
        r_ref = refs[2] if res is not None else None
        tin = refs[n_mm_in:n_mm_in + len(t_ins)]
        o_ref = refs[n_mm_in + len(t_ins)]
        tout = refs[n_mm_in + len(t_ins) + 1:n_mm_in + len(t_ins) + 1 + len(t_outs)]
        rest = refs[n_mm_in + len(t_ins) + 1 + len(t_outs):]
        acc_scr, sems = (rest[0], rest[1:]) if scratch else (None, rest)
        i, j, kk = pl.program_id(0), pl.program_id(1), pl.program_id(2)
        if task is not None:
            @pl.when((i == 0) & (j == 0) & (kk == 0))
            def _():
                task.start(tin, tout, sems)

        p = _dot(a_ref[...], b_ref[...], mode)
        if nk == 1:
            if r_ref is not None:
                p = p + r_ref[...]
            o_ref[...] = p.astype(o_ref.dtype)
        else:
            acc = o_ref if acc_in_out else acc_scr

            @pl.when(kk == 0)
            def _():
                acc[...] = p if r_ref is None or not acc_in_out else p + r_ref[...]

            @pl.when(kk > 0)
            def _():
                acc[...] += p

            if not acc_in_out:
                @pl.when(kk == nk - 1)
                def _():
                    o = acc[...]
                    if r_ref is not None:
                        o = o + r_ref[...]
                    o_ref[...] = o.astype(o_ref.dtype)

        if task is not None:
            @pl.when((i == ni - 1) & (j == nj - 1) & (kk == nk - 1))
            def _():
                task.finish(tin, tout, sems)

    args = (a, b) + ((res,) if res is not None else ()) + tuple(t_ins)
    out_shape = jax.ShapeDtypeStruct((m, n), out_dtype)
    if task is None:
        return pl.pallas_call(
            body, name=name, grid=(ni, nj, nk), in_specs=in_specs, out_specs=o_spec, out_shape=out_shape,
            scratch_shapes=scratch, compiler_params=_params(("parallel", "parallel", "arbitrary")))(*args)
    aliases = {n_mm_in + src: 1 + dst for src, dst in task.aliases.items()}
    return pl.pallas_call(
        body, name=name, grid=(ni, nj, nk), in_specs=in_specs, out_specs=[o_spec] + [ANY] * len(t_outs),
        out_shape=[out_shape] + list(t_outs), input_output_aliases=aliases, scratch_shapes=scratch + list(t_sems),
        compiler_params=pltpu.CompilerParams(dimension_semantics=("arbitrary", "arbitrary", "arbitrary"),
                                             vmem_limit_bytes=VMEM_LIMIT, has_side_effects=True))(*args)


def _rms_fwd(x, w, name):
    s, d = x.shape
    tr = _tile(s, ROW_BLOCK)

    def body(x_ref, w_ref, o_ref):
        xv = x_ref[...]
        r = lax.rsqrt(jnp.mean(xv * xv, axis=-1, keepdims=True) + EPS)
        o_ref[...] = (xv * r * w_ref[...]).astype(BF16)

    row = pl.BlockSpec((tr, d), lambda i: (i, 0))
    vec = pl.BlockSpec((1, d), lambda i: (0, 0))
    return pl.pallas_call(body, name=name, grid=(s // tr,), in_specs=[row, vec], out_specs=row,
                          out_shape=jax.ShapeDtypeStruct((s, d), BF16),
                          compiler_params=_params(("parallel",)))(x, w.reshape(1, d))


def _rms_bwd(x, w, dh, dres, name):
    s, d = x.shape
    tr = _tile(s, ROW_BLOCK)

    def body(x_ref, w_ref, dh_ref, dr_ref, dx_ref, dxb_ref, dw_ref):
        xv = x_ref[...]
        r = lax.rsqrt(jnp.mean(xv * xv, axis=-1, keepdims=True) + EPS)
        xh = xv * r
        dy = dh_ref[...].astype(F32)
        dxh = dy * w_ref[...]
        dx = dr_ref[...] + r * (dxh - xh * jnp.mean(dxh * xh, axis=-1, keepdims=True))
        dx_ref[...] = dx
        dxb_ref[...] = dx.astype(BF16)

        @pl.when(pl.program_id(0) == 0)
        def _():
            dw_ref[...] = jnp.zeros_like(dw_ref)

        dw_ref[...] += jnp.sum(dy * xh, axis=0, keepdims=True)

    row = pl.BlockSpec((tr, d), lambda i: (i, 0))
    vec = pl.BlockSpec((1, d), lambda i: (0, 0))
    return pl.pallas_call(
        body, name=name, grid=(s // tr,), in_specs=[row, vec, row, row], out_specs=[row, row, vec],
        out_shape=[jax.ShapeDtypeStruct((s, d), F32), jax.ShapeDtypeStruct((s, d), BF16),
                   jax.ShapeDtypeStruct((1, d), F32)],
        compiler_params=_params(("arbitrary",)))(x, w.reshape(1, d), dh, dres)


def _loss_head(x, w, tgt):
    s, d = x.shape
    tr = _tile(s, ROW_BLOCK)

    def body(x_ref, w_ref, t_ref, dx_ref, dxb_ref, dw_ref, l_ref):
        xv = x_ref[...]
        r = lax.rsqrt(jnp.mean(xv * xv, axis=-1, keepdims=True) + EPS)
        xh = xv * r
        e = xh * w_ref[...] - t_ref[...]
        dy = e * (1.0 / d)
        dxh = dy * w_ref[...]
        dx = r * (dxh - xh * jnp.mean(dxh * xh, axis=-1, keepdims=True))
        dx_ref[...] = dx
        dxb_ref[...] = dx.astype(BF16)

        @pl.when(pl.program_id(0) == 0)
        def _():
            dw_ref[...] = jnp.zeros_like(dw_ref)
            l_ref[...] = jnp.zeros_like(l_ref)

        dw_ref[...] += jnp.sum(dy * xh, axis=0, keepdims=True)
        l_ref[...] += jnp.sum(jnp.sum(e * e, axis=1, keepdims=True), axis=0, keepdims=True)

    row = pl.BlockSpec((tr, d), lambda i: (i, 0))
    vec = pl.BlockSpec((1, d), lambda i: (0, 0))
    one = pl.BlockSpec((1, 1), lambda i: (0, 0))
    return pl.pallas_call(
        body, name="loss_head", grid=(s // tr,), in_specs=[row, vec, row], out_specs=[row, row, vec, one],
        out_shape=[jax.ShapeDtypeStruct((s, d), F32), jax.ShapeDtypeStruct((s, d), BF16),
                   jax.ShapeDtypeStruct((1, d), F32), jax.ShapeDtypeStruct((1, 1), F32)],
        compiler_params=_params(("arbitrary",)))(x, w.reshape(1, d), tgt)


def _ret_tables(s, h, t):
    half = RET_DK // 2
    inv = ROPE_BASE ** (-jnp.arange(half, dtype=F32) / half)
    ang = jnp.arange(s, dtype=F32)[:, None] * inv[None, :]
    cos, sin = jnp.cos(ang), jnp.sin(ang)
    cosf = jnp.concatenate([cos, cos], axis=1)
    sinf = jnp.concatenate([-sin, sin], axis=1)
    log_g = jnp.log1p(-(2.0 ** (-5.0 - jnp.arange(h, dtype=F32))))
    idx = jnp.arange(t, dtype=F32)
    chunk = jnp.arange(t) // CHUNK
    allowed = chunk[None, :] <= chunk[:, None]
    dm = jnp.where(allowed[None], jnp.exp(log_g[:, None, None] * jnp.abs(idx[:, None] - idx[None, :])), 0.0)
    qd = jnp.exp(log_g[:, None] * (idx[None, :] + 1.0))
    kd = jnp.exp(log_g[:, None] * (t - 1.0 - idx[None, :]))
    qd = jnp.broadcast_to(qd[:, :, None], (h, t, RET_DK))
    kd = jnp.broadcast_to(kd[:, :, None], (h, t, RET_DK))
    cd = jnp.broadcast_to(jnp.exp(log_g * t)[:, None, None], (h, 1, RET_DV))
    return cosf, sinf, dm, qd, kd, cd


def _rot(x, cos, sin):
    return x * cos + pltpu.roll(x, RET_DK // 2, 1) * sin


def _rot_t(x, cos, sin):
    return x * cos - pltpu.roll(x, RET_DK // 2, 1) * sin


def _ret_heads_per_step(h):
    return RET_HEADS_PER_STEP if h % RET_HEADS_PER_STEP == 0 else 1


def _ret_in_specs(h, t, rev_nb=None):
    hb = _ret_heads_per_step(h)
    ng = h // hb
    blk = (lambda b: b) if rev_nb is None else (lambda b: rev_nb - 1 - b)
    return [
        pl.BlockSpec((t, hb * RET_DK), lambda hh, b: (blk(b), hh)),
        pl.BlockSpec((t, hb * RET_DK), lambda hh, b: (blk(b), ng + hh)),
        pl.BlockSpec((t, hb * RET_DV), lambda hh, b: (blk(b), ng + hh)),
        pl.BlockSpec((t, hb * RET_DV), lambda hh, b: (blk(b), 2 * ng + hh)),
        pl.BlockSpec((t, RET_DK), lambda hh, b: (blk(b), 0)),
        pl.BlockSpec((t, RET_DK), lambda hh, b: (blk(b), 0)),
        pl.BlockSpec((hb, t, t), lambda hh, b: (hh, 0, 0)),
        pl.BlockSpec((hb, t, RET_DK), lambda hh, b: (hh, 0, 0)),
        pl.BlockSpec((hb, t, RET_DK), lambda hh, b: (hh, 0, 0)),
        pl.BlockSpec((hb, 1, RET_DV), lambda hh, b: (hh, 0, 0)),
        pl.BlockSpec((1, hb * RET_DV), lambda hh, b: (0, hh)),
    ]


def _ret_fwd(z, gn_w, tables, h, name):
    s = z.shape[0]
    t = _tile(s, RET_BLOCK)
    nb = s // t
    hb = _ret_heads_per_step(h)
    scale = RET_DK ** -0.5

    def body(q_ref, k_ref, v_ref, g_ref, cos_ref, sin_ref, dm_ref, qd_ref, kd_ref, cd_ref, gn_ref,
             o_ref, st_ref, st_scr):
        @pl.when(pl.program_id(1) == 0)
        def _():
            st_scr[...] = jnp.zeros_like(st_scr)

        cos, sin = cos_ref[...], sin_ref[...]
        for u in range(hb):
            ck = slice(u * RET_DK, (u + 1) * RET_DK)
            cv = slice(u * RET_DV, (u + 1) * RET_DV)
            qf = _rot(q_ref[:, ck].astype(F32), cos, sin) * scale
            kf = _rot(k_ref[:, ck].astype(F32), cos, sin)
            vb = v_ref[:, cv]
            p = _dot(qf.astype(BF16), kf.astype(BF16), "nt") * dm_ref[u]
            st = st_scr[u]
            stb = st.astype(BF16)
            st_ref[0, u] = stb
            o = _dot(p.astype(BF16), vb) + _dot((qf * qd_ref[u]).astype(BF16), stb)
            st_scr[u] = st * cd_ref[u] + _dot((kf * kd_ref[u]).astype(BF16), vb, "tn")
            dlt = o - jnp.mean(o, axis=-1, keepdims=True)
            oh = dlt * lax.rsqrt(jnp.mean(dlt * dlt, axis=-1, keepdims=True) + EPS)
            g = g_ref[:, cv].astype(F32)
            o_ref[:, cv] = (g * _sigmoid(g) * oh * gn_ref[:, cv]).astype(BF16)

    return pl.pallas_call(
        body, name=name, grid=(h // hb, nb), in_specs=_ret_in_specs(h, t),
        out_specs=[pl.BlockSpec((t, hb * RET_DV), lambda hh, b: (b, hh)),
                   pl.BlockSpec((1, hb, RET_DK, RET_DV), lambda hh, b: (b, hh, 0, 0))],
        out_shape=[jax.ShapeDtypeStruct((s, h * RET_DV), BF16),
                   jax.ShapeDtypeStruct((nb, h, RET_DK, RET_DV), BF16)],
        scratch_shapes=[pltpu.VMEM((hb, RET_DK, RET_DV), F32)],
        compiler_params=_params(("parallel", "arbitrary")))(z, z, z, z, *tables, gn_w.reshape(1, -1))


def _ret_bwd(z, dga, states, gn_w, tables, h, name):
    s = z.shape[0]
    t = _tile(s, RET_BLOCK)
    nb = s // t
    hb = _ret_heads_per_step(h)
    scale = RET_DK ** -0.5

    def body(q_ref, k_ref, v_ref, g_ref, cos_ref, sin_ref, dm_ref, qd_ref, kd_ref, cd_ref, gn_ref,
             dga_ref, st_ref, dq_ref, dk_ref, dv_ref, dg_ref, dgn_ref, dst_scr):
        @pl.when(pl.program_id(1) == 0)
        def _():
            dst_scr[...] = jnp.zeros_like(dst_scr)
            dgn_ref[...] = jnp.zeros_like(dgn_ref)

        cos, sin = cos_ref[...], sin_ref[...]
        for u in range(hb):
            ck = slice(u * RET_DK, (u + 1) * RET_DK)
            cv = slice(u * RET_DV, (u + 1) * RET_DV)
            dm = dm_ref[u]
            qf = _rot(q_ref[:, ck].astype(F32), cos, sin) * scale
            kf = _rot(k_ref[:, ck].astype(F32), cos, sin)
            qb, kb, vb = qf.astype(BF16), kf.astype(BF16), v_ref[:, cv]
            qdb = (qf * qd_ref[u]).astype(BF16)
            kdb = (kf * kd_ref[u]).astype(BF16)
            stb = st_ref[0, u]
            pb = (_dot(qb, kb, "nt") * dm).astype(BF16)
            o = _dot(pb, vb) + _dot(qdb, stb)
            dlt = o - jnp.mean(o, axis=-1, keepdims=True)
            rstd = lax.rsqrt(jnp.mean(dlt * dlt, axis=-1, keepdims=True) + EPS)
            oh = dlt * rstd
            gn = gn_ref[:, cv]
            g = g_ref[:, cv].astype(F32)
            sg = _sigmoid(g)
            dga_v = dga_ref[:, cv].astype(F32)
            dret = dga_v * g * sg
            dg_ref[:, cv] = (dga_v * oh * gn * sg * (1.0 + g * (1.0 - sg))).astype(BF16)
            dgn_ref[:, cv] += jnp.sum(dret * oh, axis=0, keepdims=True)
            doh = dret * gn
            do = rstd * (doh - jnp.mean(doh, axis=-1, keepdims=True)
                         - oh * jnp.mean(doh * oh, axis=-1, keepdims=True))
            dob = do.astype(BF16)
            dst = dst_scr[u]
            dstb = dst.astype(BF16)
            dv_ref[:, cv] = (_dot(pb, dob, "tn") + _dot(kdb, dstb)).astype(BF16)
            dpb = (_dot(dob, vb, "nt") * dm).astype(BF16)
            dqf = _dot(dpb, kb) + _dot(dob, stb, "nt") * qd_ref[u]
            dkf = _dot(dpb, qb, "tn") + _dot(vb, dstb, "nt") * kd_ref[u]
            dst_scr[u] = dst * cd_ref[u] + _dot(qdb, dob, "tn")
            dq_ref[:, ck] = _rot_t(dqf * scale, cos, sin).astype(BF16)
            dk_ref[:, ck] = _rot_t(dkf, cos, sin).astype(BF16)

    rb = lambda hh, b: (nb - 1 - b, hh)
    in_specs = _ret_in_specs(h, t, rev_nb=nb) + [
        pl.BlockSpec((t, hb * RET_DV), rb),
        pl.BlockSpec((1, hb, RET_DK, RET_DV), lambda hh, b: (nb - 1 - b, hh, 0, 0))]
    return pl.pallas_call(
        body, name=name, grid=(h // hb, nb), in_specs=in_specs,
        out_specs=[pl.BlockSpec((t, hb * RET_DK), rb), pl.BlockSpec((t, hb * RET_DK), rb),
                   pl.BlockSpec((t, hb * RET_DV), rb), pl.BlockSpec((t, hb * RET_DV), rb),
                   pl.BlockSpec((1, hb * RET_DV), lambda hh, b: (0, hh))],
        out_shape=[jax.ShapeDtypeStruct((s, h * RET_DK), BF16), jax.ShapeDtypeStruct((s, h * RET_DK), BF16),
                   jax.ShapeDtypeStruct((s, h * RET_DV), BF16), jax.ShapeDtypeStruct((s, h * RET_DV), BF16),
                   jax.ShapeDtypeStruct((1, h * RET_DV), F32)],
        scratch_shapes=[pltpu.VMEM((hb, RET_DK, RET_DV), F32)],
        compiler_params=_params(("parallel", "arbitrary")))(z, z, z, z, *tables, gn_w.reshape(1, -1), dga, states)


def _sgu_fwd(z, ln_w, ln_b, ws_m, bs, col0, w, name):
    s = z.shape[0]
    t = _tile(s, SGU_BLOCK)
    groups = ws_m.shape[0]
    ch = w // groups
    cb = col0 // w

    def body(su_ref, sv_ref, lw_ref, lb_ref, ws_ref, bs_ref, o_ref):
        zv = _gelu(sv_ref[...].astype(F32))
        dlt = zv - jnp.mean(zv, axis=-1, keepdims=True)
        vn = dlt * lax.rsqrt(jnp.mean(dlt * dlt, axis=-1, keepdims=True) + EPS) * lw_ref[...] + lb_ref[...]
        vnb = vn.astype(BF16)
        for r in range(t // SGU_LEN):
            rows = slice(r * SGU_LEN, (r + 1) * SGU_LEN)
            for gi in range(groups):
                cols = slice(gi * ch, (gi + 1) * ch)
                mixed = _dot(ws_ref[gi], vnb[rows, cols]) + bs_ref[gi]
                o_ref[rows, cols] = (_gelu(su_ref[rows, cols].astype(F32)) * mixed).astype(BF16)

    row = lambda off: pl.BlockSpec((t, w), lambda i: (i, cb + off))
    vec = pl.BlockSpec((1, w), lambda i: (0, 0))
    return pl.pallas_call(
        body, name=name, grid=(s // t,),
        in_specs=[row(0), row(1), vec, vec,
                  pl.BlockSpec((groups, SGU_LEN, SGU_LEN), lambda i: (0, 0, 0)),
                  pl.BlockSpec((groups, SGU_LEN, 1), lambda i: (0, 0, 0))],
        out_specs=pl.BlockSpec((t, w), lambda i: (i, 0)),
        out_shape=jax.ShapeDtypeStruct((s, w), BF16),
        compiler_params=_params(("parallel",)))(z, z, ln_w.reshape(1, w), ln_b.reshape(1, w), ws_m, bs)


def _sgu_bwd(z, dsg, ln_w, ln_b, ws_m, ws_mt, bs, col0, w, name):
    s = z.shape[0]
    t = _tile(s, SGU_BLOCK)
    groups = ws_m.shape[0]
    ch = w // groups
    cb = col0 // w

    def body(su_ref, sv_ref, dsg_ref, lw_ref, lb_ref, ws_ref, wst_ref, bs_ref,
             dsu_ref, dsv_ref, dlw_ref, dlb_ref, dws_ref, dbs_ref, dvn_scr):
        @pl.when(pl.program_id(0) == 0)
        def _():
            dlw_ref[...] = jnp.zeros_like(dlw_ref)
            dlb_ref[...] = jnp.zeros_like(dlb_ref)
            dws_ref[...] = jnp.zeros_like(dws_ref)
            dbs_ref[...] = jnp.zeros_like(dbs_ref)

        sv = sv_ref[...].astype(F32)
        zv = _gelu(sv)
        dlt = zv - jnp.mean(zv, axis=-1, keepdims=True)
        rstd = lax.rsqrt(jnp.mean(dlt * dlt, axis=-1, keepdims=True) + EPS)
        vh = dlt * rstd
        vnb = (vh * lw_ref[...] + lb_ref[...]).astype(BF16)
        for r in range(t // SGU_LEN):
            rows = slice(r * SGU_LEN, (r + 1) * SGU_LEN)
            for gi in range(groups):
                cols = slice(gi * ch, (gi + 1) * ch)
                vn_p = vnb[rows, cols]
                mixed = _dot(ws_ref[gi], vn_p) + bs_ref[gi]
                su = su_ref[rows, cols].astype(F32)
                dsg_p = dsg_ref[rows, cols].astype(F32)
                dsu_ref[rows, cols] = (dsg_p * mixed * _gelu_grad(su)).astype(BF16)
                dmix = dsg_p * _gelu(su)
                dmixb = dmix.astype(BF16)
                dvn_scr[rows, cols] = _dot(wst_ref[gi], dmixb)
                dws_ref[gi] += _dot(dmixb, vn_p, "nt")
                dbs_ref[gi] += jnp.sum(dmix, axis=1, keepdims=True)
        dvn = dvn_scr[...]
        dlw_ref[...] += jnp.sum(dvn * vh, axis=0, keepdims=True)
        dlb_ref[...] += jnp.sum(dvn, axis=0, keepdims=True)
        dvh = dvn * lw_ref[...]
        dzv = rstd * (dvh - jnp.mean(dvh, axis=-1, keepdims=True)
                      - vh * jnp.mean(dvh * vh, axis=-1, keepdims=True))
        dsv_ref[...] = (dzv * _gelu_grad(sv)).astype(BF16)

    row = lambda off: pl.BlockSpec((t, w), lambda i: (i, cb + off))
    out_row = pl.BlockSpec((t, w), lambda i: (i, 0))
    vec = pl.BlockSpec((1, w), lambda i: (0, 0))
    mat = pl.BlockSpec((groups, SGU_LEN, SGU_LEN), lambda i: (0, 0, 0))
    col = pl.BlockSpec((groups, SGU_LEN, 1), lambda i: (0, 0, 0))
    return pl.pallas_call(
        body, name=name, grid=(s // t,),
        in_specs=[row(0), row(1), out_row, vec, vec, mat, mat, col],
        out_specs=[out_row, out_row, vec, vec, mat, col],
        out_shape=[jax.ShapeDtypeStruct((s, w), BF16), jax.ShapeDtypeStruct((s, w), BF16),
                   jax.ShapeDtypeStruct((1, w), F32), jax.ShapeDtypeStruct((1, w), F32),
                   jax.ShapeDtypeStruct((groups, SGU_LEN, SGU_LEN), F32),
                   jax.ShapeDtypeStruct((groups, SGU_LEN, 1), F32)],
        scratch_shapes=[pltpu.VMEM((t, w), F32)],
        compiler_params=_params(("arbitrary",)))(z, z, dsg, ln_w.reshape(1, w), ln_b.reshape(1, w), ws_m, ws_mt, bs)


def _merge_fwd(a, b, z, col0, name):
    s, d = a.shape
    tr = _tile(s, ROW_BLOCK)
    cb = col0 // d

    def body(a_ref, b_ref, ga_ref, gb_ref, o_ref):
        o_ref[...] = (_sigmoid(ga_ref[...].astype(F32)) * a_ref[...].astype(F32)
                      + _sigmoid(gb_ref[...].astype(F32)) * b_ref[...].astype(F32)).astype(BF16)

    row = pl.BlockSpec((tr, d), lambda i: (i, 0))
    gate = lambda off: pl.BlockSpec((tr, d), lambda i: (i, cb + off))
    return pl.pallas_call(body, name=name, grid=(s // tr,), in_specs=[row, row, gate(0), gate(1)],
                          out_specs=row, out_shape=jax.ShapeDtypeStruct((s, d), BF16),
                          compiler_params=_params(("parallel",)))(a, b, z, z)


def _merge_bwd(dmg, a, b, z, col0, name):
    s, d = a.shape
    tr = _tile(s, ROW_BLOCK)
    cb = col0 // d

    def body(dm_ref, a_ref, b_ref, ga_ref, gb_ref, da_ref, db_ref, dgt_ref):
        dm = dm_ref[...].astype(F32)
        sa = _sigmoid(ga_ref[...].astype(F32))
        sb = _sigmoid(gb_ref[...].astype(F32))
        da_ref[...] = (dm * sa).astype(BF16)
        db_ref[...] = (dm * sb).astype(BF16)
        dgt_ref[:, :d] = (dm * a_ref[...].astype(F32) * sa * (1.0 - sa)).astype(BF16)
        dgt_ref[:, d:] = (dm * b_ref[...].astype(F32) * sb * (1.0 - sb)).astype(BF16)

    row = pl.BlockSpec((tr, d), lambda i: (i, 0))
    wide = pl.BlockSpec((tr, 2 * d), lambda i: (i, 0))
    gate = lambda off: pl.BlockSpec((tr, d), lambda i: (i, cb + off))
    return pl.pallas_call(
        body, name=name, grid=(s // tr,), in_specs=[row, row, row, gate(0), gate(1)],
        out_specs=[row, row, wide],
        out_shape=[jax.ShapeDtypeStruct((s, d), BF16), jax.ShapeDtypeStruct((s, d), BF16),
                   jax.ShapeDtypeStruct((s, 2 * d), BF16)],
        compiler_params=_params(("parallel",)))(dmg, a, b, z, z)


def _swiglu_fwd(ac, name):
    s, f2 = ac.shape
    f = f2 // 2
    tr = _tile(s, ROW_BLOCK)

    def body(a_ref, c_ref, o_ref):
        a = a_ref[...].astype(F32)
        o_ref[...] = (a * _sigmoid(a) * c_ref[...].astype(F32)).astype(BF16)

    half = lambda off: pl.BlockSpec((tr, f), lambda i: (i, off))
    return pl.pallas_call(body, name=name, grid=(s // tr,), in_specs=[half(0), half(1)], out_specs=half(0),
                          out_shape=jax.ShapeDtypeStruct((s, f), BF16),
                          compiler_params=_params(("parallel",)))(ac, ac)


def _swiglu_bwd(ac, df, name):
    s, f2 = ac.shape
    f = f2 // 2
    tr = _tile(s, ROW_BLOCK)

    def body(a_ref, c_ref, df_ref, o_ref):
        a = a_ref[...].astype(F32)
        sg = _sigmoid(a)
        dfv = df_ref[...].astype(F32)
        o_ref[:, :f] = (dfv * c_ref[...].astype(F32) * sg * (1.0 + a * (1.0 - sg))).astype(BF16)
        o_ref[:, f:] = (dfv * a * sg).astype(BF16)

    half = lambda off: pl.BlockSpec((tr, f), lambda i: (i, off))
    return pl.pallas_call(body, name=name, grid=(s // tr,), in_specs=[half(0), half(1), half(0)],
                          out_specs=pl.BlockSpec((tr, f2), lambda i: (i, 0)),
                          out_shape=jax.ShapeDtypeStruct((s, f2), BF16),
                          compiler_params=_params(("parallel",)))(ac, ac, df)


def _adamw(w, g, m, v, name):
    r, c = w.shape
    tr = _rtile(r, LANES)
    c1 = 1.0 - ADAM_B1 ** ADAM_STEP
    c2 = 1.0 - ADAM_B2 ** ADAM_STEP

    def body(w_ref, g_ref, m_ref, v_ref, d_ref, mo_ref, vo_ref):
        gv = g_ref[...]
        mn = ADAM_B1 * m_ref[...] + (1.0 - ADAM_B1) * gv
        vn = ADAM_B2 * v_ref[...] + (1.0 - ADAM_B2) * (gv * gv)
        mo_ref[...] = mn
        vo_ref[...] = vn
        d_ref[...] = -ADAM_LR * ((mn / c1) / (jnp.sqrt(vn / c2) + ADAM_EPS) + ADAM_WD * w_ref[...])

    blk = pl.BlockSpec((tr, c), lambda i: (i, 0))
    shp = jax.ShapeDtypeStruct((r, c), F32)
    return pl.pallas_call(body, name=name, grid=(r // tr,), in_specs=[blk] * 4, out_specs=[blk] * 3,
                          out_shape=[shp] * 3, compiler_params=_params(("parallel",)))(w, g, m, v)


def _place():
    x, y, c = lax.axis_index("x"), lax.axis_index("y"), lax.axis_index("c")
    chips = [(1 - x, y), (x, 1 - y), (1 - x, 1 - y)]
    return x, y, c, chips


def _block(ref, kind, chip, half, shard_shape):
    rs, cs = shard_shape
    if kind == "col":
        rows = pl.ds(0, rs) if half is None else pl.ds(half * (rs // 2), rs // 2)
        return ref.at[rows, pl.ds(chip * cs, cs)]
    rows = pl.ds(chip * rs, rs) if half is None else pl.ds(chip * rs + half * (rs // 2), rs // 2)
    return ref.at[rows, :]


def _half_rows(ref, half):
    rs = ref.shape[0]
    return ref.at[pl.ds(half * (rs // 2), rs // 2), :]


class _Task:
    def __init__(self, ins, out_shapes, sems, start, finish, aliases=None):
        self.ins, self.out_shapes, self.sems = list(ins), list(out_shapes), list(sems)
        self.start, self.finish, self.aliases = start, finish, dict(aliases or {})


def _run_task(task, name):
    n_in, n_out = len(task.ins), len(task.out_shapes)

    def body(*refs):
        ins, outs, sems = refs[:n_in], refs[n_in:n_in + n_out], refs[n_in + n_out:]
        task.start(ins, outs, sems)
        task.finish(ins, outs, sems)

    return pl.pallas_call(
        body, name=name, in_specs=[ANY] * n_in, out_specs=[ANY] * n_out, out_shape=task.out_shapes,
        input_output_aliases=task.aliases, scratch_shapes=task.sems,
        compiler_params=pltpu.CompilerParams(has_side_effects=True))(*task.ins)


def _remote(src, dst, send_sem, recv_sem, device):
    return pltpu.make_async_remote_copy(src_ref=src, dst_ref=dst, send_sem=send_sem, recv_sem=recv_sem,
                                        device_id=device, device_id_type=MESH)


def _ag_task(shards, kinds):
    n = len(shards)
    out_shapes = [jax.ShapeDtypeStruct((s.shape[0], 4 * s.shape[1]) if k == "col" else (4 * s.shape[0], s.shape[1]),
                                       s.dtype) for s, k in zip(shards, kinds)]

    def copies(ins, outs, sems):
        send_sems, recv_sems, fsend_sems, frecv_sems, own_send_sems, own_recv_sems = sems
        x, y, c, chips = _place()
        me = 2 * x + y
        own, ici, landed, fwd, passed = [], [], [], [], []
        for i in range(n):
            shp = ins[i].shape
            own.append(_remote(ins[i], _block(outs[i], kinds[i], me, None, shp), own_send_sems.at[i],
                               own_recv_sems.at[i], (x, y, 1 - c)))
            for j, (px, py) in enumerate(chips):
                k = 3 * i + j
                ici.append(_remote(_half_rows(ins[i], c), _block(outs[i], kinds[i], me, c, shp),
                                   send_sems.at[k], recv_sems.at[k], (px, py, c)))
                got = _block(outs[i], kinds[i], 2 * px + py, c, shp)
                landed.append(_remote(got, got, send_sems.at[k], recv_sems.at[k], (px, py, c)))
                fwd.append(_remote(got, got, fsend_sems.at[k], frecv_sems.at[k], (x, y, 1 - c)))
                theirs = _block(outs[i], kinds[i], 2 * px + py, 1 - c, shp)
                passed.append(_remote(theirs, theirs, fsend_sems.at[k], frecv_sems.at[k], (x, y, 1 - c)))
        return own, ici, landed, fwd, passed

    def start(ins, outs, sems):
        own, ici, _, _, _ = copies(ins, outs, sems)
        for cp in own + ici:
            cp.start()

    def finish(ins, outs, sems):
        own, ici, landed, fwd, passed = copies(ins, outs, sems)
        for got, cp in zip(landed, fwd):
            got.wait_recv()
            cp.start()
        for cp in passed:
            cp.wait_recv()
        for cp in own:
            cp.wait()
        for cp in ici + fwd:
            cp.wait_send()

    sems = [pltpu.SemaphoreType.DMA((3 * n,))] * 4 + [pltpu.SemaphoreType.DMA((n,))] * 2
    return _Task(shards, out_shapes, sems, start, finish)


def _exchange_task(grads, kinds, shard_shapes):
    n = len(grads)
    out_shapes = [jax.ShapeDtypeStruct((4, rs // 2, cs), F32) for rs, cs in shard_shapes]

    def copies(ins, outs, sems):
        send_sems, recv_sems = sems
        x, y, c, _ = _place()
        return [_remote(_block(ins[i], kinds[i], q, 1 - c, shard_shapes[i]), outs[i].at[q],
                        send_sems.at[4 * i + q], recv_sems.at[4 * i + q], (x, y, 1 - c))
                for i in range(n) for q in range(4)]

    def start(ins, outs, sems):
        for cp in copies(ins, outs, sems):
            cp.start()

    def finish(ins, outs, sems):
        for cp in copies(ins, outs, sems):
            cp.wait()

    return _Task(grads, out_shapes, [pltpu.SemaphoreType.DMA((4 * n,))] * 2, start, finish)


def _grad_block_map(kind, nt):
    if kind == "col":
        return lambda j, t, p: (p[0] * nt + t, p[1 + j])
    return lambda j, t, p: ((p[1 + j] * 2 + p[0]) * nt + t, 0)


def _chip_sum(grad, sib, kind, shard_shape, place, name):
    rs, cs = shard_shape
    hr = rs // 2
    tr = _rtile(hr, 256)
    nt = hr // tr
    g_map = _grad_block_map(kind, nt)

    def body(p_ref, g_ref, s_ref, o_ref):
        o_ref[0] = (g_ref[...] + s_ref[0]).astype(BF16)

    return pl.pallas_call(
        body, name=name,
        grid_spec=pltpu.PrefetchScalarGridSpec(
            num_scalar_prefetch=1, grid=(3, nt),
            in_specs=[pl.BlockSpec((tr, cs), g_map),
                      pl.BlockSpec((1, tr, cs), lambda j, t, p: (p[1 + j], t, 0))],
            out_specs=pl.BlockSpec((1, tr, cs), lambda j, t, p: (j, t, 0))),
        out_shape=jax.ShapeDtypeStruct((3, hr, cs), BF16),
        compiler_params=_params(("arbitrary", "arbitrary")))(place, grad, sib)


def _scatter_task(parts):
    n = len(parts)

    def copies(ins, outs, sems):
        send_sems, recv_sems = sems
        _, _, c, chips = _place()
        return [_remote(ins[i].at[j], outs[i].at[j], send_sems.at[3 * i + j], recv_sems.at[3 * i + j], (px, py, c))
                for i in range(n) for j, (px, py) in enumerate(chips)]

    def start(ins, outs, sems):
        for cp in copies(ins, outs, sems):
            cp.start()

    def finish(ins, outs, sems):
        for cp in copies(ins, outs, sems):
            cp.wait()

    return _Task(parts, [jax.ShapeDtypeStruct(p.shape, p.dtype) for p in parts],
                 [pltpu.SemaphoreType.DMA((3 * n,))] * 2, start, finish)


def _final_sum(grad, sib, recv, kind, shard_shape, place, name):
    rs, cs = shard_shape
    hr = rs // 2
    tr = _rtile(hr, 256)
    nt = hr // tr
    g_map = _grad_block_map(kind, nt)

    def body(p_ref, g_ref, s_ref, r_ref, out_ref):
        acc = g_ref[...] + s_ref[0]
        for j in range(3):
            acc = acc + r_ref[j].astype(F32)
        out_ref[...] = acc

    return pl.pallas_call(
        body, name=name,
        grid_spec=pltpu.PrefetchScalarGridSpec(
            num_scalar_prefetch=1, grid=(nt,),
            in_specs=[pl.BlockSpec((tr, cs), lambda t, p: g_map(3, t, p)),
                      pl.BlockSpec((1, tr, cs), lambda t, p: (p[4], t, 0)),
                      pl.BlockSpec((3, tr, cs), lambda t, p: (0, t, 0))],
            out_specs=pl.BlockSpec((tr, cs), lambda t, p: (p[0] * nt + t, 0))),
        out_shape=jax.ShapeDtypeStruct((rs, cs), F32),
        compiler_params=_params(("arbitrary",)))(place, grad, sib, recv)


def _join_task(shards):
    n = len(shards)

    def copies(outs, sems):
        send_sems, recv_sems = sems
        x, y, c, _ = _place()
        mine = [_half_rows(outs[i], c) for i in range(n)]
        theirs = [_half_rows(outs[i], 1 - c) for i in range(n)]
        send = [_remote(mine[i], mine[i], send_sems.at[i], recv_sems.at[i], (x, y, 1 - c)) for i in range(n)]
        recv = [_remote(theirs[i], theirs[i], send_sems.at[i], recv_sems.at[i], (x, y, 1 - c)) for i in range(n)]
        return send, recv

    def start(ins, outs, sems):
        for cp in copies(outs, sems)[0]:
            cp.start()

    def finish(ins, outs, sems):
        send, recv = copies(outs, sems)
        for cp in send:
            cp.wait_send()
        for cp in recv:
            cp.wait_recv()

    return _Task(shards, [jax.ShapeDtypeStruct(s.shape, F32) for s in shards],
                 [pltpu.SemaphoreType.DMA((n,))] * 2, start, finish, aliases={i: i for i in range(n)})


def _all_reduce_small(v, name):
    rows = v.shape[0]

    def body(v_ref, o_ref, buf, send_sems, recv_sems):
        x, y, c, _ = _place()
        coord = lambda p: ((1 - x) if p & 4 else x, (1 - y) if p & 2 else y, (1 - c) if p & 1 else c)
        me = 4 * x + 2 * y + c
        buf[me] = v_ref[...]
        copies = []
        for p in range(1, 8):
            cp = _remote(v_ref, buf.at[me], send_sems.at[p - 1], recv_sems.at[p - 1], coord(p))
            cp.start()
            copies.append(cp)
        for p in range(1, 8):
            px, py, pc = coord(p)
            _remote(v_ref, buf.at[4 * px + 2 * py + pc], send_sems.at[p - 1], recv_sems.at[p - 1],
                    coord(p)).wait_recv()
        for cp in copies:
            cp.wait_send()
        acc = buf[0]
        for dev in range(1, 8):
            acc = acc + buf[dev]
        o_ref[...] = acc

    vm = pl.BlockSpec(memory_space=pltpu.VMEM)
    return pl.pallas_call(
        body, name=name, in_specs=[vm], out_specs=vm, out_shape=jax.ShapeDtypeStruct(v.shape, F32),
        scratch_shapes=[pltpu.VMEM((8, rows, LANES), F32), pltpu.SemaphoreType.DMA((7,)),
                        pltpu.SemaphoreType.DMA((7,))],
        compiler_params=pltpu.CompilerParams(vmem_limit_bytes=VMEM_LIMIT))(v)


def _reduce_scatter_grads(grads, kinds, shard_shapes, place, tag):
    sib = _run_task(_exchange_task(grads, kinds, shard_shapes), "rs_core_exchange_" + tag)
    peers = [_chip_sum(g, sib[i], kinds[i], shard_shapes[i], place, "rs_chip_sum%d_%s" % (i, tag))
             for i, g in enumerate(grads)]
    recv = _run_task(_scatter_task(peers), "rs_scatter_" + tag)
    halves = [_final_sum(g, sib[i], recv[i], kinds[i], shard_shapes[i], place, "rs_final_sum%d_%s" % (i, tag))
              for i, g in enumerate(grads)]
    return _run_task(_join_task(halves), "rs_core_join_" + tag)


BIG = ["w_in", "ret_proj", "sgu_proj", "w_out", "w_ffn_in", "w_ffn_out"]
BIG_KIND = {"w_in": "col", "ret_proj": "row", "sgu_proj": "row", "w_out": "row", "w_ffn_in": "col",
            "w_ffn_out": "row"}
KINDS = [BIG_KIND[n] for n in BIG]
SMALL = ["norm_mix_w", "ret_gn_w", "sgu_ln_w", "sgu_ln_b", "sgu_w_s", "sgu_b_s", "norm_ffn_w"]
ORDER = ["norm_mix_w", "w_in", "ret_gn_w", "ret_proj", "sgu_ln_w", "sgu_ln_b", "sgu_w_s", "sgu_b_s",
         "sgu_proj", "w_out", "norm_ffn_w", "w_ffn_in", "w_ffn_out", "final_norm_w"]
AG_BEHIND = {"mm_in": [(0, 1), (0, 2), (0, 3), (0, 4), (0, 5)], "mm_ffn_in": [(1, 0)],
             "mm_ffn_out": [(1, 1), (1, 2), (1, 3)]}
EXCHANGE_BEHIND = {"mm_dffn_out_x": [0], "mm_dffn_out_w": [1, 2, 3, 4, 5]}
SCATTER_BEHIND = {"mm_dffn_in_x": [0, 1, 2, 3], "mm_dffn_in_w": [4, 5]}


def _layer_fwd(x, l, full, shards, sm, tables, dims):
    h, d, w = dims
    c_su, c_gate = 6 * h * RET_DK, 6 * h * RET_DK + 2 * w
    tag = "l%d" % l
    wt = full[l]

    def mm(a, wname, out_dtype, key, res=None):
        todo = [(l + dl, i) for dl, i in AG_BEHIND.get(key, []) if l + dl < len(full) and BIG[i] not in full[l + dl]]
        if not todo:
            return _matmul(a, wt[wname], "nn", out_dtype, key + "_" + tag, res=res)
        task = _ag_task([shards[ll][i] for ll, i in todo], [KINDS[i] for _, i in todo])
        out, *got = _matmul(a, wt[wname], "nn", out_dtype, key + "_" + tag, res=res, task=task)
        for (ll, i), g in zip(todo, got):
            full[ll][BIG[i]] = g
        return out

    h1 = _rms_fwd(x, sm["norm_mix_w"], "rms_mix_fwd_" + tag)
    z = mm(h1, "w_in", BF16, "mm_in")
    ga, states = _ret_fwd(z, sm["ret_gn_w"], tables, h, "ret_fwd_" + tag)
    sg = _sgu_fwd(z, sm["sgu_ln_w"], sm["sgu_ln_b"], sm["ws_m"], sm["bs"], c_su, w, "sgu_fwd_" + tag)
    a = mm(ga, "ret_proj", BF16, "mm_ret_proj")
    b = mm(sg, "sgu_proj", BF16, "mm_sgu_proj")
    mg = _merge_fwd(a, b, z, c_gate, "merge_fwd_" + tag)
    x1 = mm(mg, "w_out", F32, "mm_out", res=x)
    h2 = _rms_fwd(x1, sm["norm_ffn_w"], "rms_ffn_fwd_" + tag)
    ac = mm(h2, "w_ffn_in", BF16, "mm_ffn_in")
    f = _swiglu_fwd(ac, "swiglu_fwd_" + tag)
    x2 = mm(f, "w_ffn_out", F32, "mm_ffn_out", res=x1)
    saved = dict(x=x, h1=h1, z=z, states=states, ga=ga, sg=sg, a=a, b=b, mg=mg, x1=x1, h2=h2, ac=ac, f=f)
    return x2, saved


def _layer_bwd(dx2, dx2b, sv, wt, sm, tables, dims, tag, carried, shard_shapes, place):
    h, d, w = dims
    c_su, c_gate = 6 * h * RET_DK, 6 * h * RET_DK + 2 * w
    gw, gs = {}, {}
    n_big = len(BIG)
    cg, ctag = carried if carried is not None else (None, None)
    sib, recv = [None] * n_big, [None] * n_big

    def mm(a, b, mode, out_dtype, key, task_of=None, into=None):
        idx = task_of[1].get(key) if (carried is not None and task_of is not None) else None
        if idx is None:
            return _matmul(a, b, mode, out_dtype, key + "_" + tag)
        out, *got = _matmul(a, b, mode, out_dtype, key + "_" + tag, task=task_of[0](idx))
        for i, g in zip(idx, got):
            into[i] = g
        return out

    exchange = (lambda idx: _exchange_task([cg[i] for i in idx], [KINDS[i] for i in idx],
                                           [shard_shapes[i] for i in idx]), EXCHANGE_BEHIND)
    df = mm(dx2b, wt["w_ffn_out"], "nt", BF16, "mm_dffn_out_x", exchange, sib)
    gw["w_ffn_out"] = mm(sv["f"], dx2b, "tn", F32, "mm_dffn_out_w", exchange, sib)
    peers = None
    if carried is not None:
        peers = [_chip_sum(cg[i], sib[i], KINDS[i], shard_shapes[i], place, "rs_chip_sum%d_%s" % (i, ctag))
                 for i in range(n_big)]
    scatter = (lambda idx: _scatter_task([peers[i] for i in idx]), SCATTER_BEHIND)
    dac = _swiglu_bwd(sv["ac"], df, "swiglu_bwd_" + tag)
    dh2 = mm(dac, wt["w_ffn_in"], "nt", BF16, "mm_dffn_in_x", scatter, recv)
    gw["w_ffn_in"] = mm(sv["h2"], dac, "tn", F32, "mm_dffn_in_w", scatter, recv)
    halves = None
    if carried is not None:
        halves = [_final_sum(cg[i], sib[i], recv[i], KINDS[i], shard_shapes[i], place,
                             "rs_final_sum%d_%s" % (i, ctag)) for i in range(n_big)]
    dx1, dx1b, gs["norm_ffn_w"] = _rms_bwd(sv["x1"], sm["norm_ffn_w"], dh2, dx2, "rms_ffn_bwd_" + tag)
    dmg = mm(dx1b, wt["w_out"], "nt", BF16, "mm_dout_x")
    gw["w_out"] = mm(sv["mg"], dx1b, "tn", F32, "mm_dout_w")
    da, db, dgate = _merge_bwd(dmg, sv["a"], sv["b"], sv["z"], c_gate, "merge_bwd_" + tag)
    dga = mm(da, wt["ret_proj"], "nt", BF16, "mm_dret_proj_x")
    gw["ret_proj"] = mm(sv["ga"], da, "tn", F32, "mm_dret_proj_w")
    dsg = mm(db, wt["sgu_proj"], "nt", BF16, "mm_dsgu_proj_x")
    gw["sgu_proj"] = mm(sv["sg"], db, "tn", F32, "mm_dsgu_proj_w")
    dsu, dsv, gs["sgu_ln_w"], gs["sgu_ln_b"], gs["sgu_w_s"], gs["sgu_b_s"] = _sgu_bwd(
        sv["z"], dsg, sm["sgu_ln_w"], sm["sgu_ln_b"], sm["ws_m"], sm["ws_mt"], sm["bs"], c_su, w,
        "sgu_bwd_" + tag)
    dq, dk, dv, dg, gs["ret_gn_w"] = _ret_bwd(sv["z"], dga, sv["states"], sm["ret_gn_w"], tables, h,
                                             "ret_bwd_" + tag)
    dz = jnp.concatenate([dq, dk, dv, dg, dsu, dsv, dgate], axis=1)
    reduced = [None] * n_big
    join = (lambda idx: _join_task([halves[i] for i in idx]), {"mm_din_x": list(range(n_big))})
    dh1 = mm(dz, wt["w_in"], "nt", BF16, "mm_din_x", join, reduced)
    gw["w_in"] = mm(sv["h1"], dz, "tn", F32, "mm_din_w")
    dx, dxb, gs["norm_mix_w"] = _rms_bwd(sv["x"], sm["norm_mix_w"], dh1, dx1, "rms_mix_bwd_" + tag)
    return dx, dxb, gw, gs, reduced


def _sgu_mask():
    pos = jnp.arange(SGU_LEN)
    return (pos[None, :] // CHUNK) <= (pos[:, None] // CHUNK)


def kernel(x, norm_mix_w, w_in, ret_gn_w, ret_proj, sgu_ln_w, sgu_ln_b, sgu_w_s, sgu_b_s, sgu_proj, w_out, norm_ffn_w, w_ffn_in, w_ffn_out, final_norm_w, loss_target, m_norm_mix_w, m_w_in, m_ret_gn_w, m_ret_proj, m_sgu_ln_w, m_sgu_ln_b, m_sgu_w_s, m_sgu_b_s, m_sgu_proj, m_w_out, m_norm_ffn_w, m_w_ffn_in, m_w_ffn_out, m_final_norm_w, v_norm_mix_w, v_w_in, v_ret_gn_w, v_ret_proj, v_sgu_ln_w, v_sgu_ln_b, v_sgu_w_s, v_sgu_b_s, v_sgu_proj, v_w_out, v_norm_ffn_w, v_w_ffn_in, v_w_ffn_out, v_final_norm_w):
    weights = dict(norm_mix_w=norm_mix_w, w_in=w_in, ret_gn_w=ret_gn_w, ret_proj=ret_proj, sgu_ln_w=sgu_ln_w,
                   sgu_ln_b=sgu_ln_b, sgu_w_s=sgu_w_s, sgu_b_s=sgu_b_s, sgu_proj=sgu_proj, w_out=w_out,
                   norm_ffn_w=norm_ffn_w, w_ffn_in=w_ffn_in, w_ffn_out=w_ffn_out, final_norm_w=final_norm_w)
    m_in = dict(norm_mix_w=m_norm_mix_w, w_in=m_w_in, ret_gn_w=m_ret_gn_w, ret_proj=m_ret_proj,
                sgu_ln_w=m_sgu_ln_w, sgu_ln_b=m_sgu_ln_b, sgu_w_s=m_sgu_w_s, sgu_b_s=m_sgu_b_s,
                sgu_proj=m_sgu_proj, w_out=m_w_out, norm_ffn_w=m_norm_ffn_w, w_ffn_in=m_w_ffn_in,
                w_ffn_out=m_w_ffn_out, final_norm_w=m_final_norm_w)
    v_in = dict(norm_mix_w=v_norm_mix_w, w_in=v_w_in, ret_gn_w=v_ret_gn_w, ret_proj=v_ret_proj,
                sgu_ln_w=v_sgu_ln_w, sgu_ln_b=v_sgu_ln_b, sgu_w_s=v_sgu_w_s, sgu_b_s=v_sgu_b_s,
                sgu_proj=v_sgu_proj, w_out=v_w_out, norm_ffn_w=v_norm_ffn_w, w_ffn_in=v_w_ffn_in,
                w_ffn_out=v_w_ffn_out, final_norm_w=v_final_norm_w)

    depth = w_in.shape[0]
    _, s, d = x.shape
    w = d
    in_cols = 4 * w_in.shape[2]
    h = (in_cols - 4 * d) // (2 * RET_DK + 2 * RET_DV)
    groups = sgu_w_s.shape[1]
    assert in_cols == h * (2 * RET_DK + 2 * RET_DV) + 4 * d and (6 * h * RET_DK) % d == 0
    assert s % SGU_LEN == 0 and w % groups == 0 and (w // groups) % LANES == 0
    dims = (h, d, w)
    tables = _ret_tables(s, h, _tile(s, RET_BLOCK))
    mask = _sgu_mask()
    cx, cy, cc = lax.axis_index("x"), lax.axis_index("y"), lax.axis_index("c")
    place = jnp.stack([cc, 2 * (1 - cx) + cy, 2 * cx + (1 - cy), 2 * (1 - cx) + (1 - cy),
                       2 * cx + cy]).astype(jnp.int32)

    shard_shapes = [weights[n].shape[1:] for n in BIG]
    shards = [[weights[n][l].astype(BF16) for n in BIG] for l in range(depth)]

    small = []
    for l in range(depth):
        sm = {n: weights[n][l] for n in SMALL}
        ws_m = jnp.where(mask[None], sgu_w_s[l], 0.0)
        sm["ws_m"] = ws_m.astype(BF16)
        sm["ws_mt"] = jnp.swapaxes(ws_m, 1, 2).astype(BF16)
        sm["bs"] = sgu_b_s[l][:, :, None]
        small.append(sm)

    xs = x[0]
    saved = []
    full = [{} for _ in range(depth)]
    full[0][BIG[0]], = _run_task(_ag_task(shards[0][:1], KINDS[:1]), "ag_w_in_l0")
    for l in range(depth):
        xs, sv = _layer_fwd(xs, l, full, shards, small[l], tables, dims)
        saved.append(sv)
    dx, dxb, g_final, sq = _loss_head(xs, final_norm_w, loss_target[0])
    loss = lax.psum(sq[0, 0], ("x", "y", "c")) * (0.5 / d)

    grads_big = [None] * depth
    grads_small = [None] * depth
    carried = None
    for l in reversed(range(depth)):
        dx, dxb, gw, gs, reduced = _layer_bwd(dx, dxb, saved[l], full[l], small[l], tables, dims, "l%d" % l,
                                              carried, shard_shapes, place)
        grads_small[l] = gs
        if carried is not None:
            grads_big[l + 1] = reduced
        carried = ([gw[n] for n in BIG], "l%d" % l)
    grads_big[0] = _reduce_scatter_grads(carried[0], KINDS, shard_shapes, place, carried[1])
    grad_x = dx[None]

    pieces = []
    for l in range(depth):
        gs = dict(grads_small[l])
        gs["sgu_w_s"] = jnp.where(mask[None], gs["sgu_w_s"], 0.0)
        pieces += [gs[n].reshape(-1) for n in SMALL]
    pieces.append(g_final.reshape(-1))
    flat = jnp.concatenate(pieces)
    total = flat.shape[0]
    rows = -(-total // (8 * LANES)) * 8
    flat = jnp.pad(flat, (0, rows * LANES - total)).reshape(rows, LANES)
    summed = _all_reduce_small(flat, "ar_small").reshape(-1)
    grad = {}
    off = 0
    per_layer = {n: [] for n in SMALL}
    for l in range(depth):
        for n in SMALL:
            shp = weights[n].shape[1:]
            size = math.prod(shp)
            per_layer[n].append(summed[off:off + size].reshape(shp))
            off += size
    for n in SMALL:
        grad[n] = jnp.stack(per_layer[n])
    grad["final_norm_w"] = summed[off:off + d]
    for i, n in enumerate(BIG):
        grad[n] = jnp.stack([grads_big[l][i] for l in range(depth)])

    delta, new_m, new_v = {}, {}, {}
    for n in BIG:
        shp = weights[n].shape
        two_d = lambda a: a.reshape(shp[0] * shp[1], shp[2])
        dl, mn, vn = _adamw(two_d(weights[n]), two_d(grad[n]), two_d(m_in[n]), two_d(v_in[n]), "adamw_" + n)
        delta[n], new_m[n], new_v[n] = dl.reshape(shp), mn.reshape(shp), vn.reshape(shp)
    small_names = SMALL + ["final_norm_w"]

    def pack(tree):
        fl = jnp.concatenate([tree[n].reshape(-1) for n in small_names])
        return jnp.pad(fl, (0, rows * LANES - fl.shape[0])).reshape(rows, LANES)

    dl, mn, vn = _adamw(pack(weights), pack(grad), pack(m_in), pack(v_in), "adamw_small")
    off = 0
    for n in small_names:
        shp = weights[n].shape
        size = math.prod(shp)
        for src, dst in ((dl, delta), (mn, new_m), (vn, new_v)):
            dst[n] = src.reshape(-1)[off:off + size].reshape(shp)
        off += size

    return (loss, grad_x, *[grad[n] for n in ORDER], *[delta[n] for n in ORDER],
            *[new_m[n] for n in ORDER], *[new_v[n] for n in ORDER])
```

```python
import math

import jax
import jax.numpy as jnp
from jax import lax
from jax.experimental import pallas as pl
from jax.experimental.pallas import tpu as pltpu

F32 = jnp.float32
BF16 = jnp.bfloat16

CHUNK = 64
RET_DK = 128
RET_DV = 256
SGU_LEN = 128
ROPE_BASE = 10000.0
EPS = 1e-6
ADAM_LR = 0.001
ADAM_B1 = 0.9
ADAM_B2 = 0.999
ADAM_EPS = 1e-08
ADAM_WD = 0.01
ADAM_STEP = 10

LANES = 128
VMEM_LIMIT = 56 * 1024 * 1024
RET_BLOCK = 256
SGU_BLOCK = 256
ROW_BLOCK = 256
MM_TILE = 1408
MM_KTILE = 3584
MESH = pl.DeviceIdType.MESH
ANY = pl.BlockSpec(memory_space=pl.ANY)
INV_SQRT2 = 1.0 / math.sqrt(2.0)
INV_SQRT_2PI = 1.0 / math.sqrt(2.0 * math.pi)

DN = {"nn": (((1,), (0,)), ((), ())), "nt": (((1,), (1,)), ((), ())), "tn": (((0,), (0,)), ((), ()))}


def _dot(a, b, mode="nn"):
    return lax.dot_general(a, b, DN[mode], preferred_element_type=F32)


def _tile(n, target):
    t = min(n, target) // LANES * LANES
    while t >= LANES:
        if n % t == 0:
            return t
        t -= LANES
    return n


def _rtile(n, target):
    t = min(n, target) // 16 * 16
    while t >= 16:
        if n % t == 0:
            return t
        t -= 16
    return n


def _params(sem):
    return pltpu.CompilerParams(dimension_semantics=sem, vmem_limit_bytes=VMEM_LIMIT)


def _sigmoid(x):
    return 1.0 / (1.0 + jnp.exp(-x))


def _gelu(x):
    return 0.5 * x * (1.0 + lax.erf(x * INV_SQRT2))


def _gelu_grad(x):
    return 0.5 * (1.0 + lax.erf(x * INV_SQRT2)) + x * jnp.exp(-0.5 * x * x) * INV_SQRT_2PI


def _matmul(a, b, mode, out_dtype, name, res=None, task=None, b_k0=0, window=None):
    if mode == "nn":
        (m, k), n = a.shape, b.shape[1]
    elif mode == "nt":
        (m, k), n = a.shape, b.shape[0]
    else:
        (k, m), n = a.shape, b.shape[1]
    out_cols, out_c0, into = window if window is not None else (n, 0, None)
    assert b_k0 == 0 or mode == "nt"
    tm, tn, tk = _tile(m, MM_TILE), _tile(math.gcd(n, out_c0), MM_TILE), _tile(math.gcd(k, b_k0), MM_KTILE)
    ni, nj, nk = m // tm, n // tn, k // tk
    kb0, jb0 = b_k0 // tk, out_c0 // tn
    if mode == "tn":
        a_spec = pl.BlockSpec((tk, tm), lambda i, j, kk: (kk, i))
    else:
        a_spec = pl.BlockSpec((tm, tk), lambda i, j, kk: (i, kk))
    if mode == "nt":
        b_spec = pl.BlockSpec((tn, tk), lambda i, j, kk: (j, kb0 + kk))
    else:
        b_spec = pl.BlockSpec((tk, tn), lambda i, j, kk: (kk, j))
    r_spec = pl.BlockSpec((tm, tn), lambda i, j, kk: (i, j))
    o_spec = pl.BlockSpec((tm, tn), lambda i, j, kk: (i, jb0 + j))
    n_mm_in = 2 + (res is not None)
    t_ins = (task.ins if task is not None else []) + ([into] if into is not None else [])
    t_outs = task.out_shapes if task is not None else []
    t_sems = task.sems if task is not None else []
    in_specs = [a_spec, b_spec] + ([r_spec] if res is not None else []) + [ANY] * len(t_ins)
    acc_in_out = out_dtype == F32
    scratch = [] if (nk == 1 or acc_in_out) else [pltpu.VMEM((tm, tn), F32)]

    def body(*refs):
        a_ref, b_ref = refs[0], refs[1]
        r_ref = refs[2] if res is not None else None
        tin = refs[n_mm_in:n_mm_in + len(t_ins)]
        o_ref = refs[n_mm_in + len(t_ins)]
        tout = refs[n_mm_in + len(t_ins) + 1:n_mm_in + len(t_ins) + 1 + len(t_outs)]
        rest = refs[n_mm_in + len(t_ins) + 1 + len(t_outs):]
        acc_scr, sems = (rest[0], rest[1:]) if scratch else (None, rest)
        i, j, kk = pl.program_id(0), pl.program_id(1), pl.program_id(2)
        if task is not None:
            @pl.when((i == 0) & (j == 0) & (kk == 0))
            def _():
                task.start(tin, tout, sems)

        p = _dot(a_ref[...], b_ref[...], mode)
        if nk == 1:
            if r_ref is not None:
                p = p + r_ref[...]
            o_ref[...] = p.astype(o_ref.dtype)
        else:
            acc = o_ref if acc_in_out else acc_scr

            @pl.when(kk == 0)
            def _():
                acc[...] = p if r_ref is None or not acc_in_out else p + r_ref[...]

            @pl.when(kk > 0)
            def _():
                acc[...] += p

            if not acc_in_out:
                @pl.when(kk == nk - 1)
                def _():
                    o = acc[...]
                    if r_ref is not None:
                        o = o + r_ref[...]
                    o_ref[...] = o.astype(o_ref.dtype)

        if task is not None:
            @pl.when((i == ni - 1) & (j == nj - 1) & (kk == nk - 1))
            def _():
                task.finish(tin, tout, sems)

    args = (a, b) + ((res,) if res is not None else ()) + tuple(t_ins)
    out_shape = jax.ShapeDtypeStruct((m, out_cols), out_dtype)
    into_alias = {n_mm_in + len(t_ins) - 1: 0} if into is not None else {}
    if task is None:
        return pl.pallas_call(
            body, name=name, grid=(ni, nj, nk), in_specs=in_specs, out_specs=o_spec, out_shape=out_shape,
            input_output_aliases=into_alias, scratch_shapes=scratch,
            compiler_params=_params(("parallel", "parallel", "arbitrary")))(*args)
    aliases = dict(into_alias)
    aliases.update({n_mm_in + src: 1 + dst for src, dst in task.aliases.items()})
    return pl.pallas_call(
        body, name=name, grid=(ni, nj, nk), in_specs=in_specs, out_specs=[o_spec] + [ANY] * len(t_outs),
        out_shape=[out_shape] + list(t_outs), input_output_aliases=aliases, scratch_shapes=scratch + list(t_sems),
        compiler_params=pltpu.CompilerParams(dimension_semantics=("arbitrary", "arbitrary", "arbitrary"),
                                             vmem_limit_bytes=VMEM_LIMIT, has_side_effects=True))(*args)


def _rms_fwd(x, w, name):
    s, d = x.shape
    tr = _tile(s, ROW_BLOCK)

    def body(x_ref, w_ref, o_ref):
        xv = x_ref[...]
        r = lax.rsqrt(jnp.mean(xv * xv, axis=-1, keepdims=True) + EPS)
        o_ref[...] = (xv * r * w_ref[...]).astype(BF16)

    row = pl.BlockSpec((tr, d), lambda i: (i, 0))
    vec = pl.BlockSpec((1, d), lambda i: (0, 0))
    return pl.pallas_call(body, name=name, grid=(s // tr,), in_specs=[row, vec], out_specs=row,
                          out_shape=jax.ShapeDtypeStruct((s, d), BF16),
                          compiler_params=_params(("parallel",)))(x, w.reshape(1, d))


def _rms_bwd(x, w, dh, dres, name):
    s, d = x.shape
    tr = _tile(s, ROW_BLOCK)

    def body(x_ref, w_ref, dh_ref, dr_ref, dx_ref, dxb_ref, dw_ref):
        xv = x_ref[...]
        r = lax.rsqrt(jnp.mean(xv * xv, axis=-1, keepdims=True) + EPS)
        xh = xv * r
        dy = dh_ref[...].astype(F32)
        dxh = dy * w_ref[...]
        dx = dr_ref[...] + r * (dxh - xh * jnp.mean(dxh * xh, axis=-1, keepdims=True))
        dx_ref[...] = dx
        dxb_ref[...] = dx.astype(BF16)

        @pl.when(pl.program_id(0) == 0)
        def _():
            dw_ref[...] = jnp.zeros_like(dw_ref)

        dw_ref[...] += jnp.sum(dy * xh, axis=0, keepdims=True)

    row = pl.BlockSpec((tr, d), lambda i: (i, 0))
    vec = pl.BlockSpec((1, d), lambda i: (0, 0))
    return pl.pallas_call(
        body, name=name, grid=(s // tr,), in_specs=[row, vec, row, row], out_specs=[row, row, vec],
        out_shape=[jax.ShapeDtypeStruct((s, d), F32), jax.ShapeDtypeStruct((s, d), BF16),
                   jax.ShapeDtypeStruct((1, d), F32)],
        compiler_params=_params(("arbitrary",)))(x, w.reshape(1, d), dh, dres)


def _loss_head(x, w, tgt):
    s, d = x.shape
    tr = _tile(s, ROW_BLOCK)

    def body(x_ref, w_ref, t_ref, dx_ref, dxb_ref, dw_ref, l_ref):
        xv = x_ref[...]
        r = lax.rsqrt(jnp.mean(xv * xv, axis=-1, keepdims=True) + EPS)
        xh = xv * r
        e = xh * w_ref[...] - t_ref[...]
        dy = e * (1.0 / d)
        dxh = dy * w_ref[...]
        dx = r * (dxh - xh * jnp.mean(dxh * xh, axis=-1, keepdims=True))
        dx_ref[...] = dx
        dxb_ref[...] = dx.astype(BF16)

        @pl.when(pl.program_id(0) == 0)
        def _():
            dw_ref[...] = jnp.zeros_like(dw_ref)
            l_ref[...] = jnp.zeros_like(l_ref)

        dw_ref[...] += jnp.sum(dy * xh, axis=0, keepdims=True)
        l_ref[...] += jnp.sum(jnp.sum(e * e, axis=1, keepdims=True), axis=0, keepdims=True)

    row = pl.BlockSpec((tr, d), lambda i: (i, 0))
    vec = pl.BlockSpec((1, d), lambda i: (0, 0))
    one = pl.BlockSpec((1, 1), lambda i: (0, 0))
    return pl.pallas_call(
        body, name="loss_head", grid=(s // tr,), in_specs=[row, vec, row], out_specs=[row, row, vec, one],
        out_shape=[jax.ShapeDtypeStruct((s, d), F32), jax.ShapeDtypeStruct((s, d), BF16),
                   jax.ShapeDtypeStruct((1, d), F32), jax.ShapeDtypeStruct((1, 1), F32)],
        compiler_params=_params(("arbitrary",)))(x, w.reshape(1, d), tgt)


def _ret_tables(s, h, t):
    half = RET_DK // 2
    inv = ROPE_BASE ** (-jnp.arange(half, dtype=F32) / half)
    ang = jnp.arange(s, dtype=F32)[:, None] * inv[None, :]
    cos, sin = jnp.cos(ang), jnp.sin(ang)
    cosf = jnp.concatenate([cos, cos], axis=1)
    sinf = jnp.concatenate([-sin, sin], axis=1)
    log_g = jnp.log1p(-(2.0 ** (-5.0 - jnp.arange(h, dtype=F32))))
    idx = jnp.arange(t, dtype=F32)
    chunk = jnp.arange(t) // CHUNK
    allowed = chunk[None, :] <= chunk[:, None]
    dm = jnp.where(allowed[None], jnp.exp(log_g[:, None, None] * jnp.abs(idx[:, None] - idx[None, :])), 0.0)
    qd = jnp.exp(log_g[:, None] * (idx[None, :] + 1.0))
    kd = jnp.exp(log_g[:, None] * (t - 1.0 - idx[None, :]))
    qd = jnp.broadcast_to(qd[:, :, None], (h, t, RET_DK))
    kd = jnp.broadcast_to(kd[:, :, None], (h, t, RET_DK))
    cd = jnp.broadcast_to(jnp.exp(log_g * t)[:, None, None], (h, 1, RET_DV))
    return cosf, sinf, dm, qd, kd, cd


def _rot(x, cos, sin):
    return x * cos + pltpu.roll(x, RET_DK // 2, 1) * sin


def _rot_t(x, cos, sin):
    return x * cos - pltpu.roll(x, RET_DK // 2, 1) * sin


def _ret_heads_per_step(h):
    return h


def _ret_in_specs(h, t, rev_nb=None):
    hb = _ret_heads_per_step(h)
    ng = h // hb
    blk = (lambda b: b) if rev_nb is None else (lambda b: rev_nb - 1 - b)
    return [
        pl.BlockSpec((t, hb * RET_DK), lambda hh, b: (blk(b), hh)),
        pl.BlockSpec((t, hb * RET_DK), lambda hh, b: (blk(b), ng + hh)),
        pl.BlockSpec((t, hb * RET_DV), lambda hh, b: (blk(b), ng + hh)),
        pl.BlockSpec((t, hb * RET_DV), lambda hh, b: (blk(b), 2 * ng + hh)),
        pl.BlockSpec((t, RET_DK), lambda hh, b: (blk(b), 0)),
        pl.BlockSpec((t, RET_DK), lambda hh, b: (blk(b), 0)),
        pl.BlockSpec((hb, t, t), lambda hh, b: (hh, 0, 0)),
        pl.BlockSpec((hb, t, RET_DK), lambda hh, b: (hh, 0, 0)),
        pl.BlockSpec((hb, t, RET_DK), lambda hh, b: (hh, 0, 0)),
        pl.BlockSpec((hb, 1, RET_DV), lambda hh, b: (hh, 0, 0)),
        pl.BlockSpec((1, hb * RET_DV), lambda hh, b: (0, hh)),
    ]


def _ret_fwd(z, gn_w, tables, h, name):
    s = z.shape[0]
    t = _tile(s, RET_BLOCK)
    nb = s // t
    hb = _ret_heads_per_step(h)
    scale = RET_DK ** -0.5

    def body(q_ref, k_ref, v_ref, g_ref, cos_ref, sin_ref, dm_ref, qd_ref, kd_ref, cd_ref, gn_ref,
             o_ref, st_ref, st_scr):
        @pl.when(pl.program_id(1) == 0)
        def _():
            st_scr[...] = jnp.zeros_like(st_scr)

        cos, sin = cos_ref[...], sin_ref[...]
        for u in range(hb):
            ck = slice(u * RET_DK, (u + 1) * RET_DK)
            cv = slice(u * RET_DV, (u + 1) * RET_DV)
            qf = _rot(q_ref[:, ck].astype(F32), cos, sin) * scale
            kf = _rot(k_ref[:, ck].astype(F32), cos, sin)
            vb = v_ref[:, cv]
            p = _dot(qf.astype(BF16), kf.astype(BF16), "nt") * dm_ref[u]
            st = st_scr[u]
            stb = st.astype(BF16)
            st_ref[0, u] = stb
            o = _dot(p.astype(BF16), vb) + _dot((qf * qd_ref[u]).astype(BF16), stb)
            st_scr[u] = st * cd_ref[u] + _dot((kf * kd_ref[u]).astype(BF16), vb, "tn")
            dlt = o - jnp.mean(o, axis=-1, keepdims=True)
            oh = dlt * lax.rsqrt(jnp.mean(dlt * dlt, axis=-1, keepdims=True) + EPS)
            g = g_ref[:, cv].astype(F32)
            o_ref[:, cv] = (g * _sigmoid(g) * oh * gn_ref[:, cv]).astype(BF16)

    return pl.pallas_call(
        body, name=name, grid=(h // hb, nb), in_specs=_ret_in_specs(h, t),
        out_specs=[pl.BlockSpec((t, hb * RET_DV), lambda hh, b: (b, hh)),
                   pl.BlockSpec((1, hb, RET_DK, RET_DV), lambda hh, b: (b, hh, 0, 0))],
        out_shape=[jax.ShapeDtypeStruct((s, h * RET_DV), BF16),
                   jax.ShapeDtypeStruct((nb, h, RET_DK, RET_DV), BF16)],
        scratch_shapes=[pltpu.VMEM((hb, RET_DK, RET_DV), F32)],
        compiler_params=_params(("parallel", "arbitrary")))(z, z, z, z, *tables, gn_w.reshape(1, -1))


def _ret_bwd(z, dga, states, gn_w, tables, h, name):
    s = z.shape[0]
    t = _tile(s, RET_BLOCK)
    nb = s // t
    hb = _ret_heads_per_step(h)
    assert hb == h
    scale = RET_DK ** -0.5
    c_k, c_v, c_g = h * RET_DK, 2 * h * RET_DK, 2 * h * RET_DK + h * RET_DV

    def body(q_ref, k_ref, v_ref, g_ref, cos_ref, sin_ref, dm_ref, qd_ref, kd_ref, cd_ref, gn_ref,
             dga_ref, st_ref, dz_ref, dgn_ref, dst_scr):
        @pl.when(pl.program_id(1) == 0)
        def _():
            dst_scr[...] = jnp.zeros_like(dst_scr)
            dgn_ref[...] = jnp.zeros_like(dgn_ref)

        dq_ref, dk_ref = dz_ref.at[:, 0:c_k], dz_ref.at[:, c_k:c_v]
        dv_ref, dg_ref = dz_ref.at[:, c_v:c_g], dz_ref.at[:, c_g:c_g + h * RET_DV]
        cos, sin = cos_ref[...], sin_ref[...]
        for u in range(hb):
            ck = slice(u * RET_DK, (u + 1) * RET_DK)
            cv = slice(u * RET_DV, (u + 1) * RET_DV)
            dm = dm_ref[u]
            qf = _rot(q_ref[:, ck].astype(F32), cos, sin) * scale
            kf = _rot(k_ref[:, ck].astype(F32), cos, sin)
            qb, kb, vb = qf.astype(BF16), kf.astype(BF16), v_ref[:, cv]
            qdb = (qf * qd_ref[u]).astype(BF16)
            kdb = (kf * kd_ref[u]).astype(BF16)
            stb = st_ref[0, u]
            pb = (_dot(qb, kb, "nt") * dm).astype(BF16)
            o = _dot(pb, vb) + _dot(qdb, stb)
            dlt = o - jnp.mean(o, axis=-1, keepdims=True)
            rstd = lax.rsqrt(jnp.mean(dlt * dlt, axis=-1, keepdims=True) + EPS)
            oh = dlt * rstd
            gn = gn_ref[:, cv]
            g = g_ref[:, cv].astype(F32)
            sg = _sigmoid(g)
            dga_v = dga_ref[:, cv].astype(F32)
            dret = dga_v * g * sg
            dg_ref[:, cv] = (dga_v * oh * gn * sg * (1.0 + g * (1.0 - sg))).astype(BF16)
            dgn_ref[:, cv] += jnp.sum(dret * oh, axis=0, keepdims=True)
            doh = dret * gn
            do = rstd * (doh - jnp.mean(doh, axis=-1, keepdims=True)
                         - oh * jnp.mean(doh * oh, axis=-1, keepdims=True))
            dob = do.astype(BF16)
            dst = dst_scr[u]
            dstb = dst.astype(BF16)
            dv_ref[:, cv] = (_dot(pb, dob, "tn") + _dot(kdb, dstb)).astype(BF16)
            dpb = (_dot(dob, vb, "nt") * dm).astype(BF16)
            dqf = _dot(dpb, kb) + _dot(dob, stb, "nt") * qd_ref[u]
            dkf = _dot(dpb, qb, "tn") + _dot(vb, dstb, "nt") * kd_ref[u]
            dst_scr[u] = dst * cd_ref[u] + _dot(qdb, dob, "tn")
            dq_ref[:, ck] = _rot_t(dqf * scale, cos, sin).astype(BF16)
            dk_ref[:, ck] = _rot_t(dkf, cos, sin).astype(BF16)

    rb = lambda hh, b: (nb - 1 - b, hh)
    in_specs = _ret_in_specs(h, t, rev_nb=nb) + [
        pl.BlockSpec((t, hb * RET_DV), rb),
        pl.BlockSpec((1, hb, RET_DK, RET_DV), lambda hh, b: (nb - 1 - b, hh, 0, 0))]
    return pl.pallas_call(
        body, name=name, grid=(h // hb, nb), in_specs=in_specs,
        out_specs=[pl.BlockSpec((t, c_g + h * RET_DV), rb),
                   pl.BlockSpec((1, hb * RET_DV), lambda hh, b: (0, hh))],
        out_shape=[jax.ShapeDtypeStruct((s, c_g + h * RET_DV), BF16), jax.ShapeDtypeStruct((1, h * RET_DV), F32)],
        scratch_shapes=[pltpu.VMEM((hb, RET_DK, RET_DV), F32)],
        compiler_params=_params(("parallel", "arbitrary")))(z, z, z, z, *tables, gn_w.reshape(1, -1), dga, states)


def _sgu_fwd(z, ln_w, ln_b, ws_m, bs, col0, w, name):
    s = z.shape[0]
    t = _tile(s, SGU_BLOCK)
    groups = ws_m.shape[0]
    ch = w // groups
    cb = col0 // w

    def body(su_ref, sv_ref, lw_ref, lb_ref, ws_ref, bs_ref, o_ref):
        zv = _gelu(sv_ref[...].astype(F32))
        dlt = zv - jnp.mean(zv, axis=-1, keepdims=True)
        vn = dlt * lax.rsqrt(jnp.mean(dlt * dlt, axis=-1, keepdims=True) + EPS) * lw_ref[...] + lb_ref[...]
        vnb = vn.astype(BF16)
        for r in range(t // SGU_LEN):
            rows = slice(r * SGU_LEN, (r + 1) * SGU_LEN)
            for gi in range(groups):
                cols = slice(gi * ch, (gi + 1) * ch)
                mixed = _dot(ws_ref[gi], vnb[rows, cols]) + bs_ref[gi]
                o_ref[rows, cols] = (_gelu(su_ref[rows, cols].astype(F32)) * mixed).astype(BF16)

    row = lambda off: pl.BlockSpec((t, w), lambda i: (i, cb + off))
    vec = pl.BlockSpec((1, w), lambda i: (0, 0))
    return pl.pallas_call(
        body, name=name, grid=(s // t,),
        in_specs=[row(0), row(1), vec, vec,
                  pl.BlockSpec((groups, SGU_LEN, SGU_LEN), lambda i: (0, 0, 0)),
                  pl.BlockSpec((groups, SGU_LEN, 1), lambda i: (0, 0, 0))],
        out_specs=pl.BlockSpec((t, w), lambda i: (i, 0)),
        out_shape=jax.ShapeDtypeStruct((s, w), BF16),
        compiler_params=_params(("parallel",)))(z, z, ln_w.reshape(1, w), ln_b.reshape(1, w), ws_m, bs)


def _sgu_bwd(z, dsg, dz_tail, ln_w, ln_b, ws_m, ws_mt, bs, col0, w, name):
    s = z.shape[0]
    t = _tile(s, SGU_BLOCK)
    groups = ws_m.shape[0]
    ch = w // groups
    cb = col0 // w

    def body(su_ref, sv_ref, dsg_ref, lw_ref, lb_ref, ws_ref, wst_ref, bs_ref, tail_ref,
             dz_ref, dlw_ref, dlb_ref, dws_ref, dbs_ref, dvn_scr):
        @pl.when(pl.program_id(0) == 0)
        def _():
            dlw_ref[...] = jnp.zeros_like(dlw_ref)
            dlb_ref[...] = jnp.zeros_like(dlb_ref)
            dws_ref[...] = jnp.zeros_like(dws_ref)
            dbs_ref[...] = jnp.zeros_like(dbs_ref)

        dsu_ref, dsv_ref = dz_ref.at[:, 0:w], dz_ref.at[:, w:2 * w]
        sv = sv_ref[...].astype(F32)
        zv = _gelu(sv)
        dlt = zv - jnp.mean(zv, axis=-1, keepdims=True)
        rstd = lax.rsqrt(jnp.mean(dlt * dlt, axis=-1, keepdims=True) + EPS)
        vh = dlt * rstd
        vnb = (vh * lw_ref[...] + lb_ref[...]).astype(BF16)
        for r in range(t // SGU_LEN):
            rows = slice(r * SGU_LEN, (r + 1) * SGU_LEN)
            for gi in range(groups):
                cols = slice(gi * ch, (gi + 1) * ch)
                vn_p = vnb[rows, cols]
                mixed = _dot(ws_ref[gi], vn_p) + bs_ref[gi]
                su = su_ref[rows, cols].astype(F32)
                dsg_p = dsg_ref[rows, cols].astype(F32)
                dsu_ref[rows, cols] = (dsg_p * mixed * _gelu_grad(su)).astype(BF16)
                dmix = dsg_p * _gelu(su)
                dmixb = dmix.astype(BF16)
                dvn_scr[rows, cols] = _dot(wst_ref[gi], dmixb)
                dws_ref[gi] += _dot(dmixb, vn_p, "nt")
                dbs_ref[gi] += jnp.sum(dmix, axis=1, keepdims=True)
        dvn = dvn_scr[...]
        dlw_ref[...] += jnp.sum(dvn * vh, axis=0, keepdims=True)
        dlb_ref[...] += jnp.sum(dvn, axis=0, keepdims=True)
        dvh = dvn * lw_ref[...]
        dzv = rstd * (dvh - jnp.mean(dvh, axis=-1, keepdims=True)
                      - vh * jnp.mean(dvh * vh, axis=-1, keepdims=True))
        dsv_ref[...] = (dzv * _gelu_grad(sv)).astype(BF16)

    row = lambda off: pl.BlockSpec((t, w), lambda i: (i, cb + off))
    out_row = pl.BlockSpec((t, w), lambda i: (i, 0))
    vec = pl.BlockSpec((1, w), lambda i: (0, 0))
    mat = pl.BlockSpec((groups, SGU_LEN, SGU_LEN), lambda i: (0, 0, 0))
    col = pl.BlockSpec((groups, SGU_LEN, 1), lambda i: (0, 0, 0))
    return pl.pallas_call(
        body, name=name, grid=(s // t,),
        in_specs=[row(0), row(1), out_row, vec, vec, mat, mat, col, ANY],
        out_specs=[pl.BlockSpec((t, 2 * w), lambda i: (i, 0)), vec, vec, mat, col],
        out_shape=[jax.ShapeDtypeStruct(dz_tail.shape, BF16),
                   jax.ShapeDtypeStruct((1, w), F32), jax.ShapeDtypeStruct((1, w), F32),
                   jax.ShapeDtypeStruct((groups, SGU_LEN, SGU_LEN), F32),
                   jax.ShapeDtypeStruct((groups, SGU_LEN, 1), F32)],
        input_output_aliases={8: 0},
        scratch_shapes=[pltpu.VMEM((t, w), F32)],
        compiler_params=_params(("arbitrary",)))(z, z, dsg, ln_w.reshape(1, w), ln_b.reshape(1, w), ws_m, ws_mt, bs,
                                                 dz_tail)


def _merge_fwd(a, b, z, col0, name):
    s, d = a.shape
    tr = _tile(s, ROW_BLOCK)
    cb = col0 // d

    def body(a_ref, b_ref, ga_ref, gb_ref, o_ref):
        o_ref[...] = (_sigmoid(ga_ref[...].astype(F32)) * a_ref[...].astype(F32)
                      + _sigmoid(gb_ref[...].astype(F32)) * b_ref[...].astype(F32)).astype(BF16)

    row = pl.BlockSpec((tr, d), lambda i: (i, 0))
    gate = lambda off: pl.BlockSpec((tr, d), lambda i: (i, cb + off))
    return pl.pallas_call(body, name=name, grid=(s // tr,), in_specs=[row, row, gate(0), gate(1)],
                          out_specs=row, out_shape=jax.ShapeDtypeStruct((s, d), BF16),
                          compiler_params=_params(("parallel",)))(a, b, z, z)


def _merge_bwd(dmg, a, b, z, col0, name):
    s, d = a.shape
    tr = _tile(s, ROW_BLOCK)
    cb = col0 // d

    def body(dm_ref, a_ref, b_ref, ga_ref, gb_ref, da_ref, db_ref, dgt_ref):
        dm = dm_ref[...].astype(F32)
        sa = _sigmoid(ga_ref[...].astype(F32))
        sb = _sigmoid(gb_ref[...].astype(F32))
        da_ref[...] = (dm * sa).astype(BF16)
        db_ref[...] = (dm * sb).astype(BF16)
        dgt_ref[:, :d] = (dm * a_ref[...].astype(F32) * sa * (1.0 - sa)).astype(BF16)
        dgt_ref[:, d:] = (dm * b_ref[...].astype(F32) * sb * (1.0 - sb)).astype(BF16)

    row = pl.BlockSpec((tr, d), lambda i: (i, 0))
    wide = pl.BlockSpec((tr, 2 * d), lambda i: (i, 1))
    gate = lambda off: pl.BlockSpec((tr, d), lambda i: (i, cb + off))
    return pl.pallas_call(
        body, name=name, grid=(s // tr,), in_specs=[row, row, row, gate(0), gate(1)],
        out_specs=[row, row, wide],
        out_shape=[jax.ShapeDtypeStruct((s, d), BF16), jax.ShapeDtypeStruct((s, d), BF16),
                   jax.ShapeDtypeStruct((s, 4 * d), BF16)],
        compiler_params=_params(("parallel",)))(dmg, a, b, z, z)


def _swiglu_fwd(ac, name):
    s, f2 = ac.shape
    f = f2 // 2
    tr = _tile(s, ROW_BLOCK)

    def body(a_ref, c_ref, o_ref):
        a = a_ref[...].astype(F32)
        o_ref[...] = (a * _sigmoid(a) * c_ref[...].astype(F32)).astype(BF16)

    half = lambda off: pl.BlockSpec((tr, f), lambda i: (i, off))
    return pl.pallas_call(body, name=name, grid=(s // tr,), in_specs=[half(0), half(1)], out_specs=half(0),
                          out_shape=jax.ShapeDtypeStruct((s, f), BF16),
                          compiler_params=_params(("parallel",)))(ac, ac)


def _swiglu_bwd(ac, df, name):
    s, f2 = ac.shape
    f = f2 // 2
    tr = _tile(s, ROW_BLOCK)

    def body(a_ref, c_ref, df_ref, o_ref):
        a = a_ref[...].astype(F32)
        sg = _sigmoid(a)
        dfv = df_ref[...].astype(F32)
        o_ref[:, :f] = (dfv * c_ref[...].astype(F32) * sg * (1.0 + a * (1.0 - sg))).astype(BF16)
        o_ref[:, f:] = (dfv * a * sg).astype(BF16)

    half = lambda off: pl.BlockSpec((tr, f), lambda i: (i, off))
    return pl.pallas_call(body, name=name, grid=(s // tr,), in_specs=[half(0), half(1), half(0)],
                          out_specs=pl.BlockSpec((tr, f2), lambda i: (i, 0)),
                          out_shape=jax.ShapeDtypeStruct((s, f2), BF16),
                          compiler_params=_params(("parallel",)))(ac, ac, df)


def _adamw(w, g, m, v, name):
    r, c = w.shape
    tr = _rtile(r, LANES)
    c1 = 1.0 - ADAM_B1 ** ADAM_STEP
    c2 = 1.0 - ADAM_B2 ** ADAM_STEP

    def body(w_ref, g_ref, m_ref, v_ref, d_ref, mo_ref, vo_ref):
        gv = g_ref[...]
        mn = ADAM_B1 * m_ref[...] + (1.0 - ADAM_B1) * gv
        vn = ADAM_B2 * v_ref[...] + (1.0 - ADAM_B2) * (gv * gv)
        mo_ref[...] = mn
        vo_ref[...] = vn
        d_ref[...] = -ADAM_LR * ((mn / c1) / (jnp.sqrt(vn / c2) + ADAM_EPS) + ADAM_WD * w_ref[...])

    blk = pl.BlockSpec((tr, c), lambda i: (i, 0))
    shp = jax.ShapeDtypeStruct((r, c), F32)
    return pl.pallas_call(body, name=name, grid=(r // tr,), in_specs=[blk] * 4, out_specs=[blk] * 3,
                          out_shape=[shp] * 3, compiler_params=_params(("parallel",)))(w, g, m, v)


def _place():
    x, y, c = lax.axis_index("x"), lax.axis_index("y"), lax.axis_index("c")
    chips = [(1 - x, y), (x, 1 - y), (1 - x, 1 - y)]
    return x, y, c, chips


def _block(ref, kind, chip, half, shard_shape):
    rs, cs = shard_shape
    if kind == "col":
        rows = pl.ds(0, rs) if half is None else pl.ds(half * (rs // 2), rs // 2)
        return ref.at[rows, pl.ds(chip * cs, cs)]
    rows = pl.ds(chip * rs, rs) if half is None else pl.ds(chip * rs + half * (rs // 2), rs // 2)
    return ref.at[rows, :]


def _half_rows(ref, half):
    rs = ref.shape[0]
    return ref.at[pl.ds(half * (rs // 2), rs // 2), :]


class _Task:
    def __init__(self, ins, out_shapes, sems, start, finish, aliases=None):
        self.ins, self.out_shapes, self.sems = list(ins), list(out_shapes), list(sems)
        self.start, self.finish, self.aliases = start, finish, dict(aliases or {})


def _run_task(task, name):
    n_in, n_out = len(task.ins), len(task.out_shapes)

    def body(*refs):
        ins, outs, sems = refs[:n_in], refs[n_in:n_in + n_out], refs[n_in + n_out:]
        task.start(ins, outs, sems)
        task.finish(ins, outs, sems)

    return pl.pallas_call(
        body, name=name, in_specs=[ANY] * n_in, out_specs=[ANY] * n_out, out_shape=task.out_shapes,
        input_output_aliases=task.aliases, scratch_shapes=task.sems,
        compiler_params=pltpu.CompilerParams(has_side_effects=True))(*task.ins)


def _remote(src, dst, send_sem, recv_sem, device):
    return pltpu.make_async_remote_copy(src_ref=src, dst_ref=dst, send_sem=send_sem, recv_sem=recv_sem,
                                        device_id=device, device_id_type=MESH)


def _ag_task(shards, kinds):
    n = len(shards)
    out_shapes = [jax.ShapeDtypeStruct((s.shape[0], 4 * s.shape[1]) if k == "col" else (4 * s.shape[0], s.shape[1]),
                                       s.dtype) for s, k in zip(shards, kinds)]

    def copies(ins, outs, sems):
        send_sems, recv_sems, fsend_sems, frecv_sems, own_send_sems, own_recv_sems = sems
        x, y, c, chips = _place()
        me = 2 * x + y
        own, ici, landed, fwd, passed = [], [], [], [], []
        for i in range(n):
            shp = ins[i].shape
            own.append(_remote(ins[i], _block(outs[i], kinds[i], me, None, shp), own_send_sems.at[i],
                               own_recv_sems.at[i], (x, y, 1 - c)))
            for j, (px, py) in enumerate(chips):
                k = 3 * i + j
                ici.append(_remote(_half_rows(ins[i], c), _block(outs[i], kinds[i], me, c, shp),
                                   send_sems.at[k], recv_sems.at[k], (px, py, c)))
                got = _block(outs[i], kinds[i], 2 * px + py, c, shp)
                landed.append(_remote(got, got, send_sems.at[k], recv_sems.at[k], (px, py, c)))
                fwd.append(_remote(got, got, fsend_sems.at[k], frecv_sems.at[k], (x, y, 1 - c)))
                theirs = _block(outs[i], kinds[i], 2 * px + py, 1 - c, shp)
                passed.append(_remote(theirs, theirs, fsend_sems.at[k], frecv_sems.at[k], (x, y, 1 - c)))
        return own, ici, landed, fwd, passed

    def start(ins, outs, sems):
        own, ici, _, _, _ = copies(ins, outs, sems)
        for cp in own + ici:
            cp.start()

    def finish(ins, outs, sems):
        own, ici, landed, fwd, passed = copies(ins, outs, sems)
        for got, cp in zip(landed, fwd):
            got.wait_recv()
            cp.start()
        for cp in passed:
            cp.wait_recv()
        for cp in own:
            cp.wait()
        for cp in ici + fwd:
            cp.wait_send()

    sems = [pltpu.SemaphoreType.DMA((3 * n,))] * 4 + [pltpu.SemaphoreType.DMA((n,))] * 2
    return _Task(shards, out_shapes, sems, start, finish)


def _exchange_task(grads, kinds, shard_shapes):
    n = len(grads)
    out_shapes = [jax.ShapeDtypeStruct((4, rs // 2, cs), F32) for rs, cs in shard_shapes]

    def copies(ins, outs, sems):
        send_sems, recv_sems = sems
        x, y, c, _ = _place()
        return [_remote(_block(ins[i], kinds[i], q, 1 - c, shard_shapes[i]), outs[i].at[q],
                        send_sems.at[4 * i + q], recv_sems.at[4 * i + q], (x, y, 1 - c))
                for i in range(n) for q in range(4)]

    def start(ins, outs, sems):
        for cp in copies(ins, outs, sems):
            cp.start()

    def finish(ins, outs, sems):
        for cp in copies(ins, outs, sems):
            cp.wait()

    return _Task(grads, out_shapes, [pltpu.SemaphoreType.DMA((4 * n,))] * 2, start, finish)


def _grad_block_map(kind, nt):
    if kind == "col":
        return lambda j, t, p: (p[0] * nt + t, p[1 + j])
    return lambda j, t, p: ((p[1 + j] * 2 + p[0]) * nt + t, 0)


def _chip_sum(grad, sib, kind, shard_shape, place, name):
    rs, cs = shard_shape
    hr = rs // 2
    tr = _rtile(hr, 256)
    nt = hr // tr
    g_map = _grad_block_map(kind, nt)

    def body(p_ref, g_ref, s_ref, o_ref):
        o_ref[0] = (g_ref[...] + s_ref[0]).astype(BF16)

    return pl.pallas_call(
        body, name=name,
        grid_spec=pltpu.PrefetchScalarGridSpec(
            num_scalar_prefetch=1, grid=(3, nt),
            in_specs=[pl.BlockSpec((tr, cs), g_map),
                      pl.BlockSpec((1, tr, cs), lambda j, t, p: (p[1 + j], t, 0))],
            out_specs=pl.BlockSpec((1, tr, cs), lambda j, t, p: (j, t, 0))),
        out_shape=jax.ShapeDtypeStruct((3, hr, cs), BF16),
        compiler_params=_params(("arbitrary", "arbitrary")))(place, grad, sib)


def _scatter_task(parts):
    n = len(parts)

    def copies(ins, outs, sems):
        send_sems, recv_sems = sems
        _, _, c, chips = _place()
        return [_remote(ins[i].at[j], outs[i].at[j], send_sems.at[3 * i + j], recv_sems.at[3 * i + j], (px, py, c))
                for i in range(n) for j, (px, py) in enumerate(chips)]

    def start(ins, outs, sems):
        for cp in copies(ins, outs, sems):
            cp.start()

    def finish(ins, outs, sems):
        for cp in copies(ins, outs, sems):
            cp.wait()

    return _Task(parts, [jax.ShapeDtypeStruct(p.shape, p.dtype) for p in parts],
                 [pltpu.SemaphoreType.DMA((3 * n,))] * 2, start, finish)


def _final_sum(grad, sib, recv, kind, shard_shape, place, name):
    rs, cs = shard_shape
    hr = rs // 2
    tr = _rtile(hr, 256)
    nt = hr // tr
    g_map = _grad_block_map(kind, nt)

    def body(p_ref, g_ref, s_ref, r_ref, out_ref):
        acc = g_ref[...] + s_ref[0]
        for j in range(3):
            acc = acc + r_ref[j].astype(F32)
        out_ref[...] = acc

    return pl.pallas_call(
        body, name=name,
        grid_spec=pltpu.PrefetchScalarGridSpec(
            num_scalar_prefetch=1, grid=(nt,),
            in_specs=[pl.BlockSpec((tr, cs), lambda t, p: g_map(3, t, p)),
                      pl.BlockSpec((1, tr, cs), lambda t, p: (p[4], t, 0)),
                      pl.BlockSpec((3, tr, cs), lambda t, p: (0, t, 0))],
            out_specs=pl.BlockSpec((tr, cs), lambda t, p: (p[0] * nt + t, 0))),
        out_shape=jax.ShapeDtypeStruct((rs, cs), F32),
        compiler_params=_params(("arbitrary",)))(place, grad, sib, recv)


def _join_task(shards):
    n = len(shards)

    def copies(outs, sems):
        send_sems, recv_sems = sems
        x, y, c, _ = _place()
        mine = [_half_rows(outs[i], c) for i in range(n)]
        theirs = [_half_rows(outs[i], 1 - c) for i in range(n)]
        send = [_remote(mine[i], mine[i], send_sems.at[i], recv_sems.at[i], (x, y, 1 - c)) for i in range(n)]
        recv = [_remote(theirs[i], theirs[i], send_sems.at[i], recv_sems.at[i], (x, y, 1 - c)) for i in range(n)]
        return send, recv

    def start(ins, outs, sems):
        for cp in copies(outs, sems)[0]:
            cp.start()

    def finish(ins, outs, sems):
        send, recv = copies(outs, sems)
        for cp in send:
            cp.wait_send()
        for cp in recv:
            cp.wait_recv()

    return _Task(shards, [jax.ShapeDtypeStruct(s.shape, F32) for s in shards],
                 [pltpu.SemaphoreType.DMA((n,))] * 2, start, finish, aliases={i: i for i in range(n)})


def _all_reduce_small(v, name):
    rows = v.shape[0]

    def body(v_ref, o_ref, buf, send_sems, recv_sems):
        x, y, c, _ = _place()
        coord = lambda p: ((1 - x) if p & 4 else x, (1 - y) if p & 2 else y, (1 - c) if p & 1 else c)
        me = 4 * x + 2 * y + c
        buf[me] = v_ref[...]
        copies = []
        for p in range(1, 8):
            cp = _remote(v_ref, buf.at[me], send_sems.at[p - 1], recv_sems.at[p - 1], coord(p))
            cp.start()
            copies.append(cp)
        for p in range(1, 8):
            px, py, pc = coord(p)
            _remote(v_ref, buf.at[4 * px + 2 * py + pc], send_sems.at[p - 1], recv_sems.at[p - 1],
                    coord(p)).wait_recv()
        for cp in copies:
            cp.wait_send()
        acc = buf[0]
        for dev in range(1, 8):
            acc = acc + buf[dev]
        o_ref[...] = acc

    vm = pl.BlockSpec(memory_space=pltpu.VMEM)
    return pl.pallas_call(
        body, name=name, in_specs=[vm], out_specs=vm, out_shape=jax.ShapeDtypeStruct(v.shape, F32),
        scratch_shapes=[pltpu.VMEM((8, rows, LANES), F32), pltpu.SemaphoreType.DMA((7,)),
                        pltpu.SemaphoreType.DMA((7,))],
        compiler_params=pltpu.CompilerParams(vmem_limit_bytes=VMEM_LIMIT))(v)


def _reduce_scatter_grads(grads, kinds, shard_shapes, place, tag):
    sib = _run_task(_exchange_task(grads, kinds, shard_shapes), "rs_core_exchange_" + tag)
    peers = [_chip_sum(g, sib[i], kinds[i], shard_shapes[i], place, "rs_chip_sum%d_%s" % (i, tag))
             for i, g in enumerate(grads)]
    recv = _run_task(_scatter_task(peers), "rs_scatter_" + tag)
    halves = [_final_sum(g, sib[i], recv[i], kinds[i], shard_shapes[i], place, "rs_final_sum%d_%s" % (i, tag))
              for i, g in enumerate(grads)]
    return _run_task(_join_task(halves), "rs_core_join_" + tag)


BIG = ["w_in", "ret_proj", "sgu_proj", "w_out", "w_ffn_in", "w_ffn_out"]
BIG_KIND = {"w_in": "col", "ret_proj": "row", "sgu_proj": "row", "w_out": "row", "w_ffn_in": "col",
            "w_ffn_out": "row"}
KINDS = [BIG_KIND[n] for n in BIG]
SMALL = ["norm_mix_w", "ret_gn_w", "sgu_ln_w", "sgu_ln_b", "sgu_w_s", "sgu_b_s", "norm_ffn_w"]
ORDER = ["norm_mix_w", "w_in", "ret_gn_w", "ret_proj", "sgu_ln_w", "sgu_ln_b", "sgu_w_s", "sgu_b_s",
         "sgu_proj", "w_out", "norm_ffn_w", "w_ffn_in", "w_ffn_out", "final_norm_w"]
AG_BEHIND = {"mm_in": [(0, 1), (0, 2), (0, 3), (0, 4), (0, 5)], "mm_ffn_in": [(1, 0)],
             "mm_ffn_out": [(1, 1), (1, 2), (1, 3)]}
EXCHANGE_BEHIND = {"mm_dffn_out_x": [0], "mm_dffn_out_w": [1, 2, 3, 4, 5]}
SCATTER_BEHIND = {"mm_dffn_in_x": [0, 1, 2, 3], "mm_dffn_in_w": [4, 5]}


def _layer_fwd(x, l, full, shards, sm, tables, dims):
    h, d, w = dims
    c_su, c_gate = 6 * h * RET_DK, 6 * h * RET_DK + 2 * w
    tag = "l%d" % l
    wt = full[l]

    def mm(a, wname, out_dtype, key, res=None):
        todo = [(l + dl, i) for dl, i in AG_BEHIND.get(key, []) if l + dl < len(full) and BIG[i] not in full[l + dl]]
        if not todo:
            return _matmul(a, wt[wname], "nn", out_dtype, key + "_" + tag, res=res)
        task = _ag_task([shards[ll][i] for ll, i in todo], [KINDS[i] for _, i in todo])
        out, *got = _matmul(a, wt[wname], "nn", out_dtype, key + "_" + tag, res=res, task=task)
        for (ll, i), g in zip(todo, got):
            full[ll][BIG[i]] = g
        return out

    h1 = _rms_fwd(x, sm["norm_mix_w"], "rms_mix_fwd_" + tag)
    z = mm(h1, "w_in", BF16, "mm_in")
    ga, states = _ret_fwd(z, sm["ret_gn_w"], tables, h, "ret_fwd_" + tag)
    sg = _sgu_fwd(z, sm["sgu_ln_w"], sm["sgu_ln_b"], sm["ws_m"], sm["bs"], c_su, w, "sgu_fwd_" + tag)
    a = mm(ga, "ret_proj", BF16, "mm_ret_proj")
    b = mm(sg, "sgu_proj", BF16, "mm_sgu_proj")
    mg = _merge_fwd(a, b, z, c_gate, "merge_fwd_" + tag)
    x1 = mm(mg, "w_out", F32, "mm_out", res=x)
    h2 = _rms_fwd(x1, sm["norm_ffn_w"], "rms_ffn_fwd_" + tag)
    ac = mm(h2, "w_ffn_in", BF16, "mm_ffn_in")
    f = _swiglu_fwd(ac, "swiglu_fwd_" + tag)
    x2 = mm(f, "w_ffn_out", F32, "mm_ffn_out", res=x1)
    saved = dict(x=x, h1=h1, z=z, states=states, ga=ga, sg=sg, a=a, b=b, mg=mg, x1=x1, h2=h2, ac=ac, f=f)
    return x2, saved


def _layer_bwd(dx2, dx2b, sv, wt, sm, tables, dims, tag, carried, shard_shapes, place):
    h, d, w = dims
    c_su, c_gate = 6 * h * RET_DK, 6 * h * RET_DK + 2 * w
    gw, gs = {}, {}
    n_big = len(BIG)
    cg, ctag = carried if carried is not None else (None, None)
    sib, recv = [None] * n_big, [None] * n_big

    def mm(a, b, mode, out_dtype, key, task_of=None, into=None, **kw):
        idx = task_of[1].get(key) if (carried is not None and task_of is not None) else None
        if idx is None:
            return _matmul(a, b, mode, out_dtype, key + "_" + tag, **kw)
        out, *got = _matmul(a, b, mode, out_dtype, key + "_" + tag, task=task_of[0](idx), **kw)
        for i, g in zip(idx, got):
            into[i] = g
        return out

    exchange = (lambda idx: _exchange_task([cg[i] for i in idx], [KINDS[i] for i in idx],
                                           [shard_shapes[i] for i in idx]), EXCHANGE_BEHIND)
    df = mm(dx2b, wt["w_ffn_out"], "nt", BF16, "mm_dffn_out_x", exchange, sib)
    gw["w_ffn_out"] = mm(sv["f"], dx2b, "tn", F32, "mm_dffn_out_w", exchange, sib)
    peers = None
    if carried is not None:
        peers = [_chip_sum(cg[i], sib[i], KINDS[i], shard_shapes[i], place, "rs_chip_sum%d_%s" % (i, ctag))
                 for i in range(n_big)]
    scatter = (lambda idx: _scatter_task([peers[i] for i in idx]), SCATTER_BEHIND)
    dac = _swiglu_bwd(sv["ac"], df, "swiglu_bwd_" + tag)
    dh2 = mm(dac, wt["w_ffn_in"], "nt", BF16, "mm_dffn_in_x", scatter, recv)
    gw["w_ffn_in"] = mm(sv["h2"], dac, "tn", F32, "mm_dffn_in_w", scatter, recv)
    halves = None
    if carried is not None:
        halves = [_final_sum(cg[i], sib[i], recv[i], KINDS[i], shard_shapes[i], place,
                             "rs_final_sum%d_%s" % (i, ctag)) for i in range(n_big)]
    dx1, dx1b, gs["norm_ffn_w"] = _rms_bwd(sv["x1"], sm["norm_ffn_w"], dh2, dx2, "rms_ffn_bwd_" + tag)
    dmg = mm(dx1b, wt["w_out"], "nt", BF16, "mm_dout_x")
    gw["w_out"] = mm(sv["mg"], dx1b, "tn", F32, "mm_dout_w")
    da, db, dz_tail = _merge_bwd(dmg, sv["a"], sv["b"], sv["z"], c_gate, "merge_bwd_" + tag)
    dga = mm(da, wt["ret_proj"], "nt", BF16, "mm_dret_proj_x")
    gw["ret_proj"] = mm(sv["ga"], da, "tn", F32, "mm_dret_proj_w")
    dsg = mm(db, wt["sgu_proj"], "nt", BF16, "mm_dsgu_proj_x")
    gw["sgu_proj"] = mm(sv["sg"], db, "tn", F32, "mm_dsgu_proj_w")
    dz_tail, gs["sgu_ln_w"], gs["sgu_ln_b"], gs["sgu_w_s"], gs["sgu_b_s"] = _sgu_bwd(
        sv["z"], dsg, dz_tail, sm["sgu_ln_w"], sm["sgu_ln_b"], sm["ws_m"], sm["ws_mt"], sm["bs"], c_su, w,
        "sgu_bwd_" + tag)
    dz_ret, gs["ret_gn_w"] = _ret_bwd(sv["z"], dga, sv["states"], sm["ret_gn_w"], tables, h, "ret_bwd_" + tag)
    in_cols = c_su + 4 * w
    reduced = [None] * n_big
    join = (lambda idx: _join_task([halves[i] for i in idx]), {"mm_din_x_ret": list(range(n_big))})
    dh1 = mm(dz_ret, wt["w_in"], "nt", F32, "mm_din_x_ret", join, reduced)
    dh1 = _matmul(dz_tail, wt["w_in"], "nt", BF16, "mm_din_x_tail_" + tag, res=dh1, b_k0=c_su)
    gw_in = _matmul(sv["h1"], dz_ret, "tn", F32, "mm_din_w_ret_" + tag, window=(in_cols, 0, None))
    gw["w_in"] = _matmul(sv["h1"], dz_tail, "tn", F32, "mm_din_w_tail_" + tag, window=(in_cols, c_su, gw_in))
    dx, dxb, gs["norm_mix_w"] = _rms_bwd(sv["x"], sm["norm_mix_w"], dh1, dx1, "rms_mix_bwd_" + tag)
    return dx, dxb, gw, gs, reduced


def _sgu_mask():
    pos = jnp.arange(SGU_LEN)
    return (pos[None, :] // CHUNK) <= (pos[:, None] // CHUNK)


def kernel(x, norm_mix_w, w_in, ret_gn_w, ret_proj, sgu_ln_w, sgu_ln_b, sgu_w_s, sgu_b_s, sgu_proj, w_out, norm_ffn_w, w_ffn_in, w_ffn_out, final_norm_w, loss_target, m_norm_mix_w, m_w_in, m_ret_gn_w, m_ret_proj, m_sgu_ln_w, m_sgu_ln_b, m_sgu_w_s, m_sgu_b_s, m_sgu_proj, m_w_out, m_norm_ffn_w, m_w_ffn_in, m_w_ffn_out, m_final_norm_w, v_norm_mix_w, v_w_in, v_ret_gn_w, v_ret_proj, v_sgu_ln_w, v_sgu_ln_b, v_sgu_w_s, v_sgu_b_s, v_sgu_proj, v_w_out, v_norm_ffn_w, v_w_ffn_in, v_w_ffn_out, v_final_norm_w):
    weights = dict(norm_mix_w=norm_mix_w, w_in=w_in, ret_gn_w=ret_gn_w, ret_proj=ret_proj, sgu_ln_w=sgu_ln_w,
                   sgu_ln_b=sgu_ln_b, sgu_w_s=sgu_w_s, sgu_b_s=sgu_b_s, sgu_proj=sgu_proj, w_out=w_out,
                   norm_ffn_w=norm_ffn_w, w_ffn_in=w_ffn_in, w_ffn_out=w_ffn_out, final_norm_w=final_norm_w)
    m_in = dict(norm_mix_w=m_norm_mix_w, w_in=m_w_in, ret_gn_w=m_ret_gn_w, ret_proj=m_ret_proj,
                sgu_ln_w=m_sgu_ln_w, sgu_ln_b=m_sgu_ln_b, sgu_w_s=m_sgu_w_s, sgu_b_s=m_sgu_b_s,
                sgu_proj=m_sgu_proj, w_out=m_w_out, norm_ffn_w=m_norm_ffn_w, w_ffn_in=m_w_ffn_in,
                w_ffn_out=m_w_ffn_out, final_norm_w=m_final_norm_w)
    v_in = dict(norm_mix_w=v_norm_mix_w, w_in=v_w_in, ret_gn_w=v_ret_gn_w, ret_proj=v_ret_proj,
                sgu_ln_w=v_sgu_ln_w, sgu_ln_b=v_sgu_ln_b, sgu_w_s=v_sgu_w_s, sgu_b_s=v_sgu_b_s,
                sgu_proj=v_sgu_proj, w_out=v_w_out, norm_ffn_w=v_norm_ffn_w, w_ffn_in=v_w_ffn_in,
                w_ffn_out=v_w_ffn_out, final_norm_w=v_final_norm_w)

    depth = w_in.shape[0]
    _, s, d = x.shape
    w = d
    in_cols = 4 * w_in.shape[2]
    h = (in_cols - 4 * d) // (2 * RET_DK + 2 * RET_DV)
    groups = sgu_w_s.shape[1]
    assert in_cols == h * (2 * RET_DK + 2 * RET_DV) + 4 * d and (6 * h * RET_DK) % d == 0
    assert s % SGU_LEN == 0 and w % groups == 0 and (w // groups) % LANES == 0
    dims = (h, d, w)
    tables = _ret_tables(s, h, _tile(s, RET_BLOCK))
    mask = _sgu_mask()
    cx, cy, cc = lax.axis_index("x"), lax.axis_index("y"), lax.axis_index("c")
    place = jnp.stack([cc, 2 * (1 - cx) + cy, 2 * cx + (1 - cy), 2 * (1 - cx) + (1 - cy),
                       2 * cx + cy]).astype(jnp.int32)

    shard_shapes = [weights[n].shape[1:] for n in BIG]
    shards = [[weights[n][l].astype(BF16) for n in BIG] for l in range(depth)]

    small = []
    for l in range(depth):
        sm = {n: weights[n][l] for n in SMALL}
        ws_m = jnp.where(mask[None], sgu_w_s[l], 0.0)
        sm["ws_m"] = ws_m.astype(BF16)
        sm["ws_mt"] = jnp.swapaxes(ws_m, 1, 2).astype(BF16)
        sm["bs"] = sgu_b_s[l][:, :, None]
        small.append(sm)

    xs = x[0]
    saved = []
    full = [{} for _ in range(depth)]
    full[0][BIG[0]], = _run_task(_ag_task(shards[0][:1], KINDS[:1]), "ag_w_in_l0")
    for l in range(depth):
        xs, sv = _layer_fwd(xs, l, full, shards, small[l], tables, dims)
        saved.append(sv)
    dx, dxb, g_final, sq = _loss_head(xs, final_norm_w, loss_target[0])
    loss = lax.psum(sq[0, 0], ("x", "y", "c")) * (0.5 / d)

    grads_big = [None] * depth
    grads_small = [None] * depth
    carried = None
    for l in reversed(range(depth)):
        dx, dxb, gw, gs, reduced = _layer_bwd(dx, dxb, saved[l], full[l], small[l], tables, dims, "l%d" % l,
                                              carried, shard_shapes, place)
        grads_small[l] = gs
        if carried is not None:
            grads_big[l + 1] = reduced
        carried = ([gw[n] for n in BIG], "l%d" % l)
    grads_big[0] = _reduce_scatter_grads(carried[0], KINDS, shard_shapes, place, carried[1])
    grad_x = dx[None]

    pieces = []
    for l in range(depth):
        gs = dict(grads_small[l])
        gs["sgu_w_s"] = jnp.where(mask[None], gs["sgu_w_s"], 0.0)
        pieces += [gs[n].reshape(-1) for n in SMALL]
    pieces.append(g_final.reshape(-1))
    flat = jnp.concatenate(pieces)
    total = flat.shape[0]
    rows = -(-total // (8 * LANES)) * 8
    flat = jnp.pad(flat, (0, rows * LANES - total)).reshape(rows, LANES)
    summed = _all_reduce_small(flat, "ar_small").reshape(-1)
    grad = {}
    off = 0
    per_layer = {n: [] for n in SMALL}
    for l in range(depth):
        for n in SMALL:
            shp = weights[n].shape[1:]
            size = math.prod(shp)
            per_layer[n].append(summed[off:off + size].reshape(shp))
            off += size
    for n in SMALL:
        grad[n] = jnp.stack(per_layer[n])
    grad["final_norm_w"] = summed[off:off + d]
    for i, n in enumerate(BIG):
        grad[n] = jnp.stack([grads_big[l][i] for l in range(depth)])

    delta, new_m, new_v = {}, {}, {}
    for n in BIG:
        shp = weights[n].shape
        two_d = lambda a: a.reshape(shp[0] * shp[1], shp[2])
        dl, mn, vn = _adamw(two_d(weights[n]), two_d(grad[n]), two_d(m_in[n]), two_d(v_in[n]), "adamw_" + n)
        delta[n], new_m[n], new_v[n] = dl.reshape(shp), mn.reshape(shp), vn.reshape(shp)
    small_names = SMALL + ["final_norm_w"]

    def pack(tree):
        fl = jnp.concatenate([tree[n].reshape(-1) for n in small_names])
        return jnp.pad(fl, (0, rows * LANES - fl.shape[0])).reshape(rows, LANES)

    dl, mn, vn = _adamw(pack(weights), pack(grad), pack(m_in), pack(v_in), "adamw_small")
    off = 0
    for n in small_names:
        shp = weights[n].shape
        size = math.prod(shp)
        for src, dst in ((dl, delta), (mn, new_m), (vn, new_v)):
            dst[n] = src.reshape(-1)[off:off + size].reshape(shp)
        off += size

    return (loss, grad_x, *[grad[n] for n in ORDER], *[delta[n] for n in ORDER],
            *[new_m[n] for n in ORDER], *[new_v[n] for n in ORDER])
```

```python
import math

import jax
import jax.numpy as jnp
from jax import lax
from jax.experimental import pallas as pl
from jax.experimental.pallas import tpu as pltpu

F32 = jnp.float32
BF16 = jnp.bfloat16

CHUNK = 64
RET_DK = 128
RET_DV = 256
SGU_LEN = 128
ROPE_BASE = 10000.0
EPS = 1e-6
ADAM_LR = 0.001
ADAM_B1 = 0.9
ADAM_B2 = 0.999
ADAM_EPS = 1e-08
ADAM_WD = 0.01
ADAM_STEP = 10

LANES = 128
VMEM_LIMIT = 56 * 1024 * 1024
RET_BLOCK = 256
SGU_BLOCK = 256
ROW_BLOCK = 256
MM_TILE_PREFERRED = 1024
MM_TILE = 1408
MM_KTILE = 3584
FFN_TILE = 512
MESH = pl.DeviceIdType.MESH
ANY = pl.BlockSpec(memory_space=pl.ANY)
INV_SQRT2 = 1.0 / math.sqrt(2.0)
INV_SQRT_2PI = 1.0 / math.sqrt(2.0 * math.pi)

DN = {"nn": (((1,), (0,)), ((), ())), "nt": (((1,), (1,)), ((), ())), "tn": (((0,), (0,)), ((), ()))}


def _dot(a, b, mode="nn"):
    return lax.dot_general(a, b, DN[mode], preferred_element_type=F32)


def _tile(n, target):
    t = min(n, target) // LANES * LANES
    while t >= LANES:
        if n % t == 0:
            return t
        t -= LANES
    return n


def _rtile(n, target):
    t = min(n, target) // 16 * 16
    while t >= 16:
        if n % t == 0:
            return t
        t -= 16
    return n


def _out_tile(n):
    return MM_TILE_PREFERRED if n % MM_TILE_PREFERRED == 0 else _tile(n, MM_TILE)


def _params(sem):
    return pltpu.CompilerParams(dimension_semantics=sem, vmem_limit_bytes=VMEM_LIMIT)


def _sigmoid(x):
    return 1.0 / (1.0 + jnp.exp(-x))


def _gelu(x):
    return 0.5 * x * (1.0 + lax.erf(x * INV_SQRT2))


def _gelu_grad(x):
    return 0.5 * (1.0 + lax.erf(x * INV_SQRT2)) + x * jnp.exp(-0.5 * x * x) * INV_SQRT_2PI


def _matmul(a, b, mode, out_dtype, name, res=None, task=None, b_k0=0, window=None):
    if mode == "nn":
        (m, k), n = a.shape, b.shape[1]
    elif mode == "nt":
        (m, k), n = a.shape, b.shape[0]
    else:
        (k, m), n = a.shape, b.shape[1]
    out_cols, out_c0, into = window if window is not None else (n, 0, None)
    assert b_k0 == 0 or mode == "nt"
    tm, tn, tk = _out_tile(m), _out_tile(math.gcd(n, out_c0)), _tile(math.gcd(k, b_k0), MM_KTILE)
    ni, nj, nk = m // tm, n // tn, k // tk
    kb0, jb0 = b_k0 // tk, out_c0 // tn
    if mode == "tn":
        a_spec = pl.BlockSpec((tk, tm), lambda i, j, kk: (kk, i))
    else:
        a_spec = pl.BlockSpec((tm, tk), lambda i, j, kk: (i, kk))
    if mode == "nt":
        b_spec = pl.BlockSpec((tn, tk), lambda i, j, kk: (j, kb0 + kk))
    else:
        b_spec = pl.BlockSpec((tk, tn), lambda i, j, kk: (kk, j))
    r_spec = pl.BlockSpec((tm, tn), lambda i, j, kk: (i, j))
    o_spec = pl.BlockSpec((tm, tn), lambda i, j, kk: (i, jb0 + j))
    n_mm_in = 2 + (res is not None)
    t_ins = (task.ins if task is not None else []) + ([into] if into is not None else [])
    t_outs = task.out_shapes if task is not None else []
    t_sems = task.sems if task is not None else []
    in_specs = [a_spec, b_spec] + ([r_spec] if res is not None else []) + [ANY] * len(t_ins)
    acc_in_out = out_dtype == F32
    scratch = [] if (nk == 1 or acc_in_out) else [pltpu.VMEM((tm, tn), F32)]

    def body(*refs):
        a_ref, b_ref = refs[0], refs[1]
        r_ref = refs[2] if res is not None else None
        tin = refs[n_mm_in:n_mm_in + len(t_ins)]
        o_ref = refs[n_mm_in + len(t_ins)]
        tout = refs[n_mm_in + len(t_ins) + 1:n_mm_in + len(t_ins) + 1 + len(t_outs)]
        rest = refs[n_mm_in + len(t_ins) + 1 + len(t_outs):]
        acc_scr, sems = (rest[0], rest[1:]) if scratch else (None, rest)
        i, j, kk = pl.program_id(0), pl.program_id(1), pl.program_id(2)
        if task is not None:
            @pl.when((i == 0) & (j == 0) & (kk == 0))
            def _():
                task.start(tin, tout, sems)

        p = _dot(a_ref[...], b_ref[...], mode)
        if nk == 1:
            if r_ref is not None:
                p = p + r_ref[...]
            o_ref[...] = p.astype(o_ref.dtype)
        else:
            acc = o_ref if acc_in_out else acc_scr

            @pl.when(kk == 0)
            def _():
                acc[...] = p if r_ref is None or not acc_in_out else p + r_ref[...]

            @pl.when(kk > 0)
            def _():
                acc[...] += p

            if not acc_in_out:
                @pl.when(kk == nk - 1)
                def _():
                    o = acc[...]
                    if r_ref is not None:
                        o = o + r_ref[...]
                    o_ref[...] = o.astype(o_ref.dtype)

        if task is not None:
            @pl.when((i == ni - 1) & (j == nj - 1) & (kk == nk - 1))
            def _():
                task.finish(tin, tout, sems)

    args = (a, b) + ((res,) if res is not None else ()) + tuple(t_ins)
    out_shape = jax.ShapeDtypeStruct((m, out_cols), out_dtype)
    into_alias = {n_mm_in + len(t_ins) - 1: 0} if into is not None else {}
    if task is None:
        return pl.pallas_call(
            body, name=name, grid=(ni, nj, nk), in_specs=in_specs, out_specs=o_spec, out_shape=out_shape,
            input_output_aliases=into_alias, scratch_shapes=scratch,
            compiler_params=_params(("parallel", "parallel", "arbitrary")))(*args)
    aliases = dict(into_alias)
    aliases.update({n_mm_in + src: 1 + dst for src, dst in task.aliases.items()})
    return pl.pallas_call(
        body, name=name, grid=(ni, nj, nk), in_specs=in_specs, out_specs=[o_spec] + [ANY] * len(t_outs),
        out_shape=[out_shape] + list(t_outs), input_output_aliases=aliases, scratch_shapes=scratch + list(t_sems),
        compiler_params=pltpu.CompilerParams(dimension_semantics=("arbitrary", "arbitrary", "arbitrary"),
                                             vmem_limit_bytes=VMEM_LIMIT, has_side_effects=True))(*args)


def _ffn_in_swiglu(x, w, name, task=None):
    s, k = x.shape
    f = w.shape[1] // 2
    assert k <= MM_KTILE
    tm, tn = _out_tile(s), _tile(f, FFN_TILE)
    ni, nj = s // tm, f // tn
    t_ins = task.ins if task is not None else []
    t_outs = task.out_shapes if task is not None else []
    t_sems = task.sems if task is not None else []

    def body(*refs):
        x_ref, wa_ref, wc_ref = refs[:3]
        tin = refs[3:3 + len(t_ins)]
        a_ref, c_ref, o_ref = refs[3 + len(t_ins):6 + len(t_ins)]
        tout = refs[6 + len(t_ins):6 + len(t_ins) + len(t_outs)]
        sems = refs[6 + len(t_ins) + len(t_outs):]
        i, j = pl.program_id(0), pl.program_id(1)
        if task is not None:
            @pl.when((i == 0) & (j == 0))
            def _():
                task.start(tin, tout, sems)

        xv = x_ref[...]
        a = _dot(xv, wa_ref[...])
        c = _dot(xv, wc_ref[...])
        a_ref[...] = a.astype(BF16)
        c_ref[...] = c.astype(BF16)
        o_ref[...] = (a * _sigmoid(a) * c).astype(BF16)

        if task is not None:
            @pl.when((i == ni - 1) & (j == nj - 1))
            def _():
                task.finish(tin, tout, sems)

    o_spec = pl.BlockSpec((tm, tn), lambda i, j: (i, j))
    in_specs = [pl.BlockSpec((tm, k), lambda i, j: (i, 0)), pl.BlockSpec((k, tn), lambda i, j: (0, j)),
                pl.BlockSpec((k, tn), lambda i, j: (0, nj + j))] + [ANY] * len(t_ins)
    shp = jax.ShapeDtypeStruct((s, f), BF16)
    if task is None:
        params = _params(("parallel", "parallel"))
    else:
        params = pltpu.CompilerParams(dimension_semantics=("arbitrary", "arbitrary"),
                                      vmem_limit_bytes=VMEM_LIMIT, has_side_effects=True)
    return pl.pallas_call(
        body, name=name, grid=(ni, nj), in_specs=in_specs, out_specs=[o_spec] * 3 + [ANY] * len(t_outs),
        out_shape=[shp] * 3 + list(t_outs),
        input_output_aliases={3 + src: 3 + dst for src, dst in (task.aliases.items() if task is not None else [])},
        scratch_shapes=list(t_sems), compiler_params=params)(x, w, w, *t_ins)


def _rms_fwd(x, w, name):
    s, d = x.shape
    tr = _tile(s, ROW_BLOCK)

    def body(x_ref, w_ref, o_ref):
        xv = x_ref[...]
        r = lax.rsqrt(jnp.mean(xv * xv, axis=-1, keepdims=True) + EPS)
        o_ref[...] = (xv * r * w_ref[...]).astype(BF16)

    row = pl.BlockSpec((tr, d), lambda i: (i, 0))
    vec = pl.BlockSpec((1, d), lambda i: (0, 0))
    return pl.pallas_call(body, name=name, grid=(s // tr,), in_specs=[row, vec], out_specs=row,
                          out_shape=jax.ShapeDtypeStruct((s, d), BF16),
                          compiler_params=_params(("parallel",)))(x, w.reshape(1, d))


def _rms_bwd(x, w, dh, dres, name):
    s, d = x.shape
    tr = _tile(s, ROW_BLOCK)

    def body(x_ref, w_ref, dh_ref, dr_ref, dx_ref, dxb_ref, dw_ref):
        xv = x_ref[...]
        r = lax.rsqrt(jnp.mean(xv * xv, axis=-1, keepdims=True) + EPS)
        xh = xv * r
        dy = dh_ref[...].astype(F32)
        dxh = dy * w_ref[...]
        dx = dr_ref[...] + r * (dxh - xh * jnp.mean(dxh * xh, axis=-1, keepdims=True))
        dx_ref[...] = dx
        dxb_ref[...] = dx.astype(BF16)

        @pl.when(pl.program_id(0) == 0)
        def _():
            dw_ref[...] = jnp.zeros_like(dw_ref)

        dw_ref[...] += jnp.sum(dy * xh, axis=0, keepdims=True)

    row = pl.BlockSpec((tr, d), lambda i: (i, 0))
    vec = pl.BlockSpec((1, d), lambda i: (0, 0))
    return pl.pallas_call(
        body, name=name, grid=(s // tr,), in_specs=[row, vec, row, row], out_specs=[row, row, vec],
        out_shape=[jax.ShapeDtypeStruct((s, d), F32), jax.ShapeDtypeStruct((s, d), BF16),
                   jax.ShapeDtypeStruct((1, d), F32)],
        compiler_params=_params(("arbitrary",)))(x, w.reshape(1, d), dh, dres)


def _loss_head(x, w, tgt):
    s, d = x.shape
    tr = _tile(s, ROW_BLOCK)

    def body(x_ref, w_ref, t_ref, dx_ref, dxb_ref, dw_ref, l_ref):
        xv = x_ref[...]
        r = lax.rsqrt(jnp.mean(xv * xv, axis=-1, keepdims=True) + EPS)
        xh = xv * r
        e = xh * w_ref[...] - t_ref[...]
        dy = e * (1.0 / d)
        dxh = dy * w_ref[...]
        dx = r * (dxh - xh * jnp.mean(dxh * xh, axis=-1, keepdims=True))
        dx_ref[...] = dx
        dxb_ref[...] = dx.astype(BF16)

        @pl.when(pl.program_id(0) == 0)
        def _():
            dw_ref[...] = jnp.zeros_like(dw_ref)
            l_ref[...] = jnp.zeros_like(l_ref)

        dw_ref[...] += jnp.sum(dy * xh, axis=0, keepdims=True)
        l_ref[...] += jnp.sum(jnp.sum(e * e, axis=1, keepdims=True), axis=0, keepdims=True)

    row = pl.BlockSpec((tr, d), lambda i: (i, 0))
    vec = pl.BlockSpec((1, d), lambda i: (0, 0))
    one = pl.BlockSpec((1, 1), lambda i: (0, 0))
    return pl.pallas_call(
        body, name="loss_head", grid=(s // tr,), in_specs=[row, vec, row], out_specs=[row, row, vec, one],
        out_shape=[jax.ShapeDtypeStruct((s, d), F32), jax.ShapeDtypeStruct((s, d), BF16),
                   jax.ShapeDtypeStruct((1, d), F32), jax.ShapeDtypeStruct((1, 1), F32)],
        compiler_params=_params(("arbitrary",)))(x, w.reshape(1, d), tgt)


def _ret_tables(s, h, t):
    half = RET_DK // 2
    inv = ROPE_BASE ** (-jnp.arange(half, dtype=F32) / half)
    ang = jnp.arange(s, dtype=F32)[:, None] * inv[None, :]
    cos, sin = jnp.cos(ang), jnp.sin(ang)
    cosf = jnp.concatenate([cos, cos], axis=1)
    sinf = jnp.concatenate([-sin, sin], axis=1)
    log_g = jnp.log1p(-(2.0 ** (-5.0 - jnp.arange(h, dtype=F32))))
    idx = jnp.arange(t, dtype=F32)
    chunk = jnp.arange(t) // CHUNK
    allowed = chunk[None, :] <= chunk[:, None]
    dm = jnp.where(allowed[None], jnp.exp(log_g[:, None, None] * jnp.abs(idx[:, None] - idx[None, :])), 0.0)
    qd = jnp.exp(log_g[:, None] * (idx[None, :] + 1.0))
    kd = jnp.exp(log_g[:, None] * (t - 1.0 - idx[None, :]))
    qd = jnp.broadcast_to(qd[:, :, None], (h, t, RET_DK))
    kd = jnp.broadcast_to(kd[:, :, None], (h, t, RET_DK))
    cd = jnp.broadcast_to(jnp.exp(log_g * t)[:, None, None], (h, 1, RET_DV))
    return cosf, sinf, dm, qd, kd, cd


def _rot(x, cos, sin):
    return x * cos + pltpu.roll(x, RET_DK // 2, 1) * sin


def _rot_t(x, cos, sin):
    return x * cos - pltpu.roll(x, RET_DK // 2, 1) * sin


def _ret_heads_per_step(h):
    return h


def _ret_in_specs(h, t, rev_nb=None):
    hb = _ret_heads_per_step(h)
    ng = h // hb
    blk = (lambda b: b) if rev_nb is None else (lambda b: rev_nb - 1 - b)
    return [
        pl.BlockSpec((t, hb * RET_DK), lambda hh, b: (blk(b), hh)),
        pl.BlockSpec((t, hb * RET_DK), lambda hh, b: (blk(b), ng + hh)),
        pl.BlockSpec((t, hb * RET_DV), lambda hh, b: (blk(b), ng + hh)),
        pl.BlockSpec((t, hb * RET_DV), lambda hh, b: (blk(b), 2 * ng + hh)),
        pl.BlockSpec((t, RET_DK), lambda hh, b: (blk(b), 0)),
        pl.BlockSpec((t, RET_DK), lambda hh, b: (blk(b), 0)),
        pl.BlockSpec((hb, t, t), lambda hh, b: (hh, 0, 0)),
        pl.BlockSpec((hb, t, RET_DK), lambda hh, b: (hh, 0, 0)),
        pl.BlockSpec((hb, t, RET_DK), lambda hh, b: (hh, 0, 0)),
        pl.BlockSpec((hb, 1, RET_DV), lambda hh, b: (hh, 0, 0)),
        pl.BlockSpec((1, hb * RET_DV), lambda hh, b: (0, hh)),
    ]


def _ret_fwd(z, gn_w, tables, h, name):
    s = z.shape[0]
    t = _tile(s, RET_BLOCK)
    nb = s // t
    hb = _ret_heads_per_step(h)
    scale = RET_DK ** -0.5

    def body(q_ref, k_ref, v_ref, g_ref, cos_ref, sin_ref, dm_ref, qd_ref, kd_ref, cd_ref, gn_ref,
             o_ref, st_ref, st_scr):
        @pl.when(pl.program_id(1) == 0)
        def _():
            st_scr[...] = jnp.zeros_like(st_scr)

        cos, sin = cos_ref[...], sin_ref[...]
        for u in range(hb):
            ck = slice(u * RET_DK, (u + 1) * RET_DK)
            cv = slice(u * RET_DV, (u + 1) * RET_DV)
            qf = _rot(q_ref[:, ck].astype(F32), cos, sin) * scale
            kf = _rot(k_ref[:, ck].astype(F32), cos, sin)
            vb = v_ref[:, cv]
            p = _dot(qf.astype(BF16), kf.astype(BF16), "nt") * dm_ref[u]
            st = st_scr[u]
            stb = st.astype(BF16)
            st_ref[0, u] = stb
            o = _dot(p.astype(BF16), vb) + _dot((qf * qd_ref[u]).astype(BF16), stb)
            st_scr[u] = st * cd_ref[u] + _dot((kf * kd_ref[u]).astype(BF16), vb, "tn")
            dlt = o - jnp.mean(o, axis=-1, keepdims=True)
            oh = dlt * lax.rsqrt(jnp.mean(dlt * dlt, axis=-1, keepdims=True) + EPS)
            g = g_ref[:, cv].astype(F32)
            o_ref[:, cv] = (g * _sigmoid(g) * oh * gn_ref[:, cv]).astype(BF16)

    return pl.pallas_call(
        body, name=name, grid=(h // hb, nb), in_specs=_ret_in_specs(h, t),
        out_specs=[pl.BlockSpec((t, hb * RET_DV), lambda hh, b: (b, hh)),
                   pl.BlockSpec((1, hb, RET_DK, RET_DV), lambda hh, b: (b, hh, 0, 0))],
        out_shape=[jax.ShapeDtypeStruct((s, h * RET_DV), BF16),
                   jax.ShapeDtypeStruct((nb, h, RET_DK, RET_DV), BF16)],
        scratch_shapes=[pltpu.VMEM((hb, RET_DK, RET_DV), F32)],
        compiler_params=_params(("parallel", "arbitrary")))(z, z, z, z, *tables, gn_w.reshape(1, -1))


def _ret_bwd(z, dga, states, gn_w, tables, h, name):
    s = z.shape[0]
    t = _tile(s, RET_BLOCK)
    nb = s // t
    hb = _ret_heads_per_step(h)
    assert hb == h
    scale = RET_DK ** -0.5
    c_k, c_v, c_g = h * RET_DK, 2 * h * RET_DK, 2 * h * RET_DK + h * RET_DV

    def body(q_ref, k_ref, v_ref, g_ref, cos_ref, sin_ref, dm_ref, qd_ref, kd_ref, cd_ref, gn_ref,
             dga_ref, st_ref, dz_ref, dgn_ref, dst_scr):
        @pl.when(pl.program_id(1) == 0)
        def _():
            dst_scr[...] = jnp.zeros_like(dst_scr)
            dgn_ref[...] = jnp.zeros_like(dgn_ref)

        dq_ref, dk_ref = dz_ref.at[:, 0:c_k], dz_ref.at[:, c_k:c_v]
        dv_ref, dg_ref = dz_ref.at[:, c_v:c_g], dz_ref.at[:, c_g:c_g + h * RET_DV]
        cos, sin = cos_ref[...], sin_ref[...]
        for u in range(hb):
            ck = slice(u * RET_DK, (u + 1) * RET_DK)
            cv = slice(u * RET_DV, (u + 1) * RET_DV)
            dm = dm_ref[u]
            qf = _rot(q_ref[:, ck].astype(F32), cos, sin) * scale
            kf = _rot(k_ref[:, ck].astype(F32), cos, sin)
            qb, kb, vb = qf.astype(BF16), kf.astype(BF16), v_ref[:, cv]
            qdb = (qf * qd_ref[u]).astype(BF16)
            kdb = (kf * kd_ref[u]).astype(BF16)
            stb = st_ref[0, u]
            pb = (_dot(qb, kb, "nt") * dm).astype(BF16)
            o = _dot(pb, vb) + _dot(qdb, stb)
            dlt = o - jnp.mean(o, axis=-1, keepdims=True)
            rstd = lax.rsqrt(jnp.mean(dlt * dlt, axis=-1, keepdims=True) + EPS)
            oh = dlt * rstd
            gn = gn_ref[:, cv]
            g = g_ref[:, cv].astype(F32)
            sg = _sigmoid(g)
            dga_v = dga_ref[:, cv].astype(F32)
            dret = dga_v * g * sg
            dg_ref[:, cv] = (dga_v * oh * gn * sg * (1.0 + g * (1.0 - sg))).astype(BF16)
            dgn_ref[:, cv] += jnp.sum(dret * oh, axis=0, keepdims=True)
            doh = dret * gn
            do = rstd * (doh - jnp.mean(doh, axis=-1, keepdims=True)
                         - oh * jnp.mean(doh * oh, axis=-1, keepdims=True))
            dob = do.astype(BF16)
            dst = dst_scr[u]
            dstb = dst.astype(BF16)
            dv_ref[:, cv] = (_dot(pb, dob, "tn") + _dot(kdb, dstb)).astype(BF16)
            dpb = (_dot(dob, vb, "nt") * dm).astype(BF16)
            dqf = _dot(dpb, kb) + _dot(dob, stb, "nt") * qd_ref[u]
            dkf = _dot(dpb, qb, "tn") + _dot(vb, dstb, "nt") * kd_ref[u]
            dst_scr[u] = dst * cd_ref[u] + _dot(qdb, dob, "tn")
            dq_ref[:, ck] = _rot_t(dqf * scale, cos, sin).astype(BF16)
            dk_ref[:, ck] = _rot_t(dkf, cos, sin).astype(BF16)

    rb = lambda hh, b: (nb - 1 - b, hh)
    in_specs = _ret_in_specs(h, t, rev_nb=nb) + [
        pl.BlockSpec((t, hb * RET_DV), rb),
        pl.BlockSpec((1, hb, RET_DK, RET_DV), lambda hh, b: (nb - 1 - b, hh, 0, 0))]
    return pl.pallas_call(
        body, name=name, grid=(h // hb, nb), in_specs=in_specs,
        out_specs=[pl.BlockSpec((t, c_g + h * RET_DV), rb),
                   pl.BlockSpec((1, hb * RET_DV), lambda hh, b: (0, hh))],
        out_shape=[jax.ShapeDtypeStruct((s, c_g + h * RET_DV), BF16), jax.ShapeDtypeStruct((1, h * RET_DV), F32)],
        scratch_shapes=[pltpu.VMEM((hb, RET_DK, RET_DV), F32)],
        compiler_params=_params(("parallel", "arbitrary")))(z, z, z, z, *tables, gn_w.reshape(1, -1), dga, states)


def _sgu_fwd(z, ln_w, ln_b, ws_m, bs, col0, w, name):
    s = z.shape[0]
    t = _tile(s, SGU_BLOCK)
    groups = ws_m.shape[0]
    ch = w // groups
    cb = col0 // w

    def body(su_ref, sv_ref, lw_ref, lb_ref, ws_ref, bs_ref, o_ref):
        zv = _gelu(sv_ref[...].astype(F32))
        dlt = zv - jnp.mean(zv, axis=-1, keepdims=True)
        vn = dlt * lax.rsqrt(jnp.mean(dlt * dlt, axis=-1, keepdims=True) + EPS) * lw_ref[...] + lb_ref[...]
        vnb = vn.astype(BF16)
        for r in range(t // SGU_LEN):
            rows = slice(r * SGU_LEN, (r + 1) * SGU_LEN)
            for gi in range(groups):
                cols = slice(gi * ch, (gi + 1) * ch)
                mixed = _dot(ws_ref[gi], vnb[rows, cols]) + bs_ref[gi]
                o_ref[rows, cols] = (_gelu(su_ref[rows, cols].astype(F32)) * mixed).astype(BF16)

    row = lambda off: pl.BlockSpec((t, w), lambda i: (i, cb + off))
    vec = pl.BlockSpec((1, w), lambda i: (0, 0))
    return pl.pallas_call(
        body, name=name, grid=(s // t,),
        in_specs=[row(0), row(1), vec, vec,
                  pl.BlockSpec((groups, SGU_LEN, SGU_LEN), lambda i: (0, 0, 0)),
                  pl.BlockSpec((groups, SGU_LEN, 1), lambda i: (0, 0, 0))],
        out_specs=pl.BlockSpec((t, w), lambda i: (i, 0)),
        out_shape=jax.ShapeDtypeStruct((s, w), BF16),
        compiler_params=_params(("parallel",)))(z, z, ln_w.reshape(1, w), ln_b.reshape(1, w), ws_m, bs)


def _sgu_bwd(z, dsg, dz_tail, ln_w, ln_b, ws_m, ws_mt, bs, col0, w, name):
    s = z.shape[0]
    t = _tile(s, SGU_BLOCK)
    groups = ws_m.shape[0]
    ch = w // groups
    cb = col0 // w

    def body(su_ref, sv_ref, dsg_ref, lw_ref, lb_ref, ws_ref, wst_ref, bs_ref, tail_ref,
             dz_ref, dlw_ref, dlb_ref, dws_ref, dbs_ref, dvn_scr):
        @pl.when(pl.program_id(0) == 0)
        def _():
            dlw_ref[...] = jnp.zeros_like(dlw_ref)
            dlb_ref[...] = jnp.zeros_like(dlb_ref)
            dws_ref[...] = jnp.zeros_like(dws_ref)
            dbs_ref[...] = jnp.zeros_like(dbs_ref)

        dsu_ref, dsv_ref = dz_ref.at[:, 0:w], dz_ref.at[:, w:2 * w]
        sv = sv_ref[...].astype(F32)
        zv = _gelu(sv)
        dlt = zv - jnp.mean(zv, axis=-1, keepdims=True)
        rstd = lax.rsqrt(jnp.mean(dlt * dlt, axis=-1, keepdims=True) + EPS)
        vh = dlt * rstd
        vnb = (vh * lw_ref[...] + lb_ref[...]).astype(BF16)
        for r in range(t // SGU_LEN):
            rows = slice(r * SGU_LEN, (r + 1) * SGU_LEN)
            for gi in range(groups):
                cols = slice(gi * ch, (gi + 1) * ch)
                vn_p = vnb[rows, cols]
                mixed = _dot(ws_ref[gi], vn_p) + bs_ref[gi]
                su = su_ref[rows, cols].astype(F32)
                dsg_p = dsg_ref[rows, cols].astype(F32)
                dsu_ref[rows, cols] = (dsg_p * mixed * _gelu_grad(su)).astype(BF16)
                dmix = dsg_p * _gelu(su)
                dmixb = dmix.astype(BF16)
                dvn_scr[rows, cols] = _dot(wst_ref[gi], dmixb)
                dws_ref[gi] += _dot(dmixb, vn_p, "nt")
                dbs_ref[gi] += jnp.sum(dmix, axis=1, keepdims=True)
        dvn = dvn_scr[...]
        dlw_ref[...] += jnp.sum(dvn * vh, axis=0, keepdims=True)
        dlb_ref[...] += jnp.sum(dvn, axis=0, keepdims=True)
        dvh = dvn * lw_ref[...]
        dzv = rstd * (dvh - jnp.mean(dvh, axis=-1, keepdims=True)
                      - vh * jnp.mean(dvh * vh, axis=-1, keepdims=True))
        dsv_ref[...] = (dzv * _gelu_grad(sv)).astype(BF16)

    row = lambda off: pl.BlockSpec((t, w), lambda i: (i, cb + off))
    out_row = pl.BlockSpec((t, w), lambda i: (i, 0))
    vec = pl.BlockSpec((1, w), lambda i: (0, 0))
    mat = pl.BlockSpec((groups, SGU_LEN, SGU_LEN), lambda i: (0, 0, 0))
    col = pl.BlockSpec((groups, SGU_LEN, 1), lambda i: (0, 0, 0))
    return pl.pallas_call(
        body, name=name, grid=(s // t,),
        in_specs=[row(0), row(1), out_row, vec, vec, mat, mat, col, ANY],
        out_specs=[pl.BlockSpec((t, 2 * w), lambda i: (i, 0)), vec, vec, mat, col],
        out_shape=[jax.ShapeDtypeStruct(dz_tail.shape, BF16),
                   jax.ShapeDtypeStruct((1, w), F32), jax.ShapeDtypeStruct((1, w), F32),
                   jax.ShapeDtypeStruct((groups, SGU_LEN, SGU_LEN), F32),
                   jax.ShapeDtypeStruct((groups, SGU_LEN, 1), F32)],
        input_output_aliases={8: 0},
        scratch_shapes=[pltpu.VMEM((t, w), F32)],
        compiler_params=_params(("arbitrary",)))(z, z, dsg, ln_w.reshape(1, w), ln_b.reshape(1, w), ws_m, ws_mt, bs,
                                                 dz_tail)


def _merge_fwd(a, b, z, col0, name):
    s, d = a.shape
    tr = _tile(s, ROW_BLOCK)
    cb = col0 // d

    def body(a_ref, b_ref, ga_ref, gb_ref, o_ref):
        o_ref[...] = (_sigmoid(ga_ref[...].astype(F32)) * a_ref[...].astype(F32)
                      + _sigmoid(gb_ref[...].astype(F32)) * b_ref[...].astype(F32)).astype(BF16)

    row = pl.BlockSpec((tr, d), lambda i: (i, 0))
    gate = lambda off: pl.BlockSpec((tr, d), lambda i: (i, cb + off))
    return pl.pallas_call(body, name=name, grid=(s // tr,), in_specs=[row, row, gate(0), gate(1)],
                          out_specs=row, out_shape=jax.ShapeDtypeStruct((s, d), BF16),
                          compiler_params=_params(("parallel",)))(a, b, z, z)


def _merge_bwd(dmg, a, b, z, col0, name):
    s, d = a.shape
    tr = _tile(s, ROW_BLOCK)
    cb = col0 // d

    def body(dm_ref, a_ref, b_ref, ga_ref, gb_ref, da_ref, db_ref, dgt_ref):
        dm = dm_ref[...].astype(F32)
        sa = _sigmoid(ga_ref[...].astype(F32))
        sb = _sigmoid(gb_ref[...].astype(F32))
        da_ref[...] = (dm * sa).astype(BF16)
        db_ref[...] = (dm * sb).astype(BF16)
        dgt_ref[:, :d] = (dm * a_ref[...].astype(F32) * sa * (1.0 - sa)).astype(BF16)
        dgt_ref[:, d:] = (dm * b_ref[...].astype(F32) * sb * (1.0 - sb)).astype(BF16)

    row = pl.BlockSpec((tr, d), lambda i: (i, 0))
    wide = pl.BlockSpec((tr, 2 * d), lambda i: (i, 1))
    gate = lambda off: pl.BlockSpec((tr, d), lambda i: (i, cb + off))
    return pl.pallas_call(
        body, name=name, grid=(s // tr,), in_specs=[row, row, row, gate(0), gate(1)],
        out_specs=[row, row, wide],
        out_shape=[jax.ShapeDtypeStruct((s, d), BF16), jax.ShapeDtypeStruct((s, d), BF16),
                   jax.ShapeDtypeStruct((s, 4 * d), BF16)],
        compiler_params=_params(("parallel",)))(dmg, a, b, z, z)


def _swiglu_bwd(a, c, df, name):
    s, f = a.shape
    f2 = 2 * f
    tr = _tile(s, ROW_BLOCK)

    def body(a_ref, c_ref, df_ref, o_ref):
        a = a_ref[...].astype(F32)
        sg = _sigmoid(a)
        dfv = df_ref[...].astype(F32)
        o_ref[:, :f] = (dfv * c_ref[...].astype(F32) * sg * (1.0 + a * (1.0 - sg))).astype(BF16)
        o_ref[:, f:] = (dfv * a * sg).astype(BF16)

    row = pl.BlockSpec((tr, f), lambda i: (i, 0))
    return pl.pallas_call(body, name=name, grid=(s // tr,), in_specs=[row, row, row],
                          out_specs=pl.BlockSpec((tr, f2), lambda i: (i, 0)),
                          out_shape=jax.ShapeDtypeStruct((s, f2), BF16),
                          compiler_params=_params(("parallel",)))(a, c, df)


def _adamw(w, g, m, v, name):
    r, c = w.shape
    tr = _rtile(r, LANES)
    c1 = 1.0 - ADAM_B1 ** ADAM_STEP
    c2 = 1.0 - ADAM_B2 ** ADAM_STEP

    def body(w_ref, g_ref, m_ref, v_ref, d_ref, mo_ref, vo_ref):
        gv = g_ref[...]
        mn = ADAM_B1 * m_ref[...] + (1.0 - ADAM_B1) * gv
        vn = ADAM_B2 * v_ref[...] + (1.0 - ADAM_B2) * (gv * gv)
        mo_ref[...] = mn
        vo_ref[...] = vn
        d_ref[...] = -ADAM_LR * ((mn / c1) / (jnp.sqrt(vn / c2) + ADAM_EPS) + ADAM_WD * w_ref[...])

    blk = pl.BlockSpec((tr, c), lambda i: (i, 0))
    shp = jax.ShapeDtypeStruct((r, c), F32)
    return pl.pallas_call(body, name=name, grid=(r // tr,), in_specs=[blk] * 4, out_specs=[blk] * 3,
                          out_shape=[shp] * 3, compiler_params=_params(("parallel",)))(w, g, m, v)


def _place():
    x, y, c = lax.axis_index("x"), lax.axis_index("y"), lax.axis_index("c")
    chips = [(1 - x, y), (x, 1 - y), (1 - x, 1 - y)]
    return x, y, c, chips


def _block(ref, kind, chip, half, shard_shape):
    rs, cs = shard_shape
    if kind == "col":
        rows = pl.ds(0, rs) if half is None else pl.ds(half * (rs // 2), rs // 2)
        return ref.at[rows, pl.ds(chip * cs, cs)]
    rows = pl.ds(chip * rs, rs) if half is None else pl.ds(chip * rs + half * (rs // 2), rs // 2)
    return ref.at[rows, :]


def _half_rows(ref, half):
    rs = ref.shape[0]
    return ref.at[pl.ds(half * (rs // 2), rs // 2), :]


class _Task:
    def __init__(self, ins, out_shapes, sems, start, finish, aliases=None):
        self.ins, self.out_shapes, self.sems = list(ins), list(out_shapes), list(sems)
        self.start, self.finish, self.aliases = start, finish, dict(aliases or {})


def _run_task(task, name):
    n_in, n_out = len(task.ins), len(task.out_shapes)

    def body(*refs):
        ins, outs, sems = refs[:n_in], refs[n_in:n_in + n_out], refs[n_in + n_out:]
        task.start(ins, outs, sems)
        task.finish(ins, outs, sems)

    return pl.pallas_call(
        body, name=name, in_specs=[ANY] * n_in, out_specs=[ANY] * n_out, out_shape=task.out_shapes,
        input_output_aliases=task.aliases, scratch_shapes=task.sems,
        compiler_params=pltpu.CompilerParams(has_side_effects=True))(*task.ins)


def _remote(src, dst, send_sem, recv_sem, device):
    return pltpu.make_async_remote_copy(src_ref=src, dst_ref=dst, send_sem=send_sem, recv_sem=recv_sem,
                                        device_id=device, device_id_type=MESH)


def _join_tasks(tasks):
    n_in = [len(t.ins) for t in tasks]
    n_out = [len(t.out_shapes) for t in tasks]
    n_sem = [len(t.sems) for t in tasks]

    def parts(refs, counts):
        out, at = [], 0
        for cnt in counts:
            out.append(refs[at:at + cnt])
            at += cnt
        return out

    def start(ins, outs, sems):
        for t, i, o, s in zip(tasks, parts(ins, n_in), parts(outs, n_out), parts(sems, n_sem)):
            t.start(i, o, s)

    def finish(ins, outs, sems):
        for t, i, o, s in zip(tasks, parts(ins, n_in), parts(outs, n_out), parts(sems, n_sem)):
            t.finish(i, o, s)

    aliases = {}
    for k, t in enumerate(tasks):
        aliases.update({sum(n_in[:k]) + src: sum(n_out[:k]) + dst for src, dst in t.aliases.items()})
    return _Task([a for t in tasks for a in t.ins], [s for t in tasks for s in t.out_shapes],
                 [s for t in tasks for s in t.sems], start, finish, aliases)


def _ag_send_task(shards, kinds):
    n = len(shards)
    out_shapes = [jax.ShapeDtypeStruct((s.shape[0], 4 * s.shape[1]) if k == "col" else (4 * s.shape[0], s.shape[1]),
                                       s.dtype) for s, k in zip(shards, kinds)]

    def copies(ins, outs, sems):
        send_sems, recv_sems, own_send_sems, own_recv_sems = sems
        x, y, c, chips = _place()
        me = 2 * x + y
        own, ici, landed = [], [], []
        for i in range(n):
            shp = ins[i].shape
            own.append(_remote(ins[i], _block(outs[i], kinds[i], me, None, shp), own_send_sems.at[i],
                               own_recv_sems.at[i], (x, y, 1 - c)))
            for j, (px, py) in enumerate(chips):
                k = 3 * i + j
                ici.append(_remote(_half_rows(ins[i], c), _block(outs[i], kinds[i], me, c, shp),
                                   send_sems.at[k], recv_sems.at[k], (px, py, c)))
                got = _block(outs[i], kinds[i], 2 * px + py, c, shp)
                landed.append(_remote(got, got, send_sems.at[k], recv_sems.at[k], (px, py, c)))
        return own, ici, landed

    def start(ins, outs, sems):
        own, ici, _ = copies(ins, outs, sems)
        for cp in own + ici:
            cp.start()

    def finish(ins, outs, sems):
        own, ici, landed = copies(ins, outs, sems)
        for cp in landed:
            cp.wait_recv()
        for cp in own:
            cp.wait()
        for cp in ici:
            cp.wait_send()

    sems = [pltpu.SemaphoreType.DMA((3 * n,))] * 2 + [pltpu.SemaphoreType.DMA((n,))] * 2
    return _Task(shards, out_shapes, sems, start, finish)


def _ag_forward_task(partial, kinds, shard_shapes):
    n = len(partial)

    def copies(outs, sems):
        fsend_sems, frecv_sems = sems
        x, y, c, chips = _place()
        fwd, passed = [], []
        for i in range(n):
            for j, (px, py) in enumerate(chips):
                k = 3 * i + j
                got = _block(outs[i], kinds[i], 2 * px + py, c, shard_shapes[i])
                fwd.append(_remote(got, got, fsend_sems.at[k], frecv_sems.at[k], (x, y, 1 - c)))
                theirs = _block(outs[i], kinds[i], 2 * px + py, 1 - c, shard_shapes[i])
                passed.append(_remote(theirs, theirs, fsend_sems.at[k], frecv_sems.at[k], (x, y, 1 - c)))
        return fwd, passed

    def start(ins, outs, sems):
        for cp in copies(outs, sems)[0]:
            cp.start()

    def finish(ins, outs, sems):
        fwd, passed = copies(outs, sems)
        for cp in passed:
            cp.wait_recv()
        for cp in fwd:
            cp.wait_send()

    return _Task(partial, [jax.ShapeDtypeStruct(p.shape, p.dtype) for p in partial],
                 [pltpu.SemaphoreType.DMA((3 * n,))] * 2, start, finish, aliases={i: i for i in range(n)})


def _ag_task(shards, kinds):
    send = _ag_send_task(shards, kinds)
    forward = _ag_forward_task(send.out_shapes, kinds, [s.shape for s in shards])
    n_send_sems = len(send.sems)

    def start(ins, outs, sems):
        send.start(ins, outs, sems[:n_send_sems])

    def finish(ins, outs, sems):
        send.finish(ins, outs, sems[:n_send_sems])
        forward.start(outs, outs, sems[n_send_sems:])
        forward.finish(outs, outs, sems[n_send_sems:])

    return _Task(shards, send.out_shapes, send.sems + forward.sems, start, finish)


def _exchange_task(grads, kinds, shard_shapes):
    n = len(grads)
    out_shapes = [jax.ShapeDtypeStruct((4, rs // 2, cs), F32) for rs, cs in shard_shapes]

    def copies(ins, outs, sems):
        send_sems, recv_sems = sems
        x, y, c, _ = _place()
        return [_remote(_block(ins[i], kinds[i], q, 1 - c, shard_shapes[i]), outs[i].at[q],
                        send_sems.at[4 * i + q], recv_sems.at[4 * i + q], (x, y, 1 - c))
                for i in range(n) for q in range(4)]

    def start(ins, outs, sems):
        for cp in copies(ins, outs, sems):
            cp.start()

    def finish(ins, outs, sems):
        for cp in copies(ins, outs, sems):
            cp.wait()

    return _Task(grads, out_shapes, [pltpu.SemaphoreType.DMA((4 * n,))] * 2, start, finish)


def _grad_block_map(kind, nt):
    if kind == "col":
        return lambda j, t, p: (p[0] * nt + t, p[1 + j])
    return lambda j, t, p: ((p[1 + j] * 2 + p[0]) * nt + t, 0)


def _chip_sum(grad, sib, kind, shard_shape, place, name):
    rs, cs = shard_shape
    hr = rs // 2
    tr = _rtile(hr, 256)
    nt = hr // tr
    g_map = _grad_block_map(kind, nt)

    def body(p_ref, g_ref, s_ref, o_ref):
        o_ref[0] = (g_ref[...] + s_ref[0]).astype(BF16)

    return pl.pallas_call(
        body, name=name,
        grid_spec=pltpu.PrefetchScalarGridSpec(
            num_scalar_prefetch=1, grid=(3, nt),
            in_specs=[pl.BlockSpec((tr, cs), g_map),
                      pl.BlockSpec((1, tr, cs), lambda j, t, p: (p[1 + j], t, 0))],
            out_specs=pl.BlockSpec((1, tr, cs), lambda j, t, p: (j, t, 0))),
        out_shape=jax.ShapeDtypeStruct((3, hr, cs), BF16),
        compiler_params=_params(("arbitrary", "arbitrary")))(place, grad, sib)


def _scatter_task(parts):
    n = len(parts)

    def copies(ins, outs, sems):
        send_sems, recv_sems = sems
        _, _, c, chips = _place()
        return [_remote(ins[i].at[j], outs[i].at[j], send_sems.at[3 * i + j], recv_sems.at[3 * i + j], (px, py, c))
                for i in range(n) for j, (px, py) in enumerate(chips)]

    def start(ins, outs, sems):
        for cp in copies(ins, outs, sems):
            cp.start()

    def finish(ins, outs, sems):
        for cp in copies(ins, outs, sems):
            cp.wait()

    return _Task(parts, [jax.ShapeDtypeStruct(p.shape, p.dtype) for p in parts],
                 [pltpu.SemaphoreType.DMA((3 * n,))] * 2, start, finish)


def _final_sum(grad, sib, recv, kind, shard_shape, place, name):
    rs, cs = shard_shape
    hr = rs // 2
    tr = _rtile(hr, 256)
    nt = hr // tr
    g_map = _grad_block_map(kind, nt)

    def body(p_ref, g_ref, s_ref, r_ref, out_ref):
        acc = g_ref[...] + s_ref[0]
        for j in range(3):
            acc = acc + r_ref[j].astype(F32)
        out_ref[...] = acc

    return pl.pallas_call(
        body, name=name,
        grid_spec=pltpu.PrefetchScalarGridSpec(
            num_scalar_prefetch=1, grid=(nt,),
            in_specs=[pl.BlockSpec((tr, cs), lambda t, p: g_map(3, t, p)),
                      pl.BlockSpec((1, tr, cs), lambda t, p: (p[4], t, 0)),
                      pl.BlockSpec((3, tr, cs), lambda t, p: (0, t, 0))],
            out_specs=pl.BlockSpec((tr, cs), lambda t, p: (p[0] * nt + t, 0))),
        out_shape=jax.ShapeDtypeStruct((rs, cs), F32),
        compiler_params=_params(("arbitrary",)))(place, grad, sib, recv)


def _join_task(shards):
    n = len(shards)

    def copies(outs, sems):
        send_sems, recv_sems = sems
        x, y, c, _ = _place()
        mine = [_half_rows(outs[i], c) for i in range(n)]
        theirs = [_half_rows(outs[i], 1 - c) for i in range(n)]
        send = [_remote(mine[i], mine[i], send_sems.at[i], recv_sems.at[i], (x, y, 1 - c)) for i in range(n)]
        recv = [_remote(theirs[i], theirs[i], send_sems.at[i], recv_sems.at[i], (x, y, 1 - c)) for i in range(n)]
        return send, recv

    def start(ins, outs, sems):
        for cp in copies(outs, sems)[0]:
            cp.start()

    def finish(ins, outs, sems):
        send, recv = copies(outs, sems)
        for cp in send:
            cp.wait_send()
        for cp in recv:
            cp.wait_recv()

    return _Task(shards, [jax.ShapeDtypeStruct(s.shape, F32) for s in shards],
                 [pltpu.SemaphoreType.DMA((n,))] * 2, start, finish, aliases={i: i for i in range(n)})


def _all_reduce_small(v, name):
    rows = v.shape[0]

    def body(v_ref, o_ref, buf, send_sems, recv_sems):
        x, y, c, _ = _place()
        coord = lambda p: ((1 - x) if p & 4 else x, (1 - y) if p & 2 else y, (1 - c) if p & 1 else c)
        me = 4 * x + 2 * y + c
        buf[me] = v_ref[...]
        copies = []
        for p in range(1, 8):
            cp = _remote(v_ref, buf.at[me], send_sems.at[p - 1], recv_sems.at[p - 1], coord(p))
            cp.start()
            copies.append(cp)
        for p in range(1, 8):
            px, py, pc = coord(p)
            _remote(v_ref, buf.at[4 * px + 2 * py + pc], send_sems.at[p - 1], recv_sems.at[p - 1],
                    coord(p)).wait_recv()
        for cp in copies:
            cp.wait_send()
        acc = buf[0]
        for dev in range(1, 8):
            acc = acc + buf[dev]
        o_ref[...] = acc

    vm = pl.BlockSpec(memory_space=pltpu.VMEM)
    return pl.pallas_call(
        body, name=name, in_specs=[vm], out_specs=vm, out_shape=jax.ShapeDtypeStruct(v.shape, F32),
        scratch_shapes=[pltpu.VMEM((8, rows, LANES), F32), pltpu.SemaphoreType.DMA((7,)),
                        pltpu.SemaphoreType.DMA((7,))],
        compiler_params=pltpu.CompilerParams(vmem_limit_bytes=VMEM_LIMIT))(v)


def _reduce_scatter_grads(grads, kinds, shard_shapes, place, tag):
    sib = _run_task(_exchange_task(grads, kinds, shard_shapes), "rs_core_exchange_" + tag)
    peers = [_chip_sum(g, sib[i], kinds[i], shard_shapes[i], place, "rs_chip_sum%d_%s" % (i, tag))
             for i, g in enumerate(grads)]
    recv = _run_task(_scatter_task(peers), "rs_scatter_" + tag)
    halves = [_final_sum(g, sib[i], recv[i], kinds[i], shard_shapes[i], place, "rs_final_sum%d_%s" % (i, tag))
              for i, g in enumerate(grads)]
    return _run_task(_join_task(halves), "rs_core_join_" + tag)


BIG = ["w_in", "ret_proj", "sgu_proj", "w_out", "w_ffn_in", "w_ffn_out"]
BIG_KIND = {"w_in": "col", "ret_proj": "row", "sgu_proj": "row", "w_out": "row", "w_ffn_in": "col",
            "w_ffn_out": "row"}
KINDS = [BIG_KIND[n] for n in BIG]
SMALL = ["norm_mix_w", "ret_gn_w", "sgu_ln_w", "sgu_ln_b", "sgu_w_s", "sgu_b_s", "norm_ffn_w"]
ORDER = ["norm_mix_w", "w_in", "ret_gn_w", "ret_proj", "sgu_ln_w", "sgu_ln_b", "sgu_w_s", "sgu_b_s",
         "sgu_proj", "w_out", "norm_ffn_w", "w_ffn_in", "w_ffn_out", "final_norm_w"]
AG_WHOLE_BEHIND = {"mm_in": [(0, 1), (0, 2), (0, 3)]}
AG_SEND_BEHIND = {"mm_in": [(0, 4), (0, 5)], "mm_ffn_in": [(1, 0)], "mm_ffn_out": [(1, 1), (1, 2), (1, 3)]}
AG_FORWARD_BEHIND = {"mm_ret_proj": [(0, 4), (0, 5)], "mm_ffn_out": [(1, 0)], "mm_in": [(0, 1), (0, 2), (0, 3)]}
EXCHANGE_BEHIND = {"mm_dffn_out_x": [0], "mm_dffn_out_w": [1, 2, 3, 4, 5]}
SCATTER_BEHIND = {"mm_dffn_in_x": [0, 1, 2, 3], "mm_dffn_in_w": [4, 5]}


def _layer_fwd(x, l, full, partial, shards, sm, tables, dims):
    h, d, w = dims
    c_su, c_gate = 6 * h * RET_DK, 6 * h * RET_DK + 2 * w
    tag = "l%d" % l
    wt = full[l]

    def due(plan, key, where, present):
        return [(l + dl, i) for dl, i in plan.get(key, [])
                if l + dl < len(full) and (BIG[i] in where[l + dl]) == present and BIG[i] not in full[l + dl]]

    def mm(a, wname, out_dtype, key, res=None):
        stages = [(due(AG_WHOLE_BEHIND, key, partial, False), _ag_task, full),
                  (due(AG_SEND_BEHIND, key, partial, False), _ag_send_task, partial)]
        tasks, sinks = [], []
        for todo, make, sink in stages:
            if todo:
                tasks.append(make([shards[ll][i] for ll, i in todo], [KINDS[i] for _, i in todo]))
                sinks.append((todo, sink))
        todo = due(AG_FORWARD_BEHIND, key, partial, True)
        if todo:
            tasks.append(_ag_forward_task([partial[ll][BIG[i]] for ll, i in todo], [KINDS[i] for _, i in todo],
                                          [shards[ll][i].shape for ll, i in todo]))
            sinks.append((todo, full))
        task = _join_tasks(tasks) if tasks else None
        if key == "mm_ffn_in":
            got = list(_ffn_in_swiglu(a, wt[wname], key + "_" + tag, task=task))
            out = tuple(got[:3])
            del got[:3]
        elif task is None:
            return _matmul(a, wt[wname], "nn", out_dtype, key + "_" + tag, res=res)
        else:
            out, *got = _matmul(a, wt[wname], "nn", out_dtype, key + "_" + tag, res=res, task=task)
        for todo, sink in sinks:
            for ll, i in todo:
                sink[ll][BIG[i]] = got.pop(0)
        return out

    h1 = _rms_fwd(x, sm["norm_mix_w"], "rms_mix_fwd_" + tag)
    z = mm(h1, "w_in", BF16, "mm_in")
    ga, states = _ret_fwd(z, sm["ret_gn_w"], tables, h, "ret_fwd_" + tag)
    sg = _sgu_fwd(z, sm["sgu_ln_w"], sm["sgu_ln_b"], sm["ws_m"], sm["bs"], c_su, w, "sgu_fwd_" + tag)
    a = mm(ga, "ret_proj", BF16, "mm_ret_proj")
    b = mm(sg, "sgu_proj", BF16, "mm_sgu_proj")
    mg = _merge_fwd(a, b, z, c_gate, "merge_fwd_" + tag)
    x1 = mm(mg, "w_out", F32, "mm_out", res=x)
    h2 = _rms_fwd(x1, sm["norm_ffn_w"], "rms_ffn_fwd_" + tag)
    fa, fc, f = mm(h2, "w_ffn_in", BF16, "mm_ffn_in")
    x2 = mm(f, "w_ffn_out", F32, "mm_ffn_out", res=x1)
    saved = dict(x=x, h1=h1, z=z, states=states, ga=ga, sg=sg, a=a, b=b, mg=mg, x1=x1, h2=h2, fa=fa, fc=fc, f=f)
    return x2, saved


def _layer_bwd(dx2, dx2b, sv, wt, sm, tables, dims, tag, carried, shard_shapes, place):
    h, d, w = dims
    c_su, c_gate = 6 * h * RET_DK, 6 * h * RET_DK + 2 * w
    gw, gs = {}, {}
    n_big = len(BIG)
    cg, ctag = carried if carried is not None else (None, None)
    sib, recv = [None] * n_big, [None] * n_big

    def mm(a, b, mode, out_dtype, key, task_of=None, into=None, **kw):
        idx = task_of[1].get(key) if (carried is not None and task_of is not None) else None
        if idx is None:
            return _matmul(a, b, mode, out_dtype, key + "_" + tag, **kw)
        out, *got = _matmul(a, b, mode, out_dtype, key + "_" + tag, task=task_of[0](idx), **kw)
        for i, g in zip(idx, got):
            into[i] = g
        return out

    exchange = (lambda idx: _exchange_task([cg[i] for i in idx], [KINDS[i] for i in idx],
                                           [shard_shapes[i] for i in idx]), EXCHANGE_BEHIND)
    df = mm(dx2b, wt["w_ffn_out"], "nt", BF16, "mm_dffn_out_x", exchange, sib)
    gw["w_ffn_out"] = mm(sv["f"], dx2b, "tn", F32, "mm_dffn_out_w", exchange, sib)
    peers = None
    if carried is not None:
        peers = [_chip_sum(cg[i], sib[i], KINDS[i], shard_shapes[i], place, "rs_chip_sum%d_%s" % (i, ctag))
                 for i in range(n_big)]
    scatter = (lambda idx: _scatter_task([peers[i] for i in idx]), SCATTER_BEHIND)
    dac = _swiglu_bwd(sv["fa"], sv["fc"], df, "swiglu_bwd_" + tag)
    dh2 = mm(dac, wt["w_ffn_in"], "nt", BF16, "mm_dffn_in_x", scatter, recv)
    gw["w_ffn_in"] = mm(sv["h2"], dac, "tn", F32, "mm_dffn_in_w", scatter, recv)
    halves = None
    if carried is not None:
        halves = [_final_sum(cg[i], sib[i], recv[i], KINDS[i], shard_shapes[i], place,
                             "rs_final_sum%d_%s" % (i, ctag)) for i in range(n_big)]
    dx1, dx1b, gs["norm_ffn_w"] = _rms_bwd(sv["x1"], sm["norm_ffn_w"], dh2, dx2, "rms_ffn_bwd_" + tag)
    dmg = mm(dx1b, wt["w_out"], "nt", BF16, "mm_dout_x")
    gw["w_out"] = mm(sv["mg"], dx1b, "tn", F32, "mm_dout_w")
    da, db, dz_tail = _merge_bwd(dmg, sv["a"], sv["b"], sv["z"], c_gate, "merge_bwd_" + tag)
    dga = mm(da, wt["ret_proj"], "nt", BF16, "mm_dret_proj_x")
    gw["ret_proj"] = mm(sv["ga"], da, "tn", F32, "mm_dret_proj_w")
    dsg = mm(db, wt["sgu_proj"], "nt", BF16, "mm_dsgu_proj_x")
    gw["sgu_proj"] = mm(sv["sg"], db, "tn", F32, "mm_dsgu_proj_w")
    dz_tail, gs["sgu_ln_w"], gs["sgu_ln_b"], gs["sgu_w_s"], gs["sgu_b_s"] = _sgu_bwd(
        sv["z"], dsg, dz_tail, sm["sgu_ln_w"], sm["sgu_ln_b"], sm["ws_m"], sm["ws_mt"], sm["bs"], c_su, w,
        "sgu_bwd_" + tag)
    dz_ret, gs["ret_gn_w"] = _ret_bwd(sv["z"], dga, sv["states"], sm["ret_gn_w"], tables, h, "ret_bwd_" + tag)
    in_cols = c_su + 4 * w
    reduced = [None] * n_big
    join = (lambda idx: _join_task([halves[i] for i in idx]), {"mm_din_x_ret": list(range(n_big))})
    dh1 = mm(dz_ret, wt["w_in"], "nt", F32, "mm_din_x_ret", join, reduced)
    dh1 = _matmul(dz_tail, wt["w_in"], "nt", BF16, "mm_din_x_tail_" + tag, res=dh1, b_k0=c_su)
    gw_in = _matmul(sv["h1"], dz_ret, "tn", F32, "mm_din_w_ret_" + tag, window=(in_cols, 0, None))
    gw["w_in"] = _matmul(sv["h1"], dz_tail, "tn", F32, "mm_din_w_tail_" + tag, window=(in_cols, c_su, gw_in))
    dx, dxb, gs["norm_mix_w"] = _rms_bwd(sv["x"], sm["norm_mix_w"], dh1, dx1, "rms_mix_bwd_" + tag)
    return dx, dxb, gw, gs, reduced


def _sgu_mask():
    pos = jnp.arange(SGU_LEN)
    return (pos[None, :] // CHUNK) <= (pos[:, None] // CHUNK)


def kernel(x, norm_mix_w, w_in, ret_gn_w, ret_proj, sgu_ln_w, sgu_ln_b, sgu_w_s, sgu_b_s, sgu_proj, w_out, norm_ffn_w, w_ffn_in, w_ffn_out, final_norm_w, loss_target, m_norm_mix_w, m_w_in, m_ret_gn_w, m_ret_proj, m_sgu_ln_w, m_sgu_ln_b, m_sgu_w_s, m_sgu_b_s, m_sgu_proj, m_w_out, m_norm_ffn_w, m_w_ffn_in, m_w_ffn_out, m_final_norm_w, v_norm_mix_w, v_w_in, v_ret_gn_w, v_ret_proj, v_sgu_ln_w, v_sgu_ln_b, v_sgu_w_s, v_sgu_b_s, v_sgu_proj, v_w_out, v_norm_ffn_w, v_w_ffn_in, v_w_ffn_out, v_final_norm_w):
    weights = dict(norm_mix_w=norm_mix_w, w_in=w_in, ret_gn_w=ret_gn_w, ret_proj=ret_proj, sgu_ln_w=sgu_ln_w,
                   sgu_ln_b=sgu_ln_b, sgu_w_s=sgu_w_s, sgu_b_s=sgu_b_s, sgu_proj=sgu_proj, w_out=w_out,
                   norm_ffn_w=norm_ffn_w, w_ffn_in=w_ffn_in, w_ffn_out=w_ffn_out, final_norm_w=final_norm_w)
    m_in = dict(norm_mix_w=m_norm_mix_w, w_in=m_w_in, ret_gn_w=m_ret_gn_w, ret_proj=m_ret_proj,
                sgu_ln_w=m_sgu_ln_w, sgu_ln_b=m_sgu_ln_b, sgu_w_s=m_sgu_w_s, sgu_b_s=m_sgu_b_s,
                sgu_proj=m_sgu_proj, w_out=m_w_out, norm_ffn_w=m_norm_ffn_w, w_ffn_in=m_w_ffn_in,
                w_ffn_out=m_w_ffn_out, final_norm_w=m_final_norm_w)
    v_in = dict(norm_mix_w=v_norm_mix_w, w_in=v_w_in, ret_gn_w=v_ret_gn_w, ret_proj=v_ret_proj,
                sgu_ln_w=v_sgu_ln_w, sgu_ln_b=v_sgu_ln_b, sgu_w_s=v_sgu_w_s, sgu_b_s=v_sgu_b_s,
                sgu_proj=v_sgu_proj, w_out=v_w_out, norm_ffn_w=v_norm_ffn_w, w_ffn_in=v_w_ffn_in,
                w_ffn_out=v_w_ffn_out, final_norm_w=v_final_norm_w)

    depth = w_in.shape[0]
    _, s, d = x.shape
    w = d
    in_cols = 4 * w_in.shape[2]
    h = (in_cols - 4 * d) // (2 * RET_DK + 2 * RET_DV)
    groups = sgu_w_s.shape[1]
    assert in_cols == h * (2 * RET_DK + 2 * RET_DV) + 4 * d and (6 * h * RET_DK) % d == 0
    assert s % SGU_LEN == 0 and w % groups == 0 and (w // groups) % LANES == 0
    dims = (h, d, w)
    tables = _ret_tables(s, h, _tile(s, RET_BLOCK))
    mask = _sgu_mask()
    cx, cy, cc = lax.axis_index("x"), lax.axis_index("y"), lax.axis_index("c")
    place = jnp.stack([cc, 2 * (1 - cx) + cy, 2 * cx + (1 - cy), 2 * (1 - cx) + (1 - cy),
                       2 * cx + cy]).astype(jnp.int32)

    shard_shapes = [weights[n].shape[1:] for n in BIG]
    shards = [[weights[n][l].astype(BF16) for n in BIG] for l in range(depth)]

    small = []
    for l in range(depth):
        sm = {n: weights[n][l] for n in SMALL}
        ws_m = jnp.where(mask[None], sgu_w_s[l], 0.0)
        sm["ws_m"] = ws_m.astype(BF16)
        sm["ws_mt"] = jnp.swapaxes(ws_m, 1, 2).astype(BF16)
        sm["bs"] = sgu_b_s[l][:, :, None]
        small.append(sm)

    xs = x[0]
    saved = []
    full = [{} for _ in range(depth)]
    partial = [{} for _ in range(depth)]
    full[0][BIG[0]], = _run_task(_ag_task(shards[0][:1], KINDS[:1]), "ag_w_in_l0")
    for l in range(depth):
        xs, sv = _layer_fwd(xs, l, full, partial, shards, small[l], tables, dims)
        saved.append(sv)
    dx, dxb, g_final, sq = _loss_head(xs, final_norm_w, loss_target[0])
    loss = lax.psum(sq[0, 0], ("x", "y", "c")) * (0.5 / d)

    grads_big = [None] * depth
    grads_small = [None] * depth
    carried = None
    for l in reversed(range(depth)):
        dx, dxb, gw, gs, reduced = _layer_bwd(dx, dxb, saved[l], full[l], small[l], tables, dims, "l%d" % l,
                                              carried, shard_shapes, place)
        grads_small[l] = gs
        if carried is not None:
            grads_big[l + 1] = reduced
        carried = ([gw[n] for n in BIG], "l%d" % l)
    grads_big[0] = _reduce_scatter_grads(carried[0], KINDS, shard_shapes, place, carried[1])
    grad_x = dx[None]

    pieces = []
    for l in range(depth):
        gs = dict(grads_small[l])
        gs["sgu_w_s"] = jnp.where(mask[None], gs["sgu_w_s"], 0.0)
        pieces += [gs[n].reshape(-1) for n in SMALL]
    pieces.append(g_final.reshape(-1))
    flat = jnp.concatenate(pieces)
    total = flat.shape[0]
    rows = -(-total // (8 * LANES)) * 8
    flat = jnp.pad(flat, (0, rows * LANES - total)).reshape(rows, LANES)
    summed = _all_reduce_small(flat, "ar_small").reshape(-1)
    grad = {}
    off = 0
    per_layer = {n: [] for n in SMALL}
    for l in range(depth):
        for n in SMALL:
            shp = weights[n].shape[1:]
            size = math.prod(shp)
            per_layer[n].append(summed[off:off + size].reshape(shp))
            off += size
    for n in SMALL:
        grad[n] = jnp.stack(per_layer[n])
    grad["final_norm_w"] = summed[off:off + d]
    for i, n in enumerate(BIG):
        grad[n] = jnp.stack([grads_big[l][i] for l in range(depth)])

    delta, new_m, new_v = {}, {}, {}
    for n in BIG:
        shp = weights[n].shape
        two_d = lambda a: a.reshape(shp[0] * shp[1], shp[2])
        dl, mn, vn = _adamw(two_d(weights[n]), two_d(grad[n]), two_d(m_in[n]), two_d(v_in[n]), "adamw_" + n)
        delta[n], new_m[n], new_v[n] = dl.reshape(shp), mn.reshape(shp), vn.reshape(shp)
    small_names = SMALL + ["final_norm_w"]

    def pack(tree):
        fl = jnp.concatenate([tree[n].reshape(-1) for n in small_names])
        return jnp.pad(fl, (0, rows * LANES - fl.shape[0])).reshape(rows, LANES)

    dl, mn, vn = _adamw(pack(weights), pack(grad), pack(m_in), pack(v_in), "adamw_small")
    off = 0
    for n in small_names:
        shp = weights[n].shape
        size = math.prod(shp)
        for src, dst in ((dl, delta), (mn, new_m), (vn, new_v)):
            dst[n] = src.reshape(-1)[off:off + size].reshape(shp)
        off += size

    return (loss, grad_x, *[grad[n] for n in ORDER], *[delta[n] for n in ORDER],
            *[new_m[n] for n in ORDER], *[new_v[n] for n in ORDER])
```

```python
import math

import jax
import jax.numpy as jnp
from jax import lax
from jax.experimental import pallas as pl
from jax.experimental.pallas import tpu as pltpu

F32 = jnp.float32
BF16 = jnp.bfloat16

CHUNK = 64
RET_DK = 128
RET_DV = 256
SGU_LEN = 128
ROPE_BASE = 10000.0
EPS = 1e-6
ADAM_LR = 0.001
ADAM_B1 = 0.9
ADAM_B2 = 0.999
ADAM_EPS = 1e-08
ADAM_WD = 0.01
ADAM_STEP = 10

LANES = 128
VMEM_LIMIT = 56 * 1024 * 1024
RET_BLOCK = 256
SGU_BLOCK = 256
ROW_BLOCK = 256
MM_TILE_PREFERRED = 1024
MM_TILE = 1408
MM_KTILE = 3584
FFN_TILE = 512
MESH = pl.DeviceIdType.MESH
ANY = pl.BlockSpec(memory_space=pl.ANY)
INV_SQRT2 = 1.0 / math.sqrt(2.0)
INV_SQRT_2PI = 1.0 / math.sqrt(2.0 * math.pi)

DN = {"nn": (((1,), (0,)), ((), ())), "nt": (((1,), (1,)), ((), ())), "tn": (((0,), (0,)), ((), ()))}


def _dot(a, b, mode="nn"):
    return lax.dot_general(a, b, DN[mode], preferred_element_type=F32)


def _tile(n, target):
    t = min(n, target) // LANES * LANES
    while t >= LANES:
        if n % t == 0:
            return t
        t -= LANES
    return n


def _rtile(n, target):
    t = min(n, target) // 16 * 16
    while t >= 16:
        if n % t == 0:
            return t
        t -= 16
    return n


def _out_tile(n):
    return MM_TILE_PREFERRED if n % MM_TILE_PREFERRED == 0 else _tile(n, MM_TILE)


def _params(sem):
    return pltpu.CompilerParams(dimension_semantics=sem, vmem_limit_bytes=VMEM_LIMIT)


def _sigmoid(x):
    return 1.0 / (1.0 + jnp.exp(-x))


def _gelu(x):
    return 0.5 * x * (1.0 + lax.erf(x * INV_SQRT2))


def _gelu_grad(x):
    return 0.5 * (1.0 + lax.erf(x * INV_SQRT2)) + x * jnp.exp(-0.5 * x * x) * INV_SQRT_2PI


def _matmul(a, b, mode, out_dtype, name, res=None, task=None, b_k0=0, window=None):
    if mode == "nn":
        (m, k), n = a.shape, b.shape[1]
    elif mode == "nt":
        (m, k), n = a.shape, b.shape[0]
    else:
        (k, m), n = a.shape, b.shape[1]
    out_cols, out_c0, into = window if window is not None else (n, 0, None)
    assert b_k0 == 0 or mode == "nt"
    tm, tn, tk = _out_tile(m), _out_tile(math.gcd(n, out_c0)), _tile(math.gcd(k, b_k0), MM_KTILE)
    ni, nj, nk = m // tm, n // tn, k // tk
    kb0, jb0 = b_k0 // tk, out_c0 // tn
    if mode == "tn":
        a_spec = pl.BlockSpec((tk, tm), lambda i, j, kk: (kk, i))
    else:
        a_spec = pl.BlockSpec((tm, tk), lambda i, j, kk: (i, kk))
    if mode == "nt":
        b_spec = pl.BlockSpec((tn, tk), lambda i, j, kk: (j, kb0 + kk))
    else:
        b_spec = pl.BlockSpec((tk, tn), lambda i, j, kk: (kk, j))
    r_spec = pl.BlockSpec((tm, tn), lambda i, j, kk: (i, j))
    o_spec = pl.BlockSpec((tm, tn), lambda i, j, kk: (i, jb0 + j))
    n_mm_in = 2 + (res is not None)
    t_ins = (task.ins if task is not None else []) + ([into] if into is not None else [])
    t_outs = task.out_shapes if task is not None else []
    t_sems = task.sems if task is not None else []
    in_specs = [a_spec, b_spec] + ([r_spec] if res is not None else []) + [ANY] * len(t_ins)
    acc_in_out = out_dtype == F32
    scratch = [] if (nk == 1 or acc_in_out) else [pltpu.VMEM((tm, tn), F32)]

    def body(*refs):
        a_ref, b_ref = refs[0], refs[1]
        r_ref = refs[2] if res is not None else None
        tin = refs[n_mm_in:n_mm_in + len(t_ins)]
        o_ref = refs[n_mm_in + len(t_ins)]
        tout = refs[n_mm_in + len(t_ins) + 1:n_mm_in + len(t_ins) + 1 + len(t_outs)]
        rest = refs[n_mm_in + len(t_ins) + 1 + len(t_outs):]
        acc_scr, sems = (rest[0], rest[1:]) if scratch else (None, rest)
        i, j, kk = pl.program_id(0), pl.program_id(1), pl.program_id(2)
        if task is not None:
            @pl.when((i == 0) & (j == 0) & (kk == 0))
            def _():
                task.start(tin, tout, sems)

        p = _dot(a_ref[...], b_ref[...], mode)
        if nk == 1:
            if r_ref is not None:
                p = p + r_ref[...]
            o_ref[...] = p.astype(o_ref.dtype)
        else:
            acc = o_ref if acc_in_out else acc_scr

            @pl.when(kk == 0)
            def _():
                acc[...] = p if r_ref is None or not acc_in_out else p + r_ref[...]

            @pl.when(kk > 0)
            def _():
                acc[...] += p

            if not acc_in_out:
                @pl.when(kk == nk - 1)
                def _():
                    o = acc[...]
                    if r_ref is not None:
                        o = o + r_ref[...]
                    o_ref[...] = o.astype(o_ref.dtype)

        if task is not None:
            @pl.when((i == ni - 1) & (j == nj - 1) & (kk == nk - 1))
            def _():
                task.finish(tin, tout, sems)

    args = (a, b) + ((res,) if res is not None else ()) + tuple(t_ins)
    out_shape = jax.ShapeDtypeStruct((m, out_cols), out_dtype)
    into_alias = {n_mm_in + len(t_ins) - 1: 0} if into is not None else {}
    if task is None:
        return pl.pallas_call(
            body, name=name, grid=(ni, nj, nk), in_specs=in_specs, out_specs=o_spec, out_shape=out_shape,
            input_output_aliases=into_alias, scratch_shapes=scratch,
            compiler_params=_params(("parallel", "parallel", "arbitrary")))(*args)
    aliases = dict(into_alias)
    aliases.update({n_mm_in + src: 1 + dst for src, dst in task.aliases.items()})
    return pl.pallas_call(
        body, name=name, grid=(ni, nj, nk), in_specs=in_specs, out_specs=[o_spec] + [ANY] * len(t_outs),
        out_shape=[out_shape] + list(t_outs), input_output_aliases=aliases, scratch_shapes=scratch + list(t_sems),
        compiler_params=pltpu.CompilerParams(dimension_semantics=("arbitrary", "arbitrary", "arbitrary"),
                                             vmem_limit_bytes=VMEM_LIMIT, has_side_effects=True))(*args)


def _ffn_in_swiglu(x, w, name, task=None):
    s, k = x.shape
    f = w.shape[1] // 2
    assert k <= MM_KTILE
    tm, tn = _out_tile(s), _tile(f, FFN_TILE)
    ni, nj = s // tm, f // tn
    t_ins = task.ins if task is not None else []
    t_outs = task.out_shapes if task is not None else []
    t_sems = task.sems if task is not None else []

    def body(*refs):
        x_ref, wa_ref, wc_ref = refs[:3]
        tin = refs[3:3 + len(t_ins)]
        a_ref, c_ref, o_ref = refs[3 + len(t_ins):6 + len(t_ins)]
        tout = refs[6 + len(t_ins):6 + len(t_ins) + len(t_outs)]
        sems = refs[6 + len(t_ins) + len(t_outs):]
        i, j = pl.program_id(0), pl.program_id(1)
        if task is not None:
            @pl.when((i == 0) & (j == 0))
            def _():
                task.start(tin, tout, sems)

        xv = x_ref[...]
        a = _dot(xv, wa_ref[...])
        c = _dot(xv, wc_ref[...])
        a_ref[...] = a.astype(BF16)
        c_ref[...] = c.astype(BF16)
        o_ref[...] = (a * _sigmoid(a) * c).astype(BF16)

        if task is not None:
            @pl.when((i == ni - 1) & (j == nj - 1))
            def _():
                task.finish(tin, tout, sems)

    o_spec = pl.BlockSpec((tm, tn), lambda i, j: (i, j))
    in_specs = [pl.BlockSpec((tm, k), lambda i, j: (i, 0)), pl.BlockSpec((k, tn), lambda i, j: (0, j)),
                pl.BlockSpec((k, tn), lambda i, j: (0, nj + j))] + [ANY] * len(t_ins)
    shp = jax.ShapeDtypeStruct((s, f), BF16)
    if task is None:
        params = _params(("parallel", "parallel"))
    else:
        params = pltpu.CompilerParams(dimension_semantics=("arbitrary", "arbitrary"),
                                      vmem_limit_bytes=VMEM_LIMIT, has_side_effects=True)
    return pl.pallas_call(
        body, name=name, grid=(ni, nj), in_specs=in_specs, out_specs=[o_spec] * 3 + [ANY] * len(t_outs),
        out_shape=[shp] * 3 + list(t_outs),
        input_output_aliases={3 + src: 3 + dst for src, dst in (task.aliases.items() if task is not None else [])},
        scratch_shapes=list(t_sems), compiler_params=params)(x, w, w, *t_ins)


def _rms_fwd(x, w, name):
    s, d = x.shape
    tr = _tile(s, ROW_BLOCK)

    def body(x_ref, w_ref, o_ref):
        xv = x_ref[...]
        r = lax.rsqrt(jnp.mean(xv * xv, axis=-1, keepdims=True) + EPS)
        o_ref[...] = (xv * r * w_ref[...]).astype(BF16)

    row = pl.BlockSpec((tr, d), lambda i: (i, 0))
    vec = pl.BlockSpec((1, d), lambda i: (0, 0))
    return pl.pallas_call(body, name=name, grid=(s // tr,), in_specs=[row, vec], out_specs=row,
                          out_shape=jax.ShapeDtypeStruct((s, d), BF16),
                          compiler_params=_params(("parallel",)))(x, w.reshape(1, d))


def _rms_bwd(x, w, dh, dres, name):
    s, d = x.shape
    tr = _tile(s, ROW_BLOCK)

    def body(x_ref, w_ref, dh_ref, dr_ref, dx_ref, dxb_ref, dw_ref):
        xv = x_ref[...]
        r = lax.rsqrt(jnp.mean(xv * xv, axis=-1, keepdims=True) + EPS)
        xh = xv * r
        dy = dh_ref[...].astype(F32)
        dxh = dy * w_ref[...]
        dx = dr_ref[...] + r * (dxh - xh * jnp.mean(dxh * xh, axis=-1, keepdims=True))
        dx_ref[...] = dx
        dxb_ref[...] = dx.astype(BF16)

        @pl.when(pl.program_id(0) == 0)
        def _():
            dw_ref[...] = jnp.zeros_like(dw_ref)

        dw_ref[...] += jnp.sum(dy * xh, axis=0, keepdims=True)

    row = pl.BlockSpec((tr, d), lambda i: (i, 0))
    vec = pl.BlockSpec((1, d), lambda i: (0, 0))
    return pl.pallas_call(
        body, name=name, grid=(s // tr,), in_specs=[row, vec, row, row], out_specs=[row, row, vec],
        out_shape=[jax.ShapeDtypeStruct((s, d), F32), jax.ShapeDtypeStruct((s, d), BF16),
                   jax.ShapeDtypeStruct((1, d), F32)],
        compiler_params=_params(("arbitrary",)))(x, w.reshape(1, d), dh, dres)


def _loss_head(x, w, tgt):
    s, d = x.shape
    tr = _tile(s, ROW_BLOCK)

    def body(x_ref, w_ref, t_ref, dx_ref, dxb_ref, dw_ref, l_ref):
        xv = x_ref[...]
        r = lax.rsqrt(jnp.mean(xv * xv, axis=-1, keepdims=True) + EPS)
        xh = xv * r
        e = xh * w_ref[...] - t_ref[...]
        dy = e * (1.0 / d)
        dxh = dy * w_ref[...]
        dx = r * (dxh - xh * jnp.mean(dxh * xh, axis=-1, keepdims=True))
        dx_ref[...] = dx
        dxb_ref[...] = dx.astype(BF16)

        @pl.when(pl.program_id(0) == 0)
        def _():
            dw_ref[...] = jnp.zeros_like(dw_ref)
            l_ref[...] = jnp.zeros_like(l_ref)

        dw_ref[...] += jnp.sum(dy * xh, axis=0, keepdims=True)
        l_ref[...] += jnp.sum(jnp.sum(e * e, axis=1, keepdims=True), axis=0, keepdims=True)

    row = pl.BlockSpec((tr, d), lambda i: (i, 0))
    vec = pl.BlockSpec((1, d), lambda i: (0, 0))
    one = pl.BlockSpec((1, 1), lambda i: (0, 0))
    return pl.pallas_call(
        body, name="loss_head", grid=(s // tr,), in_specs=[row, vec, row], out_specs=[row, row, vec, one],
        out_shape=[jax.ShapeDtypeStruct((s, d), F32), jax.ShapeDtypeStruct((s, d), BF16),
                   jax.ShapeDtypeStruct((1, d), F32), jax.ShapeDtypeStruct((1, 1), F32)],
        compiler_params=_params(("arbitrary",)))(x, w.reshape(1, d), tgt)


def _ret_tables(s, h, t):
    half = RET_DK // 2
    inv = ROPE_BASE ** (-jnp.arange(half, dtype=F32) / half)
    ang = jnp.arange(s, dtype=F32)[:, None] * inv[None, :]
    cos, sin = jnp.cos(ang), jnp.sin(ang)
    cosf = jnp.concatenate([cos, cos], axis=1)
    sinf = jnp.concatenate([-sin, sin], axis=1)
    log_g = jnp.log1p(-(2.0 ** (-5.0 - jnp.arange(h, dtype=F32))))
    idx = jnp.arange(t, dtype=F32)
    chunk = jnp.arange(t) // CHUNK
    allowed = chunk[None, :] <= chunk[:, None]
    dm = jnp.where(allowed[None], jnp.exp(log_g[:, None, None] * jnp.abs(idx[:, None] - idx[None, :])), 0.0)
    qd = jnp.exp(log_g[:, None] * (idx[None, :] + 1.0))
    kd = jnp.exp(log_g[:, None] * (t - 1.0 - idx[None, :]))
    qd = jnp.broadcast_to(qd[:, :, None], (h, t, RET_DK))
    kd = jnp.broadcast_to(kd[:, :, None], (h, t, RET_DK))
    cd = jnp.broadcast_to(jnp.exp(log_g * t)[:, None, None], (h, 1, RET_DV))
    return cosf, sinf, dm, qd, kd, cd


def _rot(x, cos, sin):
    return x * cos + pltpu.roll(x, RET_DK // 2, 1) * sin


def _rot_t(x, cos, sin):
    return x * cos - pltpu.roll(x, RET_DK // 2, 1) * sin


def _ret_heads_per_step(h):
    return h


def _ret_in_specs(h, t, rev_nb=None):
    hb = _ret_heads_per_step(h)
    ng = h // hb
    blk = (lambda b: b) if rev_nb is None else (lambda b: rev_nb - 1 - b)
    return [
        pl.BlockSpec((t, hb * RET_DK), lambda hh, b: (blk(b), hh)),
        pl.BlockSpec((t, hb * RET_DK), lambda hh, b: (blk(b), ng + hh)),
        pl.BlockSpec((t, hb * RET_DV), lambda hh, b: (blk(b), ng + hh)),
        pl.BlockSpec((t, hb * RET_DV), lambda hh, b: (blk(b), 2 * ng + hh)),
        pl.BlockSpec((t, RET_DK), lambda hh, b: (blk(b), 0)),
        pl.BlockSpec((t, RET_DK), lambda hh, b: (blk(b), 0)),
        pl.BlockSpec((hb, t, t), lambda hh, b: (hh, 0, 0)),
        pl.BlockSpec((hb, t, RET_DK), lambda hh, b: (hh, 0, 0)),
        pl.BlockSpec((hb, t, RET_DK), lambda hh, b: (hh, 0, 0)),
        pl.BlockSpec((hb, 1, RET_DV), lambda hh, b: (hh, 0, 0)),
        pl.BlockSpec((1, hb * RET_DV), lambda hh, b: (0, hh)),
    ]


def _ret_fwd(z, gn_w, tables, h, name):
    s = z.shape[0]
    t = _tile(s, RET_BLOCK)
    nb = s // t
    hb = _ret_heads_per_step(h)
    scale = RET_DK ** -0.5

    def body(q_ref, k_ref, v_ref, g_ref, cos_ref, sin_ref, dm_ref, qd_ref, kd_ref, cd_ref, gn_ref,
             o_ref, st_ref, st_scr):
        @pl.when(pl.program_id(1) == 0)
        def _():
            st_scr[...] = jnp.zeros_like(st_scr)

        cos, sin = cos_ref[...], sin_ref[...]
        for u in range(hb):
            ck = slice(u * RET_DK, (u + 1) * RET_DK)
            cv = slice(u * RET_DV, (u + 1) * RET_DV)
            qf = _rot(q_ref[:, ck].astype(F32), cos, sin) * scale
            kf = _rot(k_ref[:, ck].astype(F32), cos, sin)
            vb = v_ref[:, cv]
            p = _dot(qf.astype(BF16), kf.astype(BF16), "nt") * dm_ref[u]
            st = st_scr[u]
            stb = st.astype(BF16)
            st_ref[0, u] = stb
            o = _dot(p.astype(BF16), vb) + _dot((qf * qd_ref[u]).astype(BF16), stb)
            st_scr[u] = st * cd_ref[u] + _dot((kf * kd_ref[u]).astype(BF16), vb, "tn")
            dlt = o - jnp.mean(o, axis=-1, keepdims=True)
            oh = dlt * lax.rsqrt(jnp.mean(dlt * dlt, axis=-1, keepdims=True) + EPS)
            g = g_ref[:, cv].astype(F32)
            o_ref[:, cv] = (g * _sigmoid(g) * oh * gn_ref[:, cv]).astype(BF16)

    return pl.pallas_call(
        body, name=name, grid=(h // hb, nb), in_specs=_ret_in_specs(h, t),
        out_specs=[pl.BlockSpec((t, hb * RET_DV), lambda hh, b: (b, hh)),
                   pl.BlockSpec((1, hb, RET_DK, RET_DV), lambda hh, b: (b, hh, 0, 0))],
        out_shape=[jax.ShapeDtypeStruct((s, h * RET_DV), BF16),
                   jax.ShapeDtypeStruct((nb, h, RET_DK, RET_DV), BF16)],
        scratch_shapes=[pltpu.VMEM((hb, RET_DK, RET_DV), F32)],
        compiler_params=_params(("parallel", "arbitrary")))(z, z, z, z, *tables, gn_w.reshape(1, -1))


def _ret_bwd(z, dga, states, gn_w, tables, h, name):
    s = z.shape[0]
    t = _tile(s, RET_BLOCK)
    nb = s // t
    hb = _ret_heads_per_step(h)
    assert hb == h
    scale = RET_DK ** -0.5
    c_k, c_v, c_g = h * RET_DK, 2 * h * RET_DK, 2 * h * RET_DK + h * RET_DV

    def body(q_ref, k_ref, v_ref, g_ref, cos_ref, sin_ref, dm_ref, qd_ref, kd_ref, cd_ref, gn_ref,
             dga_ref, st_ref, dz_ref, dgn_ref, dst_scr):
        @pl.when(pl.program_id(1) == 0)
        def _():
            dst_scr[...] = jnp.zeros_like(dst_scr)
            dgn_ref[...] = jnp.zeros_like(dgn_ref)

        dq_ref, dk_ref = dz_ref.at[:, 0:c_k], dz_ref.at[:, c_k:c_v]
        dv_ref, dg_ref = dz_ref.at[:, c_v:c_g], dz_ref.at[:, c_g:c_g + h * RET_DV]
        cos, sin = cos_ref[...], sin_ref[...]
        for u in range(hb):
            ck = slice(u * RET_DK, (u + 1) * RET_DK)
            cv = slice(u * RET_DV, (u + 1) * RET_DV)
            dm = dm_ref[u]
            qf = _rot(q_ref[:, ck].astype(F32), cos, sin) * scale
            kf = _rot(k_ref[:, ck].astype(F32), cos, sin)
            qb, kb, vb = qf.astype(BF16), kf.astype(BF16), v_ref[:, cv]
            qdb = (qf * qd_ref[u]).astype(BF16)
            kdb = (kf * kd_ref[u]).astype(BF16)
            stb = st_ref[0, u]
            pb = (_dot(qb, kb, "nt") * dm).astype(BF16)
            o = _dot(pb, vb) + _dot(qdb, stb)
            dlt = o - jnp.mean(o, axis=-1, keepdims=True)
            rstd = lax.rsqrt(jnp.mean(dlt * dlt, axis=-1, keepdims=True) + EPS)
            oh = dlt * rstd
            gn = gn_ref[:, cv]
            g = g_ref[:, cv].astype(F32)
            sg = _sigmoid(g)
            dga_v = dga_ref[:, cv].astype(F32)
            dret = dga_v * g * sg
            dg_ref[:, cv] = (dga_v * oh * gn * sg * (1.0 + g * (1.0 - sg))).astype(BF16)
            dgn_ref[:, cv] += jnp.sum(dret * oh, axis=0, keepdims=True)
            doh = dret * gn
            do = rstd * (doh - jnp.mean(doh, axis=-1, keepdims=True)
                         - oh * jnp.mean(doh * oh, axis=-1, keepdims=True))
            dob = do.astype(BF16)
            dst = dst_scr[u]
            dstb = dst.astype(BF16)
            dv_ref[:, cv] = (_dot(pb, dob, "tn") + _dot(kdb, dstb)).astype(BF16)
            dpb = (_dot(dob, vb, "nt") * dm).astype(BF16)
            dqf = _dot(dpb, kb) + _dot(dob, stb, "nt") * qd_ref[u]
            dkf = _dot(dpb, qb, "tn") + _dot(vb, dstb, "nt") * kd_ref[u]
            dst_scr[u] = dst * cd_ref[u] + _dot(qdb, dob, "tn")
            dq_ref[:, ck] = _rot_t(dqf * scale, cos, sin).astype(BF16)
            dk_ref[:, ck] = _rot_t(dkf, cos, sin).astype(BF16)

    rb = lambda hh, b: (nb - 1 - b, hh)
    in_specs = _ret_in_specs(h, t, rev_nb=nb) + [
        pl.BlockSpec((t, hb * RET_DV), rb),
        pl.BlockSpec((1, hb, RET_DK, RET_DV), lambda hh, b: (nb - 1 - b, hh, 0, 0))]
    return pl.pallas_call(
        body, name=name, grid=(h // hb, nb), in_specs=in_specs,
        out_specs=[pl.BlockSpec((t, c_g + h * RET_DV), rb),
                   pl.BlockSpec((1, hb * RET_DV), lambda hh, b: (0, hh))],
        out_shape=[jax.ShapeDtypeStruct((s, c_g + h * RET_DV), BF16), jax.ShapeDtypeStruct((1, h * RET_DV), F32)],
        scratch_shapes=[pltpu.VMEM((hb, RET_DK, RET_DV), F32)],
        compiler_params=_params(("parallel", "arbitrary")))(z, z, z, z, *tables, gn_w.reshape(1, -1), dga, states)


def _sgu_fwd(z, ln_w, ln_b, ws_m, bs, col0, w, name):
    s = z.shape[0]
    t = _tile(s, SGU_BLOCK)
    groups = ws_m.shape[0]
    ch = w // groups
    cb = col0 // w

    def body(su_ref, sv_ref, lw_ref, lb_ref, ws_ref, bs_ref, o_ref):
        zv = _gelu(sv_ref[...].astype(F32))
        dlt = zv - jnp.mean(zv, axis=-1, keepdims=True)
        vn = dlt * lax.rsqrt(jnp.mean(dlt * dlt, axis=-1, keepdims=True) + EPS) * lw_ref[...] + lb_ref[...]
        vnb = vn.astype(BF16)
        for r in range(t // SGU_LEN):
            rows = slice(r * SGU_LEN, (r + 1) * SGU_LEN)
            for gi in range(groups):
                cols = slice(gi * ch, (gi + 1) * ch)
                mixed = _dot(ws_ref[gi], vnb[rows, cols]) + bs_ref[gi]
                o_ref[rows, cols] = (_gelu(su_ref[rows, cols].astype(F32)) * mixed).astype(BF16)

    row = lambda off: pl.BlockSpec((t, w), lambda i: (i, cb + off))
    vec = pl.BlockSpec((1, w), lambda i: (0, 0))
    return pl.pallas_call(
        body, name=name, grid=(s // t,),
        in_specs=[row(0), row(1), vec, vec,
                  pl.BlockSpec((groups, SGU_LEN, SGU_LEN), lambda i: (0, 0, 0)),
                  pl.BlockSpec((groups, SGU_LEN, 1), lambda i: (0, 0, 0))],
        out_specs=pl.BlockSpec((t, w), lambda i: (i, 0)),
        out_shape=jax.ShapeDtypeStruct((s, w), BF16),
        compiler_params=_params(("parallel",)))(z, z, ln_w.reshape(1, w), ln_b.reshape(1, w), ws_m, bs)


def _sgu_bwd(z, dsg, dz_tail, ln_w, ln_b, ws_m, ws_mt, bs, col0, w, name):
    s = z.shape[0]
    t = _tile(s, SGU_BLOCK)
    groups = ws_m.shape[0]
    ch = w // groups
    cb = col0 // w

    def body(su_ref, sv_ref, dsg_ref, lw_ref, lb_ref, ws_ref, wst_ref, bs_ref, tail_ref,
             dz_ref, dlw_ref, dlb_ref, dws_ref, dbs_ref, dvn_scr):
        @pl.when(pl.program_id(0) == 0)
        def _():
            dlw_ref[...] = jnp.zeros_like(dlw_ref)
            dlb_ref[...] = jnp.zeros_like(dlb_ref)
            dws_ref[...] = jnp.zeros_like(dws_ref)
            dbs_ref[...] = jnp.zeros_like(dbs_ref)

        dsu_ref, dsv_ref = dz_ref.at[:, 0:w], dz_ref.at[:, w:2 * w]
        sv = sv_ref[...].astype(F32)
        zv = _gelu(sv)
        dlt = zv - jnp.mean(zv, axis=-1, keepdims=True)
        rstd = lax.rsqrt(jnp.mean(dlt * dlt, axis=-1, keepdims=True) + EPS)
        vh = dlt * rstd
        vnb = (vh * lw_ref[...] + lb_ref[...]).astype(BF16)
        for r in range(t // SGU_LEN):
            rows = slice(r * SGU_LEN, (r + 1) * SGU_LEN)
            for gi in range(groups):
                cols = slice(gi * ch, (gi + 1) * ch)
                vn_p = vnb[rows, cols]
                mixed = _dot(ws_ref[gi], vn_p) + bs_ref[gi]
                su = su_ref[rows, cols].astype(F32)
                dsg_p = dsg_ref[rows, cols].astype(F32)
                dsu_ref[rows, cols] = (dsg_p * mixed * _gelu_grad(su)).astype(BF16)
                dmix = dsg_p * _gelu(su)
                dmixb = dmix.astype(BF16)
                dvn_scr[rows, cols] = _dot(wst_ref[gi], dmixb)
                dws_ref[gi] += _dot(dmixb, vn_p, "nt")
                dbs_ref[gi] += jnp.sum(dmix, axis=1, keepdims=True)
        dvn = dvn_scr[...]
        dlw_ref[...] += jnp.sum(dvn * vh, axis=0, keepdims=True)
        dlb_ref[...] += jnp.sum(dvn, axis=0, keepdims=True)
        dvh = dvn * lw_ref[...]
        dzv = rstd * (dvh - jnp.mean(dvh, axis=-1, keepdims=True)
                      - vh * jnp.mean(dvh * vh, axis=-1, keepdims=True))
        dsv_ref[...] = (dzv * _gelu_grad(sv)).astype(BF16)

    row = lambda off: pl.BlockSpec((t, w), lambda i: (i, cb + off))
    out_row = pl.BlockSpec((t, w), lambda i: (i, 0))
    vec = pl.BlockSpec((1, w), lambda i: (0, 0))
    mat = pl.BlockSpec((groups, SGU_LEN, SGU_LEN), lambda i: (0, 0, 0))
    col = pl.BlockSpec((groups, SGU_LEN, 1), lambda i: (0, 0, 0))
    return pl.pallas_call(
        body, name=name, grid=(s // t,),
        in_specs=[row(0), row(1), out_row, vec, vec, mat, mat, col, ANY],
        out_specs=[pl.BlockSpec((t, 2 * w), lambda i: (i, 0)), vec, vec, mat, col],
        out_shape=[jax.ShapeDtypeStruct(dz_tail.shape, BF16),
                   jax.ShapeDtypeStruct((1, w), F32), jax.ShapeDtypeStruct((1, w), F32),
                   jax.ShapeDtypeStruct((groups, SGU_LEN, SGU_LEN), F32),
                   jax.ShapeDtypeStruct((groups, SGU_LEN, 1), F32)],
        input_output_aliases={8: 0},
        scratch_shapes=[pltpu.VMEM((t, w), F32)],
        compiler_params=_params(("arbitrary",)))(z, z, dsg, ln_w.reshape(1, w), ln_b.reshape(1, w), ws_m, ws_mt, bs,
                                                 dz_tail)


def _merge_fwd(a, b, z, col0, name):
    s, d = a.shape
    tr = _tile(s, ROW_BLOCK)
    cb = col0 // d

    def body(a_ref, b_ref, ga_ref, gb_ref, o_ref):
        o_ref[...] = (_sigmoid(ga_ref[...].astype(F32)) * a_ref[...].astype(F32)
                      + _sigmoid(gb_ref[...].astype(F32)) * b_ref[...].astype(F32)).astype(BF16)

    row = pl.BlockSpec((tr, d), lambda i: (i, 0))
    gate = lambda off: pl.BlockSpec((tr, d), lambda i: (i, cb + off))
    return pl.pallas_call(body, name=name, grid=(s // tr,), in_specs=[row, row, gate(0), gate(1)],
                          out_specs=row, out_shape=jax.ShapeDtypeStruct((s, d), BF16),
                          compiler_params=_params(("parallel",)))(a, b, z, z)


def _merge_bwd(dmg, a, b, z, col0, name):
    s, d = a.shape
    tr = _tile(s, ROW_BLOCK)
    cb = col0 // d

    def body(dm_ref, a_ref, b_ref, ga_ref, gb_ref, da_ref, db_ref, dgt_ref):
        dm = dm_ref[...].astype(F32)
        sa = _sigmoid(ga_ref[...].astype(F32))
        sb = _sigmoid(gb_ref[...].astype(F32))
        da_ref[...] = (dm * sa).astype(BF16)
        db_ref[...] = (dm * sb).astype(BF16)
        dgt_ref[:, :d] = (dm * a_ref[...].astype(F32) * sa * (1.0 - sa)).astype(BF16)
        dgt_ref[:, d:] = (dm * b_ref[...].astype(F32) * sb * (1.0 - sb)).astype(BF16)

    row = pl.BlockSpec((tr, d), lambda i: (i, 0))
    wide = pl.BlockSpec((tr, 2 * d), lambda i: (i, 1))
    gate = lambda off: pl.BlockSpec((tr, d), lambda i: (i, cb + off))
    return pl.pallas_call(
        body, name=name, grid=(s // tr,), in_specs=[row, row, row, gate(0), gate(1)],
        out_specs=[row, row, wide],
        out_shape=[jax.ShapeDtypeStruct((s, d), BF16), jax.ShapeDtypeStruct((s, d), BF16),
                   jax.ShapeDtypeStruct((s, 4 * d), BF16)],
        compiler_params=_params(("parallel",)))(dmg, a, b, z, z)


def _swiglu_bwd(a, c, df, name):
    s, f = a.shape
    f2 = 2 * f
    tr = _tile(s, ROW_BLOCK)

    def body(a_ref, c_ref, df_ref, o_ref):
        a = a_ref[...].astype(F32)
        sg = _sigmoid(a)
        dfv = df_ref[...].astype(F32)
        o_ref[:, :f] = (dfv * c_ref[...].astype(F32) * sg * (1.0 + a * (1.0 - sg))).astype(BF16)
        o_ref[:, f:] = (dfv * a * sg).astype(BF16)

    row = pl.BlockSpec((tr, f), lambda i: (i, 0))
    return pl.pallas_call(body, name=name, grid=(s // tr,), in_specs=[row, row, row],
                          out_specs=pl.BlockSpec((tr, f2), lambda i: (i, 0)),
                          out_shape=jax.ShapeDtypeStruct((s, f2), BF16),
                          compiler_params=_params(("parallel",)))(a, c, df)


def _adamw(w, g, m, v, name):
    r, c = w.shape
    tr = _rtile(r, LANES)
    c1 = 1.0 - ADAM_B1 ** ADAM_STEP
    c2 = 1.0 - ADAM_B2 ** ADAM_STEP

    def body(w_ref, g_ref, m_ref, v_ref, d_ref, mo_ref, vo_ref):
        gv = g_ref[...]
        mn = ADAM_B1 * m_ref[...] + (1.0 - ADAM_B1) * gv
        vn = ADAM_B2 * v_ref[...] + (1.0 - ADAM_B2) * (gv * gv)
        mo_ref[...] = mn
        vo_ref[...] = vn
        d_ref[...] = -ADAM_LR * ((mn / c1) / (jnp.sqrt(vn / c2) + ADAM_EPS) + ADAM_WD * w_ref[...])

    blk = pl.BlockSpec((tr, c), lambda i: (i, 0))
    shp = jax.ShapeDtypeStruct((r, c), F32)
    return pl.pallas_call(body, name=name, grid=(r // tr,), in_specs=[blk] * 4, out_specs=[blk] * 3,
                          out_shape=[shp] * 3, compiler_params=_params(("parallel",)))(w, g, m, v)


def _place():
    x, y, c = lax.axis_index("x"), lax.axis_index("y"), lax.axis_index("c")
    chips = [(1 - x, y), (x, 1 - y), (1 - x, 1 - y)]
    return x, y, c, chips


def _block(ref, kind, chip, half, shard_shape):
    rs, cs = shard_shape
    if kind == "col":
        rows = pl.ds(0, rs) if half is None else pl.ds(half * (rs // 2), rs // 2)
        return ref.at[rows, pl.ds(chip * cs, cs)]
    rows = pl.ds(chip * rs, rs) if half is None else pl.ds(chip * rs + half * (rs // 2), rs // 2)
    return ref.at[rows, :]


def _half_rows(ref, half):
    rs = ref.shape[0]
    return ref.at[pl.ds(half * (rs // 2), rs // 2), :]


class _Task:
    def __init__(self, ins, out_shapes, sems, start, finish, aliases=None):
        self.ins, self.out_shapes, self.sems = list(ins), list(out_shapes), list(sems)
        self.start, self.finish, self.aliases = start, finish, dict(aliases or {})


def _run_task(task, name):
    n_in, n_out = len(task.ins), len(task.out_shapes)

    def body(*refs):
        ins, outs, sems = refs[:n_in], refs[n_in:n_in + n_out], refs[n_in + n_out:]
        task.start(ins, outs, sems)
        task.finish(ins, outs, sems)

    return pl.pallas_call(
        body, name=name, in_specs=[ANY] * n_in, out_specs=[ANY] * n_out, out_shape=task.out_shapes,
        input_output_aliases=task.aliases, scratch_shapes=task.sems,
        compiler_params=pltpu.CompilerParams(has_side_effects=True))(*task.ins)


def _remote(src, dst, send_sem, recv_sem, device):
    return pltpu.make_async_remote_copy(src_ref=src, dst_ref=dst, send_sem=send_sem, recv_sem=recv_sem,
                                        device_id=device, device_id_type=MESH)


def _join_tasks(tasks):
    n_in = [len(t.ins) for t in tasks]
    n_out = [len(t.out_shapes) for t in tasks]
    n_sem = [len(t.sems) for t in tasks]

    def parts(refs, counts):
        out, at = [], 0
        for cnt in counts:
            out.append(refs[at:at + cnt])
            at += cnt
        return out

    def start(ins, outs, sems):
        for t, i, o, s in zip(tasks, parts(ins, n_in), parts(outs, n_out), parts(sems, n_sem)):
            t.start(i, o, s)

    def finish(ins, outs, sems):
        for t, i, o, s in zip(tasks, parts(ins, n_in), parts(outs, n_out), parts(sems, n_sem)):
            t.finish(i, o, s)

    aliases = {}
    for k, t in enumerate(tasks):
        aliases.update({sum(n_in[:k]) + src: sum(n_out[:k]) + dst for src, dst in t.aliases.items()})
    return _Task([a for t in tasks for a in t.ins], [s for t in tasks for s in t.out_shapes],
                 [s for t in tasks for s in t.sems], start, finish, aliases)


def _ag_send_task(shards, kinds):
    n = len(shards)
    out_shapes = [jax.ShapeDtypeStruct((s.shape[0], 4 * s.shape[1]) if k == "col" else (4 * s.shape[0], s.shape[1]),
                                       s.dtype) for s, k in zip(shards, kinds)]

    def copies(ins, outs, sems):
        send_sems, recv_sems, own_send_sems, own_recv_sems = sems
        x, y, c, chips = _place()
        me = 2 * x + y
        own, ici, landed = [], [], []
        for i in range(n):
            shp = ins[i].shape
            own.append(_remote(ins[i], _block(outs[i], kinds[i], me, None, shp), own_send_sems.at[i],
                               own_recv_sems.at[i], (x, y, 1 - c)))
            for j, (px, py) in enumerate(chips):
                k = 3 * i + j
                ici.append(_remote(_half_rows(ins[i], c), _block(outs[i], kinds[i], me, c, shp),
                                   send_sems.at[k], recv_sems.at[k], (px, py, c)))
                got = _block(outs[i], kinds[i], 2 * px + py, c, shp)
                landed.append(_remote(got, got, send_sems.at[k], recv_sems.at[k], (px, py, c)))
        return own, ici, landed

    def start(ins, outs, sems):
        own, ici, _ = copies(ins, outs, sems)
        for cp in own + ici:
            cp.start()

    def finish(ins, outs, sems):
        own, ici, landed = copies(ins, outs, sems)
        for cp in landed:
            cp.wait_recv()
        for cp in own:
            cp.wait()
        for cp in ici:
            cp.wait_send()

    sems = [pltpu.SemaphoreType.DMA((3 * n,))] * 2 + [pltpu.SemaphoreType.DMA((n,))] * 2
    return _Task(shards, out_shapes, sems, start, finish)


def _ag_forward_task(partial, kinds, shard_shapes):
    n = len(partial)

    def copies(outs, sems):
        fsend_sems, frecv_sems = sems
        x, y, c, chips = _place()
        fwd, passed = [], []
        for i in range(n):
            for j, (px, py) in enumerate(chips):
                k = 3 * i + j
                got = _block(outs[i], kinds[i], 2 * px + py, c, shard_shapes[i])
                fwd.append(_remote(got, got, fsend_sems.at[k], frecv_sems.at[k], (x, y, 1 - c)))
                theirs = _block(outs[i], kinds[i], 2 * px + py, 1 - c, shard_shapes[i])
                passed.append(_remote(theirs, theirs, fsend_sems.at[k], frecv_sems.at[k], (x, y, 1 - c)))
        return fwd, passed

    def start(ins, outs, sems):
        for cp in copies(outs, sems)[0]:
            cp.start()

    def finish(ins, outs, sems):
        fwd, passed = copies(outs, sems)
        for cp in passed:
            cp.wait_recv()
        for cp in fwd:
            cp.wait_send()

    return _Task(partial, [jax.ShapeDtypeStruct(p.shape, p.dtype) for p in partial],
                 [pltpu.SemaphoreType.DMA((3 * n,))] * 2, start, finish, aliases={i: i for i in range(n)})


def _ag_task(shards, kinds):
    send = _ag_send_task(shards, kinds)
    forward = _ag_forward_task(send.out_shapes, kinds, [s.shape for s in shards])
    n_send_sems = len(send.sems)

    def start(ins, outs, sems):
        send.start(ins, outs, sems[:n_send_sems])

    def finish(ins, outs, sems):
        send.finish(ins, outs, sems[:n_send_sems])
        forward.start(outs, outs, sems[n_send_sems:])
        forward.finish(outs, outs, sems[n_send_sems:])

    return _Task(shards, send.out_shapes, send.sems + forward.sems, start, finish)


def _exchange_task(grads, kinds, shard_shapes):
    n = len(grads)
    out_shapes = [jax.ShapeDtypeStruct((4, rs // 2, cs), F32) for rs, cs in shard_shapes]

    def copies(ins, outs, sems):
        send_sems, recv_sems = sems
        x, y, c, _ = _place()
        return [_remote(_block(ins[i], kinds[i], q, 1 - c, shard_shapes[i]), outs[i].at[q],
                        send_sems.at[4 * i + q], recv_sems.at[4 * i + q], (x, y, 1 - c))
                for i in range(n) for q in range(4)]

    def start(ins, outs, sems):
        for cp in copies(ins, outs, sems):
            cp.start()

    def finish(ins, outs, sems):
        for cp in copies(ins, outs, sems):
            cp.wait()

    return _Task(grads, out_shapes, [pltpu.SemaphoreType.DMA((4 * n,))] * 2, start, finish)


def _grad_block_map(kind, nt):
    if kind == "col":
        return lambda j, t, p: (p[0] * nt + t, p[1 + j])
    return lambda j, t, p: ((p[1 + j] * 2 + p[0]) * nt + t, 0)


def _chip_sum(grad, sib, kind, shard_shape, place, name):
    rs, cs = shard_shape
    hr = rs // 2
    tr = _rtile(hr, 256)
    nt = hr // tr
    g_map = _grad_block_map(kind, nt)

    def body(p_ref, g_ref, s_ref, o_ref):
        o_ref[0] = (g_ref[...] + s_ref[0]).astype(BF16)

    return pl.pallas_call(
        body, name=name,
        grid_spec=pltpu.PrefetchScalarGridSpec(
            num_scalar_prefetch=1, grid=(3, nt),
            in_specs=[pl.BlockSpec((tr, cs), g_map),
                      pl.BlockSpec((1, tr, cs), lambda j, t, p: (p[1 + j], t, 0))],
            out_specs=pl.BlockSpec((1, tr, cs), lambda j, t, p: (j, t, 0))),
        out_shape=jax.ShapeDtypeStruct((3, hr, cs), BF16),
        compiler_params=_params(("arbitrary", "arbitrary")))(place, grad, sib)


def _scatter_task(parts):
    n = len(parts)

    def copies(ins, outs, sems):
        send_sems, recv_sems = sems
        _, _, c, chips = _place()
        return [_remote(ins[i].at[j], outs[i].at[j], send_sems.at[3 * i + j], recv_sems.at[3 * i + j], (px, py, c))
                for i in range(n) for j, (px, py) in enumerate(chips)]

    def start(ins, outs, sems):
        for cp in copies(ins, outs, sems):
            cp.start()

    def finish(ins, outs, sems):
        for cp in copies(ins, outs, sems):
            cp.wait()

    return _Task(parts, [jax.ShapeDtypeStruct(p.shape, p.dtype) for p in parts],
                 [pltpu.SemaphoreType.DMA((3 * n,))] * 2, start, finish)


def _final_sum(grad, sib, recv, kind, shard_shape, place, name):
    rs, cs = shard_shape
    hr = rs // 2
    tr = _rtile(hr, 256)
    nt = hr // tr
    g_map = _grad_block_map(kind, nt)

    def body(p_ref, g_ref, s_ref, r_ref, out_ref):
        acc = g_ref[...] + s_ref[0]
        for j in range(3):
            acc = acc + r_ref[j].astype(F32)
        out_ref[...] = acc

    return pl.pallas_call(
        body, name=name,
        grid_spec=pltpu.PrefetchScalarGridSpec(
            num_scalar_prefetch=1, grid=(nt,),
            in_specs=[pl.BlockSpec((tr, cs), lambda t, p: g_map(3, t, p)),
                      pl.BlockSpec((1, tr, cs), lambda t, p: (p[4], t, 0)),
                      pl.BlockSpec((3, tr, cs), lambda t, p: (0, t, 0))],
            out_specs=pl.BlockSpec((tr, cs), lambda t, p: (p[0] * nt + t, 0))),
        out_shape=jax.ShapeDtypeStruct((rs, cs), F32),
        compiler_params=_params(("arbitrary",)))(place, grad, sib, recv)


def _join_task(shards):
    n = len(shards)

    def copies(outs, sems):
        send_sems, recv_sems = sems
        x, y, c, _ = _place()
        mine = [_half_rows(outs[i], c) for i in range(n)]
        theirs = [_half_rows(outs[i], 1 - c) for i in range(n)]
        send = [_remote(mine[i], mine[i], send_sems.at[i], recv_sems.at[i], (x, y, 1 - c)) for i in range(n)]
        recv = [_remote(theirs[i], theirs[i], send_sems.at[i], recv_sems.at[i], (x, y, 1 - c)) for i in range(n)]
        return send, recv

    def start(ins, outs, sems):
        for cp in copies(outs, sems)[0]:
            cp.start()

    def finish(ins, outs, sems):
        send, recv = copies(outs, sems)
        for cp in send:
            cp.wait_send()
        for cp in recv:
            cp.wait_recv()

    return _Task(shards, [jax.ShapeDtypeStruct(s.shape, F32) for s in shards],
                 [pltpu.SemaphoreType.DMA((n,))] * 2, start, finish, aliases={i: i for i in range(n)})


def _all_reduce_small(v, name):
    rows = v.shape[0]

    def body(v_ref, o_ref, buf, send_sems, recv_sems):
        x, y, c, _ = _place()
        coord = lambda p: ((1 - x) if p & 4 else x, (1 - y) if p & 2 else y, (1 - c) if p & 1 else c)
        me = 4 * x + 2 * y + c
        buf[me] = v_ref[...]
        copies = []
        for p in range(1, 8):
            cp = _remote(v_ref, buf.at[me], send_sems.at[p - 1], recv_sems.at[p - 1], coord(p))
            cp.start()
            copies.append(cp)
        for p in range(1, 8):
            px, py, pc = coord(p)
            _remote(v_ref, buf.at[4 * px + 2 * py + pc], send_sems.at[p - 1], recv_sems.at[p - 1],
                    coord(p)).wait_recv()
        for cp in copies:
            cp.wait_send()
        acc = buf[0]
        for dev in range(1, 8):
            acc = acc + buf[dev]
        o_ref[...] = acc

    vm = pl.BlockSpec(memory_space=pltpu.VMEM)
    return pl.pallas_call(
        body, name=name, in_specs=[vm], out_specs=vm, out_shape=jax.ShapeDtypeStruct(v.shape, F32),
        scratch_shapes=[pltpu.VMEM((8, rows, LANES), F32), pltpu.SemaphoreType.DMA((7,)),
                        pltpu.SemaphoreType.DMA((7,))],
        compiler_params=pltpu.CompilerParams(vmem_limit_bytes=VMEM_LIMIT))(v)


class _GradReduce:
    def __init__(self, kinds, shard_shapes, place, tag):
        self.kinds, self.shapes, self.place, self.tag = kinds, shard_shapes, place, tag
        self.grads, self.sib, self.peers, self.recv, self.halves, self.done = {}, {}, {}, {}, {}, {}

    def _carry(self, task, idx, into):
        def sink(outs):
            into.update(zip(idx, outs))
        return task, sink

    def exchange(self, idx):
        task = _exchange_task([self.grads[i] for i in idx], [self.kinds[i] for i in idx],
                              [self.shapes[i] for i in idx])
        return self._carry(task, idx, self.sib)

    def chip_sum(self, idx):
        for i in idx:
            self.peers[i] = _chip_sum(self.grads[i], self.sib[i], self.kinds[i], self.shapes[i], self.place,
                                      "rs_chip_sum%d_%s" % (i, self.tag))

    def scatter(self, idx):
        return self._carry(_scatter_task([self.peers[i] for i in idx]), idx, self.recv)

    def final_sum(self, idx):
        for i in idx:
            self.halves[i] = _final_sum(self.grads[i], self.sib[i], self.recv[i], self.kinds[i], self.shapes[i],
                                        self.place, "rs_final_sum%d_%s" % (i, self.tag))

    def join(self, idx):
        return self._carry(_join_task([self.halves[i] for i in idx]), idx, self.done)


def _matmul_carrying(carries, *args, **kw):
    carries = [c for c in carries if c is not None]
    if not carries:
        return _matmul(*args, **kw)
    out, *got = _matmul(*args, task=_join_tasks([task for task, _ in carries]), **kw)
    for task, sink in carries:
        sink(got[:len(task.out_shapes)])
        del got[:len(task.out_shapes)]
    return out


def _run_carry(carry, name):
    task, sink = carry
    sink(_run_task(task, name))


BIG = ["w_in", "ret_proj", "sgu_proj", "w_out", "w_ffn_in", "w_ffn_out"]
BIG_KIND = {"w_in": "col", "ret_proj": "row", "sgu_proj": "row", "w_out": "row", "w_ffn_in": "col",
            "w_ffn_out": "row"}
KINDS = [BIG_KIND[n] for n in BIG]
SMALL = ["norm_mix_w", "ret_gn_w", "sgu_ln_w", "sgu_ln_b", "sgu_w_s", "sgu_b_s", "norm_ffn_w"]
ORDER = ["norm_mix_w", "w_in", "ret_gn_w", "ret_proj", "sgu_ln_w", "sgu_ln_b", "sgu_w_s", "sgu_b_s",
         "sgu_proj", "w_out", "norm_ffn_w", "w_ffn_in", "w_ffn_out", "final_norm_w"]
AG_WHOLE_BEHIND = {"mm_in": [(0, 1), (0, 2), (0, 3)]}
AG_SEND_BEHIND = {"mm_in": [(0, 4), (0, 5)], "mm_ffn_in": [(1, 0)], "mm_ffn_out": [(1, 1), (1, 2), (1, 3)]}
AG_FORWARD_BEHIND = {"mm_ret_proj": [(0, 4), (0, 5)], "mm_ffn_out": [(1, 0)], "mm_in": [(0, 1), (0, 2), (0, 3)]}


def _layer_fwd(x, l, full, partial, shards, sm, tables, dims):
    h, d, w = dims
    c_su, c_gate = 6 * h * RET_DK, 6 * h * RET_DK + 2 * w
    tag = "l%d" % l
    wt = full[l]

    def due(plan, key, where, present):
        return [(l + dl, i) for dl, i in plan.get(key, [])
                if l + dl < len(full) and (BIG[i] in where[l + dl]) == present and BIG[i] not in full[l + dl]]

    def mm(a, wname, out_dtype, key, res=None):
        stages = [(due(AG_WHOLE_BEHIND, key, partial, False), _ag_task, full),
                  (due(AG_SEND_BEHIND, key, partial, False), _ag_send_task, partial)]
        tasks, sinks = [], []
        for todo, make, sink in stages:
            if todo:
                tasks.append(make([shards[ll][i] for ll, i in todo], [KINDS[i] for _, i in todo]))
                sinks.append((todo, sink))
        todo = due(AG_FORWARD_BEHIND, key, partial, True)
        if todo:
            tasks.append(_ag_forward_task([partial[ll][BIG[i]] for ll, i in todo], [KINDS[i] for _, i in todo],
                                          [shards[ll][i].shape for ll, i in todo]))
            sinks.append((todo, full))
        task = _join_tasks(tasks) if tasks else None
        if key == "mm_ffn_in":
            got = list(_ffn_in_swiglu(a, wt[wname], key + "_" + tag, task=task))
            out = tuple(got[:3])
            del got[:3]
        elif task is None:
            return _matmul(a, wt[wname], "nn", out_dtype, key + "_" + tag, res=res)
        else:
            out, *got = _matmul(a, wt[wname], "nn", out_dtype, key + "_" + tag, res=res, task=task)
        for todo, sink in sinks:
            for ll, i in todo:
                sink[ll][BIG[i]] = got.pop(0)
        return out

    h1 = _rms_fwd(x, sm["norm_mix_w"], "rms_mix_fwd_" + tag)
    z = mm(h1, "w_in", BF16, "mm_in")
    ga, states = _ret_fwd(z, sm["ret_gn_w"], tables, h, "ret_fwd_" + tag)
    sg = _sgu_fwd(z, sm["sgu_ln_w"], sm["sgu_ln_b"], sm["ws_m"], sm["bs"], c_su, w, "sgu_fwd_" + tag)
    a = mm(ga, "ret_proj", BF16, "mm_ret_proj")
    b = mm(sg, "sgu_proj", BF16, "mm_sgu_proj")
    mg = _merge_fwd(a, b, z, c_gate, "merge_fwd_" + tag)
    x1 = mm(mg, "w_out", F32, "mm_out", res=x)
    h2 = _rms_fwd(x1, sm["norm_ffn_w"], "rms_ffn_fwd_" + tag)
    fa, fc, f = mm(h2, "w_ffn_in", BF16, "mm_ffn_in")
    x2 = mm(f, "w_ffn_out", F32, "mm_ffn_out", res=x1)
    saved = dict(x=x, h1=h1, z=z, states=states, ga=ga, sg=sg, a=a, b=b, mg=mg, x1=x1, h2=h2, fa=fa, fc=fc, f=f)
    return x2, saved


def _layer_bwd(dx2, dx2b, sv, wt, sm, tables, dims, tag, above, own, last):
    h, d, w = dims
    c_su, c_gate = 6 * h * RET_DK, 6 * h * RET_DK + 2 * w
    gs = {}
    every = list(range(len(BIG)))
    early, ffn, proj = [1, 2, 3, 4, 5], [4, 5], [1, 2, 3]

    def mm(carries, a, b, mode, out_dtype, key, **kw):
        return _matmul_carrying(carries, a, b, mode, out_dtype, key + "_" + tag, **kw)

    has_above = above is not None
    df = mm([above.exchange([0]) if has_above else None], dx2b, wt["w_ffn_out"], "nt", BF16, "mm_dffn_out_x")
    own.grads[5] = mm([above.exchange(early) if has_above else None], sv["f"], dx2b, "tn", F32, "mm_dffn_out_w")
    if has_above:
        above.chip_sum(every)
    dac = _swiglu_bwd(sv["fa"], sv["fc"], df, "swiglu_bwd_" + tag)
    dh2 = mm([above.scatter([0] + proj) if has_above else None], dac, wt["w_ffn_in"], "nt", BF16, "mm_dffn_in_x")
    own.grads[4] = mm([above.scatter(ffn) if has_above else None], sv["h2"], dac, "tn", F32, "mm_dffn_in_w")
    if has_above:
        above.final_sum(every)
    dx1, dx1b, gs["norm_ffn_w"] = _rms_bwd(sv["x1"], sm["norm_ffn_w"], dh2, dx2, "rms_ffn_bwd_" + tag)
    dmg = mm([], dx1b, wt["w_out"], "nt", BF16, "mm_dout_x")
    own.grads[3] = mm([], sv["mg"], dx1b, "tn", F32, "mm_dout_w")
    da, db, dz_tail = _merge_bwd(dmg, sv["a"], sv["b"], sv["z"], c_gate, "merge_bwd_" + tag)
    dga = mm([], da, wt["ret_proj"], "nt", BF16, "mm_dret_proj_x")
    own.grads[1] = mm([], sv["ga"], da, "tn", F32, "mm_dret_proj_w")
    dsg = mm([], db, wt["sgu_proj"], "nt", BF16, "mm_dsgu_proj_x")
    own.grads[2] = mm([], sv["sg"], db, "tn", F32, "mm_dsgu_proj_w")
    dz_tail, gs["sgu_ln_w"], gs["sgu_ln_b"], gs["sgu_w_s"], gs["sgu_b_s"] = _sgu_bwd(
        sv["z"], dsg, dz_tail, sm["sgu_ln_w"], sm["sgu_ln_b"], sm["ws_m"], sm["ws_mt"], sm["bs"], c_su, w,
        "sgu_bwd_" + tag)
    dz_ret, gs["ret_gn_w"] = _ret_bwd(sv["z"], dga, sv["states"], sm["ret_gn_w"], tables, h, "ret_bwd_" + tag)
    in_cols = c_su + 4 * w
    gw_in = mm([own.exchange(early) if last else None], sv["h1"], dz_ret, "tn", F32, "mm_din_w_ret",
               window=(in_cols, 0, None))
    if last:
        own.chip_sum(early)
    own.grads[0] = mm([own.scatter(ffn) if last else None], sv["h1"], dz_tail, "tn", F32, "mm_din_w_tail",
                      window=(in_cols, c_su, gw_in))
    dh1 = mm([above.join(every) if has_above else None, own.scatter(proj) if last else None,
              own.exchange([0]) if last else None], dz_ret, wt["w_in"], "nt", F32, "mm_din_x_ret")
    if last:
        own.chip_sum([0])
        own.final_sum(early)
    dh1 = mm([own.join(early) if last else None, own.scatter([0]) if last else None], dz_tail, wt["w_in"], "nt",
             BF16, "mm_din_x_tail", res=dh1, b_k0=c_su)
    dx, dxb, gs["norm_mix_w"] = _rms_bwd(sv["x"], sm["norm_mix_w"], dh1, dx1, "rms_mix_bwd_" + tag)
    if last:
        own.final_sum([0])
        _run_carry(own.join([0]), "rs_core_join_w_in_" + tag)
    return dx, dxb, gs


def _sgu_mask():
    pos = jnp.arange(SGU_LEN)
    return (pos[None, :] // CHUNK) <= (pos[:, None] // CHUNK)


def kernel(x, norm_mix_w, w_in, ret_gn_w, ret_proj, sgu_ln_w, sgu_ln_b, sgu_w_s, sgu_b_s, sgu_proj, w_out, norm_ffn_w, w_ffn_in, w_ffn_out, final_norm_w, loss_target, m_norm_mix_w, m_w_in, m_ret_gn_w, m_ret_proj, m_sgu_ln_w, m_sgu_ln_b, m_sgu_w_s, m_sgu_b_s, m_sgu_proj, m_w_out, m_norm_ffn_w, m_w_ffn_in, m_w_ffn_out, m_final_norm_w, v_norm_mix_w, v_w_in, v_ret_gn_w, v_ret_proj, v_sgu_ln_w, v_sgu_ln_b, v_sgu_w_s, v_sgu_b_s, v_sgu_proj, v_w_out, v_norm_ffn_w, v_w_ffn_in, v_w_ffn_out, v_final_norm_w):
    weights = dict(norm_mix_w=norm_mix_w, w_in=w_in, ret_gn_w=ret_gn_w, ret_proj=ret_proj, sgu_ln_w=sgu_ln_w,
                   sgu_ln_b=sgu_ln_b, sgu_w_s=sgu_w_s, sgu_b_s=sgu_b_s, sgu_proj=sgu_proj, w_out=w_out,
                   norm_ffn_w=norm_ffn_w, w_ffn_in=w_ffn_in, w_ffn_out=w_ffn_out, final_norm_w=final_norm_w)
    m_in = dict(norm_mix_w=m_norm_mix_w, w_in=m_w_in, ret_gn_w=m_ret_gn_w, ret_proj=m_ret_proj,
                sgu_ln_w=m_sgu_ln_w, sgu_ln_b=m_sgu_ln_b, sgu_w_s=m_sgu_w_s, sgu_b_s=m_sgu_b_s,
                sgu_proj=m_sgu_proj, w_out=m_w_out, norm_ffn_w=m_norm_ffn_w, w_ffn_in=m_w_ffn_in,
                w_ffn_out=m_w_ffn_out, final_norm_w=m_final_norm_w)
    v_in = dict(norm_mix_w=v_norm_mix_w, w_in=v_w_in, ret_gn_w=v_ret_gn_w, ret_proj=v_ret_proj,
                sgu_ln_w=v_sgu_ln_w, sgu_ln_b=v_sgu_ln_b, sgu_w_s=v_sgu_w_s, sgu_b_s=v_sgu_b_s,
                sgu_proj=v_sgu_proj, w_out=v_w_out, norm_ffn_w=v_norm_ffn_w, w_ffn_in=v_w_ffn_in,
                w_ffn_out=v_w_ffn_out, final_norm_w=v_final_norm_w)

    depth = w_in.shape[0]
    _, s, d = x.shape
    w = d
    in_cols = 4 * w_in.shape[2]
    h = (in_cols - 4 * d) // (2 * RET_DK + 2 * RET_DV)
    groups = sgu_w_s.shape[1]
    assert in_cols == h * (2 * RET_DK + 2 * RET_DV) + 4 * d and (6 * h * RET_DK) % d == 0
    assert s % SGU_LEN == 0 and w % groups == 0 and (w // groups) % LANES == 0
    dims = (h, d, w)
    tables = _ret_tables(s, h, _tile(s, RET_BLOCK))
    mask = _sgu_mask()
    cx, cy, cc = lax.axis_index("x"), lax.axis_index("y"), lax.axis_index("c")
    place = jnp.stack([cc, 2 * (1 - cx) + cy, 2 * cx + (1 - cy), 2 * (1 - cx) + (1 - cy),
                       2 * cx + cy]).astype(jnp.int32)

    shard_shapes = [weights[n].shape[1:] for n in BIG]
    shards = [[weights[n][l].astype(BF16) for n in BIG] for l in range(depth)]

    small = []
    for l in range(depth):
        sm = {n: weights[n][l] for n in SMALL}
        ws_m = jnp.where(mask[None], sgu_w_s[l], 0.0)
        sm["ws_m"] = ws_m.astype(BF16)
        sm["ws_mt"] = jnp.swapaxes(ws_m, 1, 2).astype(BF16)
        sm["bs"] = sgu_b_s[l][:, :, None]
        small.append(sm)

    xs = x[0]
    saved = []
    full = [{} for _ in range(depth)]
    partial = [{} for _ in range(depth)]
    full[0][BIG[0]], = _run_task(_ag_task(shards[0][:1], KINDS[:1]), "ag_w_in_l0")
    for l in range(depth):
        xs, sv = _layer_fwd(xs, l, full, partial, shards, small[l], tables, dims)
        saved.append(sv)
    dx, dxb, g_final, sq = _loss_head(xs, final_norm_w, loss_target[0])
    loss = lax.psum(sq[0, 0], ("x", "y", "c")) * (0.5 / d)

    grads_small = [None] * depth
    reduce = [_GradReduce(KINDS, shard_shapes, place, "l%d" % l) for l in range(depth)]
    for l in reversed(range(depth)):
        dx, dxb, grads_small[l] = _layer_bwd(dx, dxb, saved[l], full[l], small[l], tables, dims, "l%d" % l,
                                             reduce[l + 1] if l + 1 < depth else None, reduce[l], l == 0)
    grads_big = [[r.done[i] for i in range(len(BIG))] for r in reduce]
    grad_x = dx[None]

    pieces = []
    for l in range(depth):
        gs = dict(grads_small[l])
        gs["sgu_w_s"] = jnp.where(mask[None], gs["sgu_w_s"], 0.0)
        pieces += [gs[n].reshape(-1) for n in SMALL]
    pieces.append(g_final.reshape(-1))
    flat = jnp.concatenate(pieces)
    total = flat.shape[0]
    rows = -(-total // (8 * LANES)) * 8
    flat = jnp.pad(flat, (0, rows * LANES - total)).reshape(rows, LANES)
    summed = _all_reduce_small(flat, "ar_small").reshape(-1)
    grad = {}
    off = 0
    per_layer = {n: [] for n in SMALL}
    for l in range(depth):
        for n in SMALL:
            shp = weights[n].shape[1:]
            size = math.prod(shp)
            per_layer[n].append(summed[off:off + size].reshape(shp))
            off += size
    for n in SMALL:
        grad[n] = jnp.stack(per_layer[n])
    grad["final_norm_w"] = summed[off:off + d]
    for i, n in enumerate(BIG):
        grad[n] = jnp.stack([grads_big[l][i] for l in range(depth)])

    delta, new_m, new_v = {}, {}, {}
    for n in BIG:
        shp = weights[n].shape
        two_d = lambda a: a.reshape(shp[0] * shp[1], shp[2])
        dl, mn, vn = _adamw(two_d(weights[n]), two_d(grad[n]), two_d(m_in[n]), two_d(v_in[n]), "adamw_" + n)
        delta[n], new_m[n], new_v[n] = dl.reshape(shp), mn.reshape(shp), vn.reshape(shp)
    small_names = SMALL + ["final_norm_w"]

    def pack(tree):
        fl = jnp.concatenate([tree[n].reshape(-1) for n in small_names])
        return jnp.pad(fl, (0, rows * LANES - fl.shape[0])).reshape(rows, LANES)

    dl, mn, vn = _adamw(pack(weights), pack(grad), pack(m_in), pack(v_in), "adamw_small")
    off = 0
    for n in small_names:
        shp = weights[n].shape
        size = math.prod(shp)
        for src, dst in ((dl, delta), (mn, new_m), (vn, new_v)):
            dst[n] = src.reshape(-1)[off:off + size].reshape(shp)
        off += size

    return (loss, grad_x, *[grad[n] for n in ORDER], *[delta[n] for n in ORDER],
            *[new_m[n] for n in ORDER], *[new_v[n] for n in ORDER])
```

```python
import math

import jax
import jax.numpy as jnp
from jax import lax
from jax.experimental import pallas as pl
from jax.experimental.pallas import tpu as pltpu

F32 = jnp.float32
BF16 = jnp.bfloat16

CHUNK = 64
RET_DK = 128
RET_DV = 256
SGU_LEN = 128
ROPE_BASE = 10000.0
EPS = 1e-6
ADAM_LR = 0.001
ADAM_B1 = 0.9
ADAM_B2 = 0.999
ADAM_EPS = 1e-08
ADAM_WD = 0.01
ADAM_STEP = 10

LANES = 128
VMEM_LIMIT = 56 * 1024 * 1024
RET_BLOCK = 256
SGU_BLOCK = 256
ROW_BLOCK = 256
MM_TILE_PREFERRED = 1024
MM_TILE = 1408
MM_KTILE = 3584
FFN_TILE = 512
MESH = pl.DeviceIdType.MESH
ANY = pl.BlockSpec(memory_space=pl.ANY)
INV_SQRT2 = 1.0 / math.sqrt(2.0)
INV_SQRT_2PI = 1.0 / math.sqrt(2.0 * math.pi)

DN = {"nn": (((1,), (0,)), ((), ())), "nt": (((1,), (1,)), ((), ())), "tn": (((0,), (0,)), ((), ()))}


def _dot(a, b, mode="nn"):
    return lax.dot_general(a, b, DN[mode], preferred_element_type=F32)


def _tile(n, target):
    t = min(n, target) // LANES * LANES
    while t >= LANES:
        if n % t == 0:
            return t
        t -= LANES
    return n


def _rtile(n, target):
    t = min(n, target) // 16 * 16
    while t >= 16:
        if n % t == 0:
            return t
        t -= 16
    return n


def _out_tile(n):
    return MM_TILE_PREFERRED if n % MM_TILE_PREFERRED == 0 else _tile(n, MM_TILE)


def _params(sem):
    return pltpu.CompilerParams(dimension_semantics=sem, vmem_limit_bytes=VMEM_LIMIT)


def _sigmoid(x):
    return 1.0 / (1.0 + jnp.exp(-x))


def _gelu(x):
    return 0.5 * x * (1.0 + lax.erf(x * INV_SQRT2))


def _gelu_grad(x):
    return 0.5 * (1.0 + lax.erf(x * INV_SQRT2)) + x * jnp.exp(-0.5 * x * x) * INV_SQRT_2PI


def _matmul(a, b, mode, out_dtype, name, res=None, task=None, b_k0=0, window=None):
    if mode == "nn":
        (m, k), n = a.shape, b.shape[1]
    elif mode == "nt":
        (m, k), n = a.shape, b.shape[0]
    else:
        (k, m), n = a.shape, b.shape[1]
    out_cols, out_c0, into = window if window is not None else (n, 0, None)
    assert b_k0 == 0 or mode == "nt"
    tm, tn, tk = _out_tile(m), _out_tile(math.gcd(n, out_c0)), _tile(math.gcd(k, b_k0), MM_KTILE)
    ni, nj, nk = m // tm, n // tn, k // tk
    kb0, jb0 = b_k0 // tk, out_c0 // tn
    if mode == "tn":
        a_spec = pl.BlockSpec((tk, tm), lambda i, j, kk: (kk, i))
    else:
        a_spec = pl.BlockSpec((tm, tk), lambda i, j, kk: (i, kk))
    if mode == "nt":
        b_spec = pl.BlockSpec((tn, tk), lambda i, j, kk: (j, kb0 + kk))
    else:
        b_spec = pl.BlockSpec((tk, tn), lambda i, j, kk: (kk, j))
    r_spec = pl.BlockSpec((tm, tn), lambda i, j, kk: (i, j))
    o_spec = pl.BlockSpec((tm, tn), lambda i, j, kk: (i, jb0 + j))
    n_mm_in = 2 + (res is not None)
    t_ins = (task.ins if task is not None else []) + ([into] if into is not None else [])
    t_outs = task.out_shapes if task is not None else []
    t_sems = task.sems if task is not None else []
    in_specs = [a_spec, b_spec] + ([r_spec] if res is not None else []) + [ANY] * len(t_ins)
    acc_in_out = out_dtype == F32
    scratch = [] if (nk == 1 or acc_in_out) else [pltpu.VMEM((tm, tn), F32)]

    def body(*refs):
        a_ref, b_ref = refs[0], refs[1]
        r_ref = refs[2] if res is not None else None
        tin = refs[n_mm_in:n_mm_in + len(t_ins)]
        o_ref = refs[n_mm_in + len(t_ins)]
        tout = refs[n_mm_in + len(t_ins) + 1:n_mm_in + len(t_ins) + 1 + len(t_outs)]
        rest = refs[n_mm_in + len(t_ins) + 1 + len(t_outs):]
        acc_scr, sems = (rest[0], rest[1:]) if scratch else (None, rest)
        i, j, kk = pl.program_id(0), pl.program_id(1), pl.program_id(2)
        if task is not None:
            @pl.when((i == 0) & (j == 0) & (kk == 0))
            def _():
                task.start(tin, tout, sems)

        p = _dot(a_ref[...], b_ref[...], mode)
        if nk == 1:
            if r_ref is not None:
                p = p + r_ref[...]
            o_ref[...] = p.astype(o_ref.dtype)
        else:
            acc = o_ref if acc_in_out else acc_scr

            @pl.when(kk == 0)
            def _():
                acc[...] = p if r_ref is None or not acc_in_out else p + r_ref[...]

            @pl.when(kk > 0)
            def _():
                acc[...] += p

            if not acc_in_out:
                @pl.when(kk == nk - 1)
                def _():
                    o = acc[...]
                    if r_ref is not None:
                        o = o + r_ref[...]
                    o_ref[...] = o.astype(o_ref.dtype)

        if task is not None:
            @pl.when((i == ni - 1) & (j == nj - 1) & (kk == nk - 1))
            def _():
                task.finish(tin, tout, sems)

    args = (a, b) + ((res,) if res is not None else ()) + tuple(t_ins)
    out_shape = jax.ShapeDtypeStruct((m, out_cols), out_dtype)
    into_alias = {n_mm_in + len(t_ins) - 1: 0} if into is not None else {}
    if task is None:
        return pl.pallas_call(
            body, name=name, grid=(ni, nj, nk), in_specs=in_specs, out_specs=o_spec, out_shape=out_shape,
            input_output_aliases=into_alias, scratch_shapes=scratch,
            compiler_params=_params(("parallel", "parallel", "arbitrary")))(*args)
    aliases = dict(into_alias)
    aliases.update({n_mm_in + src: 1 + dst for src, dst in task.aliases.items()})
    return pl.pallas_call(
        body, name=name, grid=(ni, nj, nk), in_specs=in_specs, out_specs=[o_spec] + [ANY] * len(t_outs),
        out_shape=[out_shape] + list(t_outs), input_output_aliases=aliases, scratch_shapes=scratch + list(t_sems),
        compiler_params=pltpu.CompilerParams(dimension_semantics=("arbitrary", "arbitrary", "arbitrary"),
                                             vmem_limit_bytes=VMEM_LIMIT, has_side_effects=True))(*args)


def _ffn_in_swiglu(x, w, name, task=None):
    s, k = x.shape
    f = w.shape[1] // 2
    assert k <= MM_KTILE
    tm, tn = _out_tile(s), _tile(f, FFN_TILE)
    ni, nj = s // tm, f // tn
    t_ins = task.ins if task is not None else []
    t_outs = task.out_shapes if task is not None else []
    t_sems = task.sems if task is not None else []

    def body(*refs):
        x_ref, wa_ref, wc_ref = refs[:3]
        tin = refs[3:3 + len(t_ins)]
        a_ref, c_ref, o_ref = refs[3 + len(t_ins):6 + len(t_ins)]
        tout = refs[6 + len(t_ins):6 + len(t_ins) + len(t_outs)]
        sems = refs[6 + len(t_ins) + len(t_outs):]
        i, j = pl.program_id(0), pl.program_id(1)
        if task is not None:
            @pl.when((i == 0) & (j == 0))
            def _():
                task.start(tin, tout, sems)

        xv = x_ref[...]
        a = _dot(xv, wa_ref[...])
        c = _dot(xv, wc_ref[...])
        a_ref[...] = a.astype(BF16)
        c_ref[...] = c.astype(BF16)
        o_ref[...] = (a * _sigmoid(a) * c).astype(BF16)

        if task is not None:
            @pl.when((i == ni - 1) & (j == nj - 1))
            def _():
                task.finish(tin, tout, sems)

    o_spec = pl.BlockSpec((tm, tn), lambda i, j: (i, j))
    in_specs = [pl.BlockSpec((tm, k), lambda i, j: (i, 0)), pl.BlockSpec((k, tn), lambda i, j: (0, j)),
                pl.BlockSpec((k, tn), lambda i, j: (0, nj + j))] + [ANY] * len(t_ins)
    shp = jax.ShapeDtypeStruct((s, f), BF16)
    if task is None:
        params = _params(("parallel", "parallel"))
    else:
        params = pltpu.CompilerParams(dimension_semantics=("arbitrary", "arbitrary"),
                                      vmem_limit_bytes=VMEM_LIMIT, has_side_effects=True)
    return pl.pallas_call(
        body, name=name, grid=(ni, nj), in_specs=in_specs, out_specs=[o_spec] * 3 + [ANY] * len(t_outs),
        out_shape=[shp] * 3 + list(t_outs),
        input_output_aliases={3 + src: 3 + dst for src, dst in (task.aliases.items() if task is not None else [])},
        scratch_shapes=list(t_sems), compiler_params=params)(x, w, w, *t_ins)


def _ffn_out_bwd(dy, w, a, c, name, task=None):
    s, k = dy.shape
    f = w.shape[0]
    assert k <= MM_KTILE
    tm, tn = _out_tile(s), _tile(f, FFN_TILE)
    ni, nj = s // tm, f // tn
    t_ins = task.ins if task is not None else []
    t_outs = task.out_shapes if task is not None else []
    t_sems = task.sems if task is not None else []

    def body(*refs):
        dy_ref, w_ref, a_ref, c_ref = refs[:4]
        tin = refs[4:4 + len(t_ins)]
        o_ref = refs[4 + len(t_ins)]
        tout = refs[5 + len(t_ins):5 + len(t_ins) + len(t_outs)]
        dc_scr = refs[5 + len(t_ins) + len(t_outs)]
        sems = refs[6 + len(t_ins) + len(t_outs):]
        i, j, p = pl.program_id(0), pl.program_id(1), pl.program_id(2)
        if task is not None:
            @pl.when((i == 0) & (j == 0) & (p == 0))
            def _():
                task.start(tin, tout, sems)

        @pl.when(p == 0)
        def _():
            df = _dot(dy_ref[...], w_ref[...], "nt")
            av = a_ref[...].astype(F32)
            sg = _sigmoid(av)
            o_ref[...] = (df * c_ref[...].astype(F32) * sg * (1.0 + av * (1.0 - sg))).astype(BF16)
            dc_scr[...] = (df * av * sg).astype(BF16)

        @pl.when(p == 1)
        def _():
            o_ref[...] = dc_scr[...]

        if task is not None:
            @pl.when((i == ni - 1) & (j == nj - 1) & (p == 1))
            def _():
                task.finish(tin, tout, sems)

    tile = pl.BlockSpec((tm, tn), lambda i, j, p: (i, j))
    in_specs = [pl.BlockSpec((tm, k), lambda i, j, p: (i, 0)), pl.BlockSpec((tn, k), lambda i, j, p: (j, 0)),
                tile, tile] + [ANY] * len(t_ins)
    o_spec = pl.BlockSpec((tm, tn), lambda i, j, p: (i, p * nj + j))
    params = pltpu.CompilerParams(dimension_semantics=("arbitrary",) * 3, vmem_limit_bytes=VMEM_LIMIT,
                                  has_side_effects=task is not None)
    res = pl.pallas_call(
        body, name=name, grid=(ni, nj, 2), in_specs=in_specs, out_specs=[o_spec] + [ANY] * len(t_outs),
        out_shape=[jax.ShapeDtypeStruct((s, 2 * f), BF16)] + list(t_outs),
        input_output_aliases={4 + src: 1 + dst for src, dst in (task.aliases.items() if task is not None else [])},
        scratch_shapes=[pltpu.VMEM((tm, tn), BF16)] + list(t_sems), compiler_params=params)(dy, w, a, c, *t_ins)
    return res if task is not None else res[0]


def _rms_fwd(x, w, name):
    s, d = x.shape
    tr = _tile(s, ROW_BLOCK)

    def body(x_ref, w_ref, o_ref):
        xv = x_ref[...]
        r = lax.rsqrt(jnp.mean(xv * xv, axis=-1, keepdims=True) + EPS)
        o_ref[...] = (xv * r * w_ref[...]).astype(BF16)

    row = pl.BlockSpec((tr, d), lambda i: (i, 0))
    vec = pl.BlockSpec((1, d), lambda i: (0, 0))
    return pl.pallas_call(body, name=name, grid=(s // tr,), in_specs=[row, vec], out_specs=row,
                          out_shape=jax.ShapeDtypeStruct((s, d), BF16),
                          compiler_params=_params(("parallel",)))(x, w.reshape(1, d))


def _rms_bwd(x, w, dh, dres, name):
    s, d = x.shape
    tr = _tile(s, ROW_BLOCK)

    def body(x_ref, w_ref, dh_ref, dr_ref, dx_ref, dxb_ref, dw_ref):
        xv = x_ref[...]
        r = lax.rsqrt(jnp.mean(xv * xv, axis=-1, keepdims=True) + EPS)
        xh = xv * r
        dy = dh_ref[...].astype(F32)
        dxh = dy * w_ref[...]
        dx = dr_ref[...] + r * (dxh - xh * jnp.mean(dxh * xh, axis=-1, keepdims=True))
        dx_ref[...] = dx
        dxb_ref[...] = dx.astype(BF16)

        @pl.when(pl.program_id(0) == 0)
        def _():
            dw_ref[...] = jnp.zeros_like(dw_ref)

        dw_ref[...] += jnp.sum(dy * xh, axis=0, keepdims=True)

    row = pl.BlockSpec((tr, d), lambda i: (i, 0))
    vec = pl.BlockSpec((1, d), lambda i: (0, 0))
    return pl.pallas_call(
        body, name=name, grid=(s // tr,), in_specs=[row, vec, row, row], out_specs=[row, row, vec],
        out_shape=[jax.ShapeDtypeStruct((s, d), F32), jax.ShapeDtypeStruct((s, d), BF16),
                   jax.ShapeDtypeStruct((1, d), F32)],
        compiler_params=_params(("arbitrary",)))(x, w.reshape(1, d), dh, dres)


def _loss_head(x, w, tgt):
    s, d = x.shape
    tr = _tile(s, ROW_BLOCK)

    def body(x_ref, w_ref, t_ref, dx_ref, dxb_ref, dw_ref, l_ref):
        xv = x_ref[...]
        r = lax.rsqrt(jnp.mean(xv * xv, axis=-1, keepdims=True) + EPS)
        xh = xv * r
        e = xh * w_ref[...] - t_ref[...]
        dy = e * (1.0 / d)
        dxh = dy * w_ref[...]
        dx = r * (dxh - xh * jnp.mean(dxh * xh, axis=-1, keepdims=True))
        dx_ref[...] = dx
        dxb_ref[...] = dx.astype(BF16)

        @pl.when(pl.program_id(0) == 0)
        def _():
            dw_ref[...] = jnp.zeros_like(dw_ref)
            l_ref[...] = jnp.zeros_like(l_ref)

        dw_ref[...] += jnp.sum(dy * xh, axis=0, keepdims=True)
        l_ref[...] += jnp.sum(jnp.sum(e * e, axis=1, keepdims=True), axis=0, keepdims=True)

    row = pl.BlockSpec((tr, d), lambda i: (i, 0))
    vec = pl.BlockSpec((1, d), lambda i: (0, 0))
    one = pl.BlockSpec((1, 1), lambda i: (0, 0))
    return pl.pallas_call(
        body, name="loss_head", grid=(s // tr,), in_specs=[row, vec, row], out_specs=[row, row, vec, one],
        out_shape=[jax.ShapeDtypeStruct((s, d), F32), jax.ShapeDtypeStruct((s, d), BF16),
                   jax.ShapeDtypeStruct((1, d), F32), jax.ShapeDtypeStruct((1, 1), F32)],
        compiler_params=_params(("arbitrary",)))(x, w.reshape(1, d), tgt)


def _ret_tables(s, h, t):
    half = RET_DK // 2
    inv = ROPE_BASE ** (-jnp.arange(half, dtype=F32) / half)
    ang = jnp.arange(s, dtype=F32)[:, None] * inv[None, :]
    cos, sin = jnp.cos(ang), jnp.sin(ang)
    cosf = jnp.concatenate([cos, cos], axis=1)
    sinf = jnp.concatenate([-sin, sin], axis=1)
    log_g = jnp.log1p(-(2.0 ** (-5.0 - jnp.arange(h, dtype=F32))))
    idx = jnp.arange(t, dtype=F32)
    chunk = jnp.arange(t) // CHUNK
    allowed = chunk[None, :] <= chunk[:, None]
    dm = jnp.where(allowed[None], jnp.exp(log_g[:, None, None] * jnp.abs(idx[:, None] - idx[None, :])), 0.0)
    qd = jnp.exp(log_g[:, None] * (idx[None, :] + 1.0))
    kd = jnp.exp(log_g[:, None] * (t - 1.0 - idx[None, :]))
    qd = jnp.broadcast_to(qd[:, :, None], (h, t, RET_DK))
    kd = jnp.broadcast_to(kd[:, :, None], (h, t, RET_DK))
    cd = jnp.broadcast_to(jnp.exp(log_g * t)[:, None, None], (h, 1, RET_DV))
    return cosf, sinf, dm, qd, kd, cd


def _rot(x, cos, sin):
    return x * cos + pltpu.roll(x, RET_DK // 2, 1) * sin


def _rot_t(x, cos, sin):
    return x * cos - pltpu.roll(x, RET_DK // 2, 1) * sin


def _ret_heads_per_step(h):
    return h


def _ret_in_specs(h, t, rev_nb=None):
    hb = _ret_heads_per_step(h)
    ng = h // hb
    blk = (lambda b: b) if rev_nb is None else (lambda b: rev_nb - 1 - b)
    return [
        pl.BlockSpec((t, hb * RET_DK), lambda hh, b: (blk(b), hh)),
        pl.BlockSpec((t, hb * RET_DK), lambda hh, b: (blk(b), ng + hh)),
        pl.BlockSpec((t, hb * RET_DV), lambda hh, b: (blk(b), ng + hh)),
        pl.BlockSpec((t, hb * RET_DV), lambda hh, b: (blk(b), 2 * ng + hh)),
        pl.BlockSpec((t, RET_DK), lambda hh, b: (blk(b), 0)),
        pl.BlockSpec((t, RET_DK), lambda hh, b: (blk(b), 0)),
        pl.BlockSpec((hb, t, t), lambda hh, b: (hh, 0, 0)),
        pl.BlockSpec((hb, t, RET_DK), lambda hh, b: (hh, 0, 0)),
        pl.BlockSpec((hb, t, RET_DK), lambda hh, b: (hh, 0, 0)),
        pl.BlockSpec((hb, 1, RET_DV), lambda hh, b: (hh, 0, 0)),
        pl.BlockSpec((1, hb * RET_DV), lambda hh, b: (0, hh)),
    ]


def _ret_fwd(z, gn_w, tables, h, name):
    s = z.shape[0]
    t = _tile(s, RET_BLOCK)
    nb = s // t
    hb = _ret_heads_per_step(h)
    scale = RET_DK ** -0.5

    def body(q_ref, k_ref, v_ref, g_ref, cos_ref, sin_ref, dm_ref, qd_ref, kd_ref, cd_ref, gn_ref,
             o_ref, st_ref, st_scr):
        @pl.when(pl.program_id(1) == 0)
        def _():
            st_scr[...] = jnp.zeros_like(st_scr)

        cos, sin = cos_ref[...], sin_ref[...]
        for u in range(hb):
            ck = slice(u * RET_DK, (u + 1) * RET_DK)
            cv = slice(u * RET_DV, (u + 1) * RET_DV)
            qf = _rot(q_ref[:, ck].astype(F32), cos, sin) * scale
            kf = _rot(k_ref[:, ck].astype(F32), cos, sin)
            vb = v_ref[:, cv]
            p = _dot(qf.astype(BF16), kf.astype(BF16), "nt") * dm_ref[u]
            st = st_scr[u]
            stb = st.astype(BF16)
            st_ref[0, u] = stb
            o = _dot(p.astype(BF16), vb) + _dot((qf * qd_ref[u]).astype(BF16), stb)
            st_scr[u] = st * cd_ref[u] + _dot((kf * kd_ref[u]).astype(BF16), vb, "tn")
            dlt = o - jnp.mean(o, axis=-1, keepdims=True)
            oh = dlt * lax.rsqrt(jnp.mean(dlt * dlt, axis=-1, keepdims=True) + EPS)
            g = g_ref[:, cv].astype(F32)
            o_ref[:, cv] = (g * _sigmoid(g) * oh * gn_ref[:, cv]).astype(BF16)

    return pl.pallas_call(
        body, name=name, grid=(h // hb, nb), in_specs=_ret_in_specs(h, t),
        out_specs=[pl.BlockSpec((t, hb * RET_DV), lambda hh, b: (b, hh)),
                   pl.BlockSpec((1, hb, RET_DK, RET_DV), lambda hh, b: (b, hh, 0, 0))],
        out_shape=[jax.ShapeDtypeStruct((s, h * RET_DV), BF16),
                   jax.ShapeDtypeStruct((nb, h, RET_DK, RET_DV), BF16)],
        scratch_shapes=[pltpu.VMEM((hb, RET_DK, RET_DV), F32)],
        compiler_params=_params(("parallel", "arbitrary")))(z, z, z, z, *tables, gn_w.reshape(1, -1))


def _ret_bwd(z, dga, states, gn_w, tables, h, name):
    s = z.shape[0]
    t = _tile(s, RET_BLOCK)
    nb = s // t
    hb = _ret_heads_per_step(h)
    assert hb == h
    scale = RET_DK ** -0.5
    c_k, c_v, c_g = h * RET_DK, 2 * h * RET_DK, 2 * h * RET_DK + h * RET_DV

    def body(q_ref, k_ref, v_ref, g_ref, cos_ref, sin_ref, dm_ref, qd_ref, kd_ref, cd_ref, gn_ref,
             dga_ref, st_ref, dz_ref, dgn_ref, dst_scr):
        @pl.when(pl.program_id(1) == 0)
        def _():
            dst_scr[...] = jnp.zeros_like(dst_scr)
            dgn_ref[...] = jnp.zeros_like(dgn_ref)

        dq_ref, dk_ref = dz_ref.at[:, 0:c_k], dz_ref.at[:, c_k:c_v]
        dv_ref, dg_ref = dz_ref.at[:, c_v:c_g], dz_ref.at[:, c_g:c_g + h * RET_DV]
        cos, sin = cos_ref[...], sin_ref[...]
        for u in range(hb):
            ck = slice(u * RET_DK, (u + 1) * RET_DK)
            cv = slice(u * RET_DV, (u + 1) * RET_DV)
            dm = dm_ref[u]
            qf = _rot(q_ref[:, ck].astype(F32), cos, sin) * scale
            kf = _rot(k_ref[:, ck].astype(F32), cos, sin)
            qb, kb, vb = qf.astype(BF16), kf.astype(BF16), v_ref[:, cv]
            qdb = (qf * qd_ref[u]).astype(BF16)
            kdb = (kf * kd_ref[u]).astype(BF16)
            stb = st_ref[0, u]
            pb = (_dot(qb, kb, "nt") * dm).astype(BF16)
            o = _dot(pb, vb) + _dot(qdb, stb)
            dlt = o - jnp.mean(o, axis=-1, keepdims=True)
            rstd = lax.rsqrt(jnp.mean(dlt * dlt, axis=-1, keepdims=True) + EPS)
            oh = dlt * rstd
            gn = gn_ref[:, cv]
            g = g_ref[:, cv].astype(F32)
            sg = _sigmoid(g)
            dga_v = dga_ref[:, cv].astype(F32)
            dret = dga_v * g * sg
            dg_ref[:, cv] = (dga_v * oh * gn * sg * (1.0 + g * (1.0 - sg))).astype(BF16)
            dgn_ref[:, cv] += jnp.sum(dret * oh, axis=0, keepdims=True)
            doh = dret * gn
            do = rstd * (doh - jnp.mean(doh, axis=-1, keepdims=True)
                         - oh * jnp.mean(doh * oh, axis=-1, keepdims=True))
            dob = do.astype(BF16)
            dst = dst_scr[u]
            dstb = dst.astype(BF16)
            dv_ref[:, cv] = (_dot(pb, dob, "tn") + _dot(kdb, dstb)).astype(BF16)
            dpb = (_dot(dob, vb, "nt") * dm).astype(BF16)
            dqf = _dot(dpb, kb) + _dot(dob, stb, "nt") * qd_ref[u]
            dkf = _dot(dpb, qb, "tn") + _dot(vb, dstb, "nt") * kd_ref[u]
            dst_scr[u] = dst * cd_ref[u] + _dot(qdb, dob, "tn")
            dq_ref[:, ck] = _rot_t(dqf * scale, cos, sin).astype(BF16)
            dk_ref[:, ck] = _rot_t(dkf, cos, sin).astype(BF16)

    rb = lambda hh, b: (nb - 1 - b, hh)
    in_specs = _ret_in_specs(h, t, rev_nb=nb) + [
        pl.BlockSpec((t, hb * RET_DV), rb),
        pl.BlockSpec((1, hb, RET_DK, RET_DV), lambda hh, b: (nb - 1 - b, hh, 0, 0))]
    return pl.pallas_call(
        body, name=name, grid=(h // hb, nb), in_specs=in_specs,
        out_specs=[pl.BlockSpec((t, c_g + h * RET_DV), rb),
                   pl.BlockSpec((1, hb * RET_DV), lambda hh, b: (0, hh))],
        out_shape=[jax.ShapeDtypeStruct((s, c_g + h * RET_DV), BF16), jax.ShapeDtypeStruct((1, h * RET_DV), F32)],
        scratch_shapes=[pltpu.VMEM((hb, RET_DK, RET_DV), F32)],
        compiler_params=_params(("parallel", "arbitrary")))(z, z, z, z, *tables, gn_w.reshape(1, -1), dga, states)


def _sgu_fwd(z, ln_w, ln_b, ws_m, bs, col0, w, name):
    s = z.shape[0]
    t = _tile(s, SGU_BLOCK)
    groups = ws_m.shape[0]
    ch = w // groups
    cb = col0 // w

    def body(su_ref, sv_ref, lw_ref, lb_ref, ws_ref, bs_ref, o_ref):
        zv = _gelu(sv_ref[...].astype(F32))
        dlt = zv - jnp.mean(zv, axis=-1, keepdims=True)
        vn = dlt * lax.rsqrt(jnp.mean(dlt * dlt, axis=-1, keepdims=True) + EPS) * lw_ref[...] + lb_ref[...]
        vnb = vn.astype(BF16)
        for r in range(t // SGU_LEN):
            rows = slice(r * SGU_LEN, (r + 1) * SGU_LEN)
            for gi in range(groups):
                cols = slice(gi * ch, (gi + 1) * ch)
                mixed = _dot(ws_ref[gi], vnb[rows, cols]) + bs_ref[gi]
                o_ref[rows, cols] = (_gelu(su_ref[rows, cols].astype(F32)) * mixed).astype(BF16)

    row = lambda off: pl.BlockSpec((t, w), lambda i: (i, cb + off))
    vec = pl.BlockSpec((1, w), lambda i: (0, 0))
    return pl.pallas_call(
        body, name=name, grid=(s // t,),
        in_specs=[row(0), row(1), vec, vec,
                  pl.BlockSpec((groups, SGU_LEN, SGU_LEN), lambda i: (0, 0, 0)),
                  pl.BlockSpec((groups, SGU_LEN, 1), lambda i: (0, 0, 0))],
        out_specs=pl.BlockSpec((t, w), lambda i: (i, 0)),
        out_shape=jax.ShapeDtypeStruct((s, w), BF16),
        compiler_params=_params(("parallel",)))(z, z, ln_w.reshape(1, w), ln_b.reshape(1, w), ws_m, bs)


def _sgu_bwd(z, dsg, dz_tail, ln_w, ln_b, ws_m, ws_mt, bs, col0, w, name):
    s = z.shape[0]
    t = _tile(s, SGU_BLOCK)
    groups = ws_m.shape[0]
    ch = w // groups
    cb = col0 // w

    def body(su_ref, sv_ref, dsg_ref, lw_ref, lb_ref, ws_ref, wst_ref, bs_ref, tail_ref,
             dz_ref, dlw_ref, dlb_ref, dws_ref, dbs_ref, dvn_scr):
        @pl.when(pl.program_id(0) == 0)
        def _():
            dlw_ref[...] = jnp.zeros_like(dlw_ref)
            dlb_ref[...] = jnp.zeros_like(dlb_ref)
            dws_ref[...] = jnp.zeros_like(dws_ref)
            dbs_ref[...] = jnp.zeros_like(dbs_ref)

        dsu_ref, dsv_ref = dz_ref.at[:, 0:w], dz_ref.at[:, w:2 * w]
        sv = sv_ref[...].astype(F32)
        zv = _gelu(sv)
        dlt = zv - jnp.mean(zv, axis=-1, keepdims=True)
        rstd = lax.rsqrt(jnp.mean(dlt * dlt, axis=-1, keepdims=True) + EPS)
        vh = dlt * rstd
        vnb = (vh * lw_ref[...] + lb_ref[...]).astype(BF16)
        for r in range(t // SGU_LEN):
            rows = slice(r * SGU_LEN, (r + 1) * SGU_LEN)
            for gi in range(groups):
                cols = slice(gi * ch, (gi + 1) * ch)
                vn_p = vnb[rows, cols]
                mixed = _dot(ws_ref[gi], vn_p) + bs_ref[gi]
                su = su_ref[rows, cols].astype(F32)
                dsg_p = dsg_ref[rows, cols].astype(F32)
                dsu_ref[rows, cols] = (dsg_p * mixed * _gelu_grad(su)).astype(BF16)
                dmix = dsg_p * _gelu(su)
                dmixb = dmix.astype(BF16)
                dvn_scr[rows, cols] = _dot(wst_ref[gi], dmixb)
                dws_ref[gi] += _dot(dmixb, vn_p, "nt")
                dbs_ref[gi] += jnp.sum(dmix, axis=1, keepdims=True)
        dvn = dvn_scr[...]
        dlw_ref[...] += jnp.sum(dvn * vh, axis=0, keepdims=True)
        dlb_ref[...] += jnp.sum(dvn, axis=0, keepdims=True)
        dvh = dvn * lw_ref[...]
        dzv = rstd * (dvh - jnp.mean(dvh, axis=-1, keepdims=True)
                      - vh * jnp.mean(dvh * vh, axis=-1, keepdims=True))
        dsv_ref[...] = (dzv * _gelu_grad(sv)).astype(BF16)

    row = lambda off: pl.BlockSpec((t, w), lambda i: (i, cb + off))
    out_row = pl.BlockSpec((t, w), lambda i: (i, 0))
    vec = pl.BlockSpec((1, w), lambda i: (0, 0))
    mat = pl.BlockSpec((groups, SGU_LEN, SGU_LEN), lambda i: (0, 0, 0))
    col = pl.BlockSpec((groups, SGU_LEN, 1), lambda i: (0, 0, 0))
    return pl.pallas_call(
        body, name=name, grid=(s // t,),
        in_specs=[row(0), row(1), out_row, vec, vec, mat, mat, col, ANY],
        out_specs=[pl.BlockSpec((t, 2 * w), lambda i: (i, 0)), vec, vec, mat, col],
        out_shape=[jax.ShapeDtypeStruct(dz_tail.shape, BF16),
                   jax.ShapeDtypeStruct((1, w), F32), jax.ShapeDtypeStruct((1, w), F32),
                   jax.ShapeDtypeStruct((groups, SGU_LEN, SGU_LEN), F32),
                   jax.ShapeDtypeStruct((groups, SGU_LEN, 1), F32)],
        input_output_aliases={8: 0},
        scratch_shapes=[pltpu.VMEM((t, w), F32)],
        compiler_params=_params(("arbitrary",)))(z, z, dsg, ln_w.reshape(1, w), ln_b.reshape(1, w), ws_m, ws_mt, bs,
                                                 dz_tail)


def _merge_fwd(a, b, z, col0, name):
    s, d = a.shape
    tr = _tile(s, ROW_BLOCK)
    cb = col0 // d

    def body(a_ref, b_ref, ga_ref, gb_ref, o_ref):
        o_ref[...] = (_sigmoid(ga_ref[...].astype(F32)) * a_ref[...].astype(F32)
                      + _sigmoid(gb_ref[...].astype(F32)) * b_ref[...].astype(F32)).astype(BF16)

    row = pl.BlockSpec((tr, d), lambda i: (i, 0))
    gate = lambda off: pl.BlockSpec((tr, d), lambda i: (i, cb + off))
    return pl.pallas_call(body, name=name, grid=(s // tr,), in_specs=[row, row, gate(0), gate(1)],
                          out_specs=row, out_shape=jax.ShapeDtypeStruct((s, d), BF16),
                          compiler_params=_params(("parallel",)))(a, b, z, z)


def _merge_bwd(dmg, a, b, z, col0, name):
    s, d = a.shape
    tr = _tile(s, ROW_BLOCK)
    cb = col0 // d

    def body(dm_ref, a_ref, b_ref, ga_ref, gb_ref, da_ref, db_ref, dgt_ref):
        dm = dm_ref[...].astype(F32)
        sa = _sigmoid(ga_ref[...].astype(F32))
        sb = _sigmoid(gb_ref[...].astype(F32))
        da_ref[...] = (dm * sa).astype(BF16)
        db_ref[...] = (dm * sb).astype(BF16)
        dgt_ref[:, :d] = (dm * a_ref[...].astype(F32) * sa * (1.0 - sa)).astype(BF16)
        dgt_ref[:, d:] = (dm * b_ref[...].astype(F32) * sb * (1.0 - sb)).astype(BF16)

    row = pl.BlockSpec((tr, d), lambda i: (i, 0))
    wide = pl.BlockSpec((tr, 2 * d), lambda i: (i, 1))
    gate = lambda off: pl.BlockSpec((tr, d), lambda i: (i, cb + off))
    return pl.pallas_call(
        body, name=name, grid=(s // tr,), in_specs=[row, row, row, gate(0), gate(1)],
        out_specs=[row, row, wide],
        out_shape=[jax.ShapeDtypeStruct((s, d), BF16), jax.ShapeDtypeStruct((s, d), BF16),
                   jax.ShapeDtypeStruct((s, 4 * d), BF16)],
        compiler_params=_params(("parallel",)))(dmg, a, b, z, z)


def _adamw(w, g, m, v, name):
    r, c = w.shape
    tr = _rtile(r, LANES)
    c1 = 1.0 - ADAM_B1 ** ADAM_STEP
    c2 = 1.0 - ADAM_B2 ** ADAM_STEP

    def body(w_ref, g_ref, m_ref, v_ref, d_ref, mo_ref, vo_ref):
        gv = g_ref[...]
        mn = ADAM_B1 * m_ref[...] + (1.0 - ADAM_B1) * gv
        vn = ADAM_B2 * v_ref[...] + (1.0 - ADAM_B2) * (gv * gv)
        mo_ref[...] = mn
        vo_ref[...] = vn
        d_ref[...] = -ADAM_LR * ((mn / c1) / (jnp.sqrt(vn / c2) + ADAM_EPS) + ADAM_WD * w_ref[...])

    blk = pl.BlockSpec((tr, c), lambda i: (i, 0))
    shp = jax.ShapeDtypeStruct((r, c), F32)
    return pl.pallas_call(body, name=name, grid=(r // tr,), in_specs=[blk] * 4, out_specs=[blk] * 3,
                          out_shape=[shp] * 3, compiler_params=_params(("parallel",)))(w, g, m, v)


def _place():
    x, y, c = lax.axis_index("x"), lax.axis_index("y"), lax.axis_index("c")
    chips = [(1 - x, y), (x, 1 - y), (1 - x, 1 - y)]
    return x, y, c, chips


def _block(ref, kind, chip, half, shard_shape):
    rs, cs = shard_shape
    if kind == "col":
        rows = pl.ds(0, rs) if half is None else pl.ds(half * (rs // 2), rs // 2)
        return ref.at[rows, pl.ds(chip * cs, cs)]
    rows = pl.ds(chip * rs, rs) if half is None else pl.ds(chip * rs + half * (rs // 2), rs // 2)
    return ref.at[rows, :]


def _half_rows(ref, half):
    rs = ref.shape[0]
    return ref.at[pl.ds(half * (rs // 2), rs // 2), :]


class _Task:
    def __init__(self, ins, out_shapes, sems, start, finish, aliases=None):
        self.ins, self.out_shapes, self.sems = list(ins), list(out_shapes), list(sems)
        self.start, self.finish, self.aliases = start, finish, dict(aliases or {})


def _run_task(task, name):
    n_in, n_out = len(task.ins), len(task.out_shapes)

    def body(*refs):
        ins, outs, sems = refs[:n_in], refs[n_in:n_in + n_out], refs[n_in + n_out:]
        task.start(ins, outs, sems)
        task.finish(ins, outs, sems)

    return pl.pallas_call(
        body, name=name, in_specs=[ANY] * n_in, out_specs=[ANY] * n_out, out_shape=task.out_shapes,
        input_output_aliases=task.aliases, scratch_shapes=task.sems,
        compiler_params=pltpu.CompilerParams(has_side_effects=True))(*task.ins)


def _remote(src, dst, send_sem, recv_sem, device):
    return pltpu.make_async_remote_copy(src_ref=src, dst_ref=dst, send_sem=send_sem, recv_sem=recv_sem,
                                        device_id=device, device_id_type=MESH)


def _join_tasks(tasks):
    n_in = [len(t.ins) for t in tasks]
    n_out = [len(t.out_shapes) for t in tasks]
    n_sem = [len(t.sems) for t in tasks]

    def parts(refs, counts):
        out, at = [], 0
        for cnt in counts:
            out.append(refs[at:at + cnt])
            at += cnt
        return out

    def start(ins, outs, sems):
        for t, i, o, s in zip(tasks, parts(ins, n_in), parts(outs, n_out), parts(sems, n_sem)):
            t.start(i, o, s)

    def finish(ins, outs, sems):
        for t, i, o, s in zip(tasks, parts(ins, n_in), parts(outs, n_out), parts(sems, n_sem)):
            t.finish(i, o, s)

    aliases = {}
    for k, t in enumerate(tasks):
        aliases.update({sum(n_in[:k]) + src: sum(n_out[:k]) + dst for src, dst in t.aliases.items()})
    return _Task([a for t in tasks for a in t.ins], [s for t in tasks for s in t.out_shapes],
                 [s for t in tasks for s in t.sems], start, finish, aliases)


def _ag_send_task(shards, kinds):
    n = len(shards)
    out_shapes = [jax.ShapeDtypeStruct((s.shape[0], 4 * s.shape[1]) if k == "col" else (4 * s.shape[0], s.shape[1]),
                                       s.dtype) for s, k in zip(shards, kinds)]

    def copies(ins, outs, sems):
        send_sems, recv_sems, own_send_sems, own_recv_sems = sems
        x, y, c, chips = _place()
        me = 2 * x + y
        own, ici, landed = [], [], []
        for i in range(n):
            shp = ins[i].shape
            own.append(_remote(ins[i], _block(outs[i], kinds[i], me, None, shp), own_send_sems.at[i],
                               own_recv_sems.at[i], (x, y, 1 - c)))
            for j, (px, py) in enumerate(chips):
                k = 3 * i + j
                ici.append(_remote(_half_rows(ins[i], c), _block(outs[i], kinds[i], me, c, shp),
                                   send_sems.at[k], recv_sems.at[k], (px, py, c)))
                got = _block(outs[i], kinds[i], 2 * px + py, c, shp)
                landed.append(_remote(got, got, send_sems.at[k], recv_sems.at[k], (px, py, c)))
        return own, ici, landed

    def start(ins, outs, sems):
        own, ici, _ = copies(ins, outs, sems)
        for cp in own + ici:
            cp.start()

    def finish(ins, outs, sems):
        own, ici, landed = copies(ins, outs, sems)
        for cp in landed:
            cp.wait_recv()
        for cp in own:
            cp.wait()
        for cp in ici:
            cp.wait_send()

    sems = [pltpu.SemaphoreType.DMA((3 * n,))] * 2 + [pltpu.SemaphoreType.DMA((n,))] * 2
    return _Task(shards, out_shapes, sems, start, finish)


def _ag_forward_task(partial, kinds, shard_shapes):
    n = len(partial)

    def copies(outs, sems):
        fsend_sems, frecv_sems = sems
        x, y, c, chips = _place()
        fwd, passed = [], []
        for i in range(n):
            for j, (px, py) in enumerate(chips):
                k = 3 * i + j
                got = _block(outs[i], kinds[i], 2 * px + py, c, shard_shapes[i])
                fwd.append(_remote(got, got, fsend_sems.at[k], frecv_sems.at[k], (x, y, 1 - c)))
                theirs = _block(outs[i], kinds[i], 2 * px + py, 1 - c, shard_shapes[i])
                passed.append(_remote(theirs, theirs, fsend_sems.at[k], frecv_sems.at[k], (x, y, 1 - c)))
        return fwd, passed

    def start(ins, outs, sems):
        for cp in copies(outs, sems)[0]:
            cp.start()

    def finish(ins, outs, sems):
        fwd, passed = copies(outs, sems)
        for cp in passed:
            cp.wait_recv()
        for cp in fwd:
            cp.wait_send()

    return _Task(partial, [jax.ShapeDtypeStruct(p.shape, p.dtype) for p in partial],
                 [pltpu.SemaphoreType.DMA((3 * n,))] * 2, start, finish, aliases={i: i for i in range(n)})


def _ag_task(shards, kinds):
    send = _ag_send_task(shards, kinds)
    forward = _ag_forward_task(send.out_shapes, kinds, [s.shape for s in shards])
    n_send_sems = len(send.sems)

    def start(ins, outs, sems):
        send.start(ins, outs, sems[:n_send_sems])

    def finish(ins, outs, sems):
        send.finish(ins, outs, sems[:n_send_sems])
        forward.start(outs, outs, sems[n_send_sems:])
        forward.finish(outs, outs, sems[n_send_sems:])

    return _Task(shards, send.out_shapes, send.sems + forward.sems, start, finish)


def _exchange_task(grads, kinds, shard_shapes):
    n = len(grads)
    out_shapes = [jax.ShapeDtypeStruct((4, rs // 2, cs), F32) for rs, cs in shard_shapes]

    def copies(ins, outs, sems):
        send_sems, recv_sems = sems
        x, y, c, _ = _place()
        return [_remote(_block(ins[i], kinds[i], q, 1 - c, shard_shapes[i]), outs[i].at[q],
                        send_sems.at[4 * i + q], recv_sems.at[4 * i + q], (x, y, 1 - c))
                for i in range(n) for q in range(4)]

    def start(ins, outs, sems):
        for cp in copies(ins, outs, sems):
            cp.start()

    def finish(ins, outs, sems):
        for cp in copies(ins, outs, sems):
            cp.wait()

    return _Task(grads, out_shapes, [pltpu.SemaphoreType.DMA((4 * n,))] * 2, start, finish)


def _grad_block_map(kind, nt):
    if kind == "col":
        return lambda j, t, p: (p[0] * nt + t, p[1 + j])
    return lambda j, t, p: ((p[1 + j] * 2 + p[0]) * nt + t, 0)


def _chip_sum(grad, sib, kind, shard_shape, place, name):
    rs, cs = shard_shape
    hr = rs // 2
    tr = _rtile(hr, 256)
    nt = hr // tr
    g_map = _grad_block_map(kind, nt)

    def body(p_ref, g_ref, s_ref, o_ref):
        o_ref[0] = (g_ref[...] + s_ref[0]).astype(BF16)

    return pl.pallas_call(
        body, name=name,
        grid_spec=pltpu.PrefetchScalarGridSpec(
            num_scalar_prefetch=1, grid=(3, nt),
            in_specs=[pl.BlockSpec((tr, cs), g_map),
                      pl.BlockSpec((1, tr, cs), lambda j, t, p: (p[1 + j], t, 0))],
            out_specs=pl.BlockSpec((1, tr, cs), lambda j, t, p: (j, t, 0))),
        out_shape=jax.ShapeDtypeStruct((3, hr, cs), BF16),
        compiler_params=_params(("arbitrary", "arbitrary")))(place, grad, sib)


def _scatter_task(parts):
    n = len(parts)

    def copies(ins, outs, sems):
        send_sems, recv_sems = sems
        _, _, c, chips = _place()
        return [_remote(ins[i].at[j], outs[i].at[j], send_sems.at[3 * i + j], recv_sems.at[3 * i + j], (px, py, c))
                for i in range(n) for j, (px, py) in enumerate(chips)]

    def start(ins, outs, sems):
        for cp in copies(ins, outs, sems):
            cp.start()

    def finish(ins, outs, sems):
        for cp in copies(ins, outs, sems):
            cp.wait()

    return _Task(parts, [jax.ShapeDtypeStruct(p.shape, p.dtype) for p in parts],
                 [pltpu.SemaphoreType.DMA((3 * n,))] * 2, start, finish)


def _final_sum(grad, sib, recv, kind, shard_shape, place, name):
    rs, cs = shard_shape
    hr = rs // 2
    tr = _rtile(hr, 256)
    nt = hr // tr
    g_map = _grad_block_map(kind, nt)

    def body(p_ref, g_ref, s_ref, r_ref, out_ref):
        acc = g_ref[...] + s_ref[0]
        for j in range(3):
            acc = acc + r_ref[j].astype(F32)
        out_ref[...] = acc

    return pl.pallas_call(
        body, name=name,
        grid_spec=pltpu.PrefetchScalarGridSpec(
            num_scalar_prefetch=1, grid=(nt,),
            in_specs=[pl.BlockSpec((tr, cs), lambda t, p: g_map(3, t, p)),
                      pl.BlockSpec((1, tr, cs), lambda t, p: (p[4], t, 0)),
                      pl.BlockSpec((3, tr, cs), lambda t, p: (0, t, 0))],
            out_specs=pl.BlockSpec((tr, cs), lambda t, p: (p[0] * nt + t, 0))),
        out_shape=jax.ShapeDtypeStruct((rs, cs), F32),
        compiler_params=_params(("arbitrary",)))(place, grad, sib, recv)


def _join_task(shards):
    n = len(shards)

    def copies(outs, sems):
        send_sems, recv_sems = sems
        x, y, c, _ = _place()
        mine = [_half_rows(outs[i], c) for i in range(n)]
        theirs = [_half_rows(outs[i], 1 - c) for i in range(n)]
        send = [_remote(mine[i], mine[i], send_sems.at[i], recv_sems.at[i], (x, y, 1 - c)) for i in range(n)]
        recv = [_remote(theirs[i], theirs[i], send_sems.at[i], recv_sems.at[i], (x, y, 1 - c)) for i in range(n)]
        return send, recv

    def start(ins, outs, sems):
        for cp in copies(outs, sems)[0]:
            cp.start()

    def finish(ins, outs, sems):
        send, recv = copies(outs, sems)
        for cp in send:
            cp.wait_send()
        for cp in recv:
            cp.wait_recv()

    return _Task(shards, [jax.ShapeDtypeStruct(s.shape, F32) for s in shards],
                 [pltpu.SemaphoreType.DMA((n,))] * 2, start, finish, aliases={i: i for i in range(n)})


def _all_reduce_small(v, name):
    rows = v.shape[0]

    def body(v_ref, o_ref, buf, send_sems, recv_sems):
        x, y, c, _ = _place()
        coord = lambda p: ((1 - x) if p & 4 else x, (1 - y) if p & 2 else y, (1 - c) if p & 1 else c)
        me = 4 * x + 2 * y + c
        buf[me] = v_ref[...]
        copies = []
        for p in range(1, 8):
            cp = _remote(v_ref, buf.at[me], send_sems.at[p - 1], recv_sems.at[p - 1], coord(p))
            cp.start()
            copies.append(cp)
        for p in range(1, 8):
            px, py, pc = coord(p)
            _remote(v_ref, buf.at[4 * px + 2 * py + pc], send_sems.at[p - 1], recv_sems.at[p - 1],
                    coord(p)).wait_recv()
        for cp in copies:
            cp.wait_send()
        acc = buf[0]
        for dev in range(1, 8):
            acc = acc + buf[dev]
        o_ref[...] = acc

    vm = pl.BlockSpec(memory_space=pltpu.VMEM)
    return pl.pallas_call(
        body, name=name, in_specs=[vm], out_specs=vm, out_shape=jax.ShapeDtypeStruct(v.shape, F32),
        scratch_shapes=[pltpu.VMEM((8, rows, LANES), F32), pltpu.SemaphoreType.DMA((7,)),
                        pltpu.SemaphoreType.DMA((7,))],
        compiler_params=pltpu.CompilerParams(vmem_limit_bytes=VMEM_LIMIT))(v)


class _GradReduce:
    def __init__(self, kinds, shard_shapes, place, tag):
        self.kinds, self.shapes, self.place, self.tag = kinds, shard_shapes, place, tag
        self.grads, self.sib, self.peers, self.recv, self.halves, self.done = {}, {}, {}, {}, {}, {}

    def _carry(self, task, idx, into):
        def sink(outs):
            into.update(zip(idx, outs))
        return task, sink

    def exchange(self, idx):
        task = _exchange_task([self.grads[i] for i in idx], [self.kinds[i] for i in idx],
                              [self.shapes[i] for i in idx])
        return self._carry(task, idx, self.sib)

    def chip_sum(self, idx):
        for i in idx:
            self.peers[i] = _chip_sum(self.grads[i], self.sib[i], self.kinds[i], self.shapes[i], self.place,
                                      "rs_chip_sum%d_%s" % (i, self.tag))

    def scatter(self, idx):
        return self._carry(_scatter_task([self.peers[i] for i in idx]), idx, self.recv)

    def final_sum(self, idx):
        for i in idx:
            self.halves[i] = _final_sum(self.grads[i], self.sib[i], self.recv[i], self.kinds[i], self.shapes[i],
                                        self.place, "rs_final_sum%d_%s" % (i, self.tag))

    def join(self, idx):
        return self._carry(_join_task([self.halves[i] for i in idx]), idx, self.done)


def _matmul_carrying(carries, *args, fn=_matmul, **kw):
    carries = [c for c in carries if c is not None]
    if not carries:
        return fn(*args, **kw)
    out, *got = fn(*args, task=_join_tasks([task for task, _ in carries]), **kw)
    for task, sink in carries:
        sink(got[:len(task.out_shapes)])
        del got[:len(task.out_shapes)]
    return out


def _run_carry(carry, name):
    task, sink = carry
    sink(_run_task(task, name))


BIG = ["w_in", "ret_proj", "sgu_proj", "w_out", "w_ffn_in", "w_ffn_out"]
BIG_KIND = {"w_in": "col", "ret_proj": "row", "sgu_proj": "row", "w_out": "row", "w_ffn_in": "col",
            "w_ffn_out": "row"}
KINDS = [BIG_KIND[n] for n in BIG]
SMALL = ["norm_mix_w", "ret_gn_w", "sgu_ln_w", "sgu_ln_b", "sgu_w_s", "sgu_b_s", "norm_ffn_w"]
ORDER = ["norm_mix_w", "w_in", "ret_gn_w", "ret_proj", "sgu_ln_w", "sgu_ln_b", "sgu_w_s", "sgu_b_s",
         "sgu_proj", "w_out", "norm_ffn_w", "w_ffn_in", "w_ffn_out", "final_norm_w"]
AG_WHOLE_BEHIND = {"mm_in": [(0, 1), (0, 2), (0, 3)]}
AG_SEND_BEHIND = {"mm_in": [(0, 4), (0, 5)], "mm_ffn_in": [(1, 0)], "mm_ffn_out": [(1, 1), (1, 2), (1, 3)]}
AG_FORWARD_BEHIND = {"mm_ret_proj": [(0, 4), (0, 5)], "mm_ffn_out": [(1, 0)], "mm_in": [(0, 1), (0, 2), (0, 3)]}


def _layer_fwd(x, l, full, partial, shards, sm, tables, dims):
    h, d, w = dims
    c_su, c_gate = 6 * h * RET_DK, 6 * h * RET_DK + 2 * w
    tag = "l%d" % l
    wt = full[l]

    def due(plan, key, where, present):
        return [(l + dl, i) for dl, i in plan.get(key, [])
                if l + dl < len(full) and (BIG[i] in where[l + dl]) == present and BIG[i] not in full[l + dl]]

    def mm(a, wname, out_dtype, key, res=None):
        stages = [(due(AG_WHOLE_BEHIND, key, partial, False), _ag_task, full),
                  (due(AG_SEND_BEHIND, key, partial, False), _ag_send_task, partial)]
        tasks, sinks = [], []
        for todo, make, sink in stages:
            if todo:
                tasks.append(make([shards[ll][i] for ll, i in todo], [KINDS[i] for _, i in todo]))
                sinks.append((todo, sink))
        todo = due(AG_FORWARD_BEHIND, key, partial, True)
        if todo:
            tasks.append(_ag_forward_task([partial[ll][BIG[i]] for ll, i in todo], [KINDS[i] for _, i in todo],
                                          [shards[ll][i].shape for ll, i in todo]))
            sinks.append((todo, full))
        task = _join_tasks(tasks) if tasks else None
        if key == "mm_ffn_in":
            got = list(_ffn_in_swiglu(a, wt[wname], key + "_" + tag, task=task))
            out = tuple(got[:3])
            del got[:3]
        elif task is None:
            return _matmul(a, wt[wname], "nn", out_dtype, key + "_" + tag, res=res)
        else:
            out, *got = _matmul(a, wt[wname], "nn", out_dtype, key + "_" + tag, res=res, task=task)
        for todo, sink in sinks:
            for ll, i in todo:
                sink[ll][BIG[i]] = got.pop(0)
        return out

    h1 = _rms_fwd(x, sm["norm_mix_w"], "rms_mix_fwd_" + tag)
    z = mm(h1, "w_in", BF16, "mm_in")
    ga, states = _ret_fwd(z, sm["ret_gn_w"], tables, h, "ret_fwd_" + tag)
    sg = _sgu_fwd(z, sm["sgu_ln_w"], sm["sgu_ln_b"], sm["ws_m"], sm["bs"], c_su, w, "sgu_fwd_" + tag)
    a = mm(ga, "ret_proj", BF16, "mm_ret_proj")
    b = mm(sg, "sgu_proj", BF16, "mm_sgu_proj")
    mg = _merge_fwd(a, b, z, c_gate, "merge_fwd_" + tag)
    x1 = mm(mg, "w_out", F32, "mm_out", res=x)
    h2 = _rms_fwd(x1, sm["norm_ffn_w"], "rms_ffn_fwd_" + tag)
    fa, fc, f = mm(h2, "w_ffn_in", BF16, "mm_ffn_in")
    x2 = mm(f, "w_ffn_out", F32, "mm_ffn_out", res=x1)
    saved = dict(x=x, h1=h1, z=z, states=states, ga=ga, sg=sg, a=a, b=b, mg=mg, x1=x1, h2=h2, fa=fa, fc=fc, f=f)
    return x2, saved


def _layer_bwd(dx2, dx2b, sv, wt, sm, tables, dims, tag, above, own, last):
    h, d, w = dims
    c_su, c_gate = 6 * h * RET_DK, 6 * h * RET_DK + 2 * w
    gs = {}
    every = list(range(len(BIG)))
    early, ffn, proj = [1, 2, 3, 4, 5], [4, 5], [1, 2, 3]

    def mm(carries, a, b, mode, out_dtype, key, **kw):
        return _matmul_carrying(carries, a, b, mode, out_dtype, key + "_" + tag, **kw)

    has_above = above is not None
    dac = _matmul_carrying([above.exchange([0]) if has_above else None], dx2b, wt["w_ffn_out"], sv["fa"], sv["fc"],
                           "mm_dffn_out_x_" + tag, fn=_ffn_out_bwd)
    own.grads[5] = mm([above.exchange(early) if has_above else None], sv["f"], dx2b, "tn", F32, "mm_dffn_out_w")
    if has_above:
        above.chip_sum(every)
    dh2 = mm([above.scatter([0] + proj) if has_above else None], dac, wt["w_ffn_in"], "nt", BF16, "mm_dffn_in_x")
    own.grads[4] = mm([above.scatter(ffn) if has_above else None], sv["h2"], dac, "tn", F32, "mm_dffn_in_w")
    if has_above:
        above.final_sum(every)
    dx1, dx1b, gs["norm_ffn_w"] = _rms_bwd(sv["x1"], sm["norm_ffn_w"], dh2, dx2, "rms_ffn_bwd_" + tag)
    dmg = mm([], dx1b, wt["w_out"], "nt", BF16, "mm_dout_x")
    own.grads[3] = mm([], sv["mg"], dx1b, "tn", F32, "mm_dout_w")
    da, db, dz_tail = _merge_bwd(dmg, sv["a"], sv["b"], sv["z"], c_gate, "merge_bwd_" + tag)
    dga = mm([], da, wt["ret_proj"], "nt", BF16, "mm_dret_proj_x")
    own.grads[1] = mm([], sv["ga"], da, "tn", F32, "mm_dret_proj_w")
    dsg = mm([], db, wt["sgu_proj"], "nt", BF16, "mm_dsgu_proj_x")
    own.grads[2] = mm([], sv["sg"], db, "tn", F32, "mm_dsgu_proj_w")
    dz_tail, gs["sgu_ln_w"], gs["sgu_ln_b"], gs["sgu_w_s"], gs["sgu_b_s"] = _sgu_bwd(
        sv["z"], dsg, dz_tail, sm["sgu_ln_w"], sm["sgu_ln_b"], sm["ws_m"], sm["ws_mt"], sm["bs"], c_su, w,
        "sgu_bwd_" + tag)
    dz_ret, gs["ret_gn_w"] = _ret_bwd(sv["z"], dga, sv["states"], sm["ret_gn_w"], tables, h, "ret_bwd_" + tag)
    in_cols = c_su + 4 * w
    gw_in = mm([own.exchange(early) if last else None], sv["h1"], dz_ret, "tn", F32, "mm_din_w_ret",
               window=(in_cols, 0, None))
    if last:
        own.chip_sum(early)
    own.grads[0] = mm([own.scatter(ffn) if last else None], sv["h1"], dz_tail, "tn", F32, "mm_din_w_tail",
                      window=(in_cols, c_su, gw_in))
    dh1 = mm([above.join(every) if has_above else None, own.scatter(proj) if last else None,
              own.exchange([0]) if last else None], dz_ret, wt["w_in"], "nt", F32, "mm_din_x_ret")
    if last:
        own.chip_sum([0])
        own.final_sum(early)
    dh1 = mm([own.join(early) if last else None, own.scatter([0]) if last else None], dz_tail, wt["w_in"], "nt",
             BF16, "mm_din_x_tail", res=dh1, b_k0=c_su)
    dx, dxb, gs["norm_mix_w"] = _rms_bwd(sv["x"], sm["norm_mix_w"], dh1, dx1, "rms_mix_bwd_" + tag)
    if last:
        own.final_sum([0])
        _run_carry(own.join([0]), "rs_core_join_w_in_" + tag)
    return dx, dxb, gs


def _sgu_mask():
    pos = jnp.arange(SGU_LEN)
    return (pos[None, :] // CHUNK) <= (pos[:, None] // CHUNK)


def kernel(x, norm_mix_w, w_in, ret_gn_w, ret_proj, sgu_ln_w, sgu_ln_b, sgu_w_s, sgu_b_s, sgu_proj, w_out, norm_ffn_w, w_ffn_in, w_ffn_out, final_norm_w, loss_target, m_norm_mix_w, m_w_in, m_ret_gn_w, m_ret_proj, m_sgu_ln_w, m_sgu_ln_b, m_sgu_w_s, m_sgu_b_s, m_sgu_proj, m_w_out, m_norm_ffn_w, m_w_ffn_in, m_w_ffn_out, m_final_norm_w, v_norm_mix_w, v_w_in, v_ret_gn_w, v_ret_proj, v_sgu_ln_w, v_sgu_ln_b, v_sgu_w_s, v_sgu_b_s, v_sgu_proj, v_w_out, v_norm_ffn_w, v_w_ffn_in, v_w_ffn_out, v_final_norm_w):
    weights = dict(norm_mix_w=norm_mix_w, w_in=w_in, ret_gn_w=ret_gn_w, ret_proj=ret_proj, sgu_ln_w=sgu_ln_w,
                   sgu_ln_b=sgu_ln_b, sgu_w_s=sgu_w_s, sgu_b_s=sgu_b_s, sgu_proj=sgu_proj, w_out=w_out,
                   norm_ffn_w=norm_ffn_w, w_ffn_in=w_ffn_in, w_ffn_out=w_ffn_out, final_norm_w=final_norm_w)
    m_in = dict(norm_mix_w=m_norm_mix_w, w_in=m_w_in, ret_gn_w=m_ret_gn_w, ret_proj=m_ret_proj,
                sgu_ln_w=m_sgu_ln_w, sgu_ln_b=m_sgu_ln_b, sgu_w_s=m_sgu_w_s, sgu_b_s=m_sgu_b_s,
                sgu_proj=m_sgu_proj, w_out=m_w_out, norm_ffn_w=m_norm_ffn_w, w_ffn_in=m_w_ffn_in,
                w_ffn_out=m_w_ffn_out, final_norm_w=m_final_norm_w)
    v_in = dict(norm_mix_w=v_norm_mix_w, w_in=v_w_in, ret_gn_w=v_ret_gn_w, ret_proj=v_ret_proj,
                sgu_ln_w=v_sgu_ln_w, sgu_ln_b=v_sgu_ln_b, sgu_w_s=v_sgu_w_s, sgu_b_s=v_sgu_b_s,
                sgu_proj=v_sgu_proj, w_out=v_w_out, norm_ffn_w=v_norm_ffn_w, w_ffn_in=v_w_ffn_in,
                w_ffn_out=v_w_ffn_out, final_norm_w=v_final_norm_w)

    depth = w_in.shape[0]
    _, s, d = x.shape
    w = d
    in_cols = 4 * w_in.shape[2]
    h = (in_cols - 4 * d) // (2 * RET_DK + 2 * RET_DV)
    groups = sgu_w_s.shape[1]
    assert in_cols == h * (2 * RET_DK + 2 * RET_DV) + 4 * d and (6 * h * RET_DK) % d == 0
    assert s % SGU_LEN == 0 and w % groups == 0 and (w // groups) % LANES == 0
    dims = (h, d, w)
    tables = _ret_tables(s, h, _tile(s, RET_BLOCK))
    mask = _sgu_mask()
    cx, cy, cc = lax.axis_index("x"), lax.axis_index("y"), lax.axis_index("c")
    place = jnp.stack([cc, 2 * (1 - cx) + cy, 2 * cx + (1 - cy), 2 * (1 - cx) + (1 - cy),
                       2 * cx + cy]).astype(jnp.int32)

    shard_shapes = [weights[n].shape[1:] for n in BIG]
    shards = [[weights[n][l].astype(BF16) for n in BIG] for l in range(depth)]

    small = []
    for l in range(depth):
        sm = {n: weights[n][l] for n in SMALL}
        ws_m = jnp.where(mask[None], sgu_w_s[l], 0.0)
        sm["ws_m"] = ws_m.astype(BF16)
        sm["ws_mt"] = jnp.swapaxes(ws_m, 1, 2).astype(BF16)
        sm["bs"] = sgu_b_s[l][:, :, None]
        small.append(sm)

    xs = x[0]
    saved = []
    full = [{} for _ in range(depth)]
    partial = [{} for _ in range(depth)]
    full[0][BIG[0]], = _run_task(_ag_task(shards[0][:1], KINDS[:1]), "ag_w_in_l0")
    for l in range(depth):
        xs, sv = _layer_fwd(xs, l, full, partial, shards, small[l], tables, dims)
        saved.append(sv)
    dx, dxb, g_final, sq = _loss_head(xs, final_norm_w, loss_target[0])
    loss = lax.psum(sq[0, 0], ("x", "y", "c")) * (0.5 / d)

    grads_small = [None] * depth
    reduce = [_GradReduce(KINDS, shard_shapes, place, "l%d" % l) for l in range(depth)]
    for l in reversed(range(depth)):
        dx, dxb, grads_small[l] = _layer_bwd(dx, dxb, saved[l], full[l], small[l], tables, dims, "l%d" % l,
                                             reduce[l + 1] if l + 1 < depth else None, reduce[l], l == 0)
    grads_big = [[r.done[i] for i in range(len(BIG))] for r in reduce]
    grad_x = dx[None]

    pieces = []
    for l in range(depth):
        gs = dict(grads_small[l])
        gs["sgu_w_s"] = jnp.where(mask[None], gs["sgu_w_s"], 0.0)
        pieces += [gs[n].reshape(-1) for n in SMALL]
    pieces.append(g_final.reshape(-1))
    flat = jnp.concatenate(pieces)
    total = flat.shape[0]
    rows = -(-total // (8 * LANES)) * 8
    flat = jnp.pad(flat, (0, rows * LANES - total)).reshape(rows, LANES)
    summed = _all_reduce_small(flat, "ar_small").reshape(-1)
    grad = {}
    off = 0
    per_layer = {n: [] for n in SMALL}
    for l in range(depth):
        for n in SMALL:
            shp = weights[n].shape[1:]
            size = math.prod(shp)
            per_layer[n].append(summed[off:off + size].reshape(shp))
            off += size
    for n in SMALL:
        grad[n] = jnp.stack(per_layer[n])
    grad["final_norm_w"] = summed[off:off + d]
    for i, n in enumerate(BIG):
        grad[n] = jnp.stack([grads_big[l][i] for l in range(depth)])

    delta, new_m, new_v = {}, {}, {}
    for n in BIG:
        shp = weights[n].shape
        two_d = lambda a: a.reshape(shp[0] * shp[1], shp[2])
        dl, mn, vn = _adamw(two_d(weights[n]), two_d(grad[n]), two_d(m_in[n]), two_d(v_in[n]), "adamw_" + n)
        delta[n], new_m[n], new_v[n] = dl.reshape(shp), mn.reshape(shp), vn.reshape(shp)
    small_names = SMALL + ["final_norm_w"]

    def pack(tree):
        fl = jnp.concatenate([tree[n].reshape(-1) for n in small_names])
        return jnp.pad(fl, (0, rows * LANES - fl.shape[0])).reshape(rows, LANES)

    dl, mn, vn = _adamw(pack(weights), pack(grad), pack(m_in), pack(v_in), "adamw_small")
    off = 0
    for n in small_names:
        shp = weights[n].shape
        size = math.prod(shp)
        for src, dst in ((dl, delta), (mn, new_m), (vn, new_v)):
            dst[n] = src.reshape(-1)[off:off + size].reshape(shp)
        off += size

    return (loss, grad_x, *[grad[n] for n in ORDER], *[delta[n] for n in ORDER],
            *[new_m[n] for n in ORDER], *[new_v[n] for n in ORDER])
```

```python
import math

import jax
import jax.numpy as jnp
from jax import lax
from jax.experimental import pallas as pl
from jax.experimental.pallas import tpu as pltpu

F32 = jnp.float32
BF16 = jnp.bfloat16

CHUNK = 64
RET_DK = 128
RET_DV = 256
SGU_LEN = 128
ROPE_BASE = 10000.0
EPS = 1e-6
ADAM_LR = 0.001
ADAM_B1 = 0.9
ADAM_B2 = 0.999
ADAM_EPS = 1e-08
ADAM_WD = 0.01
ADAM_STEP = 10

LANES = 128
VMEM_LIMIT = 56 * 1024 * 1024
RET_BLOCK = 256
SGU_BLOCK = 256
ROW_BLOCK = 256
MM_TILE_PREFERRED = 1024
MM_TILE = 1408
MM_KTILE = 3584
FFN_TILE = 512
MESH = pl.DeviceIdType.MESH
ANY = pl.BlockSpec(memory_space=pl.ANY)
INV_SQRT2 = 1.0 / math.sqrt(2.0)
INV_SQRT_2PI = 1.0 / math.sqrt(2.0 * math.pi)

DN = {"nn": (((1,), (0,)), ((), ())), "nt": (((1,), (1,)), ((), ())), "tn": (((0,), (0,)), ((), ()))}


def _dot(a, b, mode="nn"):
    return lax.dot_general(a, b, DN[mode], preferred_element_type=F32)


def _tile(n, target):
    t = min(n, target) // LANES * LANES
    while t >= LANES:
        if n % t == 0:
            return t
        t -= LANES
    return n


def _rtile(n, target):
    t = min(n, target) // 16 * 16
    while t >= 16:
        if n % t == 0:
            return t
        t -= 16
    return n


def _out_tile(n):
    return MM_TILE_PREFERRED if n % MM_TILE_PREFERRED == 0 else _tile(n, MM_TILE)


def _params(sem):
    return pltpu.CompilerParams(dimension_semantics=sem, vmem_limit_bytes=VMEM_LIMIT)


def _sigmoid(x):
    return 1.0 / (1.0 + jnp.exp(-x))


def _gelu(x):
    return 0.5 * x * (1.0 + lax.erf(x * INV_SQRT2))


def _gelu_grad(x):
    return 0.5 * (1.0 + lax.erf(x * INV_SQRT2)) + x * jnp.exp(-0.5 * x * x) * INV_SQRT_2PI


def _matmul(a, b, mode, out_dtype, name, res=None, task=None, b_k0=0, window=None):
    if mode == "nn":
        (m, k), n = a.shape, b.shape[1]
    elif mode == "nt":
        (m, k), n = a.shape, b.shape[0]
    else:
        (k, m), n = a.shape, b.shape[1]
    out_cols, out_c0, into = window if window is not None else (n, 0, None)
    assert b_k0 == 0 or mode == "nt"
    tm, tn, tk = _out_tile(m), _out_tile(math.gcd(n, out_c0)), _tile(math.gcd(k, b_k0), MM_KTILE)
    ni, nj, nk = m // tm, n // tn, k // tk
    kb0, jb0 = b_k0 // tk, out_c0 // tn
    if mode == "tn":
        a_spec = pl.BlockSpec((tk, tm), lambda i, j, kk: (kk, i))
    else:
        a_spec = pl.BlockSpec((tm, tk), lambda i, j, kk: (i, kk))
    if mode == "nt":
        b_spec = pl.BlockSpec((tn, tk), lambda i, j, kk: (j, kb0 + kk))
    else:
        b_spec = pl.BlockSpec((tk, tn), lambda i, j, kk: (kk, j))
    r_spec = pl.BlockSpec((tm, tn), lambda i, j, kk: (i, j))
    o_spec = pl.BlockSpec((tm, tn), lambda i, j, kk: (i, jb0 + j))
    n_mm_in = 2 + (res is not None)
    t_ins = (task.ins if task is not None else []) + ([into] if into is not None else [])
    t_outs = task.out_shapes if task is not None else []
    t_sems = task.sems if task is not None else []
    in_specs = [a_spec, b_spec] + ([r_spec] if res is not None else []) + [ANY] * len(t_ins)
    acc_in_out = out_dtype == F32
    scratch = [] if (nk == 1 or acc_in_out) else [pltpu.VMEM((tm, tn), F32)]

    def body(*refs):
        a_ref, b_ref = refs[0], refs[1]
        r_ref = refs[2] if res is not None else None
        tin = refs[n_mm_in:n_mm_in + len(t_ins)]
        o_ref = refs[n_mm_in + len(t_ins)]
        tout = refs[n_mm_in + len(t_ins) + 1:n_mm_in + len(t_ins) + 1 + len(t_outs)]
        rest = refs[n_mm_in + len(t_ins) + 1 + len(t_outs):]
        acc_scr, sems = (rest[0], rest[1:]) if scratch else (None, rest)
        i, j, kk = pl.program_id(0), pl.program_id(1), pl.program_id(2)
        if task is not None:
            @pl.when((i == 0) & (j == 0) & (kk == 0))
            def _():
                task.start(tin, tout, sems)

        p = _dot(a_ref[...], b_ref[...], mode)
        if nk == 1:
            if r_ref is not None:
                p = p + r_ref[...]
            o_ref[...] = p.astype(o_ref.dtype)
        else:
            acc = o_ref if acc_in_out else acc_scr

            @pl.when(kk == 0)
            def _():
                acc[...] = p if r_ref is None or not acc_in_out else p + r_ref[...]

            @pl.when(kk > 0)
            def _():
                acc[...] += p

            if not acc_in_out:
                @pl.when(kk == nk - 1)
                def _():
                    o = acc[...]
                    if r_ref is not None:
                        o = o + r_ref[...]
                    o_ref[...] = o.astype(o_ref.dtype)

        if task is not None:
            @pl.when((i == ni - 1) & (j == nj - 1) & (kk == nk - 1))
            def _():
                task.finish(tin, tout, sems)

    args = (a, b) + ((res,) if res is not None else ()) + tuple(t_ins)
    out_shape = jax.ShapeDtypeStruct((m, out_cols), out_dtype)
    into_alias = {n_mm_in + len(t_ins) - 1: 0} if into is not None else {}
    if task is None:
        return pl.pallas_call(
            body, name=name, grid=(ni, nj, nk), in_specs=in_specs, out_specs=o_spec, out_shape=out_shape,
            input_output_aliases=into_alias, scratch_shapes=scratch,
            compiler_params=_params(("parallel", "parallel", "arbitrary")))(*args)
    aliases = dict(into_alias)
    aliases.update({n_mm_in + src: 1 + dst for src, dst in task.aliases.items()})
    return pl.pallas_call(
        body, name=name, grid=(ni, nj, nk), in_specs=in_specs, out_specs=[o_spec] + [ANY] * len(t_outs),
        out_shape=[out_shape] + list(t_outs), input_output_aliases=aliases, scratch_shapes=scratch + list(t_sems),
        compiler_params=pltpu.CompilerParams(dimension_semantics=("arbitrary", "arbitrary", "arbitrary"),
                                             vmem_limit_bytes=VMEM_LIMIT, has_side_effects=True))(*args)


def _ffn_in_swiglu(x, w, name, task=None):
    s, k = x.shape
    f = w.shape[1] // 2
    assert k <= MM_KTILE
    tm, tn = _out_tile(s), _tile(f, FFN_TILE)
    ni, nj = s // tm, f // tn
    t_ins = task.ins if task is not None else []
    t_outs = task.out_shapes if task is not None else []
    t_sems = task.sems if task is not None else []

    def body(*refs):
        x_ref, wa_ref, wc_ref = refs[:3]
        tin = refs[3:3 + len(t_ins)]
        a_ref, c_ref, o_ref = refs[3 + len(t_ins):6 + len(t_ins)]
        tout = refs[6 + len(t_ins):6 + len(t_ins) + len(t_outs)]
        sems = refs[6 + len(t_ins) + len(t_outs):]
        i, j = pl.program_id(0), pl.program_id(1)
        if task is not None:
            @pl.when((i == 0) & (j == 0))
            def _():
                task.start(tin, tout, sems)

        xv = x_ref[...]
        a = _dot(xv, wa_ref[...])
        c = _dot(xv, wc_ref[...])
        a_ref[...] = a.astype(BF16)
        c_ref[...] = c.astype(BF16)
        o_ref[...] = (a * _sigmoid(a) * c).astype(BF16)

        if task is not None:
            @pl.when((i == ni - 1) & (j == nj - 1))
            def _():
                task.finish(tin, tout, sems)

    o_spec = pl.BlockSpec((tm, tn), lambda i, j: (i, j))
    in_specs = [pl.BlockSpec((tm, k), lambda i, j: (i, 0)), pl.BlockSpec((k, tn), lambda i, j: (0, j)),
                pl.BlockSpec((k, tn), lambda i, j: (0, nj + j))] + [ANY] * len(t_ins)
    shp = jax.ShapeDtypeStruct((s, f), BF16)
    if task is None:
        params = _params(("parallel", "parallel"))
    else:
        params = pltpu.CompilerParams(dimension_semantics=("arbitrary", "arbitrary"),
                                      vmem_limit_bytes=VMEM_LIMIT, has_side_effects=True)
    return pl.pallas_call(
        body, name=name, grid=(ni, nj), in_specs=in_specs, out_specs=[o_spec] * 3 + [ANY] * len(t_outs),
        out_shape=[shp] * 3 + list(t_outs),
        input_output_aliases={3 + src: 3 + dst for src, dst in (task.aliases.items() if task is not None else [])},
        scratch_shapes=list(t_sems), compiler_params=params)(x, w, w, *t_ins)


def _proj_merge(x, w, a, z, col0, name):
    s, k = x.shape
    n = w.shape[1]
    assert k <= MM_KTILE
    tm, tn = _out_tile(s), _out_tile(math.gcd(n, col0))
    jg = col0 // tn

    def body(x_ref, w_ref, a_ref, ga_ref, gb_ref, b_ref, o_ref):
        b = _dot(x_ref[...], w_ref[...])
        b_ref[...] = b.astype(BF16)
        o_ref[...] = (_sigmoid(ga_ref[...].astype(F32)) * a_ref[...].astype(F32)
                      + _sigmoid(gb_ref[...].astype(F32)) * b).astype(BF16)

    tile = pl.BlockSpec((tm, tn), lambda i, j: (i, j))
    gate = lambda off: pl.BlockSpec((tm, tn), lambda i, j: (i, jg + off + j))
    shp = jax.ShapeDtypeStruct((s, n), BF16)
    return pl.pallas_call(
        body, name=name, grid=(s // tm, n // tn),
        in_specs=[pl.BlockSpec((tm, k), lambda i, j: (i, 0)), pl.BlockSpec((k, tn), lambda i, j: (0, j)), tile,
                  gate(0), gate(n // tn)],
        out_specs=[tile, tile], out_shape=[shp, shp],
        compiler_params=_params(("parallel", "parallel")))(x, w, a, z, z)


def _rms_fwd(x, w, name):
    s, d = x.shape
    tr = _tile(s, ROW_BLOCK)

    def body(x_ref, w_ref, o_ref):
        xv = x_ref[...]
        r = lax.rsqrt(jnp.mean(xv * xv, axis=-1, keepdims=True) + EPS)
        o_ref[...] = (xv * r * w_ref[...]).astype(BF16)

    row = pl.BlockSpec((tr, d), lambda i: (i, 0))
    vec = pl.BlockSpec((1, d), lambda i: (0, 0))
    return pl.pallas_call(body, name=name, grid=(s // tr,), in_specs=[row, vec], out_specs=row,
                          out_shape=jax.ShapeDtypeStruct((s, d), BF16),
                          compiler_params=_params(("parallel",)))(x, w.reshape(1, d))


def _rms_bwd(x, w, dh, dres, name):
    s, d = x.shape
    tr = _tile(s, ROW_BLOCK)

    def body(x_ref, w_ref, dh_ref, dr_ref, dx_ref, dxb_ref, dw_ref):
        xv = x_ref[...]
        r = lax.rsqrt(jnp.mean(xv * xv, axis=-1, keepdims=True) + EPS)
        xh = xv * r
        dy = dh_ref[...].astype(F32)
        dxh = dy * w_ref[...]
        dx = dr_ref[...] + r * (dxh - xh * jnp.mean(dxh * xh, axis=-1, keepdims=True))
        dx_ref[...] = dx
        dxb_ref[...] = dx.astype(BF16)

        @pl.when(pl.program_id(0) == 0)
        def _():
            dw_ref[...] = jnp.zeros_like(dw_ref)

        dw_ref[...] += jnp.sum(dy * xh, axis=0, keepdims=True)

    row = pl.BlockSpec((tr, d), lambda i: (i, 0))
    vec = pl.BlockSpec((1, d), lambda i: (0, 0))
    return pl.pallas_call(
        body, name=name, grid=(s // tr,), in_specs=[row, vec, row, row], out_specs=[row, row, vec],
        out_shape=[jax.ShapeDtypeStruct((s, d), F32), jax.ShapeDtypeStruct((s, d), BF16),
                   jax.ShapeDtypeStruct((1, d), F32)],
        compiler_params=_params(("arbitrary",)))(x, w.reshape(1, d), dh, dres)


def _loss_head(x, w, tgt):
    s, d = x.shape
    tr = _tile(s, ROW_BLOCK)

    def body(x_ref, w_ref, t_ref, dx_ref, dxb_ref, dw_ref, l_ref):
        xv = x_ref[...]
        r = lax.rsqrt(jnp.mean(xv * xv, axis=-1, keepdims=True) + EPS)
        xh = xv * r
        e = xh * w_ref[...] - t_ref[...]
        dy = e * (1.0 / d)
        dxh = dy * w_ref[...]
        dx = r * (dxh - xh * jnp.mean(dxh * xh, axis=-1, keepdims=True))
        dx_ref[...] = dx
        dxb_ref[...] = dx.astype(BF16)

        @pl.when(pl.program_id(0) == 0)
        def _():
            dw_ref[...] = jnp.zeros_like(dw_ref)
            l_ref[...] = jnp.zeros_like(l_ref)

        dw_ref[...] += jnp.sum(dy * xh, axis=0, keepdims=True)
        l_ref[...] += jnp.sum(jnp.sum(e * e, axis=1, keepdims=True), axis=0, keepdims=True)

    row = pl.BlockSpec((tr, d), lambda i: (i, 0))
    vec = pl.BlockSpec((1, d), lambda i: (0, 0))
    one = pl.BlockSpec((1, 1), lambda i: (0, 0))
    return pl.pallas_call(
        body, name="loss_head", grid=(s // tr,), in_specs=[row, vec, row], out_specs=[row, row, vec, one],
        out_shape=[jax.ShapeDtypeStruct((s, d), F32), jax.ShapeDtypeStruct((s, d), BF16),
                   jax.ShapeDtypeStruct((1, d), F32), jax.ShapeDtypeStruct((1, 1), F32)],
        compiler_params=_params(("arbitrary",)))(x, w.reshape(1, d), tgt)


def _ret_tables(s, h, t):
    half = RET_DK // 2
    inv = ROPE_BASE ** (-jnp.arange(half, dtype=F32) / half)
    ang = jnp.arange(s, dtype=F32)[:, None] * inv[None, :]
    cos, sin = jnp.cos(ang), jnp.sin(ang)
    cosf = jnp.concatenate([cos, cos], axis=1)
    sinf = jnp.concatenate([-sin, sin], axis=1)
    log_g = jnp.log1p(-(2.0 ** (-5.0 - jnp.arange(h, dtype=F32))))
    idx = jnp.arange(t, dtype=F32)
    chunk = jnp.arange(t) // CHUNK
    allowed = chunk[None, :] <= chunk[:, None]
    dm = jnp.where(allowed[None], jnp.exp(log_g[:, None, None] * jnp.abs(idx[:, None] - idx[None, :])), 0.0)
    qd = jnp.exp(log_g[:, None] * (idx[None, :] + 1.0))
    kd = jnp.exp(log_g[:, None] * (t - 1.0 - idx[None, :]))
    qd = jnp.broadcast_to(qd[:, :, None], (h, t, RET_DK))
    kd = jnp.broadcast_to(kd[:, :, None], (h, t, RET_DK))
    cd = jnp.broadcast_to(jnp.exp(log_g * t)[:, None, None], (h, 1, RET_DV))
    return cosf, sinf, dm, qd, kd, cd


def _rot(x, cos, sin):
    return x * cos + pltpu.roll(x, RET_DK // 2, 1) * sin


def _rot_t(x, cos, sin):
    return x * cos - pltpu.roll(x, RET_DK // 2, 1) * sin


def _ret_heads_per_step(h):
    return h


def _ret_in_specs(h, t, rev_nb=None):
    hb = _ret_heads_per_step(h)
    ng = h // hb
    blk = (lambda b: b) if rev_nb is None else (lambda b: rev_nb - 1 - b)
    return [
        pl.BlockSpec((t, hb * RET_DK), lambda hh, b: (blk(b), hh)),
        pl.BlockSpec((t, hb * RET_DK), lambda hh, b: (blk(b), ng + hh)),
        pl.BlockSpec((t, hb * RET_DV), lambda hh, b: (blk(b), ng + hh)),
        pl.BlockSpec((t, hb * RET_DV), lambda hh, b: (blk(b), 2 * ng + hh)),
        pl.BlockSpec((t, RET_DK), lambda hh, b: (blk(b), 0)),
        pl.BlockSpec((t, RET_DK), lambda hh, b: (blk(b), 0)),
        pl.BlockSpec((hb, t, t), lambda hh, b: (hh, 0, 0)),
        pl.BlockSpec((hb, t, RET_DK), lambda hh, b: (hh, 0, 0)),
        pl.BlockSpec((hb, t, RET_DK), lambda hh, b: (hh, 0, 0)),
        pl.BlockSpec((hb, 1, RET_DV), lambda hh, b: (hh, 0, 0)),
        pl.BlockSpec((1, hb * RET_DV), lambda hh, b: (0, hh)),
    ]


def _ret_fwd(z, gn_w, tables, h, name):
    s = z.shape[0]
    t = _tile(s, RET_BLOCK)
    nb = s // t
    hb = _ret_heads_per_step(h)
    scale = RET_DK ** -0.5

    def body(q_ref, k_ref, v_ref, g_ref, cos_ref, sin_ref, dm_ref, qd_ref, kd_ref, cd_ref, gn_ref,
             o_ref, st_ref, st_scr):
        @pl.when(pl.program_id(1) == 0)
        def _():
            st_scr[...] = jnp.zeros_like(st_scr)

        cos, sin = cos_ref[...], sin_ref[...]
        for u in range(hb):
            ck = slice(u * RET_DK, (u + 1) * RET_DK)
            cv = slice(u * RET_DV, (u + 1) * RET_DV)
            qf = _rot(q_ref[:, ck].astype(F32), cos, sin) * scale
            kf = _rot(k_ref[:, ck].astype(F32), cos, sin)
            vb = v_ref[:, cv]
            p = _dot(qf.astype(BF16), kf.astype(BF16), "nt") * dm_ref[u]
            st = st_scr[u]
            stb = st.astype(BF16)
            st_ref[0, u] = stb
            o = _dot(p.astype(BF16), vb) + _dot((qf * qd_ref[u]).astype(BF16), stb)
            st_scr[u] = st * cd_ref[u] + _dot((kf * kd_ref[u]).astype(BF16), vb, "tn")
            dlt = o - jnp.mean(o, axis=-1, keepdims=True)
            oh = dlt * lax.rsqrt(jnp.mean(dlt * dlt, axis=-1, keepdims=True) + EPS)
            g = g_ref[:, cv].astype(F32)
            o_ref[:, cv] = (g * _sigmoid(g) * oh * gn_ref[:, cv]).astype(BF16)

    return pl.pallas_call(
        body, name=name, grid=(h // hb, nb), in_specs=_ret_in_specs(h, t),
        out_specs=[pl.BlockSpec((t, hb * RET_DV), lambda hh, b: (b, hh)),
                   pl.BlockSpec((1, hb, RET_DK, RET_DV), lambda hh, b: (b, hh, 0, 0))],
        out_shape=[jax.ShapeDtypeStruct((s, h * RET_DV), BF16),
                   jax.ShapeDtypeStruct((nb, h, RET_DK, RET_DV), BF16)],
        scratch_shapes=[pltpu.VMEM((hb, RET_DK, RET_DV), F32)],
        compiler_params=_params(("parallel", "arbitrary")))(z, z, z, z, *tables, gn_w.reshape(1, -1))


def _ret_bwd(z, dga, states, gn_w, tables, h, name):
    s = z.shape[0]
    t = _tile(s, RET_BLOCK)
    nb = s // t
    hb = _ret_heads_per_step(h)
    assert hb == h
    scale = RET_DK ** -0.5
    c_k, c_v, c_g = h * RET_DK, 2 * h * RET_DK, 2 * h * RET_DK + h * RET_DV

    def body(q_ref, k_ref, v_ref, g_ref, cos_ref, sin_ref, dm_ref, qd_ref, kd_ref, cd_ref, gn_ref,
             dga_ref, st_ref, dz_ref, dgn_ref, dst_scr):
        @pl.when(pl.program_id(1) == 0)
        def _():
            dst_scr[...] = jnp.zeros_like(dst_scr)
            dgn_ref[...] = jnp.zeros_like(dgn_ref)

        dq_ref, dk_ref = dz_ref.at[:, 0:c_k], dz_ref.at[:, c_k:c_v]
        dv_ref, dg_ref = dz_ref.at[:, c_v:c_g], dz_ref.at[:, c_g:c_g + h * RET_DV]
        cos, sin = cos_ref[...], sin_ref[...]
        for u in range(hb):
            ck = slice(u * RET_DK, (u + 1) * RET_DK)
            cv = slice(u * RET_DV, (u + 1) * RET_DV)
            dm = dm_ref[u]
            qf = _rot(q_ref[:, ck].astype(F32), cos, sin) * scale
            kf = _rot(k_ref[:, ck].astype(F32), cos, sin)
            qb, kb, vb = qf.astype(BF16), kf.astype(BF16), v_ref[:, cv]
            qdb = (qf * qd_ref[u]).astype(BF16)
            kdb = (kf * kd_ref[u]).astype(BF16)
            stb = st_ref[0, u]
            pb = (_dot(qb, kb, "nt") * dm).astype(BF16)
            o = _dot(pb, vb) + _dot(qdb, stb)
            dlt = o - jnp.mean(o, axis=-1, keepdims=True)
            rstd = lax.rsqrt(jnp.mean(dlt * dlt, axis=-1, keepdims=True) + EPS)
            oh = dlt * rstd
            gn = gn_ref[:, cv]
            g = g_ref[:, cv].astype(F32)
            sg = _sigmoid(g)
            dga_v = dga_ref[:, cv].astype(F32)
            dret = dga_v * g * sg
            dg_ref[:, cv] = (dga_v * oh * gn * sg * (1.0 + g * (1.0 - sg))).astype(BF16)
            dgn_ref[:, cv] += jnp.sum(dret * oh, axis=0, keepdims=True)
            doh = dret * gn
            do = rstd * (doh - jnp.mean(doh, axis=-1, keepdims=True)
                         - oh * jnp.mean(doh * oh, axis=-1, keepdims=True))
            dob = do.astype(BF16)
            dst = dst_scr[u]
            dstb = dst.astype(BF16)
            dv_ref[:, cv] = (_dot(pb, dob, "tn") + _dot(kdb, dstb)).astype(BF16)
            dpb = (_dot(dob, vb, "nt") * dm).astype(BF16)
            dqf = _dot(dpb, kb) + _dot(dob, stb, "nt") * qd_ref[u]
            dkf = _dot(dpb, qb, "tn") + _dot(vb, dstb, "nt") * kd_ref[u]
            dst_scr[u] = dst * cd_ref[u] + _dot(qdb, dob, "tn")
            dq_ref[:, ck] = _rot_t(dqf * scale, cos, sin).astype(BF16)
            dk_ref[:, ck] = _rot_t(dkf, cos, sin).astype(BF16)

    rb = lambda hh, b: (nb - 1 - b, hh)
    in_specs = _ret_in_specs(h, t, rev_nb=nb) + [
        pl.BlockSpec((t, hb * RET_DV), rb),
        pl.BlockSpec((1, hb, RET_DK, RET_DV), lambda hh, b: (nb - 1 - b, hh, 0, 0))]
    return pl.pallas_call(
        body, name=name, grid=(h // hb, nb), in_specs=in_specs,
        out_specs=[pl.BlockSpec((t, c_g + h * RET_DV), rb),
                   pl.BlockSpec((1, hb * RET_DV), lambda hh, b: (0, hh))],
        out_shape=[jax.ShapeDtypeStruct((s, c_g + h * RET_DV), BF16), jax.ShapeDtypeStruct((1, h * RET_DV), F32)],
        scratch_shapes=[pltpu.VMEM((hb, RET_DK, RET_DV), F32)],
        compiler_params=_params(("parallel", "arbitrary")))(z, z, z, z, *tables, gn_w.reshape(1, -1), dga, states)


def _sgu_fwd(z, ln_w, ln_b, ws_m, bs, col0, w, name):
    s = z.shape[0]
    t = _tile(s, SGU_BLOCK)
    groups = ws_m.shape[0]
    ch = w // groups
    cb = col0 // w

    def body(su_ref, sv_ref, lw_ref, lb_ref, ws_ref, bs_ref, o_ref):
        zv = _gelu(sv_ref[...].astype(F32))
        dlt = zv - jnp.mean(zv, axis=-1, keepdims=True)
        vn = dlt * lax.rsqrt(jnp.mean(dlt * dlt, axis=-1, keepdims=True) + EPS) * lw_ref[...] + lb_ref[...]
        vnb = vn.astype(BF16)
        for r in range(t // SGU_LEN):
            rows = slice(r * SGU_LEN, (r + 1) * SGU_LEN)
            for gi in range(groups):
                cols = slice(gi * ch, (gi + 1) * ch)
                mixed = _dot(ws_ref[gi], vnb[rows, cols]) + bs_ref[gi]
                o_ref[rows, cols] = (_gelu(su_ref[rows, cols].astype(F32)) * mixed).astype(BF16)

    row = lambda off: pl.BlockSpec((t, w), lambda i: (i, cb + off))
    vec = pl.BlockSpec((1, w), lambda i: (0, 0))
    return pl.pallas_call(
        body, name=name, grid=(s // t,),
        in_specs=[row(0), row(1), vec, vec,
                  pl.BlockSpec((groups, SGU_LEN, SGU_LEN), lambda i: (0, 0, 0)),
                  pl.BlockSpec((groups, SGU_LEN, 1), lambda i: (0, 0, 0))],
        out_specs=pl.BlockSpec((t, w), lambda i: (i, 0)),
        out_shape=jax.ShapeDtypeStruct((s, w), BF16),
        compiler_params=_params(("parallel",)))(z, z, ln_w.reshape(1, w), ln_b.reshape(1, w), ws_m, bs)


def _sgu_bwd(z, dsg, dz_tail, ln_w, ln_b, ws_m, ws_mt, bs, col0, w, name):
    s = z.shape[0]
    t = _tile(s, SGU_BLOCK)
    groups = ws_m.shape[0]
    ch = w // groups
    cb = col0 // w

    def body(su_ref, sv_ref, dsg_ref, lw_ref, lb_ref, ws_ref, wst_ref, bs_ref, tail_ref,
             dz_ref, dlw_ref, dlb_ref, dws_ref, dbs_ref, dvn_scr):
        @pl.when(pl.program_id(0) == 0)
        def _():
            dlw_ref[...] = jnp.zeros_like(dlw_ref)
            dlb_ref[...] = jnp.zeros_like(dlb_ref)
            dws_ref[...] = jnp.zeros_like(dws_ref)
            dbs_ref[...] = jnp.zeros_like(dbs_ref)

        dsu_ref, dsv_ref = dz_ref.at[:, 0:w], dz_ref.at[:, w:2 * w]
        sv = sv_ref[...].astype(F32)
        zv = _gelu(sv)
        dlt = zv - jnp.mean(zv, axis=-1, keepdims=True)
        rstd = lax.rsqrt(jnp.mean(dlt * dlt, axis=-1, keepdims=True) + EPS)
        vh = dlt * rstd
        vnb = (vh * lw_ref[...] + lb_ref[...]).astype(BF16)
        for r in range(t // SGU_LEN):
            rows = slice(r * SGU_LEN, (r + 1) * SGU_LEN)
            for gi in range(groups):
                cols = slice(gi * ch, (gi + 1) * ch)
                vn_p = vnb[rows, cols]
                mixed = _dot(ws_ref[gi], vn_p) + bs_ref[gi]
                su = su_ref[rows, cols].astype(F32)
                dsg_p = dsg_ref[rows, cols].astype(F32)
                dsu_ref[rows, cols] = (dsg_p * mixed * _gelu_grad(su)).astype(BF16)
                dmix = dsg_p * _gelu(su)
                dmixb = dmix.astype(BF16)
                dvn_scr[rows, cols] = _dot(wst_ref[gi], dmixb)
                dws_ref[gi] += _dot(dmixb, vn_p, "nt")
                dbs_ref[gi] += jnp.sum(dmix, axis=1, keepdims=True)
        dvn = dvn_scr[...]
        dlw_ref[...] += jnp.sum(dvn * vh, axis=0, keepdims=True)
        dlb_ref[...] += jnp.sum(dvn, axis=0, keepdims=True)
        dvh = dvn * lw_ref[...]
        dzv = rstd * (dvh - jnp.mean(dvh, axis=-1, keepdims=True)
                      - vh * jnp.mean(dvh * vh, axis=-1, keepdims=True))
        dsv_ref[...] = (dzv * _gelu_grad(sv)).astype(BF16)

    row = lambda off: pl.BlockSpec((t, w), lambda i: (i, cb + off))
    out_row = pl.BlockSpec((t, w), lambda i: (i, 0))
    vec = pl.BlockSpec((1, w), lambda i: (0, 0))
    mat = pl.BlockSpec((groups, SGU_LEN, SGU_LEN), lambda i: (0, 0, 0))
    col = pl.BlockSpec((groups, SGU_LEN, 1), lambda i: (0, 0, 0))
    return pl.pallas_call(
        body, name=name, grid=(s // t,),
        in_specs=[row(0), row(1), out_row, vec, vec, mat, mat, col, ANY],
        out_specs=[pl.BlockSpec((t, 2 * w), lambda i: (i, 0)), vec, vec, mat, col],
        out_shape=[jax.ShapeDtypeStruct(dz_tail.shape, BF16),
                   jax.ShapeDtypeStruct((1, w), F32), jax.ShapeDtypeStruct((1, w), F32),
                   jax.ShapeDtypeStruct((groups, SGU_LEN, SGU_LEN), F32),
                   jax.ShapeDtypeStruct((groups, SGU_LEN, 1), F32)],
        input_output_aliases={8: 0},
        scratch_shapes=[pltpu.VMEM((t, w), F32)],
        compiler_params=_params(("arbitrary",)))(z, z, dsg, ln_w.reshape(1, w), ln_b.reshape(1, w), ws_m, ws_mt, bs,
                                                 dz_tail)


def _merge_bwd(dmg, a, b, z, col0, name):
    s, d = a.shape
    tr = _tile(s, ROW_BLOCK)
    cb = col0 // d

    def body(dm_ref, a_ref, b_ref, ga_ref, gb_ref, da_ref, db_ref, dgt_ref):
        dm = dm_ref[...].astype(F32)
        sa = _sigmoid(ga_ref[...].astype(F32))
        sb = _sigmoid(gb_ref[...].astype(F32))
        da_ref[...] = (dm * sa).astype(BF16)
        db_ref[...] = (dm * sb).astype(BF16)
        dgt_ref[:, :d] = (dm * a_ref[...].astype(F32) * sa * (1.0 - sa)).astype(BF16)
        dgt_ref[:, d:] = (dm * b_ref[...].astype(F32) * sb * (1.0 - sb)).astype(BF16)

    row = pl.BlockSpec((tr, d), lambda i: (i, 0))
    wide = pl.BlockSpec((tr, 2 * d), lambda i: (i, 1))
    gate = lambda off: pl.BlockSpec((tr, d), lambda i: (i, cb + off))
    return pl.pallas_call(
        body, name=name, grid=(s // tr,), in_specs=[row, row, row, gate(0), gate(1)],
        out_specs=[row, row, wide],
        out_shape=[jax.ShapeDtypeStruct((s, d), BF16), jax.ShapeDtypeStruct((s, d), BF16),
                   jax.ShapeDtypeStruct((s, 4 * d), BF16)],
        compiler_params=_params(("parallel",)))(dmg, a, b, z, z)


def _swiglu_bwd(a, c, df, name):
    s, f = a.shape
    f2 = 2 * f
    tr = _tile(s, ROW_BLOCK)

    def body(a_ref, c_ref, df_ref, o_ref):
        a = a_ref[...].astype(F32)
        sg = _sigmoid(a)
        dfv = df_ref[...].astype(F32)
        o_ref[:, :f] = (dfv * c_ref[...].astype(F32) * sg * (1.0 + a * (1.0 - sg))).astype(BF16)
        o_ref[:, f:] = (dfv * a * sg).astype(BF16)

    row = pl.BlockSpec((tr, f), lambda i: (i, 0))
    return pl.pallas_call(body, name=name, grid=(s // tr,), in_specs=[row, row, row],
                          out_specs=pl.BlockSpec((tr, f2), lambda i: (i, 0)),
                          out_shape=jax.ShapeDtypeStruct((s, f2), BF16),
                          compiler_params=_params(("parallel",)))(a, c, df)


def _adamw(w, g, m, v, name):
    r, c = w.shape
    tr = _rtile(r, LANES)
    c1 = 1.0 - ADAM_B1 ** ADAM_STEP
    c2 = 1.0 - ADAM_B2 ** ADAM_STEP

    def body(w_ref, g_ref, m_ref, v_ref, d_ref, mo_ref, vo_ref):
        gv = g_ref[...]
        mn = ADAM_B1 * m_ref[...] + (1.0 - ADAM_B1) * gv
        vn = ADAM_B2 * v_ref[...] + (1.0 - ADAM_B2) * (gv * gv)
        mo_ref[...] = mn
        vo_ref[...] = vn
        d_ref[...] = -ADAM_LR * ((mn / c1) / (jnp.sqrt(vn / c2) + ADAM_EPS) + ADAM_WD * w_ref[...])

    blk = pl.BlockSpec((tr, c), lambda i: (i, 0))
    shp = jax.ShapeDtypeStruct((r, c), F32)
    return pl.pallas_call(body, name=name, grid=(r // tr,), in_specs=[blk] * 4, out_specs=[blk] * 3,
                          out_shape=[shp] * 3, compiler_params=_params(("parallel",)))(w, g, m, v)


def _place():
    x, y, c = lax.axis_index("x"), lax.axis_index("y"), lax.axis_index("c")
    chips = [(1 - x, y), (x, 1 - y), (1 - x, 1 - y)]
    return x, y, c, chips


def _block(ref, kind, chip, half, shard_shape):
    rs, cs = shard_shape
    if kind == "col":
        rows = pl.ds(0, rs) if half is None else pl.ds(half * (rs // 2), rs // 2)
        return ref.at[rows, pl.ds(chip * cs, cs)]
    rows = pl.ds(chip * rs, rs) if half is None else pl.ds(chip * rs + half * (rs // 2), rs // 2)
    return ref.at[rows, :]


def _half_rows(ref, half):
    rs = ref.shape[0]
    return ref.at[pl.ds(half * (rs // 2), rs // 2), :]


class _Task:
    def __init__(self, ins, out_shapes, sems, start, finish, aliases=None):
        self.ins, self.out_shapes, self.sems = list(ins), list(out_shapes), list(sems)
        self.start, self.finish, self.aliases = start, finish, dict(aliases or {})


def _run_task(task, name):
    n_in, n_out = len(task.ins), len(task.out_shapes)

    def body(*refs):
        ins, outs, sems = refs[:n_in], refs[n_in:n_in + n_out], refs[n_in + n_out:]
        task.start(ins, outs, sems)
        task.finish(ins, outs, sems)

    return pl.pallas_call(
        body, name=name, in_specs=[ANY] * n_in, out_specs=[ANY] * n_out, out_shape=task.out_shapes,
        input_output_aliases=task.aliases, scratch_shapes=task.sems,
        compiler_params=pltpu.CompilerParams(has_side_effects=True))(*task.ins)


def _remote(src, dst, send_sem, recv_sem, device):
    return pltpu.make_async_remote_copy(src_ref=src, dst_ref=dst, send_sem=send_sem, recv_sem=recv_sem,
                                        device_id=device, device_id_type=MESH)


def _join_tasks(tasks):
    n_in = [len(t.ins) for t in tasks]
    n_out = [len(t.out_shapes) for t in tasks]
    n_sem = [len(t.sems) for t in tasks]

    def parts(refs, counts):
        out, at = [], 0
        for cnt in counts:
            out.append(refs[at:at + cnt])
            at += cnt
        return out

    def start(ins, outs, sems):
        for t, i, o, s in zip(tasks, parts(ins, n_in), parts(outs, n_out), parts(sems, n_sem)):
            t.start(i, o, s)

    def finish(ins, outs, sems):
        for t, i, o, s in zip(tasks, parts(ins, n_in), parts(outs, n_out), parts(sems, n_sem)):
            t.finish(i, o, s)

    aliases = {}
    for k, t in enumerate(tasks):
        aliases.update({sum(n_in[:k]) + src: sum(n_out[:k]) + dst for src, dst in t.aliases.items()})
    return _Task([a for t in tasks for a in t.ins], [s for t in tasks for s in t.out_shapes],
                 [s for t in tasks for s in t.sems], start, finish, aliases)


def _ag_send_task(shards, kinds):
    n = len(shards)
    out_shapes = [jax.ShapeDtypeStruct((s.shape[0], 4 * s.shape[1]) if k == "col" else (4 * s.shape[0], s.shape[1]),
                                       s.dtype) for s, k in zip(shards, kinds)]

    def copies(ins, outs, sems):
        send_sems, recv_sems, own_send_sems, own_recv_sems = sems
        x, y, c, chips = _place()
        me = 2 * x + y
        own, ici, landed = [], [], []
        for i in range(n):
            shp = ins[i].shape
            own.append(_remote(ins[i], _block(outs[i], kinds[i], me, None, shp), own_send_sems.at[i],
                               own_recv_sems.at[i], (x, y, 1 - c)))
            for j, (px, py) in enumerate(chips):
                k = 3 * i + j
                ici.append(_remote(_half_rows(ins[i], c), _block(outs[i], kinds[i], me, c, shp),
                                   send_sems.at[k], recv_sems.at[k], (px, py, c)))
                got = _block(outs[i], kinds[i], 2 * px + py, c, shp)
                landed.append(_remote(got, got, send_sems.at[k], recv_sems.at[k], (px, py, c)))
        return own, ici, landed

    def start(ins, outs, sems):
        own, ici, _ = copies(ins, outs, sems)
        for cp in own + ici:
            cp.start()

    def finish(ins, outs, sems):
        own, ici, landed = copies(ins, outs, sems)
        for cp in landed:
            cp.wait_recv()
        for cp in own:
            cp.wait()
        for cp in ici:
            cp.wait_send()

    sems = [pltpu.SemaphoreType.DMA((3 * n,))] * 2 + [pltpu.SemaphoreType.DMA((n,))] * 2
    return _Task(shards, out_shapes, sems, start, finish)


def _ag_forward_task(partial, kinds, shard_shapes):
    n = len(partial)

    def copies(outs, sems):
        fsend_sems, frecv_sems = sems
        x, y, c, chips = _place()
        fwd, passed = [], []
        for i in range(n):
            for j, (px, py) in enumerate(chips):
                k = 3 * i + j
                got = _block(outs[i], kinds[i], 2 * px + py, c, shard_shapes[i])
                fwd.append(_remote(got, got, fsend_sems.at[k], frecv_sems.at[k], (x, y, 1 - c)))
                theirs = _block(outs[i], kinds[i], 2 * px + py, 1 - c, shard_shapes[i])
                passed.append(_remote(theirs, theirs, fsend_sems.at[k], frecv_sems.at[k], (x, y, 1 - c)))
        return fwd, passed

    def start(ins, outs, sems):
        for cp in copies(outs, sems)[0]:
            cp.start()

    def finish(ins, outs, sems):
        fwd, passed = copies(outs, sems)
        for cp in passed:
            cp.wait_recv()
        for cp in fwd:
            cp.wait_send()

    return _Task(partial, [jax.ShapeDtypeStruct(p.shape, p.dtype) for p in partial],
                 [pltpu.SemaphoreType.DMA((3 * n,))] * 2, start, finish, aliases={i: i for i in range(n)})


def _ag_task(shards, kinds):
    send = _ag_send_task(shards, kinds)
    forward = _ag_forward_task(send.out_shapes, kinds, [s.shape for s in shards])
    n_send_sems = len(send.sems)

    def start(ins, outs, sems):
        send.start(ins, outs, sems[:n_send_sems])

    def finish(ins, outs, sems):
        send.finish(ins, outs, sems[:n_send_sems])
        forward.start(outs, outs, sems[n_send_sems:])
        forward.finish(outs, outs, sems[n_send_sems:])

    return _Task(shards, send.out_shapes, send.sems + forward.sems, start, finish)


def _exchange_task(grads, kinds, shard_shapes):
    n = len(grads)
    out_shapes = [jax.ShapeDtypeStruct((4, rs // 2, cs), F32) for rs, cs in shard_shapes]

    def copies(ins, outs, sems):
        send_sems, recv_sems = sems
        x, y, c, _ = _place()
        return [_remote(_block(ins[i], kinds[i], q, 1 - c, shard_shapes[i]), outs[i].at[q],
                        send_sems.at[4 * i + q], recv_sems.at[4 * i + q], (x, y, 1 - c))
                for i in range(n) for q in range(4)]

    def start(ins, outs, sems):
        for cp in copies(ins, outs, sems):
            cp.start()

    def finish(ins, outs, sems):
        for cp in copies(ins, outs, sems):
            cp.wait()

    return _Task(grads, out_shapes, [pltpu.SemaphoreType.DMA((4 * n,))] * 2, start, finish)


def _grad_block_map(kind, nt):
    if kind == "col":
        return lambda j, t, p: (p[0] * nt + t, p[1 + j])
    return lambda j, t, p: ((p[1 + j] * 2 + p[0]) * nt + t, 0)


def _chip_sum(grad, sib, kind, shard_shape, place, name):
    rs, cs = shard_shape
    hr = rs // 2
    tr = _rtile(hr, 256)
    nt = hr // tr
    g_map = _grad_block_map(kind, nt)

    def body(p_ref, g_ref, s_ref, o_ref):
        o_ref[0] = (g_ref[...] + s_ref[0]).astype(BF16)

    return pl.pallas_call(
        body, name=name,
        grid_spec=pltpu.PrefetchScalarGridSpec(
            num_scalar_prefetch=1, grid=(3, nt),
            in_specs=[pl.BlockSpec((tr, cs), g_map),
                      pl.BlockSpec((1, tr, cs), lambda j, t, p: (p[1 + j], t, 0))],
            out_specs=pl.BlockSpec((1, tr, cs), lambda j, t, p: (j, t, 0))),
        out_shape=jax.ShapeDtypeStruct((3, hr, cs), BF16),
        compiler_params=_params(("arbitrary", "arbitrary")))(place, grad, sib)


def _scatter_task(parts):
    n = len(parts)

    def copies(ins, outs, sems):
        send_sems, recv_sems = sems
        _, _, c, chips = _place()
        return [_remote(ins[i].at[j], outs[i].at[j], send_sems.at[3 * i + j], recv_sems.at[3 * i + j], (px, py, c))
                for i in range(n) for j, (px, py) in enumerate(chips)]

    def start(ins, outs, sems):
        for cp in copies(ins, outs, sems):
            cp.start()

    def finish(ins, outs, sems):
        for cp in copies(ins, outs, sems):
            cp.wait()

    return _Task(parts, [jax.ShapeDtypeStruct(p.shape, p.dtype) for p in parts],
                 [pltpu.SemaphoreType.DMA((3 * n,))] * 2, start, finish)


def _final_sum(grad, sib, recv, kind, shard_shape, place, name):
    rs, cs = shard_shape
    hr = rs // 2
    tr = _rtile(hr, 256)
    nt = hr // tr
    g_map = _grad_block_map(kind, nt)

    def body(p_ref, g_ref, s_ref, r_ref, out_ref):
        acc = g_ref[...] + s_ref[0]
        for j in range(3):
            acc = acc + r_ref[j].astype(F32)
        out_ref[...] = acc

    return pl.pallas_call(
        body, name=name,
        grid_spec=pltpu.PrefetchScalarGridSpec(
            num_scalar_prefetch=1, grid=(nt,),
            in_specs=[pl.BlockSpec((tr, cs), lambda t, p: g_map(3, t, p)),
                      pl.BlockSpec((1, tr, cs), lambda t, p: (p[4], t, 0)),
                      pl.BlockSpec((3, tr, cs), lambda t, p: (0, t, 0))],
            out_specs=pl.BlockSpec((tr, cs), lambda t, p: (p[0] * nt + t, 0))),
        out_shape=jax.ShapeDtypeStruct((rs, cs), F32),
        compiler_params=_params(("arbitrary",)))(place, grad, sib, recv)


def _join_task(shards):
    n = len(shards)

    def copies(outs, sems):
        send_sems, recv_sems = sems
        x, y, c, _ = _place()
        mine = [_half_rows(outs[i], c) for i in range(n)]
        theirs = [_half_rows(outs[i], 1 - c) for i in range(n)]
        send = [_remote(mine[i], mine[i], send_sems.at[i], recv_sems.at[i], (x, y, 1 - c)) for i in range(n)]
        recv = [_remote(theirs[i], theirs[i], send_sems.at[i], recv_sems.at[i], (x, y, 1 - c)) for i in range(n)]
        return send, recv

    def start(ins, outs, sems):
        for cp in copies(outs, sems)[0]:
            cp.start()

    def finish(ins, outs, sems):
        send, recv = copies(outs, sems)
        for cp in send:
            cp.wait_send()
        for cp in recv:
            cp.wait_recv()

    return _Task(shards, [jax.ShapeDtypeStruct(s.shape, F32) for s in shards],
                 [pltpu.SemaphoreType.DMA((n,))] * 2, start, finish, aliases={i: i for i in range(n)})


def _all_reduce_small(v, name):
    rows = v.shape[0]

    def body(v_ref, o_ref, buf, send_sems, recv_sems):
        x, y, c, _ = _place()
        coord = lambda p: ((1 - x) if p & 4 else x, (1 - y) if p & 2 else y, (1 - c) if p & 1 else c)
        me = 4 * x + 2 * y + c
        buf[me] = v_ref[...]
        copies = []
        for p in range(1, 8):
            cp = _remote(v_ref, buf.at[me], send_sems.at[p - 1], recv_sems.at[p - 1], coord(p))
            cp.start()
            copies.append(cp)
        for p in range(1, 8):
            px, py, pc = coord(p)
            _remote(v_ref, buf.at[4 * px + 2 * py + pc], send_sems.at[p - 1], recv_sems.at[p - 1],
                    coord(p)).wait_recv()
        for cp in copies:
            cp.wait_send()
        acc = buf[0]
        for dev in range(1, 8):
            acc = acc + buf[dev]
        o_ref[...] = acc

    vm = pl.BlockSpec(memory_space=pltpu.VMEM)
    return pl.pallas_call(
        body, name=name, in_specs=[vm], out_specs=vm, out_shape=jax.ShapeDtypeStruct(v.shape, F32),
        scratch_shapes=[pltpu.VMEM((8, rows, LANES), F32), pltpu.SemaphoreType.DMA((7,)),
                        pltpu.SemaphoreType.DMA((7,))],
        compiler_params=pltpu.CompilerParams(vmem_limit_bytes=VMEM_LIMIT))(v)


class _GradReduce:
    def __init__(self, kinds, shard_shapes, place, tag):
        self.kinds, self.shapes, self.place, self.tag = kinds, shard_shapes, place, tag
        self.grads, self.sib, self.peers, self.recv, self.halves, self.done = {}, {}, {}, {}, {}, {}

    def _carry(self, task, idx, into):
        def sink(outs):
            into.update(zip(idx, outs))
        return task, sink

    def exchange(self, idx):
        task = _exchange_task([self.grads[i] for i in idx], [self.kinds[i] for i in idx],
                              [self.shapes[i] for i in idx])
        return self._carry(task, idx, self.sib)

    def chip_sum(self, idx):
        for i in idx:
            self.peers[i] = _chip_sum(self.grads[i], self.sib[i], self.kinds[i], self.shapes[i], self.place,
                                      "rs_chip_sum%d_%s" % (i, self.tag))

    def scatter(self, idx):
        return self._carry(_scatter_task([self.peers[i] for i in idx]), idx, self.recv)

    def final_sum(self, idx):
        for i in idx:
            self.halves[i] = _final_sum(self.grads[i], self.sib[i], self.recv[i], self.kinds[i], self.shapes[i],
                                        self.place, "rs_final_sum%d_%s" % (i, self.tag))

    def join(self, idx):
        return self._carry(_join_task([self.halves[i] for i in idx]), idx, self.done)


def _matmul_carrying(carries, *args, **kw):
    carries = [c for c in carries if c is not None]
    if not carries:
        return _matmul(*args, **kw)
    out, *got = _matmul(*args, task=_join_tasks([task for task, _ in carries]), **kw)
    for task, sink in carries:
        sink(got[:len(task.out_shapes)])
        del got[:len(task.out_shapes)]
    return out


def _run_carry(carry, name):
    task, sink = carry
    sink(_run_task(task, name))


BIG = ["w_in", "ret_proj", "sgu_proj", "w_out", "w_ffn_in", "w_ffn_out"]
BIG_KIND = {"w_in": "col", "ret_proj": "row", "sgu_proj": "row", "w_out": "row", "w_ffn_in": "col",
            "w_ffn_out": "row"}
KINDS = [BIG_KIND[n] for n in BIG]
SMALL = ["norm_mix_w", "ret_gn_w", "sgu_ln_w", "sgu_ln_b", "sgu_w_s", "sgu_b_s", "norm_ffn_w"]
ORDER = ["norm_mix_w", "w_in", "ret_gn_w", "ret_proj", "sgu_ln_w", "sgu_ln_b", "sgu_w_s", "sgu_b_s",
         "sgu_proj", "w_out", "norm_ffn_w", "w_ffn_in", "w_ffn_out", "final_norm_w"]
AG_WHOLE_BEHIND = {"mm_in": [(0, 1), (0, 2), (0, 3)]}
AG_SEND_BEHIND = {"mm_in": [(0, 4), (0, 5)], "mm_ffn_in": [(1, 0)], "mm_ffn_out": [(1, 1), (1, 2), (1, 3)]}
AG_FORWARD_BEHIND = {"mm_ret_proj": [(0, 4), (0, 5)], "mm_ffn_out": [(1, 0)], "mm_in": [(0, 1), (0, 2), (0, 3)]}


def _layer_fwd(x, l, full, partial, shards, sm, tables, dims):
    h, d, w = dims
    c_su, c_gate = 6 * h * RET_DK, 6 * h * RET_DK + 2 * w
    tag = "l%d" % l
    wt = full[l]

    def due(plan, key, where, present):
        return [(l + dl, i) for dl, i in plan.get(key, [])
                if l + dl < len(full) and (BIG[i] in where[l + dl]) == present and BIG[i] not in full[l + dl]]

    def mm(a, wname, out_dtype, key, res=None):
        stages = [(due(AG_WHOLE_BEHIND, key, partial, False), _ag_task, full),
                  (due(AG_SEND_BEHIND, key, partial, False), _ag_send_task, partial)]
        tasks, sinks = [], []
        for todo, make, sink in stages:
            if todo:
                tasks.append(make([shards[ll][i] for ll, i in todo], [KINDS[i] for _, i in todo]))
                sinks.append((todo, sink))
        todo = due(AG_FORWARD_BEHIND, key, partial, True)
        if todo:
            tasks.append(_ag_forward_task([partial[ll][BIG[i]] for ll, i in todo], [KINDS[i] for _, i in todo],
                                          [shards[ll][i].shape for ll, i in todo]))
            sinks.append((todo, full))
        task = _join_tasks(tasks) if tasks else None
        if key == "mm_ffn_in":
            got = list(_ffn_in_swiglu(a, wt[wname], key + "_" + tag, task=task))
            out = tuple(got[:3])
            del got[:3]
        elif task is None:
            return _matmul(a, wt[wname], "nn", out_dtype, key + "_" + tag, res=res)
        else:
            out, *got = _matmul(a, wt[wname], "nn", out_dtype, key + "_" + tag, res=res, task=task)
        for todo, sink in sinks:
            for ll, i in todo:
                sink[ll][BIG[i]] = got.pop(0)
        return out

    h1 = _rms_fwd(x, sm["norm_mix_w"], "rms_mix_fwd_" + tag)
    z = mm(h1, "w_in", BF16, "mm_in")
    ga, states = _ret_fwd(z, sm["ret_gn_w"], tables, h, "ret_fwd_" + tag)
    sg = _sgu_fwd(z, sm["sgu_ln_w"], sm["sgu_ln_b"], sm["ws_m"], sm["bs"], c_su, w, "sgu_fwd_" + tag)
    a = mm(ga, "ret_proj", BF16, "mm_ret_proj")
    b, mg = _proj_merge(sg, wt["sgu_proj"], a, z, c_gate, "mm_sgu_proj_merge_" + tag)
    x1 = mm(mg, "w_out", F32, "mm_out", res=x)
    h2 = _rms_fwd(x1, sm["norm_ffn_w"], "rms_ffn_fwd_" + tag)
    fa, fc, f = mm(h2, "w_ffn_in", BF16, "mm_ffn_in")
    x2 = mm(f, "w_ffn_out", F32, "mm_ffn_out", res=x1)
    saved = dict(x=x, h1=h1, z=z, states=states, ga=ga, sg=sg, a=a, b=b, mg=mg, x1=x1, h2=h2, fa=fa, fc=fc, f=f)
    return x2, saved


def _layer_bwd(dx2, dx2b, sv, wt, sm, tables, dims, tag, above, own, last):
    h, d, w = dims
    c_su, c_gate = 6 * h * RET_DK, 6 * h * RET_DK + 2 * w
    gs = {}
    every = list(range(len(BIG)))
    early, ffn, proj = [1, 2, 3, 4, 5], [4, 5], [1, 2, 3]

    def mm(carries, a, b, mode, out_dtype, key, **kw):
        return _matmul_carrying(carries, a, b, mode, out_dtype, key + "_" + tag, **kw)

    has_above = above is not None
    df = mm([above.exchange([0]) if has_above else None], dx2b, wt["w_ffn_out"], "nt", BF16, "mm_dffn_out_x")
    own.grads[5] = mm([above.exchange(early) if has_above else None], sv["f"], dx2b, "tn", F32, "mm_dffn_out_w")
    if has_above:
        above.chip_sum(every)
    dac = _swiglu_bwd(sv["fa"], sv["fc"], df, "swiglu_bwd_" + tag)
    dh2 = mm([above.scatter([0] + proj) if has_above else None], dac, wt["w_ffn_in"], "nt", BF16, "mm_dffn_in_x")
    own.grads[4] = mm([above.scatter(ffn) if has_above else None], sv["h2"], dac, "tn", F32, "mm_dffn_in_w")
    if has_above:
        above.final_sum(every)
    dx1, dx1b, gs["norm_ffn_w"] = _rms_bwd(sv["x1"], sm["norm_ffn_w"], dh2, dx2, "rms_ffn_bwd_" + tag)
    dmg = mm([], dx1b, wt["w_out"], "nt", BF16, "mm_dout_x")
    own.grads[3] = mm([], sv["mg"], dx1b, "tn", F32, "mm_dout_w")
    da, db, dz_tail = _merge_bwd(dmg, sv["a"], sv["b"], sv["z"], c_gate, "merge_bwd_" + tag)
    dga = mm([], da, wt["ret_proj"], "nt", BF16, "mm_dret_proj_x")
    own.grads[1] = mm([], sv["ga"], da, "tn", F32, "mm_dret_proj_w")
    dsg = mm([], db, wt["sgu_proj"], "nt", BF16, "mm_dsgu_proj_x")
    own.grads[2] = mm([], sv["sg"], db, "tn", F32, "mm_dsgu_proj_w")
    dz_tail, gs["sgu_ln_w"], gs["sgu_ln_b"], gs["sgu_w_s"], gs["sgu_b_s"] = _sgu_bwd(
        sv["z"], dsg, dz_tail, sm["sgu_ln_w"], sm["sgu_ln_b"], sm["ws_m"], sm["ws_mt"], sm["bs"], c_su, w,
        "sgu_bwd_" + tag)
    dz_ret, gs["ret_gn_w"] = _ret_bwd(sv["z"], dga, sv["states"], sm["ret_gn_w"], tables, h, "ret_bwd_" + tag)
    in_cols = c_su + 4 * w
    gw_in = mm([own.exchange(early) if last else None], sv["h1"], dz_ret, "tn", F32, "mm_din_w_ret",
               window=(in_cols, 0, None))
    if last:
        own.chip_sum(early)
    own.grads[0] = mm([own.scatter(ffn) if last else None], sv["h1"], dz_tail, "tn", F32, "mm_din_w_tail",
                      window=(in_cols, c_su, gw_in))
    dh1 = mm([above.join(every) if has_above else None, own.scatter(proj) if last else None,
              own.exchange([0]) if last else None], dz_ret, wt["w_in"], "nt", F32, "mm_din_x_ret")
    if last:
        own.chip_sum([0])
        own.final_sum(early)
    dh1 = mm([own.join(early) if last else None, own.scatter([0]) if last else None], dz_tail, wt["w_in"], "nt",
             BF16, "mm_din_x_tail", res=dh1, b_k0=c_su)
    dx, dxb, gs["norm_mix_w"] = _rms_bwd(sv["x"], sm["norm_mix_w"], dh1, dx1, "rms_mix_bwd_" + tag)
    if last:
        own.final_sum([0])
        _run_carry(own.join([0]), "rs_core_join_w_in_" + tag)
    return dx, dxb, gs


def _sgu_mask():
    pos = jnp.arange(SGU_LEN)
    return (pos[None, :] // CHUNK) <= (pos[:, None] // CHUNK)


def kernel(x, norm_mix_w, w_in, ret_gn_w, ret_proj, sgu_ln_w, sgu_ln_b, sgu_w_s, sgu_b_s, sgu_proj, w_out, norm_ffn_w, w_ffn_in, w_ffn_out, final_norm_w, loss_target, m_norm_mix_w, m_w_in, m_ret_gn_w, m_ret_proj, m_sgu_ln_w, m_sgu_ln_b, m_sgu_w_s, m_sgu_b_s, m_sgu_proj, m_w_out, m_norm_ffn_w, m_w_ffn_in, m_w_ffn_out, m_final_norm_w, v_norm_mix_w, v_w_in, v_ret_gn_w, v_ret_proj, v_sgu_ln_w, v_sgu_ln_b, v_sgu_w_s, v_sgu_b_s, v_sgu_proj, v_w_out, v_norm_ffn_w, v_w_ffn_in, v_w_ffn_out, v_final_norm_w):
    weights = dict(norm_mix_w=norm_mix_w, w_in=w_in, ret_gn_w=ret_gn_w, ret_proj=ret_proj, sgu_ln_w=sgu_ln_w,
                   sgu_ln_b=sgu_ln_b, sgu_w_s=sgu_w_s, sgu_b_s=sgu_b_s, sgu_proj=sgu_proj, w_out=w_out,
                   norm_ffn_w=norm_ffn_w, w_ffn_in=w_ffn_in, w_ffn_out=w_ffn_out, final_norm_w=final_norm_w)
    m_in = dict(norm_mix_w=m_norm_mix_w, w_in=m_w_in, ret_gn_w=m_ret_gn_w, ret_proj=m_ret_proj,
                sgu_ln_w=m_sgu_ln_w, sgu_ln_b=m_sgu_ln_b, sgu_w_s=m_sgu_w_s, sgu_b_s=m_sgu_b_s,
                sgu_proj=m_sgu_proj, w_out=m_w_out, norm_ffn_w=m_norm_ffn_w, w_ffn_in=m_w_ffn_in,
                w_ffn_out=m_w_ffn_out, final_norm_w=m_final_norm_w)
    v_in = dict(norm_mix_w=v_norm_mix_w, w_in=v_w_in, ret_gn_w=v_ret_gn_w, ret_proj=v_ret_proj,
                sgu_ln_w=v_sgu_ln_w, sgu_ln_b=v_sgu_ln_b, sgu_w_s=v_sgu_w_s, sgu_b_s=v_sgu_b_s,
                sgu_proj=v_sgu_proj, w_out=v_w_out, norm_ffn_w=v_norm_ffn_w, w_ffn_in=v_w_ffn_in,
                w_ffn_out=v_w_ffn_out, final_norm_w=v_final_norm_w)

    depth = w_in.shape[0]
    _, s, d = x.shape
    w = d
    in_cols = 4 * w_in.shape[2]
    h = (in_cols - 4 * d) // (2 * RET_DK + 2 * RET_DV)
    groups = sgu_w_s.shape[1]
    assert in_cols == h * (2 * RET_DK + 2 * RET_DV) + 4 * d and (6 * h * RET_DK) % d == 0
    assert s % SGU_LEN == 0 and w % groups == 0 and (w // groups) % LANES == 0
    dims = (h, d, w)
    tables = _ret_tables(s, h, _tile(s, RET_BLOCK))
    mask = _sgu_mask()
    cx, cy, cc = lax.axis_index("x"), lax.axis_index("y"), lax.axis_index("c")
    place = jnp.stack([cc, 2 * (1 - cx) + cy, 2 * cx + (1 - cy), 2 * (1 - cx) + (1 - cy),
                       2 * cx + cy]).astype(jnp.int32)

    shard_shapes = [weights[n].shape[1:] for n in BIG]
    shards = [[weights[n][l].astype(BF16) for n in BIG] for l in range(depth)]

    small = []
    for l in range(depth):
        sm = {n: weights[n][l] for n in SMALL}
        ws_m = jnp.where(mask[None], sgu_w_s[l], 0.0)
        sm["ws_m"] = ws_m.astype(BF16)
        sm["ws_mt"] = jnp.swapaxes(ws_m, 1, 2).astype(BF16)
        sm["bs"] = sgu_b_s[l][:, :, None]
        small.append(sm)

    xs = x[0]
    saved = []
    full = [{} for _ in range(depth)]
    partial = [{} for _ in range(depth)]
    full[0][BIG[0]], = _run_task(_ag_task(shards[0][:1], KINDS[:1]), "ag_w_in_l0")
    for l in range(depth):
        xs, sv = _layer_fwd(xs, l, full, partial, shards, small[l], tables, dims)
        saved.append(sv)
    dx, dxb, g_final, sq = _loss_head(xs, final_norm_w, loss_target[0])
    loss = lax.psum(sq[0, 0], ("x", "y", "c")) * (0.5 / d)

    grads_small = [None] * depth
    reduce = [_GradReduce(KINDS, shard_shapes, place, "l%d" % l) for l in range(depth)]
    for l in reversed(range(depth)):
        dx, dxb, grads_small[l] = _layer_bwd(dx, dxb, saved[l], full[l], small[l], tables, dims, "l%d" % l,
                                             reduce[l + 1] if l + 1 < depth else None, reduce[l], l == 0)
    grads_big = [[r.done[i] for i in range(len(BIG))] for r in reduce]
    grad_x = dx[None]

    pieces = []
    for l in range(depth):
        gs = dict(grads_small[l])
        gs["sgu_w_s"] = jnp.where(mask[None], gs["sgu_w_s"], 0.0)
        pieces += [gs[n].reshape(-1) for n in SMALL]
    pieces.append(g_final.reshape(-1))
    flat = jnp.concatenate(pieces)
    total = flat.shape[0]
    rows = -(-total // (8 * LANES)) * 8
    flat = jnp.pad(flat, (0, rows * LANES - total)).reshape(rows, LANES)
    summed = _all_reduce_small(flat, "ar_small").reshape(-1)
    grad = {}
    off = 0
    per_layer = {n: [] for n in SMALL}
    for l in range(depth):
        for n in SMALL:
            shp = weights[n].shape[1:]
            size = math.prod(shp)
            per_layer[n].append(summed[off:off + size].reshape(shp))
            off += size
    for n in SMALL:
        grad[n] = jnp.stack(per_layer[n])
    grad["final_norm_w"] = summed[off:off + d]
    for i, n in enumerate(BIG):
        grad[n] = jnp.stack([grads_big[l][i] for l in range(depth)])

    delta, new_m, new_v = {}, {}, {}
    for n in BIG:
        shp = weights[n].shape
        two_d = lambda a: a.reshape(shp[0] * shp[1], shp[2])
        dl, mn, vn = _adamw(two_d(weights[n]), two_d(grad[n]), two_d(m_in[n]), two_d(v_in[n]), "adamw_" + n)
        delta[n], new_m[n], new_v[n] = dl.reshape(shp), mn.reshape(shp), vn.reshape(shp)
    small_names = SMALL + ["final_norm_w"]

    def pack(tree):
        fl = jnp.concatenate([tree[n].reshape(-1) for n in small_names])
        return jnp.pad(fl, (0, rows * LANES - fl.shape[0])).reshape(rows, LANES)

    dl, mn, vn = _adamw(pack(weights), pack(grad), pack(m_in), pack(v_in), "adamw_small")
    off = 0
    for n in small_names:
        shp = weights[n].shape
        size = math.prod(shp)
        for src, dst in ((dl, delta), (mn, new_m), (vn, new_v)):
            dst[n] = src.reshape(-1)[off:off + size].reshape(shp)
        off += size

    return (loss, grad_x, *[grad[n] for n in ORDER], *[delta[n] for n in ORDER],
            *[new_m[n] for n in ORDER], *[new_v[n] for n in ORDER])
```

```python
import math

import jax
import jax.numpy as jnp
from jax import lax
from jax.experimental import pallas as pl
from jax.experimental.pallas import tpu as pltpu

F32 = jnp.float32
BF16 = jnp.bfloat16

CHUNK = 64
RET_DK = 128
RET_DV = 256
SGU_LEN = 128
ROPE_BASE = 10000.0
EPS = 1e-6
ADAM_LR = 0.001
ADAM_B1 = 0.9
ADAM_B2 = 0.999
ADAM_EPS = 1e-08
ADAM_WD = 0.01
ADAM_STEP = 10

LANES = 128
VMEM_LIMIT = 56 * 1024 * 1024
RET_BLOCK = 256
SGU_BLOCK = 256
ROW_BLOCK = 256
MM_TILE_PREFERRED = 1024
MM_TILE = 1408
MM_KTILE = 3584
FFN_TILE = 512
MESH = pl.DeviceIdType.MESH
ANY = pl.BlockSpec(memory_space=pl.ANY)
INV_SQRT2 = 1.0 / math.sqrt(2.0)
INV_SQRT_2PI = 1.0 / math.sqrt(2.0 * math.pi)

DN = {"nn": (((1,), (0,)), ((), ())), "nt": (((1,), (1,)), ((), ())), "tn": (((0,), (0,)), ((), ()))}


def _dot(a, b, mode="nn"):
    return lax.dot_general(a, b, DN[mode], preferred_element_type=F32)


def _tile(n, target):
    t = min(n, target) // LANES * LANES
    while t >= LANES:
        if n % t == 0:
            return t
        t -= LANES
    return n


def _rtile(n, target):
    t = min(n, target) // 16 * 16
    while t >= 16:
        if n % t == 0:
            return t
        t -= 16
    return n


def _out_tile(n):
    return MM_TILE_PREFERRED if n % MM_TILE_PREFERRED == 0 else _tile(n, MM_TILE)


def _params(sem):
    return pltpu.CompilerParams(dimension_semantics=sem, vmem_limit_bytes=VMEM_LIMIT)


def _sigmoid(x):
    return 1.0 / (1.0 + jnp.exp(-x))


def _gelu(x):
    return 0.5 * x * (1.0 + lax.erf(x * INV_SQRT2))


def _gelu_grad(x):
    return 0.5 * (1.0 + lax.erf(x * INV_SQRT2)) + x * jnp.exp(-0.5 * x * x) * INV_SQRT_2PI


def _matmul(a, b, mode, out_dtype, name, res=None, task=None, b_k0=0, window=None):
    if mode == "nn":
        (m, k), n = a.shape, b.shape[1]
    elif mode == "nt":
        (m, k), n = a.shape, b.shape[0]
    else:
        (k, m), n = a.shape, b.shape[1]
    out_cols, out_c0, into = window if window is not None else (n, 0, None)
    assert b_k0 == 0 or mode == "nt"
    tm, tn, tk = _out_tile(m), _out_tile(math.gcd(n, out_c0)), _tile(math.gcd(k, b_k0), MM_KTILE)
    ni, nj, nk = m // tm, n // tn, k // tk
    kb0, jb0 = b_k0 // tk, out_c0 // tn
    if mode == "tn":
        a_spec = pl.BlockSpec((tk, tm), lambda i, j, kk: (kk, i))
    else:
        a_spec = pl.BlockSpec((tm, tk), lambda i, j, kk: (i, kk))
    if mode == "nt":
        b_spec = pl.BlockSpec((tn, tk), lambda i, j, kk: (j, kb0 + kk))
    else:
        b_spec = pl.BlockSpec((tk, tn), lambda i, j, kk: (kk, j))
    r_spec = pl.BlockSpec((tm, tn), lambda i, j, kk: (i, j))
    o_spec = pl.BlockSpec((tm, tn), lambda i, j, kk: (i, jb0 + j))
    n_mm_in = 2 + (res is not None)
    t_ins = (task.ins if task is not None else []) + ([into] if into is not None else [])
    t_outs = task.out_shapes if task is not None else []
    t_sems = task.sems if task is not None else []
    in_specs = [a_spec, b_spec] + ([r_spec] if res is not None else []) + [ANY] * len(t_ins)
    acc_in_out = out_dtype == F32
    scratch = [] if (nk == 1 or acc_in_out) else [pltpu.VMEM((tm, tn), F32)]

    def body(*refs):
        a_ref, b_ref = refs[0], refs[1]
        r_ref = refs[2] if res is not None else None
        tin = refs[n_mm_in:n_mm_in + len(t_ins)]
        o_ref = refs[n_mm_in + len(t_ins)]
        tout = refs[n_mm_in + len(t_ins) + 1:n_mm_in + len(t_ins) + 1 + len(t_outs)]
        rest = refs[n_mm_in + len(t_ins) + 1 + len(t_outs):]
        acc_scr, sems = (rest[0], rest[1:]) if scratch else (None, rest)
        i, j, kk = pl.program_id(0), pl.program_id(1), pl.program_id(2)
        if task is not None:
            @pl.when((i == 0) & (j == 0) & (kk == 0))
            def _():
                task.start(tin, tout, sems)

        p = _dot(a_ref[...], b_ref[...], mode)
        if nk == 1:
            if r_ref is not None:
                p = p + r_ref[...]
            o_ref[...] = p.astype(o_ref.dtype)
        else:
            acc = o_ref if acc_in_out else acc_scr

            @pl.when(kk == 0)
            def _():
                acc[...] = p if r_ref is None or not acc_in_out else p + r_ref[...]

            @pl.when(kk > 0)
            def _():
                acc[...] += p

            if not acc_in_out:
                @pl.when(kk == nk - 1)
                def _():
                    o = acc[...]
                    if r_ref is not None:
                        o = o + r_ref[...]
                    o_ref[...] = o.astype(o_ref.dtype)

        if task is not None:
            @pl.when((i == ni - 1) & (j == nj - 1) & (kk == nk - 1))
            def _():
                task.finish(tin, tout, sems)

    args = (a, b) + ((res,) if res is not None else ()) + tuple(t_ins)
    out_shape = jax.ShapeDtypeStruct((m, out_cols), out_dtype)
    into_alias = {n_mm_in + len(t_ins) - 1: 0} if into is not None else {}
    if task is None:
        return pl.pallas_call(
            body, name=name, grid=(ni, nj, nk), in_specs=in_specs, out_specs=o_spec, out_shape=out_shape,
            input_output_aliases=into_alias, scratch_shapes=scratch,
            compiler_params=_params(("parallel", "parallel", "arbitrary")))(*args)
    aliases = dict(into_alias)
    aliases.update({n_mm_in + src: 1 + dst for src, dst in task.aliases.items()})
    return pl.pallas_call(
        body, name=name, grid=(ni, nj, nk), in_specs=in_specs, out_specs=[o_spec] + [ANY] * len(t_outs),
        out_shape=[out_shape] + list(t_outs), input_output_aliases=aliases, scratch_shapes=scratch + list(t_sems),
        compiler_params=pltpu.CompilerParams(dimension_semantics=("arbitrary", "arbitrary", "arbitrary"),
                                             vmem_limit_bytes=VMEM_LIMIT, has_side_effects=True))(*args)


def _ffn_in_swiglu(x, w, name, task=None):
    s, k = x.shape
    f = w.shape[1] // 2
    assert k <= MM_KTILE
    tm, tn = _out_tile(s), _tile(f, FFN_TILE)
    ni, nj = s // tm, f // tn
    t_ins = task.ins if task is not None else []
    t_outs = task.out_shapes if task is not None else []
    t_sems = task.sems if task is not None else []

    def body(*refs):
        x_ref, wa_ref, wc_ref = refs[:3]
        tin = refs[3:3 + len(t_ins)]
        a_ref, c_ref, o_ref = refs[3 + len(t_ins):6 + len(t_ins)]
        tout = refs[6 + len(t_ins):6 + len(t_ins) + len(t_outs)]
        sems = refs[6 + len(t_ins) + len(t_outs):]
        i, j = pl.program_id(0), pl.program_id(1)
        if task is not None:
            @pl.when((i == 0) & (j == 0))
            def _():
                task.start(tin, tout, sems)

        xv = x_ref[...]
        a = _dot(xv, wa_ref[...])
        c = _dot(xv, wc_ref[...])
        a_ref[...] = a.astype(BF16)
        c_ref[...] = c.astype(BF16)
        o_ref[...] = (a * _sigmoid(a) * c).astype(BF16)

        if task is not None:
            @pl.when((i == ni - 1) & (j == nj - 1))
            def _():
                task.finish(tin, tout, sems)

    o_spec = pl.BlockSpec((tm, tn), lambda i, j: (i, j))
    in_specs = [pl.BlockSpec((tm, k), lambda i, j: (i, 0)), pl.BlockSpec((k, tn), lambda i, j: (0, j)),
                pl.BlockSpec((k, tn), lambda i, j: (0, nj + j))] + [ANY] * len(t_ins)
    shp = jax.ShapeDtypeStruct((s, f), BF16)
    if task is None:
        params = _params(("parallel", "parallel"))
    else:
        params = pltpu.CompilerParams(dimension_semantics=("arbitrary", "arbitrary"),
                                      vmem_limit_bytes=VMEM_LIMIT, has_side_effects=True)
    return pl.pallas_call(
        body, name=name, grid=(ni, nj), in_specs=in_specs, out_specs=[o_spec] * 3 + [ANY] * len(t_outs),
        out_shape=[shp] * 3 + list(t_outs),
        input_output_aliases={3 + src: 3 + dst for src, dst in (task.aliases.items() if task is not None else [])},
        scratch_shapes=list(t_sems), compiler_params=params)(x, w, w, *t_ins)


def _proj_merge(x, w, a, z, col0, name):
    s, k = x.shape
    n = w.shape[1]
    assert k <= MM_KTILE
    tm, tn = _out_tile(s), _out_tile(math.gcd(n, col0))
    jg = col0 // tn

    def body(x_ref, w_ref, a_ref, ga_ref, gb_ref, b_ref, o_ref):
        b = _dot(x_ref[...], w_ref[...])
        b_ref[...] = b.astype(BF16)
        o_ref[...] = (_sigmoid(ga_ref[...].astype(F32)) * a_ref[...].astype(F32)
                      + _sigmoid(gb_ref[...].astype(F32)) * b).astype(BF16)

    tile = pl.BlockSpec((tm, tn), lambda i, j: (i, j))
    gate = lambda off: pl.BlockSpec((tm, tn), lambda i, j: (i, jg + off + j))
    shp = jax.ShapeDtypeStruct((s, n), BF16)
    return pl.pallas_call(
        body, name=name, grid=(s // tm, n // tn),
        in_specs=[pl.BlockSpec((tm, k), lambda i, j: (i, 0)), pl.BlockSpec((k, tn), lambda i, j: (0, j)), tile,
                  gate(0), gate(n // tn)],
        out_specs=[tile, tile], out_shape=[shp, shp],
        compiler_params=_params(("parallel", "parallel")))(x, w, a, z, z)


def _rms_fwd(x, w, name):
    s, d = x.shape
    tr = _tile(s, ROW_BLOCK)

    def body(x_ref, w_ref, o_ref):
        xv = x_ref[...]
        r = lax.rsqrt(jnp.mean(xv * xv, axis=-1, keepdims=True) + EPS)
        o_ref[...] = (xv * r * w_ref[...]).astype(BF16)

    row = pl.BlockSpec((tr, d), lambda i: (i, 0))
    vec = pl.BlockSpec((1, d), lambda i: (0, 0))
    return pl.pallas_call(body, name=name, grid=(s // tr,), in_specs=[row, vec], out_specs=row,
                          out_shape=jax.ShapeDtypeStruct((s, d), BF16),
                          compiler_params=_params(("parallel",)))(x, w.reshape(1, d))


def _rms_bwd(x, w, dh, dres, name):
    s, d = x.shape
    tr = _tile(s, ROW_BLOCK)

    def body(x_ref, w_ref, dh_ref, dr_ref, dx_ref, dxb_ref, dw_ref):
        xv = x_ref[...]
        r = lax.rsqrt(jnp.mean(xv * xv, axis=-1, keepdims=True) + EPS)
        xh = xv * r
        dy = dh_ref[...].astype(F32)
        dxh = dy * w_ref[...]
        dx = dr_ref[...] + r * (dxh - xh * jnp.mean(dxh * xh, axis=-1, keepdims=True))
        dx_ref[...] = dx
        dxb_ref[...] = dx.astype(BF16)

        @pl.when(pl.program_id(0) == 0)
        def _():
            dw_ref[...] = jnp.zeros_like(dw_ref)

        dw_ref[...] += jnp.sum(dy * xh, axis=0, keepdims=True)

    row = pl.BlockSpec((tr, d), lambda i: (i, 0))
    vec = pl.BlockSpec((1, d), lambda i: (0, 0))
    return pl.pallas_call(
        body, name=name, grid=(s // tr,), in_specs=[row, vec, row, row], out_specs=[row, row, vec],
        out_shape=[jax.ShapeDtypeStruct((s, d), F32), jax.ShapeDtypeStruct((s, d), BF16),
                   jax.ShapeDtypeStruct((1, d), F32)],
        compiler_params=_params(("arbitrary",)))(x, w.reshape(1, d), dh, dres)


def _loss_head(x, w, tgt):
    s, d = x.shape
    tr = _tile(s, ROW_BLOCK)

    def body(x_ref, w_ref, t_ref, dx_ref, dxb_ref, dw_ref, l_ref):
        xv = x_ref[...]
        r = lax.rsqrt(jnp.mean(xv * xv, axis=-1, keepdims=True) + EPS)
        xh = xv * r
        e = xh * w_ref[...] - t_ref[...]
        dy = e * (1.0 / d)
        dxh = dy * w_ref[...]
        dx = r * (dxh - xh * jnp.mean(dxh * xh, axis=-1, keepdims=True))
        dx_ref[...] = dx
        dxb_ref[...] = dx.astype(BF16)

        @pl.when(pl.program_id(0) == 0)
        def _():
            dw_ref[...] = jnp.zeros_like(dw_ref)
            l_ref[...] = jnp.zeros_like(l_ref)

        dw_ref[...] += jnp.sum(dy * xh, axis=0, keepdims=True)
        l_ref[...] += jnp.sum(jnp.sum(e * e, axis=1, keepdims=True), axis=0, keepdims=True)

    row = pl.BlockSpec((tr, d), lambda i: (i, 0))
    vec = pl.BlockSpec((1, d), lambda i: (0, 0))
    one = pl.BlockSpec((1, 1), lambda i: (0, 0))
    return pl.pallas_call(
        body, name="loss_head", grid=(s // tr,), in_specs=[row, vec, row], out_specs=[row, row, vec, one],
        out_shape=[jax.ShapeDtypeStruct((s, d), F32), jax.ShapeDtypeStruct((s, d), BF16),
                   jax.ShapeDtypeStruct((1, d), F32), jax.ShapeDtypeStruct((1, 1), F32)],
        compiler_params=_params(("arbitrary",)))(x, w.reshape(1, d), tgt)


def _ret_tables(s, h, t):
    half = RET_DK // 2
    inv = ROPE_BASE ** (-jnp.arange(half, dtype=F32) / half)
    ang = jnp.arange(s, dtype=F32)[:, None] * inv[None, :]
    cos, sin = jnp.cos(ang), jnp.sin(ang)
    cosf = jnp.concatenate([cos, cos], axis=1)
    sinf = jnp.concatenate([-sin, sin], axis=1)
    log_g = jnp.log1p(-(2.0 ** (-5.0 - jnp.arange(h, dtype=F32))))
    idx = jnp.arange(t, dtype=F32)
    chunk = jnp.arange(t) // CHUNK
    allowed = chunk[None, :] <= chunk[:, None]
    dm = jnp.where(allowed[None], jnp.exp(log_g[:, None, None] * jnp.abs(idx[:, None] - idx[None, :])), 0.0)
    qd = jnp.exp(log_g[:, None] * (idx[None, :] + 1.0))
    kd = jnp.exp(log_g[:, None] * (t - 1.0 - idx[None, :]))
    qd = jnp.broadcast_to(qd[:, :, None], (h, t, RET_DK))
    kd = jnp.broadcast_to(kd[:, :, None], (h, t, RET_DK))
    cd = jnp.broadcast_to(jnp.exp(log_g * t)[:, None, None], (h, 1, RET_DV))
    return cosf, sinf, dm, qd, kd, cd


def _rot(x, cos, sin):
    return x * cos + pltpu.roll(x, RET_DK // 2, 1) * sin


def _rot_t(x, cos, sin):
    return x * cos - pltpu.roll(x, RET_DK // 2, 1) * sin


def _ret_heads_per_step(h):
    return h


def _ret_in_specs(h, t, rev_nb=None):
    hb = _ret_heads_per_step(h)
    ng = h // hb
    blk = (lambda b: b) if rev_nb is None else (lambda b: rev_nb - 1 - b)
    return [
        pl.BlockSpec((t, hb * RET_DK), lambda hh, b: (blk(b), hh)),
        pl.BlockSpec((t, hb * RET_DK), lambda hh, b: (blk(b), ng + hh)),
        pl.BlockSpec((t, hb * RET_DV), lambda hh, b: (blk(b), ng + hh)),
        pl.BlockSpec((t, hb * RET_DV), lambda hh, b: (blk(b), 2 * ng + hh)),
        pl.BlockSpec((t, RET_DK), lambda hh, b: (blk(b), 0)),
        pl.BlockSpec((t, RET_DK), lambda hh, b: (blk(b), 0)),
        pl.BlockSpec((hb, t, t), lambda hh, b: (hh, 0, 0)),
        pl.BlockSpec((hb, t, RET_DK), lambda hh, b: (hh, 0, 0)),
        pl.BlockSpec((hb, t, RET_DK), lambda hh, b: (hh, 0, 0)),
        pl.BlockSpec((hb, 1, RET_DV), lambda hh, b: (hh, 0, 0)),
        pl.BlockSpec((1, hb * RET_DV), lambda hh, b: (0, hh)),
    ]


def _ret_fwd(z, gn_w, tables, h, name):
    s = z.shape[0]
    t = _tile(s, RET_BLOCK)
    nb = s // t
    hb = _ret_heads_per_step(h)
    scale = RET_DK ** -0.5

    def body(q_ref, k_ref, v_ref, g_ref, cos_ref, sin_ref, dm_ref, qd_ref, kd_ref, cd_ref, gn_ref,
             o_ref, st_ref, st_scr):
        @pl.when(pl.program_id(1) == 0)
        def _():
            st_scr[...] = jnp.zeros_like(st_scr)

        cos, sin = cos_ref[...], sin_ref[...]
        for u in range(hb):
            ck = slice(u * RET_DK, (u + 1) * RET_DK)
            cv = slice(u * RET_DV, (u + 1) * RET_DV)
            qf = _rot(q_ref[:, ck].astype(F32), cos, sin) * scale
            kf = _rot(k_ref[:, ck].astype(F32), cos, sin)
            vb = v_ref[:, cv]
            p = _dot(qf.astype(BF16), kf.astype(BF16), "nt") * dm_ref[u]
            st = st_scr[u]
            stb = st.astype(BF16)
            st_ref[0, u] = stb
            o = _dot(p.astype(BF16), vb) + _dot((qf * qd_ref[u]).astype(BF16), stb)
            st_scr[u] = st * cd_ref[u] + _dot((kf * kd_ref[u]).astype(BF16), vb, "tn")
            dlt = o - jnp.mean(o, axis=-1, keepdims=True)
            oh = dlt * lax.rsqrt(jnp.mean(dlt * dlt, axis=-1, keepdims=True) + EPS)
            g = g_ref[:, cv].astype(F32)
            o_ref[:, cv] = (g * _sigmoid(g) * oh * gn_ref[:, cv]).astype(BF16)

    return pl.pallas_call(
        body, name=name, grid=(h // hb, nb), in_specs=_ret_in_specs(h, t),
        out_specs=[pl.BlockSpec((t, hb * RET_DV), lambda hh, b: (b, hh)),
                   pl.BlockSpec((1, hb, RET_DK, RET_DV), lambda hh, b: (b, hh, 0, 0))],
        out_shape=[jax.ShapeDtypeStruct((s, h * RET_DV), BF16),
                   jax.ShapeDtypeStruct((nb, h, RET_DK, RET_DV), BF16)],
        scratch_shapes=[pltpu.VMEM((hb, RET_DK, RET_DV), F32)],
        compiler_params=_params(("parallel", "arbitrary")))(z, z, z, z, *tables, gn_w.reshape(1, -1))


def _ret_bwd(z, dga, states, gn_w, tables, h, name):
    s = z.shape[0]
    t = _tile(s, RET_BLOCK)
    nb = s // t
    hb = _ret_heads_per_step(h)
    assert hb == h
    scale = RET_DK ** -0.5
    c_k, c_v, c_g = h * RET_DK, 2 * h * RET_DK, 2 * h * RET_DK + h * RET_DV

    def body(q_ref, k_ref, v_ref, g_ref, cos_ref, sin_ref, dm_ref, qd_ref, kd_ref, cd_ref, gn_ref,
             dga_ref, st_ref, dz_ref, dgn_ref, dst_scr):
        @pl.when(pl.program_id(1) == 0)
        def _():
            dst_scr[...] = jnp.zeros_like(dst_scr)
            dgn_ref[...] = jnp.zeros_like(dgn_ref)

        dq_ref, dk_ref = dz_ref.at[:, 0:c_k], dz_ref.at[:, c_k:c_v]
        dv_ref, dg_ref = dz_ref.at[:, c_v:c_g], dz_ref.at[:, c_g:c_g + h * RET_DV]
        cos, sin = cos_ref[...], sin_ref[...]
        for u in range(hb):
            ck = slice(u * RET_DK, (u + 1) * RET_DK)
            cv = slice(u * RET_DV, (u + 1) * RET_DV)
            dm = dm_ref[u]
            qf = _rot(q_ref[:, ck].astype(F32), cos, sin) * scale
            kf = _rot(k_ref[:, ck].astype(F32), cos, sin)
            qb, kb, vb = qf.astype(BF16), kf.astype(BF16), v_ref[:, cv]
            qdb = (qf * qd_ref[u]).astype(BF16)
            kdb = (kf * kd_ref[u]).astype(BF16)
            stb = st_ref[0, u]
            pb = (_dot(qb, kb, "nt") * dm).astype(BF16)
            o = _dot(pb, vb) + _dot(qdb, stb)
            dlt = o - jnp.mean(o, axis=-1, keepdims=True)
            rstd = lax.rsqrt(jnp.mean(dlt * dlt, axis=-1, keepdims=True) + EPS)
            oh = dlt * rstd
            gn = gn_ref[:, cv]
            g = g_ref[:, cv].astype(F32)
            sg = _sigmoid(g)
            dga_v = dga_ref[:, cv].astype(F32)
            dret = dga_v * g * sg
            dg_ref[:, cv] = (dga_v * oh * gn * sg * (1.0 + g * (1.0 - sg))).astype(BF16)
            dgn_ref[:, cv] += jnp.sum(dret * oh, axis=0, keepdims=True)
            doh = dret * gn
            do = rstd * (doh - jnp.mean(doh, axis=-1, keepdims=True)
                         - oh * jnp.mean(doh * oh, axis=-1, keepdims=True))
            dob = do.astype(BF16)
            dst = dst_scr[u]
            dstb = dst.astype(BF16)
            dv_ref[:, cv] = (_dot(pb, dob, "tn") + _dot(kdb, dstb)).astype(BF16)
            dpb = (_dot(dob, vb, "nt") * dm).astype(BF16)
            dqf = _dot(dpb, kb) + _dot(dob, stb, "nt") * qd_ref[u]
            dkf = _dot(dpb, qb, "tn") + _dot(vb, dstb, "nt") * kd_ref[u]
            dst_scr[u] = dst * cd_ref[u] + _dot(qdb, dob, "tn")
            dq_ref[:, ck] = _rot_t(dqf * scale, cos, sin).astype(BF16)
            dk_ref[:, ck] = _rot_t(dkf, cos, sin).astype(BF16)

    rb = lambda hh, b: (nb - 1 - b, hh)
    in_specs = _ret_in_specs(h, t, rev_nb=nb) + [
        pl.BlockSpec((t, hb * RET_DV), rb),
        pl.BlockSpec((1, hb, RET_DK, RET_DV), lambda hh, b: (nb - 1 - b, hh, 0, 0))]
    return pl.pallas_call(
        body, name=name, grid=(h // hb, nb), in_specs=in_specs,
        out_specs=[pl.BlockSpec((t, c_g + h * RET_DV), rb),
                   pl.BlockSpec((1, hb * RET_DV), lambda hh, b: (0, hh))],
        out_shape=[jax.ShapeDtypeStruct((s, c_g + h * RET_DV), BF16), jax.ShapeDtypeStruct((1, h * RET_DV), F32)],
        scratch_shapes=[pltpu.VMEM((hb, RET_DK, RET_DV), F32)],
        compiler_params=_params(("parallel", "arbitrary")))(z, z, z, z, *tables, gn_w.reshape(1, -1), dga, states)


def _sgu_fwd(z, ln_w, ln_b, ws_m, bs, col0, w, name):
    s = z.shape[0]
    t = _tile(s, SGU_BLOCK)
    groups = ws_m.shape[0]
    ch = w // groups
    cb = col0 // w

    def body(su_ref, sv_ref, lw_ref, lb_ref, ws_ref, bs_ref, o_ref):
        zv = _gelu(sv_ref[...].astype(F32))
        dlt = zv - jnp.mean(zv, axis=-1, keepdims=True)
        vn = dlt * lax.rsqrt(jnp.mean(dlt * dlt, axis=-1, keepdims=True) + EPS) * lw_ref[...] + lb_ref[...]
        vnb = vn.astype(BF16)
        for r in range(t // SGU_LEN):
            rows = slice(r * SGU_LEN, (r + 1) * SGU_LEN)
            for gi in range(groups):
                cols = slice(gi * ch, (gi + 1) * ch)
                mixed = _dot(ws_ref[gi], vnb[rows, cols]) + bs_ref[gi]
                o_ref[rows, cols] = (_gelu(su_ref[rows, cols].astype(F32)) * mixed).astype(BF16)

    row = lambda off: pl.BlockSpec((t, w), lambda i: (i, cb + off))
    vec = pl.BlockSpec((1, w), lambda i: (0, 0))
    return pl.pallas_call(
        body, name=name, grid=(s // t,),
        in_specs=[row(0), row(1), vec, vec,
                  pl.BlockSpec((groups, SGU_LEN, SGU_LEN), lambda i: (0, 0, 0)),
                  pl.BlockSpec((groups, SGU_LEN, 1), lambda i: (0, 0, 0))],
        out_specs=pl.BlockSpec((t, w), lambda i: (i, 0)),
        out_shape=jax.ShapeDtypeStruct((s, w), BF16),
        compiler_params=_params(("parallel",)))(z, z, ln_w.reshape(1, w), ln_b.reshape(1, w), ws_m, bs)


def _sgu_bwd(z, dsg, dz_tail, ln_w, ln_b, ws_m, ws_mt, bs, col0, w, name):
    s = z.shape[0]
    t = _tile(s, SGU_BLOCK)
    groups = ws_m.shape[0]
    ch = w // groups
    cb = col0 // w

    def body(su_ref, sv_ref, dsg_ref, lw_ref, lb_ref, ws_ref, wst_ref, bs_ref, tail_ref,
             dz_ref, dlw_ref, dlb_ref, dws_ref, dbs_ref, dvn_scr):
        @pl.when(pl.program_id(0) == 0)
        def _():
            dlw_ref[...] = jnp.zeros_like(dlw_ref)
            dlb_ref[...] = jnp.zeros_like(dlb_ref)
            dws_ref[...] = jnp.zeros_like(dws_ref)
            dbs_ref[...] = jnp.zeros_like(dbs_ref)

        dsu_ref, dsv_ref = dz_ref.at[:, 0:w], dz_ref.at[:, w:2 * w]
        sv = sv_ref[...].astype(F32)
        zv = _gelu(sv)
        dlt = zv - jnp.mean(zv, axis=-1, keepdims=True)
        rstd = lax.rsqrt(jnp.mean(dlt * dlt, axis=-1, keepdims=True) + EPS)
        vh = dlt * rstd
        vnb = (vh * lw_ref[...] + lb_ref[...]).astype(BF16)
        for r in range(t // SGU_LEN):
            rows = slice(r * SGU_LEN, (r + 1) * SGU_LEN)
            for gi in range(groups):
                cols = slice(gi * ch, (gi + 1) * ch)
                vn_p = vnb[rows, cols]
                mixed = _dot(ws_ref[gi], vn_p) + bs_ref[gi]
                su = su_ref[rows, cols].astype(F32)
                dsg_p = dsg_ref[rows, cols].astype(F32)
                dsu_ref[rows, cols] = (dsg_p * mixed * _gelu_grad(su)).astype(BF16)
                dmix = dsg_p * _gelu(su)
                dmixb = dmix.astype(BF16)
                dvn_scr[rows, cols] = _dot(wst_ref[gi], dmixb)
                dws_ref[gi] += _dot(dmixb, vn_p, "nt")
                dbs_ref[gi] += jnp.sum(dmix, axis=1, keepdims=True)
        dvn = dvn_scr[...]
        dlw_ref[...] += jnp.sum(dvn * vh, axis=0, keepdims=True)
        dlb_ref[...] += jnp.sum(dvn, axis=0, keepdims=True)
        dvh = dvn * lw_ref[...]
        dzv = rstd * (dvh - jnp.mean(dvh, axis=-1, keepdims=True)
                      - vh * jnp.mean(dvh * vh, axis=-1, keepdims=True))
        dsv_ref[...] = (dzv * _gelu_grad(sv)).astype(BF16)

    row = lambda off: pl.BlockSpec((t, w), lambda i: (i, cb + off))
    out_row = pl.BlockSpec((t, w), lambda i: (i, 0))
    vec = pl.BlockSpec((1, w), lambda i: (0, 0))
    mat = pl.BlockSpec((groups, SGU_LEN, SGU_LEN), lambda i: (0, 0, 0))
    col = pl.BlockSpec((groups, SGU_LEN, 1), lambda i: (0, 0, 0))
    return pl.pallas_call(
        body, name=name, grid=(s // t,),
        in_specs=[row(0), row(1), out_row, vec, vec, mat, mat, col, ANY],
        out_specs=[pl.BlockSpec((t, 2 * w), lambda i: (i, 0)), vec, vec, mat, col],
        out_shape=[jax.ShapeDtypeStruct(dz_tail.shape, BF16),
                   jax.ShapeDtypeStruct((1, w), F32), jax.ShapeDtypeStruct((1, w), F32),
                   jax.ShapeDtypeStruct((groups, SGU_LEN, SGU_LEN), F32),
                   jax.ShapeDtypeStruct((groups, SGU_LEN, 1), F32)],
        input_output_aliases={8: 0},
        scratch_shapes=[pltpu.VMEM((t, w), F32)],
        compiler_params=_params(("arbitrary",)))(z, z, dsg, ln_w.reshape(1, w), ln_b.reshape(1, w), ws_m, ws_mt, bs,
                                                 dz_tail)


def _merge_bwd(dmg, a, b, z, col0, name):
    s, d = a.shape
    tr = _tile(s, ROW_BLOCK)
    cb = col0 // d

    def body(dm_ref, a_ref, b_ref, ga_ref, gb_ref, da_ref, db_ref, dgt_ref):
        dm = dm_ref[...].astype(F32)
        sa = _sigmoid(ga_ref[...].astype(F32))
        sb = _sigmoid(gb_ref[...].astype(F32))
        da_ref[...] = (dm * sa).astype(BF16)
        db_ref[...] = (dm * sb).astype(BF16)
        dgt_ref[:, :d] = (dm * a_ref[...].astype(F32) * sa * (1.0 - sa)).astype(BF16)
        dgt_ref[:, d:] = (dm * b_ref[...].astype(F32) * sb * (1.0 - sb)).astype(BF16)

    row = pl.BlockSpec((tr, d), lambda i: (i, 0))
    wide = pl.BlockSpec((tr, 2 * d), lambda i: (i, 1))
    gate = lambda off: pl.BlockSpec((tr, d), lambda i: (i, cb + off))
    return pl.pallas_call(
        body, name=name, grid=(s // tr,), in_specs=[row, row, row, gate(0), gate(1)],
        out_specs=[row, row, wide],
        out_shape=[jax.ShapeDtypeStruct((s, d), BF16), jax.ShapeDtypeStruct((s, d), BF16),
                   jax.ShapeDtypeStruct((s, 4 * d), BF16)],
        compiler_params=_params(("parallel",)))(dmg, a, b, z, z)


def _swiglu_bwd(a, c, df, name):
    s, f = a.shape
    f2 = 2 * f
    tr = _tile(s, ROW_BLOCK)

    def body(a_ref, c_ref, df_ref, o_ref):
        a = a_ref[...].astype(F32)
        sg = _sigmoid(a)
        dfv = df_ref[...].astype(F32)
        o_ref[:, :f] = (dfv * c_ref[...].astype(F32) * sg * (1.0 + a * (1.0 - sg))).astype(BF16)
        o_ref[:, f:] = (dfv * a * sg).astype(BF16)

    row = pl.BlockSpec((tr, f), lambda i: (i, 0))
    return pl.pallas_call(body, name=name, grid=(s // tr,), in_specs=[row, row, row],
                          out_specs=pl.BlockSpec((tr, f2), lambda i: (i, 0)),
                          out_shape=jax.ShapeDtypeStruct((s, f2), BF16),
                          compiler_params=_params(("parallel",)))(a, c, df)


def _adamw(w, g, m, v, name, task=None):
    r, c = w.shape
    tr = _rtile(r, LANES)
    nt = r // tr
    c1 = 1.0 - ADAM_B1 ** ADAM_STEP
    c2 = 1.0 - ADAM_B2 ** ADAM_STEP
    t_ins = task.ins if task is not None else []
    t_outs = task.out_shapes if task is not None else []
    t_sems = task.sems if task is not None else []

    def body(*refs):
        w_ref, g_ref, m_ref, v_ref = refs[:4]
        tin = refs[4:4 + len(t_ins)]
        d_ref, mo_ref, vo_ref = refs[4 + len(t_ins):7 + len(t_ins)]
        tout = refs[7 + len(t_ins):7 + len(t_ins) + len(t_outs)]
        sems = refs[7 + len(t_ins) + len(t_outs):]
        if task is not None:
            @pl.when(pl.program_id(0) == 0)
            def _():
                task.start(tin, tout, sems)

        gv = g_ref[...]
        mn = ADAM_B1 * m_ref[...] + (1.0 - ADAM_B1) * gv
        vn = ADAM_B2 * v_ref[...] + (1.0 - ADAM_B2) * (gv * gv)
        mo_ref[...] = mn
        vo_ref[...] = vn
        d_ref[...] = -ADAM_LR * ((mn / c1) / (jnp.sqrt(vn / c2) + ADAM_EPS) + ADAM_WD * w_ref[...])

        if task is not None:
            @pl.when(pl.program_id(0) == nt - 1)
            def _():
                task.finish(tin, tout, sems)

    blk = pl.BlockSpec((tr, c), lambda i: (i, 0))
    shp = jax.ShapeDtypeStruct((r, c), F32)
    if task is None:
        params = _params(("parallel",))
    else:
        params = pltpu.CompilerParams(dimension_semantics=("arbitrary",), vmem_limit_bytes=VMEM_LIMIT,
                                      has_side_effects=True)
    return pl.pallas_call(
        body, name=name, grid=(nt,), in_specs=[blk] * 4 + [ANY] * len(t_ins),
        out_specs=[blk] * 3 + [ANY] * len(t_outs), out_shape=[shp] * 3 + list(t_outs),
        input_output_aliases={4 + src: 3 + dst for src, dst in (task.aliases.items() if task is not None else [])},
        scratch_shapes=list(t_sems), compiler_params=params)(w, g, m, v, *t_ins)


def _place():
    x, y, c = lax.axis_index("x"), lax.axis_index("y"), lax.axis_index("c")
    chips = [(1 - x, y), (x, 1 - y), (1 - x, 1 - y)]
    return x, y, c, chips


def _block(ref, kind, chip, half, shard_shape):
    rs, cs = shard_shape
    if kind == "col":
        rows = pl.ds(0, rs) if half is None else pl.ds(half * (rs // 2), rs // 2)
        return ref.at[rows, pl.ds(chip * cs, cs)]
    rows = pl.ds(chip * rs, rs) if half is None else pl.ds(chip * rs + half * (rs // 2), rs // 2)
    return ref.at[rows, :]


def _half_rows(ref, half):
    rs = ref.shape[0]
    return ref.at[pl.ds(half * (rs // 2), rs // 2), :]


class _Task:
    def __init__(self, ins, out_shapes, sems, start, finish, aliases=None):
        self.ins, self.out_shapes, self.sems = list(ins), list(out_shapes), list(sems)
        self.start, self.finish, self.aliases = start, finish, dict(aliases or {})


def _run_task(task, name):
    n_in, n_out = len(task.ins), len(task.out_shapes)

    def body(*refs):
        ins, outs, sems = refs[:n_in], refs[n_in:n_in + n_out], refs[n_in + n_out:]
        task.start(ins, outs, sems)
        task.finish(ins, outs, sems)

    return pl.pallas_call(
        body, name=name, in_specs=[ANY] * n_in, out_specs=[ANY] * n_out, out_shape=task.out_shapes,
        input_output_aliases=task.aliases, scratch_shapes=task.sems,
        compiler_params=pltpu.CompilerParams(has_side_effects=True))(*task.ins)


def _remote(src, dst, send_sem, recv_sem, device):
    return pltpu.make_async_remote_copy(src_ref=src, dst_ref=dst, send_sem=send_sem, recv_sem=recv_sem,
                                        device_id=device, device_id_type=MESH)


def _join_tasks(tasks):
    n_in = [len(t.ins) for t in tasks]
    n_out = [len(t.out_shapes) for t in tasks]
    n_sem = [len(t.sems) for t in tasks]

    def parts(refs, counts):
        out, at = [], 0
        for cnt in counts:
            out.append(refs[at:at + cnt])
            at += cnt
        return out

    def start(ins, outs, sems):
        for t, i, o, s in zip(tasks, parts(ins, n_in), parts(outs, n_out), parts(sems, n_sem)):
            t.start(i, o, s)

    def finish(ins, outs, sems):
        for t, i, o, s in zip(tasks, parts(ins, n_in), parts(outs, n_out), parts(sems, n_sem)):
            t.finish(i, o, s)

    aliases = {}
    for k, t in enumerate(tasks):
        aliases.update({sum(n_in[:k]) + src: sum(n_out[:k]) + dst for src, dst in t.aliases.items()})
    return _Task([a for t in tasks for a in t.ins], [s for t in tasks for s in t.out_shapes],
                 [s for t in tasks for s in t.sems], start, finish, aliases)


def _ag_send_task(shards, kinds):
    n = len(shards)
    out_shapes = [jax.ShapeDtypeStruct((s.shape[0], 4 * s.shape[1]) if k == "col" else (4 * s.shape[0], s.shape[1]),
                                       s.dtype) for s, k in zip(shards, kinds)]

    def copies(ins, outs, sems):
        send_sems, recv_sems, own_send_sems, own_recv_sems = sems
        x, y, c, chips = _place()
        me = 2 * x + y
        own, ici, landed = [], [], []
        for i in range(n):
            shp = ins[i].shape
            own.append(_remote(ins[i], _block(outs[i], kinds[i], me, None, shp), own_send_sems.at[i],
                               own_recv_sems.at[i], (x, y, 1 - c)))
            for j, (px, py) in enumerate(chips):
                k = 3 * i + j
                ici.append(_remote(_half_rows(ins[i], c), _block(outs[i], kinds[i], me, c, shp),
                                   send_sems.at[k], recv_sems.at[k], (px, py, c)))
                got = _block(outs[i], kinds[i], 2 * px + py, c, shp)
                landed.append(_remote(got, got, send_sems.at[k], recv_sems.at[k], (px, py, c)))
        return own, ici, landed

    def start(ins, outs, sems):
        own, ici, _ = copies(ins, outs, sems)
        for cp in own + ici:
            cp.start()

    def finish(ins, outs, sems):
        own, ici, landed = copies(ins, outs, sems)
        for cp in landed:
            cp.wait_recv()
        for cp in own:
            cp.wait()
        for cp in ici:
            cp.wait_send()

    sems = [pltpu.SemaphoreType.DMA((3 * n,))] * 2 + [pltpu.SemaphoreType.DMA((n,))] * 2
    return _Task(shards, out_shapes, sems, start, finish)


def _ag_forward_task(partial, kinds, shard_shapes):
    n = len(partial)

    def copies(outs, sems):
        fsend_sems, frecv_sems = sems
        x, y, c, chips = _place()
        fwd, passed = [], []
        for i in range(n):
            for j, (px, py) in enumerate(chips):
                k = 3 * i + j
                got = _block(outs[i], kinds[i], 2 * px + py, c, shard_shapes[i])
                fwd.append(_remote(got, got, fsend_sems.at[k], frecv_sems.at[k], (x, y, 1 - c)))
                theirs = _block(outs[i], kinds[i], 2 * px + py, 1 - c, shard_shapes[i])
                passed.append(_remote(theirs, theirs, fsend_sems.at[k], frecv_sems.at[k], (x, y, 1 - c)))
        return fwd, passed

    def start(ins, outs, sems):
        for cp in copies(outs, sems)[0]:
            cp.start()

    def finish(ins, outs, sems):
        fwd, passed = copies(outs, sems)
        for cp in passed:
            cp.wait_recv()
        for cp in fwd:
            cp.wait_send()

    return _Task(partial, [jax.ShapeDtypeStruct(p.shape, p.dtype) for p in partial],
                 [pltpu.SemaphoreType.DMA((3 * n,))] * 2, start, finish, aliases={i: i for i in range(n)})


def _ag_task(shards, kinds):
    send = _ag_send_task(shards, kinds)
    forward = _ag_forward_task(send.out_shapes, kinds, [s.shape for s in shards])
    n_send_sems = len(send.sems)

    def start(ins, outs, sems):
        send.start(ins, outs, sems[:n_send_sems])

    def finish(ins, outs, sems):
        send.finish(ins, outs, sems[:n_send_sems])
        forward.start(outs, outs, sems[n_send_sems:])
        forward.finish(outs, outs, sems[n_send_sems:])

    return _Task(shards, send.out_shapes, send.sems + forward.sems, start, finish)


def _exchange_task(grads, kinds, shard_shapes):
    n = len(grads)
    out_shapes = [jax.ShapeDtypeStruct((4, rs // 2, cs), F32) for rs, cs in shard_shapes]

    def copies(ins, outs, sems):
        send_sems, recv_sems = sems
        x, y, c, _ = _place()
        return [_remote(_block(ins[i], kinds[i], q, 1 - c, shard_shapes[i]), outs[i].at[q],
                        send_sems.at[4 * i + q], recv_sems.at[4 * i + q], (x, y, 1 - c))
                for i in range(n) for q in range(4)]

    def start(ins, outs, sems):
        for cp in copies(ins, outs, sems):
            cp.start()

    def finish(ins, outs, sems):
        for cp in copies(ins, outs, sems):
            cp.wait()

    return _Task(grads, out_shapes, [pltpu.SemaphoreType.DMA((4 * n,))] * 2, start, finish)


def _grad_block_map(kind, nt):
    if kind == "col":
        return lambda j, t, p: (p[0] * nt + t, p[1 + j])
    return lambda j, t, p: ((p[1 + j] * 2 + p[0]) * nt + t, 0)


def _chip_sum(grad, sib, kind, shard_shape, place, name):
    rs, cs = shard_shape
    hr = rs // 2
    tr = _rtile(hr, 256)
    nt = hr // tr
    g_map = _grad_block_map(kind, nt)

    def body(p_ref, g_ref, s_ref, o_ref):
        o_ref[0] = (g_ref[...] + s_ref[0]).astype(BF16)

    return pl.pallas_call(
        body, name=name,
        grid_spec=pltpu.PrefetchScalarGridSpec(
            num_scalar_prefetch=1, grid=(3, nt),
            in_specs=[pl.BlockSpec((tr, cs), g_map),
                      pl.BlockSpec((1, tr, cs), lambda j, t, p: (p[1 + j], t, 0))],
            out_specs=pl.BlockSpec((1, tr, cs), lambda j, t, p: (j, t, 0))),
        out_shape=jax.ShapeDtypeStruct((3, hr, cs), BF16),
        compiler_params=_params(("arbitrary", "arbitrary")))(place, grad, sib)


def _scatter_task(parts):
    n = len(parts)

    def copies(ins, outs, sems):
        send_sems, recv_sems = sems
        _, _, c, chips = _place()
        return [_remote(ins[i].at[j], outs[i].at[j], send_sems.at[3 * i + j], recv_sems.at[3 * i + j], (px, py, c))
                for i in range(n) for j, (px, py) in enumerate(chips)]

    def start(ins, outs, sems):
        for cp in copies(ins, outs, sems):
            cp.start()

    def finish(ins, outs, sems):
        for cp in copies(ins, outs, sems):
            cp.wait()

    return _Task(parts, [jax.ShapeDtypeStruct(p.shape, p.dtype) for p in parts],
                 [pltpu.SemaphoreType.DMA((3 * n,))] * 2, start, finish)


def _final_sum(grad, sib, recv, kind, shard_shape, place, name):
    rs, cs = shard_shape
    hr = rs // 2
    tr = _rtile(hr, 256)
    nt = hr // tr
    g_map = _grad_block_map(kind, nt)

    def body(p_ref, g_ref, s_ref, r_ref, out_ref):
        acc = g_ref[...] + s_ref[0]
        for j in range(3):
            acc = acc + r_ref[j].astype(F32)
        out_ref[...] = acc

    return pl.pallas_call(
        body, name=name,
        grid_spec=pltpu.PrefetchScalarGridSpec(
            num_scalar_prefetch=1, grid=(nt,),
            in_specs=[pl.BlockSpec((tr, cs), lambda t, p: g_map(3, t, p)),
                      pl.BlockSpec((1, tr, cs), lambda t, p: (p[4], t, 0)),
                      pl.BlockSpec((3, tr, cs), lambda t, p: (0, t, 0))],
            out_specs=pl.BlockSpec((tr, cs), lambda t, p: (p[0] * nt + t, 0))),
        out_shape=jax.ShapeDtypeStruct((rs, cs), F32),
        compiler_params=_params(("arbitrary",)))(place, grad, sib, recv)


def _join_task(shards):
    n = len(shards)

    def copies(outs, sems):
        send_sems, recv_sems = sems
        x, y, c, _ = _place()
        mine = [_half_rows(outs[i], c) for i in range(n)]
        theirs = [_half_rows(outs[i], 1 - c) for i in range(n)]
        send = [_remote(mine[i], mine[i], send_sems.at[i], recv_sems.at[i], (x, y, 1 - c)) for i in range(n)]
        recv = [_remote(theirs[i], theirs[i], send_sems.at[i], recv_sems.at[i], (x, y, 1 - c)) for i in range(n)]
        return send, recv

    def start(ins, outs, sems):
        for cp in copies(outs, sems)[0]:
            cp.start()

    def finish(ins, outs, sems):
        send, recv = copies(outs, sems)
        for cp in send:
            cp.wait_send()
        for cp in recv:
            cp.wait_recv()

    return _Task(shards, [jax.ShapeDtypeStruct(s.shape, F32) for s in shards],
                 [pltpu.SemaphoreType.DMA((n,))] * 2, start, finish, aliases={i: i for i in range(n)})


def _gather_small_task(v):
    rows = v.shape[0]

    def copies(ins, outs, sems):
        send_sems, recv_sems, local_sems = sems
        x, y, c, _ = _place()
        coord = lambda p: ((1 - x) if p & 4 else x, (1 - y) if p & 2 else y, (1 - c) if p & 1 else c)
        me = 4 * x + 2 * y + c
        own = pltpu.make_async_copy(ins[0], outs[0].at[me], local_sems.at[0])
        send = [_remote(ins[0], outs[0].at[me], send_sems.at[p - 1], recv_sems.at[p - 1], coord(p))
                for p in range(1, 8)]
        recv = []
        for p in range(1, 8):
            px, py, pc = coord(p)
            slot = outs[0].at[4 * px + 2 * py + pc]
            recv.append(_remote(slot, slot, send_sems.at[p - 1], recv_sems.at[p - 1], coord(p)))
        return own, send, recv

    def start(ins, outs, sems):
        own, send, _ = copies(ins, outs, sems)
        own.start()
        for cp in send:
            cp.start()

    def finish(ins, outs, sems):
        own, send, recv = copies(ins, outs, sems)
        for cp in recv:
            cp.wait_recv()
        for cp in send:
            cp.wait_send()
        own.wait()

    return _Task([v], [jax.ShapeDtypeStruct((8, rows, LANES), F32)],
                 [pltpu.SemaphoreType.DMA((7,)), pltpu.SemaphoreType.DMA((7,)), pltpu.SemaphoreType.DMA((1,))],
                 start, finish)


def _sum_slots(buf, name):
    n, rows, lanes = buf.shape
    tr = _rtile(rows, 512)

    def body(b_ref, o_ref):
        acc = b_ref[0]
        for dev in range(1, n):
            acc = acc + b_ref[dev]
        o_ref[...] = acc

    return pl.pallas_call(
        body, name=name, grid=(rows // tr,), in_specs=[pl.BlockSpec((n, tr, lanes), lambda i: (0, i, 0))],
        out_specs=pl.BlockSpec((tr, lanes), lambda i: (i, 0)), out_shape=jax.ShapeDtypeStruct((rows, lanes), F32),
        compiler_params=_params(("parallel",)))(buf)


class _GradReduce:
    def __init__(self, kinds, shard_shapes, place, tag):
        self.kinds, self.shapes, self.place, self.tag = kinds, shard_shapes, place, tag
        self.grads, self.sib, self.peers, self.recv, self.halves, self.done = {}, {}, {}, {}, {}, {}

    def _carry(self, task, idx, into):
        def sink(outs):
            into.update(zip(idx, outs))
        return task, sink

    def exchange(self, idx):
        task = _exchange_task([self.grads[i] for i in idx], [self.kinds[i] for i in idx],
                              [self.shapes[i] for i in idx])
        return self._carry(task, idx, self.sib)

    def chip_sum(self, idx):
        for i in idx:
            self.peers[i] = _chip_sum(self.grads[i], self.sib[i], self.kinds[i], self.shapes[i], self.place,
                                      "rs_chip_sum%d_%s" % (i, self.tag))

    def scatter(self, idx):
        return self._carry(_scatter_task([self.peers[i] for i in idx]), idx, self.recv)

    def final_sum(self, idx):
        for i in idx:
            self.halves[i] = _final_sum(self.grads[i], self.sib[i], self.recv[i], self.kinds[i], self.shapes[i],
                                        self.place, "rs_final_sum%d_%s" % (i, self.tag))

    def join(self, idx):
        return self._carry(_join_task([self.halves[i] for i in idx]), idx, self.done)


def _matmul_carrying(carries, *args, **kw):
    carries = [c for c in carries if c is not None]
    if not carries:
        return _matmul(*args, **kw)
    out, *got = _matmul(*args, task=_join_tasks([task for task, _ in carries]), **kw)
    for task, sink in carries:
        sink(got[:len(task.out_shapes)])
        del got[:len(task.out_shapes)]
    return out


def _run_carry(carry, name):
    task, sink = carry
    sink(_run_task(task, name))


BIG = ["w_in", "ret_proj", "sgu_proj", "w_out", "w_ffn_in", "w_ffn_out"]
BIG_KIND = {"w_in": "col", "ret_proj": "row", "sgu_proj": "row", "w_out": "row", "w_ffn_in": "col",
            "w_ffn_out": "row"}
KINDS = [BIG_KIND[n] for n in BIG]
SMALL = ["norm_mix_w", "ret_gn_w", "sgu_ln_w", "sgu_ln_b", "sgu_w_s", "sgu_b_s", "norm_ffn_w"]
ORDER = ["norm_mix_w", "w_in", "ret_gn_w", "ret_proj", "sgu_ln_w", "sgu_ln_b", "sgu_w_s", "sgu_b_s",
         "sgu_proj", "w_out", "norm_ffn_w", "w_ffn_in", "w_ffn_out", "final_norm_w"]
AG_WHOLE_BEHIND = {"mm_in": [(0, 1), (0, 2), (0, 3)]}
AG_SEND_BEHIND = {"mm_in": [(0, 4), (0, 5)], "mm_ffn_in": [(1, 0)], "mm_ffn_out": [(1, 1), (1, 2), (1, 3)]}
AG_FORWARD_BEHIND = {"mm_ret_proj": [(0, 4), (0, 5)], "mm_ffn_out": [(1, 0)], "mm_in": [(0, 1), (0, 2), (0, 3)]}


def _layer_fwd(x, l, full, partial, shards, sm, tables, dims):
    h, d, w = dims
    c_su, c_gate = 6 * h * RET_DK, 6 * h * RET_DK + 2 * w
    tag = "l%d" % l
    wt = full[l]

    def due(plan, key, where, present):
        return [(l + dl, i) for dl, i in plan.get(key, [])
                if l + dl < len(full) and (BIG[i] in where[l + dl]) == present and BIG[i] not in full[l + dl]]

    def mm(a, wname, out_dtype, key, res=None):
        stages = [(due(AG_WHOLE_BEHIND, key, partial, False), _ag_task, full),
                  (due(AG_SEND_BEHIND, key, partial, False), _ag_send_task, partial)]
        tasks, sinks = [], []
        for todo, make, sink in stages:
            if todo:
                tasks.append(make([shards[ll][i] for ll, i in todo], [KINDS[i] for _, i in todo]))
                sinks.append((todo, sink))
        todo = due(AG_FORWARD_BEHIND, key, partial, True)
        if todo:
            tasks.append(_ag_forward_task([partial[ll][BIG[i]] for ll, i in todo], [KINDS[i] for _, i in todo],
                                          [shards[ll][i].shape for ll, i in todo]))
            sinks.append((todo, full))
        task = _join_tasks(tasks) if tasks else None
        if key == "mm_ffn_in":
            got = list(_ffn_in_swiglu(a, wt[wname], key + "_" + tag, task=task))
            out = tuple(got[:3])
            del got[:3]
        elif task is None:
            return _matmul(a, wt[wname], "nn", out_dtype, key + "_" + tag, res=res)
        else:
            out, *got = _matmul(a, wt[wname], "nn", out_dtype, key + "_" + tag, res=res, task=task)
        for todo, sink in sinks:
            for ll, i in todo:
                sink[ll][BIG[i]] = got.pop(0)
        return out

    h1 = _rms_fwd(x, sm["norm_mix_w"], "rms_mix_fwd_" + tag)
    z = mm(h1, "w_in", BF16, "mm_in")
    ga, states = _ret_fwd(z, sm["ret_gn_w"], tables, h, "ret_fwd_" + tag)
    sg = _sgu_fwd(z, sm["sgu_ln_w"], sm["sgu_ln_b"], sm["ws_m"], sm["bs"], c_su, w, "sgu_fwd_" + tag)
    a = mm(ga, "ret_proj", BF16, "mm_ret_proj")
    b, mg = _proj_merge(sg, wt["sgu_proj"], a, z, c_gate, "mm_sgu_proj_merge_" + tag)
    x1 = mm(mg, "w_out", F32, "mm_out", res=x)
    h2 = _rms_fwd(x1, sm["norm_ffn_w"], "rms_ffn_fwd_" + tag)
    fa, fc, f = mm(h2, "w_ffn_in", BF16, "mm_ffn_in")
    x2 = mm(f, "w_ffn_out", F32, "mm_ffn_out", res=x1)
    saved = dict(x=x, h1=h1, z=z, states=states, ga=ga, sg=sg, a=a, b=b, mg=mg, x1=x1, h2=h2, fa=fa, fc=fc, f=f)
    return x2, saved


def _layer_bwd(dx2, dx2b, sv, wt, sm, tables, dims, tag, above, own, last):
    h, d, w = dims
    c_su, c_gate = 6 * h * RET_DK, 6 * h * RET_DK + 2 * w
    gs = {}
    every = list(range(len(BIG)))
    early, ffn, proj = [1, 2, 3, 4, 5], [4, 5], [1, 2, 3]

    def mm(carries, a, b, mode, out_dtype, key, **kw):
        return _matmul_carrying(carries, a, b, mode, out_dtype, key + "_" + tag, **kw)

    has_above = above is not None
    df = mm([above.exchange([0]) if has_above else None], dx2b, wt["w_ffn_out"], "nt", BF16, "mm_dffn_out_x")
    own.grads[5] = mm([above.exchange(early) if has_above else None], sv["f"], dx2b, "tn", F32, "mm_dffn_out_w")
    if has_above:
        above.chip_sum(every)
    dac = _swiglu_bwd(sv["fa"], sv["fc"], df, "swiglu_bwd_" + tag)
    dh2 = mm([above.scatter([0] + proj) if has_above else None], dac, wt["w_ffn_in"], "nt", BF16, "mm_dffn_in_x")
    own.grads[4] = mm([above.scatter(ffn) if has_above else None], sv["h2"], dac, "tn", F32, "mm_dffn_in_w")
    if has_above:
        above.final_sum(every)
    dx1, dx1b, gs["norm_ffn_w"] = _rms_bwd(sv["x1"], sm["norm_ffn_w"], dh2, dx2, "rms_ffn_bwd_" + tag)
    dmg = mm([], dx1b, wt["w_out"], "nt", BF16, "mm_dout_x")
    own.grads[3] = mm([], sv["mg"], dx1b, "tn", F32, "mm_dout_w")
    da, db, dz_tail = _merge_bwd(dmg, sv["a"], sv["b"], sv["z"], c_gate, "merge_bwd_" + tag)
    dga = mm([], da, wt["ret_proj"], "nt", BF16, "mm_dret_proj_x")
    own.grads[1] = mm([], sv["ga"], da, "tn", F32, "mm_dret_proj_w")
    dsg = mm([], db, wt["sgu_proj"], "nt", BF16, "mm_dsgu_proj_x")
    own.grads[2] = mm([], sv["sg"], db, "tn", F32, "mm_dsgu_proj_w")
    dz_tail, gs["sgu_ln_w"], gs["sgu_ln_b"], gs["sgu_w_s"], gs["sgu_b_s"] = _sgu_bwd(
        sv["z"], dsg, dz_tail, sm["sgu_ln_w"], sm["sgu_ln_b"], sm["ws_m"], sm["ws_mt"], sm["bs"], c_su, w,
        "sgu_bwd_" + tag)
    dz_ret, gs["ret_gn_w"] = _ret_bwd(sv["z"], dga, sv["states"], sm["ret_gn_w"], tables, h, "ret_bwd_" + tag)
    in_cols = c_su + 4 * w
    gw_in = mm([own.exchange(early) if last else None], sv["h1"], dz_ret, "tn", F32, "mm_din_w_ret",
               window=(in_cols, 0, None))
    if last:
        own.chip_sum(early)
    own.grads[0] = mm([own.scatter(ffn) if last else None], sv["h1"], dz_tail, "tn", F32, "mm_din_w_tail",
                      window=(in_cols, c_su, gw_in))
    dh1 = mm([above.join(every) if has_above else None, own.scatter(proj) if last else None,
              own.exchange([0]) if last else None], dz_ret, wt["w_in"], "nt", F32, "mm_din_x_ret")
    if last:
        own.chip_sum([0])
        own.final_sum(early)
    dh1 = mm([own.join(early) if last else None, own.scatter([0]) if last else None], dz_tail, wt["w_in"], "nt",
             BF16, "mm_din_x_tail", res=dh1, b_k0=c_su)
    dx, dxb, gs["norm_mix_w"] = _rms_bwd(sv["x"], sm["norm_mix_w"], dh1, dx1, "rms_mix_bwd_" + tag)
    if last:
        own.final_sum([0])
        _run_carry(own.join([0]), "rs_core_join_w_in_" + tag)
    return dx, dxb, gs


def _sgu_mask():
    pos = jnp.arange(SGU_LEN)
    return (pos[None, :] // CHUNK) <= (pos[:, None] // CHUNK)


def kernel(x, norm_mix_w, w_in, ret_gn_w, ret_proj, sgu_ln_w, sgu_ln_b, sgu_w_s, sgu_b_s, sgu_proj, w_out, norm_ffn_w, w_ffn_in, w_ffn_out, final_norm_w, loss_target, m_norm_mix_w, m_w_in, m_ret_gn_w, m_ret_proj, m_sgu_ln_w, m_sgu_ln_b, m_sgu_w_s, m_sgu_b_s, m_sgu_proj, m_w_out, m_norm_ffn_w, m_w_ffn_in, m_w_ffn_out, m_final_norm_w, v_norm_mix_w, v_w_in, v_ret_gn_w, v_ret_proj, v_sgu_ln_w, v_sgu_ln_b, v_sgu_w_s, v_sgu_b_s, v_sgu_proj, v_w_out, v_norm_ffn_w, v_w_ffn_in, v_w_ffn_out, v_final_norm_w):
    weights = dict(norm_mix_w=norm_mix_w, w_in=w_in, ret_gn_w=ret_gn_w, ret_proj=ret_proj, sgu_ln_w=sgu_ln_w,
                   sgu_ln_b=sgu_ln_b, sgu_w_s=sgu_w_s, sgu_b_s=sgu_b_s, sgu_proj=sgu_proj, w_out=w_out,
                   norm_ffn_w=norm_ffn_w, w_ffn_in=w_ffn_in, w_ffn_out=w_ffn_out, final_norm_w=final_norm_w)
    m_in = dict(norm_mix_w=m_norm_mix_w, w_in=m_w_in, ret_gn_w=m_ret_gn_w, ret_proj=m_ret_proj,
                sgu_ln_w=m_sgu_ln_w, sgu_ln_b=m_sgu_ln_b, sgu_w_s=m_sgu_w_s, sgu_b_s=m_sgu_b_s,
                sgu_proj=m_sgu_proj, w_out=m_w_out, norm_ffn_w=m_norm_ffn_w, w_ffn_in=m_w_ffn_in,
                w_ffn_out=m_w_ffn_out, final_norm_w=m_final_norm_w)
    v_in = dict(norm_mix_w=v_norm_mix_w, w_in=v_w_in, ret_gn_w=v_ret_gn_w, ret_proj=v_ret_proj,
                sgu_ln_w=v_sgu_ln_w, sgu_ln_b=v_sgu_ln_b, sgu_w_s=v_sgu_w_s, sgu_b_s=v_sgu_b_s,
                sgu_proj=v_sgu_proj, w_out=v_w_out, norm_ffn_w=v_norm_ffn_w, w_ffn_in=v_w_ffn_in,
                w_ffn_out=v_w_ffn_out, final_norm_w=v_final_norm_w)

    depth = w_in.shape[0]
    _, s, d = x.shape
    w = d
    in_cols = 4 * w_in.shape[2]
    h = (in_cols - 4 * d) // (2 * RET_DK + 2 * RET_DV)
    groups = sgu_w_s.shape[1]
    assert in_cols == h * (2 * RET_DK + 2 * RET_DV) + 4 * d and (6 * h * RET_DK) % d == 0
    assert s % SGU_LEN == 0 and w % groups == 0 and (w // groups) % LANES == 0
    dims = (h, d, w)
    tables = _ret_tables(s, h, _tile(s, RET_BLOCK))
    mask = _sgu_mask()
    cx, cy, cc = lax.axis_index("x"), lax.axis_index("y"), lax.axis_index("c")
    place = jnp.stack([cc, 2 * (1 - cx) + cy, 2 * cx + (1 - cy), 2 * (1 - cx) + (1 - cy),
                       2 * cx + cy]).astype(jnp.int32)

    shard_shapes = [weights[n].shape[1:] for n in BIG]
    shards = [[weights[n][l].astype(BF16) for n in BIG] for l in range(depth)]

    small = []
    for l in range(depth):
        sm = {n: weights[n][l] for n in SMALL}
        ws_m = jnp.where(mask[None], sgu_w_s[l], 0.0)
        sm["ws_m"] = ws_m.astype(BF16)
        sm["ws_mt"] = jnp.swapaxes(ws_m, 1, 2).astype(BF16)
        sm["bs"] = sgu_b_s[l][:, :, None]
        small.append(sm)

    xs = x[0]
    saved = []
    full = [{} for _ in range(depth)]
    partial = [{} for _ in range(depth)]
    full[0][BIG[0]], = _run_task(_ag_task(shards[0][:1], KINDS[:1]), "ag_w_in_l0")
    for l in range(depth):
        xs, sv = _layer_fwd(xs, l, full, partial, shards, small[l], tables, dims)
        saved.append(sv)
    dx, dxb, g_final, sq = _loss_head(xs, final_norm_w, loss_target[0])
    loss = lax.psum(sq[0, 0], ("x", "y", "c")) * (0.5 / d)

    grads_small = [None] * depth
    reduce = [_GradReduce(KINDS, shard_shapes, place, "l%d" % l) for l in range(depth)]
    for l in reversed(range(depth)):
        dx, dxb, grads_small[l] = _layer_bwd(dx, dxb, saved[l], full[l], small[l], tables, dims, "l%d" % l,
                                             reduce[l + 1] if l + 1 < depth else None, reduce[l], l == 0)
    grads_big = [[r.done[i] for i in range(len(BIG))] for r in reduce]
    grad_x = dx[None]

    pieces = []
    for l in range(depth):
        gs = dict(grads_small[l])
        gs["sgu_w_s"] = jnp.where(mask[None], gs["sgu_w_s"], 0.0)
        pieces += [gs[n].reshape(-1) for n in SMALL]
    pieces.append(g_final.reshape(-1))
    flat = jnp.concatenate(pieces)
    total = flat.shape[0]
    rows = -(-total // (16 * LANES)) * 16
    flat = jnp.pad(flat, (0, rows * LANES - total)).reshape(rows, LANES)

    grad, delta, new_m, new_v = {}, {}, {}, {}
    gathered_small = None
    for i, n in enumerate(BIG):
        grad[n] = jnp.stack([grads_big[l][i] for l in range(depth)])
        shp = weights[n].shape
        two_d = lambda a: a.reshape(shp[0] * shp[1], shp[2])
        task = _gather_small_task(flat) if i == 0 else None
        dl, mn, vn, *rest = _adamw(two_d(weights[n]), two_d(grad[n]), two_d(m_in[n]), two_d(v_in[n]),
                                   "adamw_" + n, task=task)
        if rest:
            gathered_small = rest[0]
        delta[n], new_m[n], new_v[n] = dl.reshape(shp), mn.reshape(shp), vn.reshape(shp)

    summed = _sum_slots(gathered_small, "sum_small").reshape(-1)
    off = 0
    per_layer = {n: [] for n in SMALL}
    for l in range(depth):
        for n in SMALL:
            shp = weights[n].shape[1:]
            size = math.prod(shp)
            per_layer[n].append(summed[off:off + size].reshape(shp))
            off += size
    for n in SMALL:
        grad[n] = jnp.stack(per_layer[n])
    grad["final_norm_w"] = summed[off:off + d]
    small_names = SMALL + ["final_norm_w"]

    def pack(tree):
        fl = jnp.concatenate([tree[n].reshape(-1) for n in small_names])
        return jnp.pad(fl, (0, rows * LANES - fl.shape[0])).reshape(rows, LANES)

    dl, mn, vn = _adamw(pack(weights), pack(grad), pack(m_in), pack(v_in), "adamw_small")
    off = 0
    for n in small_names:
        shp = weights[n].shape
        size = math.prod(shp)
        for src, dst in ((dl, delta), (mn, new_m), (vn, new_v)):
            dst[n] = src.reshape(-1)[off:off + size].reshape(shp)
        off += size

    return (loss, grad_x, *[grad[n] for n in ORDER], *[delta[n] for n in ORDER],
            *[new_m[n] for n in ORDER], *[new_v[n] for n in ORDER])
```
